```python
import math
import jax
import jax.numpy as jnp
from jax import lax
import numpy as np

D_MODEL = 1024
BATCH = 16
SEQ = 4096
DEPTH = 4

GRID_W = 64
CTX_LEN = 256
N_MIXERS = 4
D_FF = 4 * D_MODEL
N_MOD = 6
EPS = 1e-6
CONV_W = 3
HG_HEAD_DIM = 128
HG_HEADS = D_MODEL // HG_HEAD_DIM
HG_CHUNK = 32
S5_GROUP = 16
S5_GROUPS = D_MODEL // S5_GROUP
S5_STATE = 64
S5_DT_MIN = 1e-3
S5_DT_MAX = 1e-1
NA_HEADS = 16
NA_HEAD_DIM = D_MODEL // NA_HEADS
NA_ROWS = 8
NA_COLS = 16
NA_STRIP = 2 * NA_COLS

kernel_name = 'hybrid_interleaved_dit_block'


def _layers_of(mixer):
    return len(range(mixer, DEPTH, N_MIXERS))


def rmsnorm(x, g):
    xf = x.astype(jnp.float32)
    y = xf * lax.rsqrt(jnp.mean(xf * xf, axis=-1, keepdims=True) + EPS)
    return (y * g.astype(jnp.float32)).astype(x.dtype)


def modulate(h, shift, scale):
    return h * (1 + scale) + shift


def squared_relu_mlp(h, w_in, w_out):
    return jnp.square(jax.nn.relu(h @ w_in)) @ w_out


def depthwise_conv3(u, w):
    return lax.conv_general_dilated(u, w.astype(u.dtype)[:, None, :], window_strides=(1,),
                                    padding=((1, 1),), dimension_numbers=('NWC', 'WIO', 'NWC'),
                                    feature_group_count=u.shape[-1])


def short_gated_conv(h, w_in, conv_w, w_out):
    b_gate, c_gate, v = jnp.split(h @ w_in, 3, axis=-1)
    return (b_gate * depthwise_conv3(c_gate * v, conv_w)) @ w_out


def gla_chunked(q, k, v, log_f, s0):
    bsz, seq, nh, dk = q.shape
    dv = v.shape[-1]
    nc = seq // HG_CHUNK

    def chunks(t):
        return t.reshape(bsz, nc, HG_CHUNK, nh, t.shape[-1]).transpose(1, 0, 3, 2, 4)

    qc, kc, vc = chunks(q.astype(jnp.float32)), chunks(k.astype(jnp.float32)), chunks(v)
    b = jnp.cumsum(chunks(log_f.astype(jnp.float32)), axis=3)
    b_mid = b[:, :, :, HG_CHUNK // 2 - 1:HG_CHUNK // 2]
    b_last = b[:, :, :, -1:]
    scores = jnp.einsum('nbhtk,nbhsk->nbhts', qc * jnp.exp(b - b_mid), kc * jnp.exp(b_mid - b))
    prefix = jnp.tril(jnp.ones((HG_CHUNK, HG_CHUNK), dtype=bool))
    o_intra = jnp.einsum('nbhts,nbhsv->nbhtv', jnp.where(prefix, scores, 0.0), vc)
    q_out = qc * jnp.exp(b)
    k_state = kc * jnp.exp(b_last - b)
    decay_chunk = jnp.exp(b_last[:, :, :, 0, :])

    def step(s, xs):
        q_o, k_s, v_s, a = xs
        o = jnp.einsum('bhtk,bhkv->bhtv', q_o, s)
        s = a[..., None] * s + jnp.einsum('bhtk,bhtv->bhkv', k_s, v_s)
        return s, o

    s_fin, o_inter = lax.scan(step, s0, (q_out, k_state, vc, decay_chunk))
    o = (o_intra + o_inter).transpose(1, 0, 3, 2, 4).reshape(bsz, seq, nh, dv)
    return o, s_fin


def hgrn2_mixer(h_ctx, h_lat, w_in, lower_bound, g_norm, w_out, ctx_out):
    bsz = h_lat.shape[0]
    s0 = jnp.zeros((bsz, HG_HEADS, HG_HEAD_DIM, HG_HEAD_DIM), jnp.float32)

    def heads(t):
        return t.reshape(*t.shape[:2], HG_HEADS, HG_HEAD_DIM)

    def forget(t):
        f = lower_bound + (1 - lower_bound) * jax.nn.sigmoid(t.astype(jnp.float32))
        return heads(1 - f), heads(jnp.log(f))

    def project(h):
        q, inp, gate, f_fwd, f_bwd = jnp.split(h @ w_in, 5, axis=-1)
        return heads(q), heads(inp), gate, forget(f_fwd), forget(f_bwd)

    def flip(t):
        return t[:, ::-1]

    qc, vc, gc, (kcf, lcf), (kcb, lcb) = project(h_ctx)
    ql, vl, gl, (klf, llf), (klb, llb) = project(h_lat)
    oc_f, st_f = gla_chunked(qc, kcf, vc, lcf, s0)
    ol_f, _ = gla_chunked(ql, klf, vl, llf, st_f)
    oc_b, st_b = gla_chunked(flip(qc), flip(kcb), flip(vc), flip(lcb), s0)
    ol_b, _ = gla_chunked(flip(ql), flip(klb), flip(vl), flip(llb), st_b)

    def readout(o, gate):
        o = rmsnorm(o, g_norm.reshape(HG_HEADS, HG_HEAD_DIM))
        o = o.reshape(*o.shape[:2], D_MODEL).astype(gate.dtype)
        return (o * jax.nn.silu(gate)) @ w_out

    y_lat = readout(ol_f + flip(ol_b), gl)
    y_ctx = readout(oc_f + flip(oc_b), gc) if ctx_out else None
    return y_ctx, y_lat


def s5_discretise(lam_re, lam_im, log_dt, b_re, b_im):
    lam_re = jnp.minimum(lam_re.astype(jnp.float32), -1e-4)
    lam_im = lam_im.astype(jnp.float32)
    dt = jnp.exp(log_dt.astype(jnp.float32))[:, None]
    mag = jnp.exp(lam_re * dt)
    a_re, a_im = mag * jnp.cos(lam_im * dt), mag * jnp.sin(lam_im * dt)
    den = lam_re * lam_re + lam_im * lam_im
    f_re = ((a_re - 1) * lam_re + a_im * lam_im) / den
    f_im = (a_im * lam_re - (a_re - 1) * lam_im) / den
    b_re, b_im = b_re.astype(jnp.float32), b_im.astype(jnp.float32)
    bb_re = f_re[..., None] * b_re - f_im[..., None] * b_im
    bb_im = f_re[..., None] * b_im + f_im[..., None] * b_re
    return a_re, a_im, bb_re, bb_im


def s5_scan(bu_re, bu_im, a_re, a_im, init, reverse):
    if init is not None:
        i_re, i_im = init
        first = -1 if reverse else 0
        bu_re = bu_re.at[:, first].add(a_re * i_re - a_im * i_im)
        bu_im = bu_im.at[:, first].add(a_re * i_im + a_im * i_re)
    seq = bu_re.shape[1]
    a_seq_re = jnp.broadcast_to(a_re, (1, seq) + a_re.shape)
    a_seq_im = jnp.broadcast_to(a_im, (1, seq) + a_im.shape)

    def combine(e1, e2):
        a1r, a1i, b1r, b1i = e1
        a2r, a2i, b2r, b2i = e2
        return (a2r * a1r - a2i * a1i, a2r * a1i + a2i * a1r,
                a2r * b1r - a2i * b1i + b2r, a2r * b1i + a2i * b1r + b2i)

    _, _, x_re, x_im = lax.associative_scan(combine, (a_seq_re, a_seq_im, bu_re, bu_im),
                                            axis=1, reverse=reverse)
    return x_re, x_im


def s5_mixer(h_ctx, h_lat, lam_re, lam_im, log_dt, b_re, b_im, c_re, c_im, d_skip, w_glu, ctx_out):
    def grouped(h):
        return h.reshape(*h.shape[:2], S5_GROUPS, S5_GROUP).astype(jnp.float32)

    u_c, u_l = grouped(h_ctx), grouped(h_lat)
    d = d_skip.reshape(S5_GROUPS, S5_GROUP).astype(jnp.float32)
    y_c, y_l = d * u_c, d * u_l
    for direction in range(2):
        reverse = direction == 1
        a_re, a_im, bb_re, bb_im = s5_discretise(lam_re[direction], lam_im[direction],
                                                 log_dt[direction], b_re, b_im)
        cr, ci = c_re[direction].astype(jnp.float32), c_im[direction].astype(jnp.float32)

        def drive(u):
            return (jnp.einsum('bsgc,gnc->bsgn', u, bb_re), jnp.einsum('bsgc,gnc->bsgn', u, bb_im))

        def readout(xr, xi):
            return jnp.einsum('bsgn,gcn->bsgc', xr, cr) - jnp.einsum('bsgn,gcn->bsgc', xi, ci)

        xc_re, xc_im = s5_scan(*drive(u_c), a_re, a_im, None, reverse)
        end = 0 if reverse else -1
        xl_re, xl_im = s5_scan(*drive(u_l), a_re, a_im, (xc_re[:, end], xc_im[:, end]), reverse)
        y_l = y_l + readout(xl_re, xl_im)
        if ctx_out:
            y_c = y_c + readout(xc_re, xc_im)

    def glu(y):
        z = jax.nn.gelu(y.reshape(*y.shape[:2], D_MODEL)).astype(w_glu.dtype)
        val, gate = jnp.split(z @ w_glu, 2, axis=-1)
        return val * jax.nn.sigmoid(gate)

    return (glu(y_c) if ctx_out else None), glu(y_l)


def neighbourhood_attention(h_ctx, h_lat, w_qkv, rpb, w_out, ctx_out):
    bsz, seq, _ = h_lat.shape
    rows = seq // GRID_W
    kr = min(NA_ROWS, rows)
    n_cb = GRID_W // NA_COLS
    n_loc = kr * NA_STRIP
    scale = NA_HEAD_DIM ** -0.5

    def heads(t):
        return t.reshape(*t.shape[:2], NA_HEADS, NA_HEAD_DIM)

    q_l, k_l, v_l = (heads(t) for t in jnp.split(h_lat @ w_qkv, 3, axis=-1))
    k_c, v_c = (heads(t) for t in jnp.split(h_ctx @ w_qkv[:, D_MODEL:], 2, axis=-1))

    r = jnp.arange(rows)
    key_rows = jnp.clip(r - kr // 2, 0, rows - kr)[:, None] + jnp.arange(kr)
    qcol = (jnp.arange(n_cb) * NA_COLS)[:, None] + jnp.arange(NA_COLS)
    key_cols = (jnp.clip(jnp.arange(n_cb) * NA_COLS - NA_COLS // 2, 0, GRID_W - NA_STRIP)[:, None]
                + jnp.arange(NA_STRIP))
    q_start = jnp.clip(qcol - NA_COLS // 2, 0, GRID_W - NA_COLS)[..., None]
    kcol = key_cols[:, None, :]
    in_win = jnp.tile((kcol >= q_start) & (kcol < q_start + NA_COLS), (1, 1, kr))
    dr = key_rows - r[:, None] + (NA_ROWS - 1)
    dc = jnp.clip(kcol - qcol[..., None], 1 - NA_COLS, NA_COLS - 1) + (NA_COLS - 1)
    bias = rpb[:, dr[:, None, None, :, None], dc[None, :, :, None, :]]
    bias = bias.reshape(NA_HEADS, rows, n_cb, NA_COLS, n_loc).astype(jnp.float32)

    def gather(t):
        g = t.reshape(bsz, rows, GRID_W, NA_HEADS, NA_HEAD_DIM)
        g = g[:, key_rows[:, None, :, None], key_cols[None, :, None, :]]
        return g.reshape(bsz, rows, n_cb, n_loc, NA_HEADS, NA_HEAD_DIM)

    q_b = q_l.reshape(bsz, rows, n_cb, NA_COLS, NA_HEADS, NA_HEAD_DIM)
    k_win, v_win = gather(k_l), gather(v_l)
    s_loc = jnp.einsum('brjqhd,brjkhd->bhrjqk', q_b, k_win).astype(jnp.float32) * scale + bias
    s_loc = jnp.where(in_win, s_loc, -jnp.inf)
    s_ctx = jnp.einsum('brjqhd,bkhd->bhrjqk', q_b, k_c).astype(jnp.float32) * scale
    p = jax.nn.softmax(jnp.concatenate([s_loc, s_ctx], axis=-1), axis=-1).astype(v_l.dtype)
    o = (jnp.einsum('bhrjqk,brjkhd->brjqhd', p[..., :n_loc], v_win)
         + jnp.einsum('bhrjqk,bkhd->brjqhd', p[..., n_loc:], v_c))
    y_lat = o.reshape(bsz, seq, D_MODEL) @ w_out
    y_ctx = None
    if ctx_out:
        q_c = heads(h_ctx @ w_qkv[:, :D_MODEL])
        s_cc = jnp.einsum('bqhd,bkhd->bhqk', q_c, k_c).astype(jnp.float32) * scale
        p_cc = jax.nn.softmax(s_cc, axis=-1).astype(v_c.dtype)
        o_c = jnp.einsum('bhqk,bkhd->bqhd', p_cc, v_c)
        y_ctx = o_c.reshape(bsz, h_ctx.shape[1], D_MODEL) @ w_out
    return y_ctx, y_lat


def _fwd_setup_inputs(seed: int = 0) -> dict:
    key = jax.random.key(seed)
    keys = iter(jax.random.split(key, 32))
    f32 = jnp.float32

    def normal(shape, std):
        return jax.random.normal(next(keys), shape, f32) * std

    n_a, n_b, n_c, n_d = (_layers_of(m) for m in range(N_MIXERS))
    d = D_MODEL
    lam_shape = (n_c, 2, S5_GROUPS, S5_STATE)
    return {
        'x': normal((BATCH, SEQ, d), 1.0),
        'c': normal((BATCH, d), 1.0),
        'ctx': normal((BATCH, CTX_LEN, d), 1.0),
        'c_ctx': normal((d,), 1.0),
        'ada_w': normal((DEPTH, d, N_MOD * d), 0.5 * d ** -0.5),
        'ada_b': normal((DEPTH, N_MOD * d), 0.02),
        'norm_gains': 1.0 + normal((DEPTH, 4, d), 0.02),
        'mlp_w_in': normal((DEPTH, d, D_FF), d ** -0.5),
        'mlp_w_out': normal((DEPTH, D_FF, d), D_FF ** -0.5),
        'sc_w_in': normal((n_a, d, 3 * d), d ** -0.5),
        'sc_conv': normal((n_a, CONV_W, d), CONV_W ** -0.5),
        'sc_w_out': normal((n_a, d, d), d ** -0.5),
        'hg_w_in': normal((n_b, d, 5 * d), d ** -0.5),
        'hg_lower_bound': normal((DEPTH, d), 0.1),
        'hg_norm': 1.0 + normal((n_b, d), 0.02),
        'hg_w_out': normal((n_b, d, d), d ** -0.5),
        's5_lam_re': -0.5 + normal(lam_shape, 0.01),
        's5_lam_im': math.pi * jnp.arange(S5_STATE, dtype=f32) + normal(lam_shape, 0.01),
        's5_log_dt': jax.random.uniform(next(keys), (n_c, 2, S5_GROUPS), f32,
                                        math.log(S5_DT_MIN), math.log(S5_DT_MAX)),
        's5_b_re': normal((n_c, S5_GROUPS, S5_STATE, S5_GROUP), (2 * S5_GROUP) ** -0.5),
        's5_b_im': normal((n_c, S5_GROUPS, S5_STATE, S5_GROUP), (2 * S5_GROUP) ** -0.5),
        's5_c_re': normal((n_c, 2, S5_GROUPS, S5_GROUP, S5_STATE), (2 * S5_STATE) ** -0.5),
        's5_c_im': normal((n_c, 2, S5_GROUPS, S5_GROUP, S5_STATE), (2 * S5_STATE) ** -0.5),
        's5_d': normal((n_c, d), 1.0),
        's5_w_glu': normal((n_c, d, 2 * d), d ** -0.5),
        'na_w_qkv': normal((n_d, d, 3 * d), d ** -0.5),
        'na_rpb': normal((n_d, NA_HEADS, 2 * NA_ROWS - 1, 2 * NA_COLS - 1), 0.1),
        'na_w_out': normal((n_d, d, d), d ** -0.5),
    }


def _fwd_reference(x, c, ctx, c_ctx, ada_w, ada_b, norm_gains, mlp_w_in, mlp_w_out,
              sc_w_in, sc_conv, sc_w_out, hg_w_in, hg_lower_bound, hg_norm, hg_w_out,
              s5_lam_re, s5_lam_im, s5_log_dt, s5_b_re, s5_b_im, s5_c_re, s5_c_im, s5_d, s5_w_glu,
              na_w_qkv, na_rpb, na_w_out):
    lb_all = jnp.cumsum(jax.nn.softmax(hg_lower_bound.astype(jnp.float32), axis=0), axis=0)
    lb_all = lb_all - lb_all[0]
    h_lat, h_ctx = x, ctx
    for i in range(DEPTH):
        kind, j = i % N_MIXERS, i // N_MIXERS
        ctx_out = i < DEPTH - 1
        mod_lat = [m[:, None, :] for m in jnp.split(jax.nn.silu(c) @ ada_w[i] + ada_b[i], N_MOD, axis=-1)]
        mod_ctx = jnp.split(jax.nn.silu(c_ctx) @ ada_w[i] + ada_b[i], N_MOD, axis=-1)
        g_pre, g_post, g_pre_ff, g_post_ff = norm_gains[i]
        a_lat = modulate(rmsnorm(h_lat, g_pre), mod_lat[0], mod_lat[1])
        a_ctx = modulate(rmsnorm(h_ctx, g_pre), mod_ctx[0], mod_ctx[1]) if (ctx_out or kind != 0) else None
        if kind == 0:
            y_lat = short_gated_conv(a_lat, sc_w_in[j], sc_conv[j], sc_w_out[j])
            y_ctx = short_gated_conv(a_ctx, sc_w_in[j], sc_conv[j], sc_w_out[j]) if ctx_out else None
        elif kind == 1:
            y_ctx, y_lat = hgrn2_mixer(a_ctx, a_lat, hg_w_in[j], lb_all[i], hg_norm[j], hg_w_out[j], ctx_out)
        elif kind == 2:
            y_ctx, y_lat = s5_mixer(a_ctx, a_lat, s5_lam_re[j], s5_lam_im[j], s5_log_dt[j], s5_b_re[j],
                                    s5_b_im[j], s5_c_re[j], s5_c_im[j], s5_d[j], s5_w_glu[j], ctx_out)
        else:
            y_ctx, y_lat = neighbourhood_attention(a_ctx, a_lat, na_w_qkv[j], na_rpb[j], na_w_out[j], ctx_out)
        h_lat = h_lat + mod_lat[2] * rmsnorm(y_lat.astype(h_lat.dtype), g_post)
        if ctx_out:
            h_ctx = h_ctx + mod_ctx[2] * rmsnorm(y_ctx.astype(h_ctx.dtype), g_post)
        f_lat = squared_relu_mlp(modulate(rmsnorm(h_lat, g_pre_ff), mod_lat[3], mod_lat[4]), mlp_w_in[i], mlp_w_out[i])
        h_lat = h_lat + mod_lat[5] * rmsnorm(f_lat.astype(h_lat.dtype), g_post_ff)
        if ctx_out:
            f_ctx = squared_relu_mlp(modulate(rmsnorm(h_ctx, g_pre_ff), mod_ctx[3], mod_ctx[4]), mlp_w_in[i], mlp_w_out[i])
            h_ctx = h_ctx + mod_ctx[5] * rmsnorm(f_ctx.astype(h_ctx.dtype), g_post_ff)
    return h_lat


import jax as _jax
import jax.numpy as _jnp

TWIN_FORMAT = 'train_step'
FWD_PARAMS = ['x', 'c', 'ctx', 'c_ctx', 'ada_w', 'ada_b', 'norm_gains', 'mlp_w_in', 'mlp_w_out', 'sc_w_in', 'sc_conv', 'sc_w_out', 'hg_w_in', 'hg_lower_bound', 'hg_norm', 'hg_w_out', 's5_lam_re', 's5_lam_im', 's5_log_dt', 's5_b_re', 's5_b_im', 's5_c_re', 's5_c_im', 's5_d', 's5_w_glu', 'na_w_qkv', 'na_rpb', 'na_w_out']
TWIN_WEIGHTS = ['c_ctx', 'ada_w', 'ada_b', 'norm_gains', 'mlp_w_in', 'mlp_w_out', 'sc_w_in', 'sc_conv', 'sc_w_out', 'hg_w_in', 'hg_lower_bound', 'hg_norm', 'hg_w_out', 's5_lam_re', 's5_lam_im', 's5_log_dt', 's5_b_re', 's5_b_im', 's5_c_re', 's5_c_im', 's5_d', 's5_w_glu', 'na_w_qkv', 'na_rpb', 'na_w_out']
TWIN_DIFF_INPUT = 'x'
TWIN_INPUTS = ['x', 'c', 'ctx', 'c_ctx', 'ada_w', 'ada_b', 'norm_gains', 'mlp_w_in', 'mlp_w_out', 'sc_w_in', 'sc_conv', 'sc_w_out', 'hg_w_in', 'hg_lower_bound', 'hg_norm', 'hg_w_out', 's5_lam_re', 's5_lam_im', 's5_log_dt', 's5_b_re', 's5_b_im', 's5_c_re', 's5_c_im', 's5_d', 's5_w_glu', 'na_w_qkv', 'na_rpb', 'na_w_out', 'loss_target', 'm_c_ctx', 'm_ada_w', 'm_ada_b', 'm_norm_gains', 'm_mlp_w_in', 'm_mlp_w_out', 'm_sc_w_in', 'm_sc_conv', 'm_sc_w_out', 'm_hg_w_in', 'm_hg_lower_bound', 'm_hg_norm', 'm_hg_w_out', 'm_s5_lam_re', 'm_s5_lam_im', 'm_s5_log_dt', 'm_s5_b_re', 'm_s5_b_im', 'm_s5_c_re', 'm_s5_c_im', 'm_s5_d', 'm_s5_w_glu', 'm_na_w_qkv', 'm_na_rpb', 'm_na_w_out', 'v_c_ctx', 'v_ada_w', 'v_ada_b', 'v_norm_gains', 'v_mlp_w_in', 'v_mlp_w_out', 'v_sc_w_in', 'v_sc_conv', 'v_sc_w_out', 'v_hg_w_in', 'v_hg_lower_bound', 'v_hg_norm', 'v_hg_w_out', 'v_s5_lam_re', 'v_s5_lam_im', 'v_s5_log_dt', 'v_s5_b_re', 'v_s5_b_im', 'v_s5_c_re', 'v_s5_c_im', 'v_s5_d', 'v_s5_w_glu', 'v_na_w_qkv', 'v_na_rpb', 'v_na_w_out']
TWIN_OUTPUTS = ['loss', 'grad_x', 'grad_c_ctx', 'grad_ada_w', 'grad_ada_b', 'grad_norm_gains', 'grad_mlp_w_in', 'grad_mlp_w_out', 'grad_sc_w_in', 'grad_sc_conv', 'grad_sc_w_out', 'grad_hg_w_in', 'grad_hg_lower_bound', 'grad_hg_norm', 'grad_hg_w_out', 'grad_s5_lam_re', 'grad_s5_lam_im', 'grad_s5_log_dt', 'grad_s5_b_re', 'grad_s5_b_im', 'grad_s5_c_re', 'grad_s5_c_im', 'grad_s5_d', 'grad_s5_w_glu', 'grad_na_w_qkv', 'grad_na_rpb', 'grad_na_w_out', 'delta_c_ctx', 'delta_ada_w', 'delta_ada_b', 'delta_norm_gains', 'delta_mlp_w_in', 'delta_mlp_w_out', 'delta_sc_w_in', 'delta_sc_conv', 'delta_sc_w_out', 'delta_hg_w_in', 'delta_hg_lower_bound', 'delta_hg_norm', 'delta_hg_w_out', 'delta_s5_lam_re', 'delta_s5_lam_im', 'delta_s5_log_dt', 'delta_s5_b_re', 'delta_s5_b_im', 'delta_s5_c_re', 'delta_s5_c_im', 'delta_s5_d', 'delta_s5_w_glu', 'delta_na_w_qkv', 'delta_na_rpb', 'delta_na_w_out', 'new_m_c_ctx', 'new_m_ada_w', 'new_m_ada_b', 'new_m_norm_gains', 'new_m_mlp_w_in', 'new_m_mlp_w_out', 'new_m_sc_w_in', 'new_m_sc_conv', 'new_m_sc_w_out', 'new_m_hg_w_in', 'new_m_hg_lower_bound', 'new_m_hg_norm', 'new_m_hg_w_out', 'new_m_s5_lam_re', 'new_m_s5_lam_im', 'new_m_s5_log_dt', 'new_m_s5_b_re', 'new_m_s5_b_im', 'new_m_s5_c_re', 'new_m_s5_c_im', 'new_m_s5_d', 'new_m_s5_w_glu', 'new_m_na_w_qkv', 'new_m_na_rpb', 'new_m_na_w_out', 'new_v_c_ctx', 'new_v_ada_w', 'new_v_ada_b', 'new_v_norm_gains', 'new_v_mlp_w_in', 'new_v_mlp_w_out', 'new_v_sc_w_in', 'new_v_sc_conv', 'new_v_sc_w_out', 'new_v_hg_w_in', 'new_v_hg_lower_bound', 'new_v_hg_norm', 'new_v_hg_w_out', 'new_v_s5_lam_re', 'new_v_s5_lam_im', 'new_v_s5_log_dt', 'new_v_s5_b_re', 'new_v_s5_b_im', 'new_v_s5_c_re', 'new_v_s5_c_im', 'new_v_s5_d', 'new_v_s5_w_glu', 'new_v_na_w_qkv', 'new_v_na_rpb', 'new_v_na_w_out']
TWIN_LEAF_KINDS = {'loss': 'loss', 'grad_x': 'grad_x', 'grad_c_ctx': 'grad_w', 'grad_ada_w': 'grad_w', 'grad_ada_b': 'grad_w', 'grad_norm_gains': 'grad_w', 'grad_mlp_w_in': 'grad_w', 'grad_mlp_w_out': 'grad_w', 'grad_sc_w_in': 'grad_w', 'grad_sc_conv': 'grad_w', 'grad_sc_w_out': 'grad_w', 'grad_hg_w_in': 'grad_w', 'grad_hg_lower_bound': 'grad_w', 'grad_hg_norm': 'grad_w', 'grad_hg_w_out': 'grad_w', 'grad_s5_lam_re': 'grad_w', 'grad_s5_lam_im': 'grad_w', 'grad_s5_log_dt': 'grad_w', 'grad_s5_b_re': 'grad_w', 'grad_s5_b_im': 'grad_w', 'grad_s5_c_re': 'grad_w', 'grad_s5_c_im': 'grad_w', 'grad_s5_d': 'grad_w', 'grad_s5_w_glu': 'grad_w', 'grad_na_w_qkv': 'grad_w', 'grad_na_rpb': 'grad_w', 'grad_na_w_out': 'grad_w', 'delta_c_ctx': 'delta_w', 'delta_ada_w': 'delta_w', 'delta_ada_b': 'delta_w', 'delta_norm_gains': 'delta_w', 'delta_mlp_w_in': 'delta_w', 'delta_mlp_w_out': 'delta_w', 'delta_sc_w_in': 'delta_w', 'delta_sc_conv': 'delta_w', 'delta_sc_w_out': 'delta_w', 'delta_hg_w_in': 'delta_w', 'delta_hg_lower_bound': 'delta_w', 'delta_hg_norm': 'delta_w', 'delta_hg_w_out': 'delta_w', 'delta_s5_lam_re': 'delta_w', 'delta_s5_lam_im': 'delta_w', 'delta_s5_log_dt': 'delta_w', 'delta_s5_b_re': 'delta_w', 'delta_s5_b_im': 'delta_w', 'delta_s5_c_re': 'delta_w', 'delta_s5_c_im': 'delta_w', 'delta_s5_d': 'delta_w', 'delta_s5_w_glu': 'delta_w', 'delta_na_w_qkv': 'delta_w', 'delta_na_rpb': 'delta_w', 'delta_na_w_out': 'delta_w', 'new_m_c_ctx': 'new_m', 'new_m_ada_w': 'new_m', 'new_m_ada_b': 'new_m', 'new_m_norm_gains': 'new_m', 'new_m_mlp_w_in': 'new_m', 'new_m_mlp_w_out': 'new_m', 'new_m_sc_w_in': 'new_m', 'new_m_sc_conv': 'new_m', 'new_m_sc_w_out': 'new_m', 'new_m_hg_w_in': 'new_m', 'new_m_hg_lower_bound': 'new_m', 'new_m_hg_norm': 'new_m', 'new_m_hg_w_out': 'new_m', 'new_m_s5_lam_re': 'new_m', 'new_m_s5_lam_im': 'new_m', 'new_m_s5_log_dt': 'new_m', 'new_m_s5_b_re': 'new_m', 'new_m_s5_b_im': 'new_m', 'new_m_s5_c_re': 'new_m', 'new_m_s5_c_im': 'new_m', 'new_m_s5_d': 'new_m', 'new_m_s5_w_glu': 'new_m', 'new_m_na_w_qkv': 'new_m', 'new_m_na_rpb': 'new_m', 'new_m_na_w_out': 'new_m', 'new_v_c_ctx': 'new_v', 'new_v_ada_w': 'new_v', 'new_v_ada_b': 'new_v', 'new_v_norm_gains': 'new_v', 'new_v_mlp_w_in': 'new_v', 'new_v_mlp_w_out': 'new_v', 'new_v_sc_w_in': 'new_v', 'new_v_sc_conv': 'new_v', 'new_v_sc_w_out': 'new_v', 'new_v_hg_w_in': 'new_v', 'new_v_hg_lower_bound': 'new_v', 'new_v_hg_norm': 'new_v', 'new_v_hg_w_out': 'new_v', 'new_v_s5_lam_re': 'new_v', 'new_v_s5_lam_im': 'new_v', 'new_v_s5_log_dt': 'new_v', 'new_v_s5_b_re': 'new_v', 'new_v_s5_b_im': 'new_v', 'new_v_s5_c_re': 'new_v', 'new_v_s5_c_im': 'new_v', 'new_v_s5_d': 'new_v', 'new_v_s5_w_glu': 'new_v', 'new_v_na_w_qkv': 'new_v', 'new_v_na_rpb': 'new_v', 'new_v_na_w_out': 'new_v'}


def _forward(args):
    return _fwd_reference(*[args[k] for k in FWD_PARAMS])


def _output_shape():
    out = _jax.eval_shape(lambda: _forward(_fwd_setup_inputs(0)))
    return out.shape, out.dtype

N_MICROBATCH = 1
ADAM_LR = 0.001
ADAM_B1 = 0.9
ADAM_B2 = 0.999
ADAM_EPS = 1e-08
ADAM_WD = 0.01
ADAM_STEP = 10
PER_EXAMPLE_BATCH_AXIS = {'x': 0, 'c': 0, 'ctx': 0, 'loss_target': 0}
SHARED_INPUTS = []
_WEIGHT_DTYPES = {'c_ctx': _jnp.float32, 'ada_w': _jnp.float32, 'ada_b': _jnp.float32, 'norm_gains': _jnp.float32, 'mlp_w_in': _jnp.float32, 'mlp_w_out': _jnp.float32, 'sc_w_in': _jnp.float32, 'sc_conv': _jnp.float32, 'sc_w_out': _jnp.float32, 'hg_w_in': _jnp.float32, 'hg_lower_bound': _jnp.float32, 'hg_norm': _jnp.float32, 'hg_w_out': _jnp.float32, 's5_lam_re': _jnp.float32, 's5_lam_im': _jnp.float32, 's5_log_dt': _jnp.float32, 's5_b_re': _jnp.float32, 's5_b_im': _jnp.float32, 's5_c_re': _jnp.float32, 's5_c_im': _jnp.float32, 's5_d': _jnp.float32, 's5_w_glu': _jnp.float32, 'na_w_qkv': _jnp.float32, 'na_rpb': _jnp.float32, 'na_w_out': _jnp.float32}
MOMENT_SCALE = {'c_ctx': 2.720654e+00, 'ada_w': 4.720091e+00, 'ada_b': 8.856695e+00, 'norm_gains': 5.724974e+00, 'mlp_w_in': 4.508480e-01, 'mlp_w_out': 2.192687e+00, 'sc_w_in': 2.271687e-01, 'sc_conv': 2.277337e-01, 'sc_w_out': 2.452541e-01, 'hg_w_in': 2.037895e-01, 'hg_lower_bound': 4.247786e-02, 'hg_norm': 2.384475e-01, 'hg_w_out': 2.396813e-01, 's5_lam_re': 1.292056e-01, 's5_lam_im': 1.471096e-01, 's5_log_dt': 7.071513e+00, 's5_b_re': 1.345283e-01, 's5_b_im': 1.344367e-01, 's5_c_re': 2.051413e-01, 's5_c_im': 2.069810e-01, 's5_d': 3.040513e+00, 's5_w_glu': 2.100461e+00, 'na_w_qkv': 2.455419e+00, 'na_rpb': 2.945033e-02, 'na_w_out': 4.235416e+00}


def _to_microbatches(a, axis):
    t = _jnp.moveaxis(a, axis, 0)
    t = t.reshape((N_MICROBATCH, t.shape[0] // N_MICROBATCH) + t.shape[1:])
    return _jnp.moveaxis(t, 1, axis + 1)


def setup_inputs(seed: int = 0) -> dict:
    inp = _fwd_setup_inputs(seed)
    key = _jax.random.fold_in(_jax.random.key(seed), 7919)
    shape, _ = _output_shape()
    out = dict(inp)
    out["loss_target"] = _jax.random.normal(_jax.random.fold_in(key, 0), shape, _jnp.float32)
    for i, name in enumerate(TWIN_WEIGHTS):
        w = inp[name].astype(_jnp.float32)
        if MOMENT_SCALE is None:
            s = _jnp.sqrt(_jnp.mean(_jnp.square(w)) + 1e-30)
        else:
            s = MOMENT_SCALE[name]
        km, kv = _jax.random.split(_jax.random.fold_in(key, i + 1))
        out[name] = w
        out["m_" + name] = s * _jax.random.normal(km, w.shape, _jnp.float32)
        out["v_" + name] = (s * s) * _jax.random.uniform(kv, w.shape, _jnp.float32, 0.5, 1.5)
    if N_MICROBATCH > 1:
        for name, axis in PER_EXAMPLE_BATCH_AXIS.items():
            out[name] = _to_microbatches(out[name], axis)
    return {'x': out['x'], 'c': out['c'], 'ctx': out['ctx'], 'c_ctx': out['c_ctx'], 'ada_w': out['ada_w'], 'ada_b': out['ada_b'], 'norm_gains': out['norm_gains'], 'mlp_w_in': out['mlp_w_in'], 'mlp_w_out': out['mlp_w_out'], 'sc_w_in': out['sc_w_in'], 'sc_conv': out['sc_conv'], 'sc_w_out': out['sc_w_out'], 'hg_w_in': out['hg_w_in'], 'hg_lower_bound': out['hg_lower_bound'], 'hg_norm': out['hg_norm'], 'hg_w_out': out['hg_w_out'], 's5_lam_re': out['s5_lam_re'], 's5_lam_im': out['s5_lam_im'], 's5_log_dt': out['s5_log_dt'], 's5_b_re': out['s5_b_re'], 's5_b_im': out['s5_b_im'], 's5_c_re': out['s5_c_re'], 's5_c_im': out['s5_c_im'], 's5_d': out['s5_d'], 's5_w_glu': out['s5_w_glu'], 'na_w_qkv': out['na_w_qkv'], 'na_rpb': out['na_rpb'], 'na_w_out': out['na_w_out'], 'loss_target': out['loss_target'], 'm_c_ctx': out['m_c_ctx'], 'm_ada_w': out['m_ada_w'], 'm_ada_b': out['m_ada_b'], 'm_norm_gains': out['m_norm_gains'], 'm_mlp_w_in': out['m_mlp_w_in'], 'm_mlp_w_out': out['m_mlp_w_out'], 'm_sc_w_in': out['m_sc_w_in'], 'm_sc_conv': out['m_sc_conv'], 'm_sc_w_out': out['m_sc_w_out'], 'm_hg_w_in': out['m_hg_w_in'], 'm_hg_lower_bound': out['m_hg_lower_bound'], 'm_hg_norm': out['m_hg_norm'], 'm_hg_w_out': out['m_hg_w_out'], 'm_s5_lam_re': out['m_s5_lam_re'], 'm_s5_lam_im': out['m_s5_lam_im'], 'm_s5_log_dt': out['m_s5_log_dt'], 'm_s5_b_re': out['m_s5_b_re'], 'm_s5_b_im': out['m_s5_b_im'], 'm_s5_c_re': out['m_s5_c_re'], 'm_s5_c_im': out['m_s5_c_im'], 'm_s5_d': out['m_s5_d'], 'm_s5_w_glu': out['m_s5_w_glu'], 'm_na_w_qkv': out['m_na_w_qkv'], 'm_na_rpb': out['m_na_rpb'], 'm_na_w_out': out['m_na_w_out'], 'v_c_ctx': out['v_c_ctx'], 'v_ada_w': out['v_ada_w'], 'v_ada_b': out['v_ada_b'], 'v_norm_gains': out['v_norm_gains'], 'v_mlp_w_in': out['v_mlp_w_in'], 'v_mlp_w_out': out['v_mlp_w_out'], 'v_sc_w_in': out['v_sc_w_in'], 'v_sc_conv': out['v_sc_conv'], 'v_sc_w_out': out['v_sc_w_out'], 'v_hg_w_in': out['v_hg_w_in'], 'v_hg_lower_bound': out['v_hg_lower_bound'], 'v_hg_norm': out['v_hg_norm'], 'v_hg_w_out': out['v_hg_w_out'], 'v_s5_lam_re': out['v_s5_lam_re'], 'v_s5_lam_im': out['v_s5_lam_im'], 'v_s5_log_dt': out['v_s5_log_dt'], 'v_s5_b_re': out['v_s5_b_re'], 'v_s5_b_im': out['v_s5_b_im'], 'v_s5_c_re': out['v_s5_c_re'], 'v_s5_c_im': out['v_s5_c_im'], 'v_s5_d': out['v_s5_d'], 'v_s5_w_glu': out['v_s5_w_glu'], 'v_na_w_qkv': out['v_na_w_qkv'], 'v_na_rpb': out['v_na_rpb'], 'v_na_w_out': out['v_na_w_out']}


def _loss(weights, diff, rest, loss_target):
    with _jax.named_scope("forward"):
        args = {**rest, TWIN_DIFF_INPUT: diff, **{k: w.astype(_WEIGHT_DTYPES[k]) for k, w in weights.items()}}
        y = _forward(args)
    with _jax.named_scope("loss_head"):
        err = _jnp.square(y.astype(_jnp.float32) - loss_target)
        return 0.5 * _jnp.sum(_jnp.mean(err, axis=-1)) if err.ndim else 0.5 * err


def _adamw(w, g, m, v):
    m = ADAM_B1 * m + (1.0 - ADAM_B1) * g
    v = ADAM_B2 * v + (1.0 - ADAM_B2) * _jnp.square(g)
    m_hat = m / (1.0 - ADAM_B1 ** ADAM_STEP)
    v_hat = v / (1.0 - ADAM_B2 ** ADAM_STEP)
    delta = -ADAM_LR * (m_hat / (_jnp.sqrt(v_hat) + ADAM_EPS) + ADAM_WD * w)
    return delta, m, v


def reference(x, c, ctx, c_ctx, ada_w, ada_b, norm_gains, mlp_w_in, mlp_w_out, sc_w_in, sc_conv, sc_w_out, hg_w_in, hg_lower_bound, hg_norm, hg_w_out, s5_lam_re, s5_lam_im, s5_log_dt, s5_b_re, s5_b_im, s5_c_re, s5_c_im, s5_d, s5_w_glu, na_w_qkv, na_rpb, na_w_out, loss_target, m_c_ctx, m_ada_w, m_ada_b, m_norm_gains, m_mlp_w_in, m_mlp_w_out, m_sc_w_in, m_sc_conv, m_sc_w_out, m_hg_w_in, m_hg_lower_bound, m_hg_norm, m_hg_w_out, m_s5_lam_re, m_s5_lam_im, m_s5_log_dt, m_s5_b_re, m_s5_b_im, m_s5_c_re, m_s5_c_im, m_s5_d, m_s5_w_glu, m_na_w_qkv, m_na_rpb, m_na_w_out, v_c_ctx, v_ada_w, v_ada_b, v_norm_gains, v_mlp_w_in, v_mlp_w_out, v_sc_w_in, v_sc_conv, v_sc_w_out, v_hg_w_in, v_hg_lower_bound, v_hg_norm, v_hg_w_out, v_s5_lam_re, v_s5_lam_im, v_s5_log_dt, v_s5_b_re, v_s5_b_im, v_s5_c_re, v_s5_c_im, v_s5_d, v_s5_w_glu, v_na_w_qkv, v_na_rpb, v_na_w_out):
    given = dict(x=x, c=c, ctx=ctx, c_ctx=c_ctx, ada_w=ada_w, ada_b=ada_b, norm_gains=norm_gains, mlp_w_in=mlp_w_in, mlp_w_out=mlp_w_out, sc_w_in=sc_w_in, sc_conv=sc_conv, sc_w_out=sc_w_out, hg_w_in=hg_w_in, hg_lower_bound=hg_lower_bound, hg_norm=hg_norm, hg_w_out=hg_w_out, s5_lam_re=s5_lam_re, s5_lam_im=s5_lam_im, s5_log_dt=s5_log_dt, s5_b_re=s5_b_re, s5_b_im=s5_b_im, s5_c_re=s5_c_re, s5_c_im=s5_c_im, s5_d=s5_d, s5_w_glu=s5_w_glu, na_w_qkv=na_w_qkv, na_rpb=na_rpb, na_w_out=na_w_out, loss_target=loss_target, m_c_ctx=m_c_ctx, m_ada_w=m_ada_w, m_ada_b=m_ada_b, m_norm_gains=m_norm_gains, m_mlp_w_in=m_mlp_w_in, m_mlp_w_out=m_mlp_w_out, m_sc_w_in=m_sc_w_in, m_sc_conv=m_sc_conv, m_sc_w_out=m_sc_w_out, m_hg_w_in=m_hg_w_in, m_hg_lower_bound=m_hg_lower_bound, m_hg_norm=m_hg_norm, m_hg_w_out=m_hg_w_out, m_s5_lam_re=m_s5_lam_re, m_s5_lam_im=m_s5_lam_im, m_s5_log_dt=m_s5_log_dt, m_s5_b_re=m_s5_b_re, m_s5_b_im=m_s5_b_im, m_s5_c_re=m_s5_c_re, m_s5_c_im=m_s5_c_im, m_s5_d=m_s5_d, m_s5_w_glu=m_s5_w_glu, m_na_w_qkv=m_na_w_qkv, m_na_rpb=m_na_rpb, m_na_w_out=m_na_w_out, v_c_ctx=v_c_ctx, v_ada_w=v_ada_w, v_ada_b=v_ada_b, v_norm_gains=v_norm_gains, v_mlp_w_in=v_mlp_w_in, v_mlp_w_out=v_mlp_w_out, v_sc_w_in=v_sc_w_in, v_sc_conv=v_sc_conv, v_sc_w_out=v_sc_w_out, v_hg_w_in=v_hg_w_in, v_hg_lower_bound=v_hg_lower_bound, v_hg_norm=v_hg_norm, v_hg_w_out=v_hg_w_out, v_s5_lam_re=v_s5_lam_re, v_s5_lam_im=v_s5_lam_im, v_s5_log_dt=v_s5_log_dt, v_s5_b_re=v_s5_b_re, v_s5_b_im=v_s5_b_im, v_s5_c_re=v_s5_c_re, v_s5_c_im=v_s5_c_im, v_s5_d=v_s5_d, v_s5_w_glu=v_s5_w_glu, v_na_w_qkv=v_na_w_qkv, v_na_rpb=v_na_rpb, v_na_w_out=v_na_w_out)
    weights = {n: given[n] for n in TWIN_WEIGHTS}
    shared = {n: given[n] for n in SHARED_INPUTS}
    per_example = {n: given[n] for n in ['x', 'c', 'ctx']}
    grad_fn = _jax.value_and_grad(_loss, argnums=(0, 1))

    def one_microbatch(ex, loss_target):
        ex = dict(ex)
        diff = ex.pop(TWIN_DIFF_INPUT)
        return grad_fn(weights, diff, {**shared, **ex}, loss_target)

    if N_MICROBATCH == 1:
        loss, (grad_w, grad_x) = one_microbatch(per_example, given["loss_target"])
    else:
        def body(carry, xs):
            loss_sum, grad_sum = carry
            l_k, (gw_k, gx_k) = one_microbatch(xs[0], xs[1])
            with _jax.named_scope("update"):
                return (loss_sum + l_k, _jax.tree.map(_jnp.add, grad_sum, gw_k)), gx_k

        init = (_jnp.zeros((), _jnp.float32), _jax.tree.map(_jnp.zeros_like, weights))
        (loss, grad_w), grad_x = _jax.lax.scan(body, init, (per_example, given["loss_target"]))
    with _jax.named_scope("update"):
        delta_w, new_m, new_v = {}, {}, {}
        for n in TWIN_WEIGHTS:
            delta_w[n], new_m[n], new_v[n] = _adamw(weights[n], grad_w[n], given["m_" + n], given["v_" + n])
    return (loss, grad_x, *[grad_w[n] for n in TWIN_WEIGHTS], *[delta_w[n] for n in TWIN_WEIGHTS],
            *[new_m[n] for n in TWIN_WEIGHTS], *[new_v[n] for n in TWIN_WEIGHTS])
```

```python
import functools
import math

import jax
import jax.numpy as jnp
from jax import lax
from jax.experimental import pallas as pl
from jax.experimental.pallas import tpu as pltpu

F32 = jnp.float32
BF16 = jnp.bfloat16
MESH = pl.DeviceIdType.MESH
EPS = 1e-6
TB = 256
LANES = 1024
VMEM_LIMIT = 48 * 1024 * 1024
N_SHARD = 4
DEPTH = 4
N_MOD = 6
GRID_W = 64
HG_HEAD_DIM = 128
HG_CHUNK = 32
S5_GROUP = 16
NA_ROWS = 8
NA_COLS = 16
ADAM_LR, ADAM_B1, ADAM_B2, ADAM_EPS, ADAM_WD, ADAM_STEP = 0.001, 0.9, 0.999, 1e-08, 0.01, 10

BIG = ["mlp_w_in", "mlp_w_out", "sc_w_in", "sc_w_out", "hg_w_in", "hg_w_out", "s5_w_glu", "na_w_qkv", "na_w_out"]
COL_SHARDED = {"mlp_w_in", "sc_w_in", "hg_w_in", "s5_w_glu", "na_w_qkv"}
SMALL_SHARDED = ["norm_gains", "sc_conv", "hg_norm", "s5_d"]
SMALL_REPL = ["hg_lower_bound", "s5_lam_re", "s5_lam_im", "s5_log_dt", "s5_b_re", "s5_b_im", "s5_c_re", "s5_c_im", "na_rpb"]
WEIGHTS = ["c_ctx", "ada_w", "ada_b", "norm_gains", "mlp_w_in", "mlp_w_out", "sc_w_in", "sc_conv", "sc_w_out", "hg_w_in",
           "hg_lower_bound", "hg_norm", "hg_w_out", "s5_lam_re", "s5_lam_im", "s5_log_dt", "s5_b_re", "s5_b_im", "s5_c_re",
           "s5_c_im", "s5_d", "s5_w_glu", "na_w_qkv", "na_rpb", "na_w_out"]


def _cparams(sem=None):
    return pltpu.CompilerParams(dimension_semantics=sem, vmem_limit_bytes=VMEM_LIMIT)


def _rowwise(fn, out_dtypes, name, nseg):
    def seg_index(t):
        return jnp.minimum(t, nseg - 1)

    def in_specs(rows, colp, segp):
        rs = [pl.BlockSpec((None, TB, r.shape[-1]), lambda b, t: (b, t, 0)) for r in rows]
        cs = pl.BlockSpec(colp.shape, lambda b, t: (0, 0))
        ss = pl.BlockSpec((None, None) + segp.shape[2:], lambda b, t: (b, seg_index(t), 0, 0))
        return rs + [cs, ss]

    def load(refs, n, nc, ns):
        rows = [r[...].astype(F32) for r in refs[:n]]
        cols = [refs[n][k:k + 1, :] for k in range(nc)]
        segs = [refs[n + 1][k:k + 1, :] for k in range(ns)]
        return rows, cols, segs

    def out_blocks(rows, colp, segp):
        one = jax.ShapeDtypeStruct((1, colp.shape[-1]), F32)
        return jax.eval_shape(fn, [jax.ShapeDtypeStruct((TB, r.shape[-1]), F32) for r in rows],
                              [one] * colp.shape[0], [one] * segp.shape[2])

    def fwd_call(rows, colp, segp):
        bsz, tlen, _ = rows[0].shape
        n, nc, ns = len(rows), colp.shape[0], segp.shape[2]
        blk = out_blocks(rows, colp, segp)

        def body(*refs):
            vals = fn(*load(refs, n, nc, ns))
            for o, v in zip(refs[n + 2:], vals):
                o[...] = v.astype(o.dtype)

        return pl.pallas_call(
            body, name=name + "_fwd", grid=(bsz, tlen // TB), in_specs=in_specs(rows, colp, segp),
            out_specs=[pl.BlockSpec((None, TB, o.shape[-1]), lambda b, t: (b, t, 0)) for o in blk],
            out_shape=[jax.ShapeDtypeStruct((bsz, tlen, o.shape[-1]), dt) for o, dt in zip(blk, out_dtypes)],
            compiler_params=_cparams(("parallel", "parallel")),
        )(*rows, colp, segp)

    def bwd_call(rows, colp, segp, cts):
        bsz, tlen, _ = rows[0].shape
        n, nc, ns, m = len(rows), colp.shape[0], segp.shape[2], len(cts)

        def body(*refs):
            b, t = pl.program_id(0), pl.program_id(1)
            prim = load(refs, n, nc, ns)
            ct = tuple(r[...].astype(F32) for r in refs[n + 2:n + 2 + m])
            _, vjp = jax.vjp(fn, *prim)
            drows, dcols, dsegs = vjp(ct)
            outs = refs[n + 2 + m:]
            for o, v in zip(outs[:n], drows):
                o[...] = v.astype(o.dtype)
            dcol_ref, dseg_ref = outs[n], outs[n + 1]

            @pl.when((b == 0) & (t == 0))
            def _():
                dcol_ref[...] = jnp.zeros_like(dcol_ref)

            for k, v in enumerate(dcols):
                dcol_ref[k:k + 1, :] += v

            @pl.when(t < nseg)
            def _():
                for k, v in enumerate(dsegs):
                    dseg_ref[k:k + 1, :] = v

            @pl.when(t >= nseg)
            def _():
                for k, v in enumerate(dsegs):
                    dseg_ref[k:k + 1, :] += v

        row_specs = [pl.BlockSpec((None, TB, r.shape[-1]), lambda b, t: (b, t, 0)) for r in rows]
        ct_specs = [pl.BlockSpec((None, TB, c.shape[-1]), lambda b, t: (b, t, 0)) for c in cts]
        return pl.pallas_call(
            body, name=name + "_bwd", grid=(bsz, tlen // TB),
            in_specs=in_specs(rows, colp, segp) + ct_specs,
            out_specs=row_specs + [pl.BlockSpec(colp.shape, lambda b, t: (0, 0)),
                                   pl.BlockSpec((None, None) + segp.shape[2:], lambda b, t: (b, seg_index(t), 0, 0))],
            out_shape=[jax.ShapeDtypeStruct(r.shape, r.dtype) for r in rows]
            + [jax.ShapeDtypeStruct(colp.shape, F32), jax.ShapeDtypeStruct(segp.shape, F32)],
            compiler_params=_cparams(("arbitrary", "arbitrary")),
        )(*rows, colp, segp, *cts)

    @jax.custom_vjp
    def op(rows, colp, segp):
        return tuple(fwd_call(rows, colp, segp))

    def op_fwd(rows, colp, segp):
        return tuple(fwd_call(rows, colp, segp)), (rows, colp, segp)

    def op_bwd(res, cts):
        rows, colp, segp = res
        outs = bwd_call(rows, colp, segp, list(cts))
        return list(outs[:len(rows)]), outs[len(rows)], outs[len(rows) + 1]

    op.defvjp(op_fwd, op_bwd)
    return op


def _rms(x, g):
    return x * lax.rsqrt(jnp.mean(x * x, axis=-1, keepdims=True) + EPS) * g


def _fn_pre(rows, cols, segs):
    return (_rms(rows[0], cols[0]) * (1.0 + segs[1]) + segs[0],)


def _fn_post_pre(rows, cols, segs):
    h2 = rows[0] + segs[0] * _rms(rows[1], cols[0])
    return h2, _rms(h2, cols[1]) * (1.0 + segs[2]) + segs[1]


def _fn_post(rows, cols, segs):
    return (rows[0] + segs[0] * _rms(rows[1], cols[0]),)


def _col_tile(n, limit):
    for t in range(min(n, limit), 127, -128):
        if n % t == 0 and t % 128 == 0:
            return t
    return n


def _relu2(x):
    r = jnp.maximum(x, 0.0)
    return r * r


def _mm(x, wb, act, out_dtype, name):
    m, k = x.shape
    n = wb.shape[1]
    tm = 512 if m % 512 == 0 else m
    tn = _col_tile(n, 1024 if k <= 1024 else 512)

    def body(x_ref, w_ref, o_ref):
        xv = x_ref[...]
        if act:
            xv = _relu2(xv.astype(F32))
        o_ref[...] = jnp.dot(xv.astype(BF16), w_ref[...], preferred_element_type=F32).astype(o_ref.dtype)

    return pl.pallas_call(
        body, name=name, grid=(n // tn, m // tm),
        in_specs=[pl.BlockSpec((tm, k), lambda j, i: (i, 0)), pl.BlockSpec((k, tn), lambda j, i: (0, j))],
        out_specs=pl.BlockSpec((tm, tn), lambda j, i: (i, j)),
        out_shape=jax.ShapeDtypeStruct((m, n), out_dtype),
        compiler_params=_cparams(("parallel", "parallel")),
    )(x, wb)


def _mm_dx(dy, wb, x, act, name):
    m, n = dy.shape
    k = wb.shape[0]
    tm = 256 if m % 256 == 0 else m

    def body(dy_ref, w_ref, x_ref, o_ref):
        acc = lax.dot_general(dy_ref[...].astype(BF16), w_ref[...], (((1,), (1,)), ((), ())),
                              preferred_element_type=F32)
        if act:
            acc = acc * (2.0 * jnp.maximum(x_ref[...].astype(F32), 0.0))
        o_ref[...] = acc.astype(o_ref.dtype)

    return pl.pallas_call(
        body, name=name, grid=(m // tm,),
        in_specs=[pl.BlockSpec((tm, n), lambda i: (i, 0)), pl.BlockSpec((k, n), lambda i: (0, 0)),
                  pl.BlockSpec((tm, k), lambda i: (i, 0))],
        out_specs=pl.BlockSpec((tm, k), lambda i: (i, 0)),
        out_shape=jax.ShapeDtypeStruct((m, k), x.dtype),
        compiler_params=_cparams(("parallel",)),
    )(dy, wb, x)


def _mm_dw(x, dy, act, name):
    m, k = x.shape
    n = dy.shape[1]
    tm = 512 if m % 512 == 0 else m
    tk, tn = _col_tile(k, 1024), _col_tile(n, 1024)

    def body(x_ref, dy_ref, o_ref):
        @pl.when(pl.program_id(2) == 0)
        def _():
            o_ref[...] = jnp.zeros_like(o_ref)

        xv = x_ref[...]
        if act:
            xv = _relu2(xv.astype(F32))
        o_ref[...] += lax.dot_general(xv.astype(BF16), dy_ref[...].astype(BF16), (((0,), (0,)), ((), ())),
                                      preferred_element_type=F32)

    return pl.pallas_call(
        body, name=name, grid=(k // tk, n // tn, m // tm),
        in_specs=[pl.BlockSpec((tm, tk), lambda a, b, i: (i, a)), pl.BlockSpec((tm, tn), lambda a, b, i: (i, b))],
        out_specs=pl.BlockSpec((tk, tn), lambda a, b, i: (a, b)),
        out_shape=jax.ShapeDtypeStruct((k, n), F32),
        compiler_params=_cparams(("parallel", "parallel", "arbitrary")),
    )(x, dy)


def _linear(act, name, out_dtype=BF16):
    def run(x, wb):
        y = _mm(x.reshape(-1, x.shape[-1]), wb, act, out_dtype, name + "_fwd")
        return y.reshape(x.shape[:-1] + (wb.shape[1],))

    @jax.custom_vjp
    def lin(x, w):
        return run(x, w.astype(BF16))

    def lin_fwd(x, w):
        wb = w.astype(BF16)
        return run(x, wb), (x, wb)

    def lin_bwd(res, dy):
        x, wb = res
        x2, dy2 = x.reshape(-1, x.shape[-1]), dy.reshape(-1, dy.shape[-1])
        dx = _mm_dx(dy2, wb, x2, act, name + "_dx").reshape(x.shape)
        return dx, _mm_dw(x2, dy2, act, name + "_dw")

    lin.defvjp(lin_fwd, lin_bwd)
    return lin


def _loss_head(y, tgt):
    bsz, seq, d = y.shape

    def body(y_ref, t_ref, l_ref, d_ref):
        err = y_ref[...] - t_ref[...]
        d_ref[...] = err * (1.0 / d)
        l_ref[...] = jnp.full(l_ref.shape, 0.5 / d, F32) * jnp.sum(err * err)

    spec = pl.BlockSpec((None, TB, d), lambda b, t: (b, t, 0))
    lblk, dy = pl.pallas_call(
        body, name="loss_head", grid=(bsz, seq // TB), in_specs=[spec, spec],
        out_specs=[pl.BlockSpec((None, None, 8, 128), lambda b, t: (b, t, 0, 0)), spec],
        out_shape=[jax.ShapeDtypeStruct((bsz, seq // TB, 8, 128), F32), jax.ShapeDtypeStruct(y.shape, F32)],
        compiler_params=_cparams(("parallel", "parallel")),
    )(y, tgt)
    return lblk[:, :, 0, 0], dy


def _row_tile(rows, limit=512):
    for tr in range(min(rows, limit), 7, -1):
        if rows % tr == 0 and tr % 8 == 0:
            return tr
    return rows


def _adamw(w, g, m, v, name):
    shape = w.shape
    cols = shape[-1]
    w2, g2, m2, v2 = (a.reshape(-1, cols) for a in (w, g, m, v))
    rows = w2.shape[0]
    tr = _row_tile(rows, max(8, (1 << 19) // cols))
    c1, c2 = 1.0 - ADAM_B1 ** ADAM_STEP, 1.0 - ADAM_B2 ** ADAM_STEP

    def body(w_ref, g_ref, m_ref, v_ref, d_ref, mo_ref, vo_ref):
        gv = g_ref[...]
        mn = ADAM_B1 * m_ref[...] + (1.0 - ADAM_B1) * gv
        vn = ADAM_B2 * v_ref[...] + (1.0 - ADAM_B2) * (gv * gv)
        d_ref[...] = -ADAM_LR * ((mn / c1) / (jnp.sqrt(vn / c2) + ADAM_EPS) + ADAM_WD * w_ref[...])
        mo_ref[...] = mn
        vo_ref[...] = vn

    spec = pl.BlockSpec((tr, cols), lambda i: (i, 0))
    outs = pl.pallas_call(
        body, name=name, grid=(rows // tr,), in_specs=[spec] * 4, out_specs=[spec] * 3,
        out_shape=[jax.ShapeDtypeStruct((rows, cols), F32)] * 3, compiler_params=_cparams(("parallel",)),
    )(w2, g2, m2, v2)
    return tuple(o.reshape(shape) for o in outs)


def _sum_leading(a, name):
    n, rows, cols = a.shape
    tr = _row_tile(rows, max(8, (1 << 18) // cols))

    def body(a_ref, o_ref):
        acc = a_ref[0]
        for j in range(1, n):
            acc = acc + a_ref[j]
        o_ref[...] = acc

    return pl.pallas_call(
        body, name=name, grid=(rows // tr,), in_specs=[pl.BlockSpec((n, tr, cols), lambda i: (0, i, 0))],
        out_specs=pl.BlockSpec((tr, cols), lambda i: (i, 0)), out_shape=jax.ShapeDtypeStruct((rows, cols), F32),
        compiler_params=_cparams(("parallel",)),
    )(a)


def _pack_rows(arrs):
    flat = [a.reshape(-1).astype(F32) for a in arrs]
    flat = [jnp.pad(f, (0, (-f.shape[0]) % LANES)) for f in flat]
    return jnp.concatenate(flat).reshape(-1, LANES)


def _unpack_rows(buf, shapes):
    out, r = [], 0
    for s in shapes:
        n = math.prod(s)
        nr = -(-n // LANES)
        out.append(buf[r:r + nr].reshape(-1)[:n].reshape(s))
        r += nr
    return out


def _pad_rows(buf, mult=8):
    return jnp.pad(buf, ((0, (-buf.shape[0]) % mult), (0, 0)))


def _ada_fwd(s, w):
    nl, d, n = w.shape
    r = s.shape[0]

    def body(s_ref, w_ref, o_ref):
        o_ref[...] = jnp.dot(s_ref[...], w_ref[...].astype(BF16), preferred_element_type=F32)

    return pl.pallas_call(
        body, name="ada_fwd", grid=(nl,),
        in_specs=[pl.BlockSpec((r, d), lambda i: (0, 0)), pl.BlockSpec((None, d, n), lambda i: (i, 0, 0))],
        out_specs=pl.BlockSpec((None, r, n), lambda i: (i, 0, 0)), out_shape=jax.ShapeDtypeStruct((nl, r, n), F32),
        compiler_params=_cparams(("parallel",)),
    )(s, w)


def _ada_dw(s, dm):
    nl, r, n = dm.shape
    d = s.shape[1]

    def body(s_ref, dm_ref, o_ref):
        o_ref[...] = lax.dot_general(s_ref[...], dm_ref[...].astype(BF16), (((0,), (0,)), ((), ())),
                                     preferred_element_type=F32)

    return pl.pallas_call(
        body, name="ada_dw", grid=(nl,),
        in_specs=[pl.BlockSpec((r, d), lambda i: (0, 0)), pl.BlockSpec((None, r, n), lambda i: (i, 0, 0))],
        out_specs=pl.BlockSpec((None, d, n), lambda i: (i, 0, 0)), out_shape=jax.ShapeDtypeStruct((nl, d, n), F32),
        compiler_params=_cparams(("parallel",)),
    )(s, dm)


def _ada_ds(dm, w):
    nl, r, n = dm.shape
    d = w.shape[1]

    def body(dm_ref, w_ref, o_ref):
        @pl.when(pl.program_id(0) == 0)
        def _():
            o_ref[...] = jnp.zeros_like(o_ref)

        o_ref[...] += lax.dot_general(dm_ref[...].astype(BF16), w_ref[...].astype(BF16), (((1,), (1,)), ((), ())),
                                      preferred_element_type=F32)

    return pl.pallas_call(
        body, name="ada_ds", grid=(nl,),
        in_specs=[pl.BlockSpec((None, r, n), lambda i: (i, 0, 0)), pl.BlockSpec((None, d, n), lambda i: (i, 0, 0))],
        out_specs=pl.BlockSpec((r, d), lambda i: (0, 0)), out_shape=jax.ShapeDtypeStruct((r, d), F32),
        compiler_params=_cparams(("arbitrary",)),
    )(dm, w)


_ANY = pl.BlockSpec(memory_space=pl.ANY)


def _position():
    return lax.axis_index("x"), lax.axis_index("y"), lax.axis_index("c")


def _all_gather8(block, name):
    m, n = block.shape

    def body(x_ref, out_ref, send_sems, recv_sems, local_sem):
        x, y, c = _position()
        me, sibling = (x, y, c), (x, y, 1 - c)
        chips = [(1 - x, y), (x, 1 - y), (1 - x, 1 - y)]

        def slot(px, py, pc):
            return out_ref.at[4 * px + 2 * py + pc]

        def copy(k, blk, to, src=None):
            return pltpu.make_async_remote_copy(
                src_ref=slot(*blk) if src is None else src, dst_ref=slot(*blk), send_sem=send_sems.at[k],
                recv_sem=recv_sems.at[k], device_id=to, device_id_type=MESH)

        mine = pltpu.make_async_copy(x_ref, slot(*me), local_sem)
        mine.start()
        first = [copy(0, me, sibling, src=x_ref)]
        first += [copy(1 + j, me, (*chip, c), src=x_ref) for j, chip in enumerate(chips)]
        for cp in first:
            cp.start()
        passed = [copy(4 + j, (*chip, c), sibling) for j, chip in enumerate(chips)]
        for j, chip in enumerate(chips):
            copy(1 + j, (*chip, c), me).wait_recv()
            passed[j].start()
        copy(0, sibling, me).wait_recv()
        for j, chip in enumerate(chips):
            copy(4 + j, (*chip, 1 - c), me).wait_recv()
        for cp in first + passed:
            cp.wait_send()
        mine.wait()

    return pl.pallas_call(
        body, name=name, out_shape=jax.ShapeDtypeStruct((8, m, n), block.dtype), in_specs=[_ANY], out_specs=_ANY,
        scratch_shapes=[pltpu.SemaphoreType.DMA((7,)), pltpu.SemaphoreType.DMA((7,)), pltpu.SemaphoreType.DMA],
    )(block)


def _gather_shards(shard, name):
    rows, cols = shard.shape
    half = rows // 2

    def body(x_ref, out_ref, send_sems, recv_sems, local_sem):
        x, y, c = _position()
        sibling = (x, y, 1 - c)
        chips = [(1 - x, y), (x, 1 - y), (1 - x, 1 - y)]

        def part(px, py, pc):
            return out_ref.at[2 * px + py, pl.ds(pc * half, half), :]

        def copy(k, blk, to, src=None):
            return pltpu.make_async_remote_copy(
                src_ref=part(*blk) if src is None else src, dst_ref=part(*blk), send_sem=send_sems.at[k],
                recv_sem=recv_sems.at[k], device_id=to, device_id_type=MESH)

        mine = pltpu.make_async_copy(x_ref, out_ref.at[2 * x + y], local_sem)
        mine.start()
        my_half = x_ref.at[pl.ds(c * half, half), :]
        first = [copy(j, (x, y, c), (*chip, c), src=my_half) for j, chip in enumerate(chips)]
        for cp in first:
            cp.start()
        passed = [copy(3 + j, (*chip, c), sibling) for j, chip in enumerate(chips)]
        for j, chip in enumerate(chips):
            copy(j, (*chip, c), sibling).wait_recv()
            passed[j].start()
        for j, chip in enumerate(chips):
            copy(3 + j, (*chip, 1 - c), sibling).wait_recv()
        for cp in first + passed:
            cp.wait_send()
        mine.wait()

    return pl.pallas_call(
        body, name=name, out_shape=jax.ShapeDtypeStruct((N_SHARD, rows, cols), shard.dtype), in_specs=[_ANY],
        out_specs=_ANY,
        scratch_shapes=[pltpu.SemaphoreType.DMA((6,)), pltpu.SemaphoreType.DMA((6,)), pltpu.SemaphoreType.DMA],
    )(shard)


def _swap_other_half(g, name):
    ns, rows, cols = g.shape
    half = rows // 2

    def body(g_ref, out_ref, send_sem, recv_sem):
        x, y, c = _position()
        cp = pltpu.make_async_remote_copy(
            src_ref=g_ref.at[:, pl.ds((1 - c) * half, half), :], dst_ref=out_ref, send_sem=send_sem,
            recv_sem=recv_sem, device_id=(x, y, 1 - c), device_id_type=MESH)
        cp.start()
        cp.wait()

    return pl.pallas_call(
        body, name=name, out_shape=jax.ShapeDtypeStruct((ns, half, cols), g.dtype), in_specs=[_ANY], out_specs=_ANY,
        scratch_shapes=[pltpu.SemaphoreType.DMA, pltpu.SemaphoreType.DMA],
    )(g)


def _add_own_half(g, r, c_idx, name):
    ns, rows, cols = g.shape
    half = rows // 2
    tr = _row_tile(half, max(8, (1 << 19) // cols))
    nb = half // tr

    def body(c_ref, g_ref, r_ref, o_ref):
        o_ref[...] = g_ref[...] + r_ref[...]

    return pl.pallas_call(
        body, name=name,
        grid_spec=pltpu.PrefetchScalarGridSpec(
            num_scalar_prefetch=1, grid=(ns, nb),
            in_specs=[pl.BlockSpec((None, tr, cols), lambda s, i, c_ref: (s, c_ref[0] * nb + i, 0)),
                      pl.BlockSpec((None, tr, cols), lambda s, i, c_ref: (s, i, 0))],
            out_specs=pl.BlockSpec((None, tr, cols), lambda s, i, c_ref: (s, i, 0))),
        out_shape=jax.ShapeDtypeStruct((ns, half, cols), F32), compiler_params=_cparams(("parallel", "parallel")),
    )(c_idx, g, r)


def _send_to_chips(a, name):
    ns, half, cols = a.shape

    def body(a_ref, out_ref, send_sems, recv_sems):
        x, y, c = _position()
        chips = [(1 - x, y), (x, 1 - y), (1 - x, 1 - y)]
        cps = [pltpu.make_async_remote_copy(
            src_ref=a_ref.at[2 * px + py], dst_ref=out_ref.at[j], send_sem=send_sems.at[j], recv_sem=recv_sems.at[j],
            device_id=(px, py, c), device_id_type=MESH) for j, (px, py) in enumerate(chips)]
        for cp in cps:
            cp.start()
        for cp in cps:
            cp.wait()

    return pl.pallas_call(
        body, name=name, out_shape=jax.ShapeDtypeStruct((3, half, cols), a.dtype), in_specs=[_ANY], out_specs=_ANY,
        scratch_shapes=[pltpu.SemaphoreType.DMA((3,)), pltpu.SemaphoreType.DMA((3,))],
    )(a)


def _add_arrivals(a, r, s_idx, name):
    ns, half, cols = a.shape
    tr = _row_tile(half, max(8, (1 << 18) // cols))

    def body(s_ref, a_ref, r_ref, o_ref):
        o_ref[...] = ((a_ref[...] + r_ref[0]) + r_ref[1]) + r_ref[2]

    return pl.pallas_call(
        body, name=name,
        grid_spec=pltpu.PrefetchScalarGridSpec(
            num_scalar_prefetch=1, grid=(half // tr,),
            in_specs=[pl.BlockSpec((None, tr, cols), lambda i, s_ref: (s_ref[0], i, 0)),
                      pl.BlockSpec((3, tr, cols), lambda i, s_ref: (0, i, 0))],
            out_specs=pl.BlockSpec((tr, cols), lambda i, s_ref: (i, 0))),
        out_shape=jax.ShapeDtypeStruct((half, cols), F32), compiler_params=_cparams(("parallel",)),
    )(s_idx, a, r)


def _join_halves(f, name):
    half, cols = f.shape

    def body(f_ref, out_ref, send_sem, recv_sem, local_sem):
        x, y, c = _position()
        mine = pltpu.make_async_copy(f_ref, out_ref.at[pl.ds(c * half, half), :], local_sem)
        mine.start()
        cp = pltpu.make_async_remote_copy(
            src_ref=f_ref, dst_ref=out_ref.at[pl.ds(c * half, half), :], send_sem=send_sem, recv_sem=recv_sem,
            device_id=(x, y, 1 - c), device_id_type=MESH)
        cp.start()
        cp.wait()
        mine.wait()

    return pl.pallas_call(
        body, name=name, out_shape=jax.ShapeDtypeStruct((2 * half, cols), f.dtype), in_specs=[_ANY], out_specs=_ANY,
        scratch_shapes=[pltpu.SemaphoreType.DMA, pltpu.SemaphoreType.DMA, pltpu.SemaphoreType.DMA],
    )(f)


def _conv_core(z, conv_w, nctx):
    d = conv_w.shape[-1]
    zf = z.astype(F32)
    b_gate, c_gate, v = zf[..., :d], zf[..., d:2 * d], zf[..., 2 * d:]
    u = c_gate * v

    def conv(s):
        p = jnp.pad(s, ((0, 0), (1, 1), (0, 0)))
        return conv_w[0] * p[:, :-2] + conv_w[1] * p[:, 1:-1] + conv_w[2] * p[:, 2:]

    return b_gate * jnp.concatenate([conv(u[:, :nctx]), conv(u[:, nctx:])], axis=1)


def _hg_chunk(q, v, tf, lb, st, rev):
    n = HG_CHUNK
    f = lb + (1.0 - lb) * jax.nn.sigmoid(tf)
    kk = 1.0 - f
    lf = jnp.log(f)
    row = lax.broadcasted_iota(jnp.int32, (n, n), 0)
    col = lax.broadcasted_iota(jnp.int32, (n, n), 1)
    tri = (col >= row) if rev else (col <= row)
    ones = tri.astype(BF16)
    h1 = lf.astype(BF16)
    r1 = lf - h1.astype(F32)
    h2 = r1.astype(BF16)
    h3 = (r1 - h2.astype(F32)).astype(BF16)
    b = (jnp.dot(ones, h1, preferred_element_type=F32) + jnp.dot(ones, h2, preferred_element_type=F32)
         + jnp.dot(ones, h3, preferred_element_type=F32))
    mid = n - n // 2 if rev else n // 2 - 1
    last = 0 if rev else n - 1
    b_mid, b_last = b[mid:mid + 1], b[last:last + 1]
    qs = (q * jnp.exp(b - b_mid)).astype(BF16)
    ks = (kk * jnp.exp(b_mid - b)).astype(BF16)
    sc = lax.dot_general(qs, ks, (((1,), (1,)), ((), ())), preferred_element_type=F32)
    sc = jnp.where(tri, sc, 0.0).astype(BF16)
    vb = v.astype(BF16)
    o = jnp.dot(sc, vb, preferred_element_type=F32) + lax.dot_general(
        (q * jnp.exp(b)).astype(BF16), st.astype(BF16), (((1,), (1,)), ((), ())), preferred_element_type=F32)
    ks2 = (kk * jnp.exp(b_last - b)).astype(BF16)
    st_new = st * jnp.exp(b_last) + lax.dot_general(vb, ks2, (((0,), (0,)), ((), ())), preferred_element_type=F32)
    return o, st_new


def _hg_scan(nctx, rev):
    hd, n = HG_HEAD_DIM, HG_CHUNK

    def geometry(z):
        bsz, tlen, d5 = z.shape
        return bsz, tlen, d5 // 5, (d5 // 5) // hd, tlen // n, nctx // n

    def chunk_of(j, nch, ncc):
        if not rev:
            return j
        return jnp.where(j < ncc, ncc - 1 - j, nch - 1 + ncc - j)

    def rows(c):
        return pl.ds(pl.multiple_of(c * n, n), n)

    def in_specs(z):
        bsz, tlen, d, nh, nch, ncc = geometry(z)
        fcol = 4 if rev else 3
        col = lambda k: pl.BlockSpec((None, tlen, hd), lambda b, h: (b, 0, k * nh + h))
        return [col(0), col(1), col(fcol), pl.BlockSpec((1, hd), lambda b, h: (0, h))]

    def fwd_call(z, lb):
        bsz, tlen, d, nh, nch, ncc = geometry(z)

        def body(q_ref, v_ref, f_ref, lb_ref, o_ref):
            lbv = lb_ref[...]

            def step(j, st):
                sl = rows(chunk_of(j, nch, ncc))
                o, st = _hg_chunk(q_ref[sl, :].astype(F32), v_ref[sl, :].astype(F32), f_ref[sl, :].astype(F32),
                                  lbv, st, rev)
                o_ref[sl, :] = o
                return st

            lax.fori_loop(0, nch, step, jnp.zeros((hd, hd), F32))

        return pl.pallas_call(
            body, name="hg_scan_rev" if rev else "hg_scan_fwd", grid=(bsz, nh), in_specs=in_specs(z),
            out_specs=pl.BlockSpec((None, tlen, hd), lambda b, h: (b, 0, h)),
            out_shape=jax.ShapeDtypeStruct((bsz, tlen, d), F32), compiler_params=_cparams(("parallel", "parallel")),
        )(z, z, z, lb)

    def bwd_call(z, lb, do):
        bsz, tlen, d, nh, nch, ncc = geometry(z)

        def body(q_ref, v_ref, f_ref, lb_ref, do_ref, dq_ref, dv_ref, df_ref, dlb_ref, st_ref):
            lbv = lb_ref[...]

            def load(sl):
                return q_ref[sl, :].astype(F32), v_ref[sl, :].astype(F32), f_ref[sl, :].astype(F32)

            def fstep(j, st):
                st_ref[j] = st
                return _hg_chunk(*load(rows(chunk_of(j, nch, ncc))), lbv, st, rev)[1]

            lax.fori_loop(0, nch, fstep, jnp.zeros((hd, hd), F32))

            def bstep(i, carry):
                dst, dlb = carry
                j = nch - 1 - i
                sl = rows(chunk_of(j, nch, ncc))
                _, vjp = jax.vjp(functools.partial(_hg_chunk, rev=rev), *load(sl), lbv, st_ref[j])
                dq, dv, df, dlb_c, dst = vjp((do_ref[sl, :], dst))
                dq_ref[sl, :] = dq.astype(dq_ref.dtype)
                dv_ref[sl, :] = dv.astype(dv_ref.dtype)
                df_ref[sl, :] = df.astype(df_ref.dtype)
                return dst, dlb + dlb_c

            _, dlb = lax.fori_loop(0, nch, bstep, (jnp.zeros((hd, hd), F32), jnp.zeros((1, hd), F32)))
            dlb_ref[...] = dlb

        head = pl.BlockSpec((None, tlen, hd), lambda b, h: (b, 0, h))
        return pl.pallas_call(
            body, name="hg_scan_rev_bwd" if rev else "hg_scan_fwd_bwd", grid=(bsz, nh),
            in_specs=in_specs(z) + [head],
            out_specs=[head, head, head, pl.BlockSpec((None, 1, hd), lambda b, h: (b, 0, h))],
            out_shape=[jax.ShapeDtypeStruct((bsz, tlen, d), BF16)] * 3 + [jax.ShapeDtypeStruct((bsz, 1, d), F32)],
            scratch_shapes=[pltpu.VMEM((nch, hd, hd), F32)],
            compiler_params=_cparams(("parallel", "parallel")),
        )(z, z, z, lb, do)

    @jax.custom_vjp
    def scan(z, lb):
        return fwd_call(z, lb)

    def scan_fwd(z, lb):
        return fwd_call(z, lb), (z, lb)

    def scan_bwd(res, do):
        z, lb = res
        dq, dv, df, dlb = bwd_call(z, lb, do)
        zero = jnp.zeros_like(dq)
        dz = jnp.concatenate([dq, dv, zero, zero, df] if rev else [dq, dv, zero, df, zero], axis=-1)
        return dz, jnp.sum(dlb, axis=0)

    scan.defvjp(scan_fwd, scan_bwd)
    return scan


def _hg_readout(o_f, o_b, gate, g_norm):
    bsz, tlen, d = gate.shape
    nh = d // HG_HEAD_DIM
    rb = TB * nh

    def fn(of, ob, g, gt):
        x = of + ob
        return x * lax.rsqrt(jnp.mean(x * x, axis=-1, keepdims=True) + EPS) * gt * (g * jax.nn.sigmoid(g))

    spec = pl.BlockSpec((None, rb, HG_HEAD_DIM), lambda b, t: (b, t, 0))
    gspec = pl.BlockSpec((rb, HG_HEAD_DIM), lambda b, t: (0, 0))
    grid = (bsz, tlen // TB)

    def fwd_call(of, ob, g, gt):
        def body(of_ref, ob_ref, g_ref, gt_ref, y_ref):
            y_ref[...] = fn(of_ref[...], ob_ref[...], g_ref[...].astype(F32), gt_ref[...]).astype(y_ref.dtype)

        return pl.pallas_call(
            body, name="hg_readout_fwd", grid=grid, in_specs=[spec, spec, spec, gspec], out_specs=spec,
            out_shape=jax.ShapeDtypeStruct(g.shape, BF16), compiler_params=_cparams(("parallel", "parallel")),
        )(of, ob, g, gt)

    def bwd_call(of, ob, g, gt, dy):
        def body(of_ref, ob_ref, g_ref, gt_ref, dy_ref, do_ref, dg_ref, dgt_ref):
            _, vjp = jax.vjp(fn, of_ref[...], ob_ref[...], g_ref[...].astype(F32), gt_ref[...])
            dof, _, dg, dgt = vjp(dy_ref[...].astype(F32))
            do_ref[...] = dof
            dg_ref[...] = dg.astype(dg_ref.dtype)

            @pl.when((pl.program_id(0) == 0) & (pl.program_id(1) == 0))
            def _():
                dgt_ref[...] = jnp.zeros_like(dgt_ref)

            dgt_ref[...] += dgt

        return pl.pallas_call(
            body, name="hg_readout_bwd", grid=grid, in_specs=[spec, spec, spec, gspec, spec],
            out_specs=[spec, spec, gspec],
            out_shape=[jax.ShapeDtypeStruct(of.shape, F32), jax.ShapeDtypeStruct(g.shape, BF16),
                       jax.ShapeDtypeStruct(gt.shape, F32)],
            compiler_params=_cparams(("arbitrary", "arbitrary")),
        )(of, ob, g, gt, dy)

    @jax.custom_vjp
    def op(of, ob, g, gt):
        return fwd_call(of, ob, g, gt)

    def op_fwd(of, ob, g, gt):
        return fwd_call(of, ob, g, gt), (of, ob, g, gt)

    def op_bwd(res, dy):
        do, dg, dgt = bwd_call(*res, dy)
        return do, do, dg, dgt

    op.defvjp(op_fwd, op_bwd)
    heads = lambda t: t.reshape(bsz, tlen * nh, HG_HEAD_DIM)
    gt = jnp.tile(g_norm.reshape(nh, HG_HEAD_DIM), (TB, 1))
    return op(heads(o_f), heads(o_b), heads(gate), gt).reshape(bsz, tlen, d)


def _hgrn_core(z, lower_bound, g_norm, nctx):
    d = g_norm.shape[-1]
    lb = lower_bound.reshape(1, d)
    o_f = _hg_scan(nctx, False)(z, lb)
    o_b = _hg_scan(nctx, True)(z, lb)
    return _hg_readout(o_f, o_b, z[..., 2 * d:3 * d], g_norm)


S5_LC = 16


def _s5_mats(lam_re, lam_im, log_dt, b_re, b_im, c_re, c_im):
    hi = lax.Precision.HIGHEST
    ng, ns = lam_re.shape
    lc, gs = S5_LC, S5_GROUP
    lam_re = jnp.minimum(lam_re, -1e-4)
    dt = jnp.exp(log_dt)[:, None]
    k = jnp.arange(lc + 1, dtype=F32)[:, None, None]
    mag, ang = jnp.exp(lam_re * dt * k), lam_im * dt * k
    p_re, p_im = mag * jnp.cos(ang), mag * jnp.sin(ang)
    a_re, a_im = p_re[1], p_im[1]
    den = lam_re * lam_re + lam_im * lam_im
    f_re = ((a_re - 1) * lam_re + a_im * lam_im) / den
    f_im = (a_im * lam_re - (a_re - 1) * lam_im) / den
    bb_re = f_re[..., None] * b_re - f_im[..., None] * b_im
    bb_im = f_re[..., None] * b_im + f_im[..., None] * b_re
    w_re = c_re[None] * p_re[:, :, None, :] - c_im[None] * p_im[:, :, None, :]
    w_im = c_re[None] * p_im[:, :, None, :] + c_im[None] * p_re[:, :, None, :]
    kk = (jnp.einsum('kgcn,gnd->kgcd', w_re[:lc], bb_re, precision=hi)
          - jnp.einsum('kgcn,gnd->kgcd', w_im[:lc], bb_im, precision=hi))
    t = jnp.arange(lc)
    lag = t[:, None] - t[None, :]
    m = jnp.where((lag >= 0)[:, :, None, None, None], kk[jnp.maximum(lag, 0)], 0.0)
    mt = m.transpose(2, 1, 4, 0, 3).reshape(ng, lc * gs, lc * gs)
    pw_re, pw_im = p_re[lc - 1 - t], p_im[lc - 1 - t]
    pr = pw_re[..., None] * bb_re[None] - pw_im[..., None] * bb_im[None]
    pi = pw_re[..., None] * bb_im[None] + pw_im[..., None] * bb_re[None]
    pt = jnp.concatenate([pr, pi], axis=2).transpose(1, 0, 3, 2).reshape(ng, lc * gs, 2 * ns)
    q = jnp.concatenate([w_re[1:], -w_im[1:]], axis=-1)
    qt = q.transpose(1, 3, 0, 2).reshape(ng, 2 * ns, lc * gs)
    a16 = jnp.concatenate([p_re[lc], p_im[lc]], axis=-1)
    return mt, pt, qt, a16


def _s5_bmm(lhs, rhs, dims, out_dtype, name, extra=None):
    nd, ng = lhs.shape[:2]
    ops = [lhs, rhs] + (list(extra[:2]) if extra else [])
    dlist = [dims] + ([extra[2]] if extra else [])

    def out_dim(a, b, dn):
        (ca,), (cb,) = dn
        return a.shape[2 + 1 - ca], b.shape[2 + 1 - cb]

    om, on = out_dim(lhs, rhs, dims)

    def body(*refs):
        acc = None
        for j, dn in enumerate(dlist):
            a, b = refs[2 * j][...].astype(BF16), refs[2 * j + 1][...].astype(BF16)
            r = lax.dot_general(a, b, (dn, ((), ())), preferred_element_type=F32)
            acc = r if acc is None else acc + r
        refs[-1][...] = acc.astype(refs[-1].dtype)

    return pl.pallas_call(
        body, name=name, grid=(nd, ng),
        in_specs=[pl.BlockSpec((None, None) + o.shape[2:], lambda d, g: (d, g, 0, 0)) for o in ops],
        out_specs=pl.BlockSpec((None, None, om, on), lambda d, g: (d, g, 0, 0)),
        out_shape=jax.ShapeDtypeStruct((nd, ng, om, on), out_dtype),
        compiler_params=_cparams(("parallel", "parallel")),
    )(*ops)


def _s5_row_block(rows):
    return 32 if rows % 32 == 0 else rows


def _s5_scan_fwd(z, a1, a2, name):
    nd, nc, rows, lanes = z.shape
    rb = _s5_row_block(rows)

    def body(z_ref, a1_ref, a2_ref, x_ref):
        a1v, a2v = a1_ref[...], a2_ref[...]

        def step(c, x):
            x_ref[c] = x
            return a1v * x + a2v * pltpu.roll(x, lanes // 2, axis=1) + z_ref[c]

        lax.fori_loop(0, nc, step, jnp.zeros((rb, lanes), F32))

    blk = pl.BlockSpec((None, nc, rb, lanes), lambda d, r: (d, 0, r, 0))
    par = pl.BlockSpec((None, rb, lanes), lambda d, r: (d, r, 0))
    return pl.pallas_call(
        body, name=name, grid=(nd, rows // rb), in_specs=[blk, par, par], out_specs=blk,
        out_shape=jax.ShapeDtypeStruct(z.shape, F32), compiler_params=_cparams(("parallel", "parallel")),
    )(z, a1, a2)


def _s5_scan_bwd(dxp, xp, a1, a2b, name):
    nd, nc, rows, lanes = dxp.shape
    rb = _s5_row_block(rows)

    def body(dxp_ref, xp_ref, a1_ref, a2_ref, dz_ref, p1_ref, p2_ref):
        a1v, a2v = a1_ref[...], a2_ref[...]
        zero = jnp.zeros((rb, lanes), F32)

        def step(j, carry):
            g_next, nxt, p1, p2 = carry
            c = nc - 1 - j
            g = nxt + a1v * g_next + a2v * pltpu.roll(g_next, lanes // 2, axis=1)
            dz_ref[c] = g
            x = xp_ref[c]
            return g, dxp_ref[c], p1 + x * g, p2 + pltpu.roll(x, lanes // 2, axis=1) * g

        _, _, p1, p2 = lax.fori_loop(0, nc, step, (zero, zero, zero, zero))
        p1_ref[...] = p1
        p2_ref[...] = p2

    blk = pl.BlockSpec((None, nc, rb, lanes), lambda d, r: (d, 0, r, 0))
    par = pl.BlockSpec((None, rb, lanes), lambda d, r: (d, r, 0))
    return pl.pallas_call(
        body, name=name, grid=(nd, rows // rb), in_specs=[blk, blk, par, par], out_specs=[blk, par, par],
        out_shape=[jax.ShapeDtypeStruct(dxp.shape, F32), jax.ShapeDtypeStruct((nd, rows, lanes), F32),
                   jax.ShapeDtypeStruct((nd, rows, lanes), F32)],
        compiler_params=_cparams(("parallel", "parallel")),
    )(dxp, xp, a1, a2b)


def _s5_rows(t, bsz):
    nd, ng, m, k = t.shape
    return t.reshape(nd, ng, bsz, m // bsz, k).transpose(0, 3, 2, 1, 4).reshape(nd, m // bsz, bsz * ng, k)


def _s5_groups(t, bsz):
    nd, nc, rows, k = t.shape
    return t.reshape(nd, nc, bsz, rows // bsz, k).transpose(0, 3, 2, 1, 4).reshape(nd, rows // bsz, bsz * nc, k)


def _s5_coeffs(a16, bsz):
    half = a16.shape[-1] // 2
    re, im = a16[..., :half], a16[..., half:]
    tile = lambda v: jnp.tile(v, (1, bsz, 1))
    return tile(jnp.concatenate([re, re], -1)), tile(jnp.concatenate([-im, im], -1)), tile(jnp.concatenate([im, -im], -1))


def _s5_apply(bsz):
    def run(u, mt, pt, qt, a16):
        a1, a2, _ = _s5_coeffs(a16, bsz)
        z = _s5_bmm(u, pt, ((1,), (0,)), F32, "s5_z")
        xp = _s5_scan_fwd(_s5_rows(z, bsz), a1, a2, "s5_scan")
        xg = _s5_groups(xp, bsz).astype(BF16)
        y = _s5_bmm(u, mt, ((1,), (0,)), F32, "s5_y", extra=(xg, qt, ((1,), (0,))))
        return y, (xp, xg)

    @jax.custom_vjp
    def apply(u, mt, pt, qt, a16):
        return run(u, mt.astype(BF16), pt.astype(BF16), qt.astype(BF16), a16)[0]

    def apply_fwd(u, mt, pt, qt, a16):
        mtb, ptb, qtb = mt.astype(BF16), pt.astype(BF16), qt.astype(BF16)
        y, (xp, xg) = run(u, mtb, ptb, qtb, a16)
        return y, (u, mtb, ptb, qtb, a16, xp, xg)

    def apply_bwd(res, dy):
        u, mtb, ptb, qtb, a16, xp, xg = res
        a1, _, a2b = _s5_coeffs(a16, bsz)
        dyb = dy.astype(BF16)
        dmt = _s5_bmm(u, dyb, ((0,), (0,)), F32, "s5_dmt")
        dqt = _s5_bmm(xg, dyb, ((0,), (0,)), F32, "s5_dqt")
        dxp = _s5_bmm(dyb, qtb, ((1,), (1,)), F32, "s5_dxp")
        dz, p1, p2 = _s5_scan_bwd(_s5_rows(dxp, bsz), xp, a1, a2b, "s5_scan_bwd")
        dzg = _s5_groups(dz, bsz).astype(BF16)
        dpt = _s5_bmm(u, dzg, ((0,), (0,)), F32, "s5_dpt")
        du = _s5_bmm(dyb, mtb, ((1,), (1,)), BF16, "s5_du", extra=(dzg, ptb, ((1,), (1,))))
        half = a16.shape[-1] // 2
        p1 = jnp.sum(p1.reshape(2, bsz, -1, 2 * half), axis=1)
        p2 = jnp.sum(p2.reshape(2, bsz, -1, 2 * half), axis=1)
        da16 = jnp.concatenate([p1[..., :half] + p1[..., half:], p2[..., half:] - p2[..., :half]], axis=-1)
        return du, dmt, dpt, dqt, da16

    apply.defvjp(apply_fwd, apply_bwd)
    return apply


def _s5_core(a, p, nctx):
    bsz, tlen, d = a.shape
    ng, lc, gs = d // S5_GROUP, S5_LC, S5_GROUP
    mats = [_s5_mats(p["s5_lam_re"][k], p["s5_lam_im"][k], p["s5_log_dt"][k], p["s5_b_re"], p["s5_b_im"],
                     p["s5_c_re"][k], p["s5_c_im"][k]) for k in range(2)]
    mt, pt, qt, a16 = (jnp.stack([mats[0][j], mats[1][j]]) for j in range(4))

    def seg_flip(t):
        return jnp.concatenate([t[:, :nctx][:, ::-1], t[:, nctx:][:, ::-1]], axis=1)

    def chunks(t):
        return t.reshape(bsz, tlen // lc, lc, ng, gs).transpose(3, 0, 1, 2, 4).reshape(ng, bsz * (tlen // lc), lc * gs)

    def unchunk(t):
        return t.reshape(ng, bsz, tlen // lc, lc, gs).transpose(1, 2, 3, 0, 4).reshape(bsz, tlen, d)

    u = jnp.stack([chunks(a), chunks(seg_flip(a))])
    y = _s5_apply(bsz)(u, mt, pt, qt, a16)
    out = p["s5_d"] * a.astype(F32) + unchunk(y[0]) + seg_flip(unchunk(y[1]))
    return jax.nn.gelu(out)


NA_LANES = 128
NA_MASKED = -1e30


def _na_tables(rpb):
    hi = lax.Precision.HIGHEST
    nh = rpb.shape[0]
    q = jnp.arange(GRID_W)
    kc = jnp.arange(GRID_W)
    q_start = jnp.clip(q - NA_COLS // 2, 0, GRID_W - NA_COLS)
    inwin = (kc[None, :] >= q_start[:, None]) & (kc[None, :] < q_start[:, None] + NA_COLS)
    dc = kc[None, :] - q[:, None] + NA_COLS - 1
    onehot = ((dc[:, :, None] == jnp.arange(2 * NA_COLS - 1)) & inwin[:, :, None]).astype(F32)
    a = jnp.arange(NA_ROWS)[None, :] - jnp.arange(NA_ROWS)[:, None] + NA_ROWS - 1
    tab = jnp.einsum('hskc,qlc->hsqkl', rpb[:, a, :], onehot, precision=hi)
    tab = jnp.where(inwin[None, None, :, None, :], tab, NA_MASKED)
    return tab.reshape(nh, NA_ROWS, GRID_W, NA_ROWS * GRID_W)


def _na_math(q2, kw, vw, kc, vc, bias, dh):
    scale = dh ** -0.5
    lane_head = lax.broadcasted_iota(jnp.int32, (1, NA_LANES), 1) // dh
    nt = (((1,), (1,)), ((), ()))
    kwb, vwb, kcb, vcb = (t.astype(BF16) for t in (kw, vw, kc, vc))
    out = jnp.zeros(q2.shape, F32)
    for j in range(NA_LANES // dh):
        mine = lane_head == j
        qh = jnp.where(mine, q2, 0.0).astype(BF16)
        s_loc = lax.dot_general(qh, kwb, nt, preferred_element_type=F32) * scale + bias[j]
        s_ctx = lax.dot_general(qh, kcb, nt, preferred_element_type=F32) * scale
        m = jnp.maximum(jnp.max(s_loc, axis=-1, keepdims=True), jnp.max(s_ctx, axis=-1, keepdims=True))
        m = lax.stop_gradient(m)
        p_loc, p_ctx = jnp.exp(s_loc - m), jnp.exp(s_ctx - m)
        den = jnp.sum(p_loc, axis=-1, keepdims=True) + jnp.sum(p_ctx, axis=-1, keepdims=True)
        inv = 1.0 / den
        o = (jnp.dot((p_loc * inv).astype(BF16), vwb, preferred_element_type=F32)
             + jnp.dot((p_ctx * inv).astype(BF16), vcb, preferred_element_type=F32))
        out = out + jnp.where(mine, o, 0.0)
    return out


def _na_attention(nctx, nh):
    def geometry(z):
        bsz, tlen, d3 = z.shape
        d = d3 // 3
        rows = (tlen - nctx) // GRID_W
        return bsz, tlen, d, rows, d // nh, d // NA_LANES

    def key_row0(r, rows):
        return jnp.clip(r - NA_ROWS // 2, 0, rows - NA_ROWS)

    def specs(z):
        bsz, tlen, d, rows, dh, nlb = geometry(z)
        hpb = NA_LANES // dh
        qs = pl.BlockSpec((None, GRID_W, NA_LANES), lambda b, h, r: (b, nctx // GRID_W + r, h))
        ks = pl.BlockSpec((None, tlen, NA_LANES), lambda b, h, r: (b, 0, nlb + h))
        vs = pl.BlockSpec((None, tlen, NA_LANES), lambda b, h, r: (b, 0, 2 * nlb + h))
        bs = pl.BlockSpec((hpb, None, GRID_W, NA_ROWS * GRID_W), lambda b, h, r: (h, r - key_row0(r, rows), 0, 0))
        os_ = pl.BlockSpec((None, GRID_W, NA_LANES), lambda b, h, r: (b, r, h))
        return qs, ks, vs, bs, os_

    def window(r, rows):
        return pl.ds(pl.multiple_of(nctx + key_row0(r, rows) * GRID_W, GRID_W), NA_ROWS * GRID_W)

    def fwd_call(z, bias):
        bsz, tlen, d, rows, dh, nlb = geometry(z)
        qs, ks, vs, bs, os_ = specs(z)

        def body(q_ref, k_ref, v_ref, b_ref, o_ref):
            win = window(pl.program_id(2), rows)
            o_ref[...] = _na_math(q_ref[...].astype(F32), k_ref[win, :], v_ref[win, :], k_ref[0:nctx, :],
                                  v_ref[0:nctx, :], b_ref[...], dh).astype(o_ref.dtype)

        return pl.pallas_call(
            body, name="na_fwd", grid=(bsz, nlb, rows), in_specs=[qs, ks, vs, bs], out_specs=os_,
            out_shape=jax.ShapeDtypeStruct((bsz, tlen - nctx, d), BF16),
            compiler_params=_cparams(("parallel", "parallel", "arbitrary")),
        )(z, z, z, bias)

    def bwd_call(z, bias, do):
        bsz, tlen, d, rows, dh, nlb = geometry(z)
        hpb = NA_LANES // dh
        qs, ks, vs, bs, os_ = specs(z)

        def body(q_ref, k_ref, v_ref, b_ref, do_ref, dq_ref, dk_ref, dv_ref, db_ref):
            r = pl.program_id(2)
            win = window(r, rows)

            @pl.when(r == 0)
            def _():
                dk_ref[...] = jnp.zeros_like(dk_ref)
                dv_ref[...] = jnp.zeros_like(dv_ref)

            prim = (q_ref[...].astype(F32), k_ref[win, :].astype(F32), v_ref[win, :].astype(F32),
                    k_ref[0:nctx, :].astype(F32), v_ref[0:nctx, :].astype(F32), b_ref[...])
            _, vjp = jax.vjp(functools.partial(_na_math, dh=dh), *prim)
            dq, dkw, dvw, dkc, dvc, db = vjp(do_ref[...].astype(F32))
            dq_ref[...] = dq.astype(dq_ref.dtype)
            dk_ref[win, :] += dkw
            dv_ref[win, :] += dvw
            dk_ref[0:nctx, :] += dkc
            dv_ref[0:nctx, :] += dvc
            prev = jnp.maximum(r - 1, 0)
            first = (r == 0) | ((r - key_row0(r, rows)) != (prev - key_row0(prev, rows)))

            @pl.when(first)
            def _():
                db_ref[...] = db

            @pl.when(jnp.logical_not(first))
            def _():
                db_ref[...] += db

        acc = pl.BlockSpec((None, tlen, NA_LANES), lambda b, h, r: (b, 0, h))
        dbs = pl.BlockSpec((None, hpb, None, GRID_W, NA_ROWS * GRID_W),
                           lambda b, h, r: (b, h, r - key_row0(r, rows), 0, 0))
        return pl.pallas_call(
            body, name="na_bwd", grid=(bsz, nlb, rows), in_specs=[qs, ks, vs, bs, os_],
            out_specs=[os_, acc, acc, dbs],
            out_shape=[jax.ShapeDtypeStruct((bsz, tlen - nctx, d), BF16), jax.ShapeDtypeStruct((bsz, tlen, d), F32),
                       jax.ShapeDtypeStruct((bsz, tlen, d), F32), jax.ShapeDtypeStruct((bsz,) + bias.shape, F32)],
            compiler_params=_cparams(("parallel", "parallel", "arbitrary")),
        )(z, z, z, bias, do)

    @jax.custom_vjp
    def attend(z, bias):
        return fwd_call(z, bias)

    def attend_fwd(z, bias):
        return fwd_call(z, bias), (z, bias)

    def attend_bwd(res, do):
        z, bias = res
        dq, dk, dv, db = bwd_call(z, bias, do)
        dq = jnp.pad(dq, ((0, 0), (nctx, 0), (0, 0)))
        dz = jnp.concatenate([dq, dk.astype(BF16), dv.astype(BF16)], axis=-1)
        return dz, _sum_leading(db.reshape(db.shape[0], -1, NA_ROWS * GRID_W), "na_sum_dbias").reshape(bias.shape)

    attend.defvjp(attend_fwd, attend_bwd)
    return attend


def _na_core(z, rpb, nctx):
    o = _na_attention(nctx, rpb.shape[0])(z, _na_tables(rpb))
    return jnp.pad(o, ((0, 0), (nctx, 0), (0, 0)))


def _forward(x, ctx, mod, p):
    nctx = ctx.shape[1]
    h = jnp.concatenate([ctx, x], axis=1)
    lb_all = jnp.cumsum(jax.nn.softmax(p["hg_lower_bound"], axis=0), axis=0)
    lb_all = lb_all - lb_all[0]
    gains = p["norm_gains"]
    pre = _rowwise(_fn_pre, [BF16], "pre", 2)
    (a,) = pre([h], gains[0, 0:1], mod[0][:, :, 0:2])
    for i in range(DEPTH):
        tag = f"l{i}"
        if i == 0:
            z = _linear(False, tag + "_sc_in")(a, p["sc_w_in"][0])
            yc = _conv_core(z, p["sc_conv"][0], nctx)
            y = _linear(False, tag + "_sc_out")(yc.astype(BF16), p["sc_w_out"][0])
        elif i == 1:
            z = _linear(False, tag + "_hg_in")(a, p["hg_w_in"][0])
            yc = _hgrn_core(z, lb_all[i], p["hg_norm"][0], nctx)
            y = _linear(False, tag + "_hg_out")(yc, p["hg_w_out"][0])
        elif i == 2:
            sp = {k: v[0] for k, v in p.items() if k.startswith("s5_") and k != "s5_w_glu"}
            gz = _s5_core(a, sp, nctx)
            vg = _linear(False, tag + "_s5_glu")(gz.astype(BF16), p["s5_w_glu"][0]).astype(F32)
            d = gz.shape[-1]
            y = (vg[..., :d] * jax.nn.sigmoid(vg[..., d:])).astype(BF16)
        else:
            z = _linear(False, tag + "_na_qkv")(a, p["na_w_qkv"][0])
            yc = _na_core(z, p["na_rpb"][0], nctx)
            y = _linear(False, tag + "_na_out")(yc, p["na_w_out"][0])
        h, a2 = _rowwise(_fn_post_pre, [F32, BF16], tag + "_mix_post", 2)(
            [h, y], gains[i, 1:3], mod[i][:, :, 2:5])
        u = _linear(False, tag + "_mlp_in")(a2, p["mlp_w_in"][i])
        f = _linear(True, tag + "_mlp_out")(u, p["mlp_w_out"][i])
        if i + 1 < DEPTH:
            cols = jnp.stack([gains[i, 3], gains[i + 1, 0]])
            segs = jnp.concatenate([mod[i][:, :, 5:6], mod[i + 1][:, :, 0:2]], axis=2)
            h, a = _rowwise(_fn_post_pre, [F32, BF16], tag + "_mlp_post", 2)([h, f], cols, segs)
        else:
            (h,) = _rowwise(_fn_post, [F32], tag + "_mlp_post", 2)([h, f], gains[i, 3:4], mod[i][:, :, 5:6])
    return h[:, nctx:]


def _local_step(x, ctx, tgt, mod, p):
    y, vjp = jax.vjp(lambda x_, mod_, p_: _forward(x_, ctx, mod_, p_), x, mod, p)
    lblk, dy = _loss_head(y, tgt)
    gx, dmod, gp = vjp(dy)
    return lblk, gx, dmod, gp


PACK_COLS = 512


def _pack_shard(ws):
    return jnp.concatenate([ws[n].reshape(-1, PACK_COLS) for n in BIG], axis=0)


def _unpack_full(buf, shard_shapes):
    out, r = {}, 0
    for n in BIG:
        s = shard_shapes[n]
        nr = math.prod(s) // PACK_COLS
        parts = buf[:, r:r + nr].reshape((N_SHARD,) + s)
        axis = 1 if n in COL_SHARDED else 0
        out[n] = [jnp.concatenate([parts[k, l] for k in range(N_SHARD)], axis=axis) for l in range(s[0])]
        r += nr
    return out


def _pack_grads(gp, shard_shapes):
    per = []
    for k in range(N_SHARD):
        rows = []
        for n in BIG:
            s = shard_shapes[n]
            axis = 1 if n in COL_SHARDED else 0
            width = s[1 + axis]
            for g in gp[n]:
                rows.append(lax.slice_in_dim(g, k * width, (k + 1) * width, axis=axis).reshape(-1, PACK_COLS))
        per.append(jnp.concatenate(rows, axis=0))
    return jnp.stack(per)


def _unpack_shard(buf, shard_shapes):
    out, r = {}, 0
    for n in BIG:
        s = shard_shapes[n]
        nr = math.prod(s) // PACK_COLS
        out[n] = buf[r:r + nr].reshape(s)
        r += nr
    return out


def _shard_cols(a, k, width):
    return lax.dynamic_slice_in_dim(a, k * width, width, axis=a.ndim - 1)


def kernel(x, c, ctx, c_ctx, ada_w, ada_b, norm_gains, mlp_w_in, mlp_w_out, sc_w_in, sc_conv, sc_w_out, hg_w_in, hg_lower_bound, hg_norm, hg_w_out, s5_lam_re, s5_lam_im, s5_log_dt, s5_b_re, s5_b_im, s5_c_re, s5_c_im, s5_d, s5_w_glu, na_w_qkv, na_rpb, na_w_out, loss_target, m_c_ctx, m_ada_w, m_ada_b, m_norm_gains, m_mlp_w_in, m_mlp_w_out, m_sc_w_in, m_sc_conv, m_sc_w_out, m_hg_w_in, m_hg_lower_bound, m_hg_norm, m_hg_w_out, m_s5_lam_re, m_s5_lam_im, m_s5_log_dt, m_s5_b_re, m_s5_b_im, m_s5_c_re, m_s5_c_im, m_s5_d, m_s5_w_glu, m_na_w_qkv, m_na_rpb, m_na_w_out, v_c_ctx, v_ada_w, v_ada_b, v_norm_gains, v_mlp_w_in, v_mlp_w_out, v_sc_w_in, v_sc_conv, v_sc_w_out, v_hg_w_in, v_hg_lower_bound, v_hg_norm, v_hg_w_out, v_s5_lam_re, v_s5_lam_im, v_s5_log_dt, v_s5_b_re, v_s5_b_im, v_s5_c_re, v_s5_c_im, v_s5_d, v_s5_w_glu, v_na_w_qkv, v_na_rpb, v_na_w_out):
    w = dict(c_ctx=c_ctx, ada_w=ada_w, ada_b=ada_b, norm_gains=norm_gains, mlp_w_in=mlp_w_in, mlp_w_out=mlp_w_out,
             sc_w_in=sc_w_in, sc_conv=sc_conv, sc_w_out=sc_w_out, hg_w_in=hg_w_in, hg_lower_bound=hg_lower_bound,
             hg_norm=hg_norm, hg_w_out=hg_w_out, s5_lam_re=s5_lam_re, s5_lam_im=s5_lam_im, s5_log_dt=s5_log_dt,
             s5_b_re=s5_b_re, s5_b_im=s5_b_im, s5_c_re=s5_c_re, s5_c_im=s5_c_im, s5_d=s5_d, s5_w_glu=s5_w_glu,
             na_w_qkv=na_w_qkv, na_rpb=na_rpb, na_w_out=na_w_out)
    mom_m = dict(zip(WEIGHTS, [m_c_ctx, m_ada_w, m_ada_b, m_norm_gains, m_mlp_w_in, m_mlp_w_out, m_sc_w_in, m_sc_conv,
                               m_sc_w_out, m_hg_w_in, m_hg_lower_bound, m_hg_norm, m_hg_w_out, m_s5_lam_re, m_s5_lam_im,
                               m_s5_log_dt, m_s5_b_re, m_s5_b_im, m_s5_c_re, m_s5_c_im, m_s5_d, m_s5_w_glu, m_na_w_qkv,
                               m_na_rpb, m_na_w_out]))
    mom_v = dict(zip(WEIGHTS, [v_c_ctx, v_ada_w, v_ada_b, v_norm_gains, v_mlp_w_in, v_mlp_w_out, v_sc_w_in, v_sc_conv,
                               v_sc_w_out, v_hg_w_in, v_hg_lower_bound, v_hg_norm, v_hg_w_out, v_s5_lam_re, v_s5_lam_im,
                               v_s5_log_dt, v_s5_b_re, v_s5_b_im, v_s5_c_re, v_s5_c_im, v_s5_d, v_s5_w_glu, v_na_w_qkv,
                               v_na_rpb, v_na_w_out]))
    bsz, _, d = x.shape
    ax, ay, ac = lax.axis_index("x"), lax.axis_index("y"), lax.axis_index("c")
    chip = 2 * ax + ay
    dev = 2 * chip + ac
    n_dev = 2 * N_SHARD
    dsh = d // N_SHARD

    shard_shapes = {n: w[n].shape for n in BIG}
    packed = _gather_shards(_pack_shard({n: w[n].astype(BF16) for n in BIG}), "gather_weights")
    full = {n: [a.astype(F32) for a in v] for n, v in _unpack_full(packed, shard_shapes).items()}

    small_shapes = [c.shape] + [w[n].shape for n in SMALL_SHARDED]
    buf_a = _all_gather8(_pad_rows(_pack_rows([c] + [w[n] for n in SMALL_SHARDED])), "gather_small")
    per_dev = [_unpack_rows(buf_a[k], small_shapes) for k in range(n_dev)]
    c_all = jnp.concatenate([per_dev[k][0] for k in range(n_dev)], axis=0)
    for j, n in enumerate(SMALL_SHARDED):
        full[n] = jnp.concatenate([per_dev[2 * s][1 + j] for s in range(N_SHARD)], axis=-1)
    for n in SMALL_REPL:
        full[n] = w[n]

    n_all = c_all.shape[0]
    s_rows = 32
    cond = jnp.concatenate([c_all, c_ctx[None]], axis=0)
    s_all = jnp.pad(jax.nn.silu(cond), ((0, s_rows - n_all - 1), (0, 0))).astype(BF16)
    mod_part = _ada_fwd(s_all, ada_w)
    nsh = mod_part.shape[-1]
    buf_b = _all_gather8(_pad_rows(mod_part.reshape(-1, LANES)), "gather_mod")
    nrow_b = mod_part.size // LANES
    mod_raw = jnp.concatenate([buf_b[2 * s, :nrow_b].reshape(mod_part.shape) for s in range(N_SHARD)], axis=-1)
    mod_raw = mod_raw + ada_b[:, None, :]
    mod_lat = lax.dynamic_slice_in_dim(mod_raw, dev * bsz, bsz, axis=1).reshape(DEPTH, bsz, 1, N_MOD, d)
    mod_ctx = jnp.broadcast_to(mod_raw[:, n_all].reshape(DEPTH, 1, 1, N_MOD, d), (DEPTH, bsz, 1, N_MOD, d))
    mod = jnp.concatenate([mod_ctx, mod_lat], axis=2)

    lblk, grad_x, dmod, gp = _local_step(x, ctx, loss_target, mod, full)

    dmod_rows = jnp.concatenate([dmod[:, :, 1].reshape(DEPTH, bsz, N_MOD * d),
                                 jnp.sum(dmod[:, :, 0], axis=1).reshape(DEPTH, 1, N_MOD * d)], axis=1)
    c_list = [dmod_rows] + [gp[n] for n in SMALL_SHARDED + SMALL_REPL] + [jnp.sum(lblk).reshape(1)]
    c_shapes = [a.shape for a in c_list]
    buf_c = _all_gather8(_pad_rows(_pack_rows(c_list)), "gather_grads")
    sum_c = _sum_leading(buf_c, "sum_grads")
    summed = _unpack_rows(sum_c, c_shapes)
    grads = {}
    for j, n in enumerate(SMALL_SHARDED):
        grads[n] = _shard_cols(summed[1 + j], chip, w[n].shape[-1])
    for j, n in enumerate(SMALL_REPL):
        grads[n] = summed[1 + len(SMALL_SHARDED) + j]
    loss = summed[-1][0]

    dmod_dev = [_unpack_rows(buf_c[k], c_shapes[:1])[0] for k in range(n_dev)]
    dm_lat = jnp.concatenate([t[:, :bsz] for t in dmod_dev], axis=1)
    dm_ctx = summed[0][:, bsz:bsz + 1]
    dm_all = jnp.concatenate([dm_lat, dm_ctx], axis=1)
    grads["ada_b"] = _sum_leading(jnp.moveaxis(dm_all, 1, 0).reshape(n_all + 1, -1, LANES), "sum_ada_b").reshape(ada_b.shape)
    dm_sh = jnp.pad(_shard_cols(dm_all, chip, nsh), ((0, 0), (0, s_rows - n_all - 1), (0, 0)))
    grads["ada_w"] = _ada_dw(s_all, dm_sh)
    ds_part = _ada_ds(dm_sh, ada_w)
    buf_d = _all_gather8(_pad_rows(ds_part[n_all:n_all + 1]), "gather_dcond")
    ds_ctx = _sum_leading(jnp.stack([buf_d[2 * s] for s in range(N_SHARD)]), "sum_dcond")[0]
    grads["c_ctx"] = jax.vjp(jax.nn.silu, c_ctx)[1](ds_ctx)[0]

    g_pack = _pack_grads(gp, shard_shapes)
    c_idx = jnp.reshape(ac, (1,)).astype(jnp.int32)
    s_idx = jnp.reshape(chip, (1,)).astype(jnp.int32)
    part = _add_own_half(g_pack, _swap_other_half(g_pack, "rs_swap_half"), c_idx, "rs_add_sibling")
    mine = _add_arrivals(part, _send_to_chips(part, "rs_send_chips"), s_idx, "rs_add_chips")
    grads.update(_unpack_shard(_join_halves(mine, "rs_join_halves"), shard_shapes))

    delta, new_m, new_v = {}, {}, {}
    for n in BIG + ["ada_w"]:
        delta[n], new_m[n], new_v[n] = _adamw(w[n], grads[n], mom_m[n], mom_v[n], "adamw_" + n)
    small = [n for n in WEIGHTS if n not in BIG and n != "ada_w"]
    shapes = [w[n].shape for n in small]
    packs = [_pad_rows(_pack_rows([src[n] for n in small])) for src in (w, grads, mom_m, mom_v)]
    outs = _adamw(*packs, "adamw_small")
    for tgt, buf in zip((delta, new_m, new_v), outs):
        for n, a in zip(small, _unpack_rows(buf, shapes)):
            tgt[n] = a
    return (loss, grad_x, *[grads[n] for n in WEIGHTS], *[delta[n] for n in WEIGHTS],
            *[new_m[n] for n in WEIGHTS], *[new_v[n] for n in WEIGHTS])
```

```python
import functools
import math

import jax
import jax.numpy as jnp
from jax import lax
from jax.experimental import pallas as pl
from jax.experimental.pallas import tpu as pltpu

F32 = jnp.float32
BF16 = jnp.bfloat16
MESH = pl.DeviceIdType.MESH
EPS = 1e-6
TB = 256
LANES = 1024
VMEM_LIMIT = 48 * 1024 * 1024
N_SHARD = 4
DEPTH = 4
N_MOD = 6
GRID_W = 64
HG_HEAD_DIM = 128
HG_CHUNK = 32
S5_GROUP = 16
NA_ROWS = 8
NA_COLS = 16
ADAM_LR, ADAM_B1, ADAM_B2, ADAM_EPS, ADAM_WD, ADAM_STEP = 0.001, 0.9, 0.999, 1e-08, 0.01, 10

BIG = ["mlp_w_in", "mlp_w_out", "sc_w_in", "sc_w_out", "hg_w_in", "hg_w_out", "s5_w_glu", "na_w_qkv", "na_w_out"]
COL_SHARDED = {"mlp_w_in", "sc_w_in", "hg_w_in", "s5_w_glu", "na_w_qkv"}
SMALL_SHARDED = ["norm_gains", "sc_conv", "hg_norm", "s5_d"]
SMALL_REPL = ["hg_lower_bound", "s5_lam_re", "s5_lam_im", "s5_log_dt", "s5_b_re", "s5_b_im", "s5_c_re", "s5_c_im", "na_rpb"]
WEIGHTS = ["c_ctx", "ada_w", "ada_b", "norm_gains", "mlp_w_in", "mlp_w_out", "sc_w_in", "sc_conv", "sc_w_out", "hg_w_in",
           "hg_lower_bound", "hg_norm", "hg_w_out", "s5_lam_re", "s5_lam_im", "s5_log_dt", "s5_b_re", "s5_b_im", "s5_c_re",
           "s5_c_im", "s5_d", "s5_w_glu", "na_w_qkv", "na_rpb", "na_w_out"]


def _cparams(sem=None):
    return pltpu.CompilerParams(dimension_semantics=sem, vmem_limit_bytes=VMEM_LIMIT)


def _rowwise(fn, out_dtypes, name, nseg):
    def seg_index(t):
        return jnp.minimum(t, nseg - 1)

    def in_specs(rows, colp, segp):
        rs = [pl.BlockSpec((None, TB, r.shape[-1]), lambda b, t: (b, t, 0)) for r in rows]
        cs = pl.BlockSpec(colp.shape, lambda b, t: (0, 0))
        ss = pl.BlockSpec((None, None) + segp.shape[2:], lambda b, t: (b, seg_index(t), 0, 0))
        return rs + [cs, ss]

    def load(refs, n, nc, ns):
        rows = [r[...].astype(F32) for r in refs[:n]]
        cols = [refs[n][k:k + 1, :] for k in range(nc)]
        segs = [refs[n + 1][k:k + 1, :] for k in range(ns)]
        return rows, cols, segs

    def out_blocks(rows, colp, segp):
        one = jax.ShapeDtypeStruct((1, colp.shape[-1]), F32)
        return jax.eval_shape(fn, [jax.ShapeDtypeStruct((TB, r.shape[-1]), F32) for r in rows],
                              [one] * colp.shape[0], [one] * segp.shape[2])

    def fwd_call(rows, colp, segp):
        bsz, tlen, _ = rows[0].shape
        n, nc, ns = len(rows), colp.shape[0], segp.shape[2]
        blk = out_blocks(rows, colp, segp)

        def body(*refs):
            vals = fn(*load(refs, n, nc, ns))
            for o, v in zip(refs[n + 2:], vals):
                o[...] = v.astype(o.dtype)

        return pl.pallas_call(
            body, name=name + "_fwd", grid=(bsz, tlen // TB), in_specs=in_specs(rows, colp, segp),
            out_specs=[pl.BlockSpec((None, TB, o.shape[-1]), lambda b, t: (b, t, 0)) for o in blk],
            out_shape=[jax.ShapeDtypeStruct((bsz, tlen, o.shape[-1]), dt) for o, dt in zip(blk, out_dtypes)],
            compiler_params=_cparams(("parallel", "parallel")),
        )(*rows, colp, segp)

    def bwd_call(rows, colp, segp, cts):
        bsz, tlen, _ = rows[0].shape
        n, nc, ns, m = len(rows), colp.shape[0], segp.shape[2], len(cts)

        def body(*refs):
            b, t = pl.program_id(0), pl.program_id(1)
            prim = load(refs, n, nc, ns)
            ct = tuple(r[...].astype(F32) for r in refs[n + 2:n + 2 + m])
            _, vjp = jax.vjp(fn, *prim)
            drows, dcols, dsegs = vjp(ct)
            outs = refs[n + 2 + m:]
            for o, v in zip(outs[:n], drows):
                o[...] = v.astype(o.dtype)
            dcol_ref, dseg_ref = outs[n], outs[n + 1]

            @pl.when((b == 0) & (t == 0))
            def _():
                dcol_ref[...] = jnp.zeros_like(dcol_ref)

            for k, v in enumerate(dcols):
                dcol_ref[k:k + 1, :] += v

            @pl.when(t < nseg)
            def _():
                for k, v in enumerate(dsegs):
                    dseg_ref[k:k + 1, :] = v

            @pl.when(t >= nseg)
            def _():
                for k, v in enumerate(dsegs):
                    dseg_ref[k:k + 1, :] += v

        row_specs = [pl.BlockSpec((None, TB, r.shape[-1]), lambda b, t: (b, t, 0)) for r in rows]
        ct_specs = [pl.BlockSpec((None, TB, c.shape[-1]), lambda b, t: (b, t, 0)) for c in cts]
        return pl.pallas_call(
            body, name=name + "_bwd", grid=(bsz, tlen // TB),
            in_specs=in_specs(rows, colp, segp) + ct_specs,
            out_specs=row_specs + [pl.BlockSpec(colp.shape, lambda b, t: (0, 0)),
                                   pl.BlockSpec((None, None) + segp.shape[2:], lambda b, t: (b, seg_index(t), 0, 0))],
            out_shape=[jax.ShapeDtypeStruct(r.shape, r.dtype) for r in rows]
            + [jax.ShapeDtypeStruct(colp.shape, F32), jax.ShapeDtypeStruct(segp.shape, F32)],
            compiler_params=_cparams(("arbitrary", "arbitrary")),
        )(*rows, colp, segp, *cts)

    @jax.custom_vjp
    def op(rows, colp, segp):
        return tuple(fwd_call(rows, colp, segp))

    def op_fwd(rows, colp, segp):
        return tuple(fwd_call(rows, colp, segp)), (rows, colp, segp)

    def op_bwd(res, cts):
        rows, colp, segp = res
        outs = bwd_call(rows, colp, segp, list(cts))
        return list(outs[:len(rows)]), outs[len(rows)], outs[len(rows) + 1]

    op.defvjp(op_fwd, op_bwd)
    return op


def _rms(x, g):
    return x * lax.rsqrt(jnp.mean(x * x, axis=-1, keepdims=True) + EPS) * g


def _fn_pre(rows, cols, segs):
    return (_rms(rows[0], cols[0]) * (1.0 + segs[1]) + segs[0],)


def _fn_post_pre(rows, cols, segs):
    h2 = rows[0] + segs[0] * _rms(rows[1], cols[0])
    return h2, _rms(h2, cols[1]) * (1.0 + segs[2]) + segs[1]


def _fn_post(rows, cols, segs):
    return (rows[0] + segs[0] * _rms(rows[1], cols[0]),)


def _col_tile(n, limit):
    for t in range(min(n, limit), 127, -128):
        if n % t == 0 and t % 128 == 0:
            return t
    return n


def _relu2(x):
    r = jnp.maximum(x, 0.0)
    return r * r


def _mm(x, wb, act, out_dtype, name):
    m, k = x.shape
    n = wb.shape[1]
    tm = 512 if m % 512 == 0 else m
    tn = _col_tile(n, 1024 if k <= 1024 else 512)

    def body(x_ref, w_ref, o_ref):
        xv = x_ref[...]
        if act:
            xv = _relu2(xv.astype(F32))
        o_ref[...] = jnp.dot(xv.astype(BF16), w_ref[...], preferred_element_type=F32).astype(o_ref.dtype)

    return pl.pallas_call(
        body, name=name, grid=(n // tn, m // tm),
        in_specs=[pl.BlockSpec((tm, k), lambda j, i: (i, 0)), pl.BlockSpec((k, tn), lambda j, i: (0, j))],
        out_specs=pl.BlockSpec((tm, tn), lambda j, i: (i, j)),
        out_shape=jax.ShapeDtypeStruct((m, n), out_dtype),
        compiler_params=_cparams(("parallel", "parallel")),
    )(x, wb)


def _mm_dx(dy, wb, x, act, name):
    m, n = dy.shape
    k = wb.shape[0]
    tm = 256 if m % 256 == 0 else m

    def body(dy_ref, w_ref, x_ref, o_ref):
        acc = lax.dot_general(dy_ref[...].astype(BF16), w_ref[...], (((1,), (1,)), ((), ())),
                              preferred_element_type=F32)
        if act:
            acc = acc * (2.0 * jnp.maximum(x_ref[...].astype(F32), 0.0))
        o_ref[...] = acc.astype(o_ref.dtype)

    return pl.pallas_call(
        body, name=name, grid=(m // tm,),
        in_specs=[pl.BlockSpec((tm, n), lambda i: (i, 0)), pl.BlockSpec((k, n), lambda i: (0, 0)),
                  pl.BlockSpec((tm, k), lambda i: (i, 0))],
        out_specs=pl.BlockSpec((tm, k), lambda i: (i, 0)),
        out_shape=jax.ShapeDtypeStruct((m, k), x.dtype),
        compiler_params=_cparams(("parallel",)),
    )(dy, wb, x)


def _mm_dw(x, dy, act, name):
    m, k = x.shape
    n = dy.shape[1]
    tm = 512 if m % 512 == 0 else m
    tk, tn = _col_tile(k, 1024), _col_tile(n, 1024)

    def body(x_ref, dy_ref, o_ref):
        @pl.when(pl.program_id(2) == 0)
        def _():
            o_ref[...] = jnp.zeros_like(o_ref)

        xv = x_ref[...]
        if act:
            xv = _relu2(xv.astype(F32))
        o_ref[...] += lax.dot_general(xv.astype(BF16), dy_ref[...].astype(BF16), (((0,), (0,)), ((), ())),
                                      preferred_element_type=F32)

    return pl.pallas_call(
        body, name=name, grid=(k // tk, n // tn, m // tm),
        in_specs=[pl.BlockSpec((tm, tk), lambda a, b, i: (i, a)), pl.BlockSpec((tm, tn), lambda a, b, i: (i, b))],
        out_specs=pl.BlockSpec((tk, tn), lambda a, b, i: (a, b)),
        out_shape=jax.ShapeDtypeStruct((k, n), F32),
        compiler_params=_cparams(("parallel", "parallel", "arbitrary")),
    )(x, dy)


def _linear(act, name, out_dtype=BF16):
    def run(x, wb):
        y = _mm(x.reshape(-1, x.shape[-1]), wb, act, out_dtype, name + "_fwd")
        return y.reshape(x.shape[:-1] + (wb.shape[1],))

    @jax.custom_vjp
    def lin(x, w):
        return run(x, w.astype(BF16))

    def lin_fwd(x, w):
        wb = w.astype(BF16)
        return run(x, wb), (x, wb)

    def lin_bwd(res, dy):
        x, wb = res
        x2, dy2 = x.reshape(-1, x.shape[-1]), dy.reshape(-1, dy.shape[-1])
        dx = _mm_dx(dy2, wb, x2, act, name + "_dx").reshape(x.shape)
        return dx, _mm_dw(x2, dy2, act, name + "_dw")

    lin.defvjp(lin_fwd, lin_bwd)
    return lin


def _loss_head(y, tgt):
    bsz, seq, d = y.shape

    def body(y_ref, t_ref, l_ref, d_ref):
        err = y_ref[...] - t_ref[...]
        d_ref[...] = err * (1.0 / d)
        l_ref[...] = jnp.full(l_ref.shape, 0.5 / d, F32) * jnp.sum(err * err)

    spec = pl.BlockSpec((None, TB, d), lambda b, t: (b, t, 0))
    lblk, dy = pl.pallas_call(
        body, name="loss_head", grid=(bsz, seq // TB), in_specs=[spec, spec],
        out_specs=[pl.BlockSpec((None, None, 8, 128), lambda b, t: (b, t, 0, 0)), spec],
        out_shape=[jax.ShapeDtypeStruct((bsz, seq // TB, 8, 128), F32), jax.ShapeDtypeStruct(y.shape, F32)],
        compiler_params=_cparams(("parallel", "parallel")),
    )(y, tgt)
    return lblk[:, :, 0, 0], dy


def _row_tile(rows, limit=512):
    for tr in range(min(rows, limit), 7, -1):
        if rows % tr == 0 and tr % 8 == 0:
            return tr
    return rows


def _adamw(w, g, m, v, name):
    shape = w.shape
    cols = shape[-1]
    w2, g2, m2, v2 = (a.reshape(-1, cols) for a in (w, g, m, v))
    rows = w2.shape[0]
    tr = _row_tile(rows, max(8, (1 << 19) // cols))
    c1, c2 = 1.0 - ADAM_B1 ** ADAM_STEP, 1.0 - ADAM_B2 ** ADAM_STEP

    def body(w_ref, g_ref, m_ref, v_ref, d_ref, mo_ref, vo_ref):
        gv = g_ref[...]
        mn = ADAM_B1 * m_ref[...] + (1.0 - ADAM_B1) * gv
        vn = ADAM_B2 * v_ref[...] + (1.0 - ADAM_B2) * (gv * gv)
        d_ref[...] = -ADAM_LR * ((mn / c1) / (jnp.sqrt(vn / c2) + ADAM_EPS) + ADAM_WD * w_ref[...])
        mo_ref[...] = mn
        vo_ref[...] = vn

    spec = pl.BlockSpec((tr, cols), lambda i: (i, 0))
    outs = pl.pallas_call(
        body, name=name, grid=(rows // tr,), in_specs=[spec] * 4, out_specs=[spec] * 3,
        out_shape=[jax.ShapeDtypeStruct((rows, cols), F32)] * 3, compiler_params=_cparams(("parallel",)),
    )(w2, g2, m2, v2)
    return tuple(o.reshape(shape) for o in outs)


def _sum_leading(a, name):
    n, rows, cols = a.shape
    tr = _row_tile(rows, max(8, (1 << 18) // cols))

    def body(a_ref, o_ref):
        acc = a_ref[0]
        for j in range(1, n):
            acc = acc + a_ref[j]
        o_ref[...] = acc

    return pl.pallas_call(
        body, name=name, grid=(rows // tr,), in_specs=[pl.BlockSpec((n, tr, cols), lambda i: (0, i, 0))],
        out_specs=pl.BlockSpec((tr, cols), lambda i: (i, 0)), out_shape=jax.ShapeDtypeStruct((rows, cols), F32),
        compiler_params=_cparams(("parallel",)),
    )(a)


def _pack_rows(arrs):
    flat = [a.reshape(-1).astype(F32) for a in arrs]
    flat = [jnp.pad(f, (0, (-f.shape[0]) % LANES)) for f in flat]
    return jnp.concatenate(flat).reshape(-1, LANES)


def _unpack_rows(buf, shapes):
    out, r = [], 0
    for s in shapes:
        n = math.prod(s)
        nr = -(-n // LANES)
        out.append(buf[r:r + nr].reshape(-1)[:n].reshape(s))
        r += nr
    return out


def _pad_rows(buf, mult=8):
    return jnp.pad(buf, ((0, (-buf.shape[0]) % mult), (0, 0)))


def _ada_fwd(s, w):
    nl, d, n = w.shape
    r = s.shape[0]

    def body(s_ref, w_ref, o_ref):
        o_ref[...] = jnp.dot(s_ref[...], w_ref[...].astype(BF16), preferred_element_type=F32)

    return pl.pallas_call(
        body, name="ada_fwd", grid=(nl,),
        in_specs=[pl.BlockSpec((r, d), lambda i: (0, 0)), pl.BlockSpec((None, d, n), lambda i: (i, 0, 0))],
        out_specs=pl.BlockSpec((None, r, n), lambda i: (i, 0, 0)), out_shape=jax.ShapeDtypeStruct((nl, r, n), F32),
        compiler_params=_cparams(("parallel",)),
    )(s, w)


def _ada_dw(s, dm):
    nl, r, n = dm.shape
    d = s.shape[1]

    def body(s_ref, dm_ref, o_ref):
        o_ref[...] = lax.dot_general(s_ref[...], dm_ref[...].astype(BF16), (((0,), (0,)), ((), ())),
                                     preferred_element_type=F32)

    return pl.pallas_call(
        body, name="ada_dw", grid=(nl,),
        in_specs=[pl.BlockSpec((r, d), lambda i: (0, 0)), pl.BlockSpec((None, r, n), lambda i: (i, 0, 0))],
        out_specs=pl.BlockSpec((None, d, n), lambda i: (i, 0, 0)), out_shape=jax.ShapeDtypeStruct((nl, d, n), F32),
        compiler_params=_cparams(("parallel",)),
    )(s, dm)


def _ada_ds(dm, w):
    nl, r, n = dm.shape
    d = w.shape[1]

    def body(dm_ref, w_ref, o_ref):
        @pl.when(pl.program_id(0) == 0)
        def _():
            o_ref[...] = jnp.zeros_like(o_ref)

        o_ref[...] += lax.dot_general(dm_ref[...].astype(BF16), w_ref[...].astype(BF16), (((1,), (1,)), ((), ())),
                                      preferred_element_type=F32)

    return pl.pallas_call(
        body, name="ada_ds", grid=(nl,),
        in_specs=[pl.BlockSpec((None, r, n), lambda i: (i, 0, 0)), pl.BlockSpec((None, d, n), lambda i: (i, 0, 0))],
        out_specs=pl.BlockSpec((r, d), lambda i: (0, 0)), out_shape=jax.ShapeDtypeStruct((r, d), F32),
        compiler_params=_cparams(("arbitrary",)),
    )(dm, w)


_ANY = pl.BlockSpec(memory_space=pl.ANY)


def _position():
    return lax.axis_index("x"), lax.axis_index("y"), lax.axis_index("c")


def _all_gather8(block, name):
    m, n = block.shape

    def body(x_ref, out_ref, send_sems, recv_sems, local_sem):
        x, y, c = _position()
        me, sibling = (x, y, c), (x, y, 1 - c)
        chips = [(1 - x, y), (x, 1 - y), (1 - x, 1 - y)]

        def slot(px, py, pc):
            return out_ref.at[4 * px + 2 * py + pc]

        def copy(k, blk, to, src=None):
            return pltpu.make_async_remote_copy(
                src_ref=slot(*blk) if src is None else src, dst_ref=slot(*blk), send_sem=send_sems.at[k],
                recv_sem=recv_sems.at[k], device_id=to, device_id_type=MESH)

        mine = pltpu.make_async_copy(x_ref, slot(*me), local_sem)
        mine.start()
        first = [copy(0, me, sibling, src=x_ref)]
        first += [copy(1 + j, me, (*chip, c), src=x_ref) for j, chip in enumerate(chips)]
        for cp in first:
            cp.start()
        passed = [copy(4 + j, (*chip, c), sibling) for j, chip in enumerate(chips)]
        for j, chip in enumerate(chips):
            copy(1 + j, (*chip, c), me).wait_recv()
            passed[j].start()
        copy(0, sibling, me).wait_recv()
        for j, chip in enumerate(chips):
            copy(4 + j, (*chip, 1 - c), me).wait_recv()
        for cp in first + passed:
            cp.wait_send()
        mine.wait()

    return pl.pallas_call(
        body, name=name, out_shape=jax.ShapeDtypeStruct((8, m, n), block.dtype), in_specs=[_ANY], out_specs=_ANY,
        scratch_shapes=[pltpu.SemaphoreType.DMA((7,)), pltpu.SemaphoreType.DMA((7,)), pltpu.SemaphoreType.DMA],
    )(block)


def _gather_shards(shard, name):
    rows, cols = shard.shape
    half = rows // 2

    def body(x_ref, out_ref, send_sems, recv_sems, local_sem):
        x, y, c = _position()
        sibling = (x, y, 1 - c)
        chips = [(1 - x, y), (x, 1 - y), (1 - x, 1 - y)]

        def part(px, py, pc):
            return out_ref.at[2 * px + py, pl.ds(pc * half, half), :]

        def copy(k, blk, to, src=None):
            return pltpu.make_async_remote_copy(
                src_ref=part(*blk) if src is None else src, dst_ref=part(*blk), send_sem=send_sems.at[k],
                recv_sem=recv_sems.at[k], device_id=to, device_id_type=MESH)

        mine = pltpu.make_async_copy(x_ref, out_ref.at[2 * x + y], local_sem)
        mine.start()
        my_half = x_ref.at[pl.ds(c * half, half), :]
        first = [copy(j, (x, y, c), (*chip, c), src=my_half) for j, chip in enumerate(chips)]
        for cp in first:
            cp.start()
        passed = [copy(3 + j, (*chip, c), sibling) for j, chip in enumerate(chips)]
        for j, chip in enumerate(chips):
            copy(j, (*chip, c), sibling).wait_recv()
            passed[j].start()
        for j, chip in enumerate(chips):
            copy(3 + j, (*chip, 1 - c), sibling).wait_recv()
        for cp in first + passed:
            cp.wait_send()
        mine.wait()

    return pl.pallas_call(
        body, name=name, out_shape=jax.ShapeDtypeStruct((N_SHARD, rows, cols), shard.dtype), in_specs=[_ANY],
        out_specs=_ANY,
        scratch_shapes=[pltpu.SemaphoreType.DMA((6,)), pltpu.SemaphoreType.DMA((6,)), pltpu.SemaphoreType.DMA],
    )(shard)


def _swap_other_half(g, name):
    ns, rows, cols = g.shape
    half = rows // 2

    def body(g_ref, out_ref, send_sem, recv_sem):
        x, y, c = _position()
        cp = pltpu.make_async_remote_copy(
            src_ref=g_ref.at[:, pl.ds((1 - c) * half, half), :], dst_ref=out_ref, send_sem=send_sem,
            recv_sem=recv_sem, device_id=(x, y, 1 - c), device_id_type=MESH)
        cp.start()
        cp.wait()

    return pl.pallas_call(
        body, name=name, out_shape=jax.ShapeDtypeStruct((ns, half, cols), g.dtype), in_specs=[_ANY], out_specs=_ANY,
        scratch_shapes=[pltpu.SemaphoreType.DMA, pltpu.SemaphoreType.DMA],
    )(g)


def _add_own_half(g, r, c_idx, name):
    ns, rows, cols = g.shape
    half = rows // 2
    tr = _row_tile(half, max(8, (1 << 19) // cols))
    nb = half // tr

    def body(c_ref, g_ref, r_ref, o_ref, ob_ref):
        acc = g_ref[...] + r_ref[...]
        o_ref[...] = acc
        ob_ref[...] = acc.astype(BF16)

    out = pl.BlockSpec((None, tr, cols), lambda s, i, c_ref: (s, i, 0))
    return pl.pallas_call(
        body, name=name,
        grid_spec=pltpu.PrefetchScalarGridSpec(
            num_scalar_prefetch=1, grid=(ns, nb),
            in_specs=[pl.BlockSpec((None, tr, cols), lambda s, i, c_ref: (s, c_ref[0] * nb + i, 0)), out],
            out_specs=[out, out]),
        out_shape=[jax.ShapeDtypeStruct((ns, half, cols), F32), jax.ShapeDtypeStruct((ns, half, cols), BF16)],
        compiler_params=_cparams(("parallel", "parallel")),
    )(c_idx, g, r)


def _send_to_chips(a, name):
    ns, half, cols = a.shape

    def body(a_ref, out_ref, send_sems, recv_sems):
        x, y, c = _position()
        chips = [(1 - x, y), (x, 1 - y), (1 - x, 1 - y)]
        cps = [pltpu.make_async_remote_copy(
            src_ref=a_ref.at[2 * px + py], dst_ref=out_ref.at[j], send_sem=send_sems.at[j], recv_sem=recv_sems.at[j],
            device_id=(px, py, c), device_id_type=MESH) for j, (px, py) in enumerate(chips)]
        for cp in cps:
            cp.start()
        for cp in cps:
            cp.wait()

    return pl.pallas_call(
        body, name=name, out_shape=jax.ShapeDtypeStruct((3, half, cols), a.dtype), in_specs=[_ANY], out_specs=_ANY,
        scratch_shapes=[pltpu.SemaphoreType.DMA((3,)), pltpu.SemaphoreType.DMA((3,))],
    )(a)


def _add_arrivals(a, r, s_idx, name):
    ns, half, cols = a.shape
    tr = _row_tile(half, max(8, (1 << 18) // cols))

    def body(s_ref, a_ref, r_ref, o_ref):
        o_ref[...] = ((a_ref[...] + r_ref[0].astype(F32)) + r_ref[1].astype(F32)) + r_ref[2].astype(F32)

    return pl.pallas_call(
        body, name=name,
        grid_spec=pltpu.PrefetchScalarGridSpec(
            num_scalar_prefetch=1, grid=(half // tr,),
            in_specs=[pl.BlockSpec((None, tr, cols), lambda i, s_ref: (s_ref[0], i, 0)),
                      pl.BlockSpec((3, tr, cols), lambda i, s_ref: (0, i, 0))],
            out_specs=pl.BlockSpec((tr, cols), lambda i, s_ref: (i, 0))),
        out_shape=jax.ShapeDtypeStruct((half, cols), F32), compiler_params=_cparams(("parallel",)),
    )(s_idx, a, r)


def _join_halves(f, name):
    half, cols = f.shape

    def body(f_ref, out_ref, send_sem, recv_sem, local_sem):
        x, y, c = _position()
        mine = pltpu.make_async_copy(f_ref, out_ref.at[pl.ds(c * half, half), :], local_sem)
        mine.start()
        cp = pltpu.make_async_remote_copy(
            src_ref=f_ref, dst_ref=out_ref.at[pl.ds(c * half, half), :], send_sem=send_sem, recv_sem=recv_sem,
            device_id=(x, y, 1 - c), device_id_type=MESH)
        cp.start()
        cp.wait()
        mine.wait()

    return pl.pallas_call(
        body, name=name, out_shape=jax.ShapeDtypeStruct((2 * half, cols), f.dtype), in_specs=[_ANY], out_specs=_ANY,
        scratch_shapes=[pltpu.SemaphoreType.DMA, pltpu.SemaphoreType.DMA, pltpu.SemaphoreType.DMA],
    )(f)


def _conv_core(z, conv_w, nctx):
    d = conv_w.shape[-1]
    zf = z.astype(F32)
    b_gate, c_gate, v = zf[..., :d], zf[..., d:2 * d], zf[..., 2 * d:]
    u = c_gate * v

    def conv(s):
        p = jnp.pad(s, ((0, 0), (1, 1), (0, 0)))
        return conv_w[0] * p[:, :-2] + conv_w[1] * p[:, 1:-1] + conv_w[2] * p[:, 2:]

    return b_gate * jnp.concatenate([conv(u[:, :nctx]), conv(u[:, nctx:])], axis=1)


def _hg_chunk(q, v, tf, lb, st, rev):
    n = HG_CHUNK
    f = lb + (1.0 - lb) * jax.nn.sigmoid(tf)
    kk = 1.0 - f
    lf = jnp.log(f)
    row = lax.broadcasted_iota(jnp.int32, (n, n), 0)
    col = lax.broadcasted_iota(jnp.int32, (n, n), 1)
    tri = (col >= row) if rev else (col <= row)
    ones = tri.astype(BF16)
    h1 = lf.astype(BF16)
    r1 = lf - h1.astype(F32)
    h2 = r1.astype(BF16)
    h3 = (r1 - h2.astype(F32)).astype(BF16)
    b = (jnp.dot(ones, h1, preferred_element_type=F32) + jnp.dot(ones, h2, preferred_element_type=F32)
         + jnp.dot(ones, h3, preferred_element_type=F32))
    mid = n - n // 2 if rev else n // 2 - 1
    last = 0 if rev else n - 1
    b_mid, b_last = b[mid:mid + 1], b[last:last + 1]
    qs = (q * jnp.exp(b - b_mid)).astype(BF16)
    ks = (kk * jnp.exp(b_mid - b)).astype(BF16)
    sc = lax.dot_general(qs, ks, (((1,), (1,)), ((), ())), preferred_element_type=F32)
    sc = jnp.where(tri, sc, 0.0).astype(BF16)
    vb = v.astype(BF16)
    o = jnp.dot(sc, vb, preferred_element_type=F32) + lax.dot_general(
        (q * jnp.exp(b)).astype(BF16), st.astype(BF16), (((1,), (1,)), ((), ())), preferred_element_type=F32)
    ks2 = (kk * jnp.exp(b_last - b)).astype(BF16)
    st_new = st * jnp.exp(b_last) + lax.dot_general(vb, ks2, (((0,), (0,)), ((), ())), preferred_element_type=F32)
    return o, st_new


HG_VMEM_LIMIT = 56 * 1024 * 1024
HG_UNROLL = 2


def _hg_scan(nctx):
    hd, n = HG_HEAD_DIM, HG_CHUNK

    def geometry(z):
        bsz, tlen, d5 = z.shape
        return bsz, tlen, d5 // 5, (d5 // 5) // hd, tlen // n, nctx // n

    def rev_chunk(j, nch, ncc):
        return jnp.where(j < ncc, ncc - 1 - j, nch - 1 + ncc - j)

    def rows(c):
        return pl.ds(pl.multiple_of(c * n, n), n)

    def in_specs(z):
        bsz, tlen, d, nh, nch, ncc = geometry(z)
        col = lambda k: pl.BlockSpec((None, tlen, hd), lambda b, h: (b, 0, k * nh + h))
        return [col(0), col(1), col(3), col(4), pl.BlockSpec((1, hd), lambda b, h: (0, h))]

    params = pltpu.CompilerParams(dimension_semantics=("parallel", "parallel"), vmem_limit_bytes=HG_VMEM_LIMIT)

    def fwd_call(z, lb):
        bsz, tlen, d, nh, nch, ncc = geometry(z)

        def body(q_ref, v_ref, ff_ref, fb_ref, lb_ref, o_ref):
            lbv = lb_ref[...]
            o_ref[...] = jnp.zeros_like(o_ref)

            def step(j, carry):
                out = []
                for rev, f_ref, st in ((False, ff_ref, carry[0]), (True, fb_ref, carry[1])):
                    sl = rows(rev_chunk(j, nch, ncc) if rev else j)
                    o, st = _hg_chunk(q_ref[sl, :].astype(F32), v_ref[sl, :].astype(F32), f_ref[sl, :].astype(F32),
                                      lbv, st, rev)
                    o_ref[sl, :] += o
                    out.append(st)
                return tuple(out)

            zero = jnp.zeros((hd, hd), F32)
            lax.fori_loop(0, nch, step, (zero, zero), unroll=HG_UNROLL)

        return pl.pallas_call(
            body, name="hg_scan", grid=(bsz, nh), in_specs=in_specs(z),
            out_specs=pl.BlockSpec((None, tlen, hd), lambda b, h: (b, 0, h)),
            out_shape=jax.ShapeDtypeStruct((bsz, tlen, d), F32), compiler_params=params,
        )(z, z, z, z, lb)

    def bwd_call(z, lb, do):
        bsz, tlen, d, nh, nch, ncc = geometry(z)

        def body(q_ref, v_ref, ff_ref, fb_ref, lb_ref, do_ref, dq_ref, dv_ref, dff_ref, dfb_ref, dlb_ref,
                 stf_ref, stb_ref):
            lbv = lb_ref[...]
            chains = ((False, ff_ref, dff_ref, stf_ref), (True, fb_ref, dfb_ref, stb_ref))

            def load(sl, f_ref):
                return q_ref[sl, :].astype(F32), v_ref[sl, :].astype(F32), f_ref[sl, :].astype(F32)

            def fstep(j, carry):
                out = []
                for (rev, f_ref, _, st_ref), st in zip(chains, carry):
                    st_ref[j] = st
                    sl = rows(rev_chunk(j, nch, ncc) if rev else j)
                    out.append(_hg_chunk(*load(sl, f_ref), lbv, st, rev)[1])
                return tuple(out)

            zero = jnp.zeros((hd, hd), F32)
            lax.fori_loop(0, nch, fstep, (zero, zero), unroll=HG_UNROLL)
            dq_ref[...] = jnp.zeros_like(dq_ref)
            dv_ref[...] = jnp.zeros_like(dv_ref)

            def bstep(i, carry):
                j = nch - 1 - i
                dlb = carry[2]
                out = []
                for (rev, f_ref, df_ref, st_ref), dst in zip(chains, carry[:2]):
                    sl = rows(rev_chunk(j, nch, ncc) if rev else j)
                    _, vjp = jax.vjp(functools.partial(_hg_chunk, rev=rev), *load(sl, f_ref), lbv, st_ref[j])
                    dq, dv, df, dlb_c, dst = vjp((do_ref[sl, :], dst))
                    dq_ref[sl, :] += dq
                    dv_ref[sl, :] += dv
                    df_ref[sl, :] = df.astype(df_ref.dtype)
                    dlb = dlb + dlb_c
                    out.append(dst)
                return out[0], out[1], dlb

            _, _, dlb = lax.fori_loop(0, nch, bstep, (zero, zero, jnp.zeros((1, hd), F32)), unroll=HG_UNROLL)
            dlb_ref[...] = dlb

        head = pl.BlockSpec((None, tlen, hd), lambda b, h: (b, 0, h))
        return pl.pallas_call(
            body, name="hg_scan_bwd", grid=(bsz, nh), in_specs=in_specs(z) + [head],
            out_specs=[head, head, head, head, pl.BlockSpec((None, 1, hd), lambda b, h: (b, 0, h))],
            out_shape=[jax.ShapeDtypeStruct((bsz, tlen, d), F32)] * 2 + [jax.ShapeDtypeStruct((bsz, tlen, d), BF16)] * 2
            + [jax.ShapeDtypeStruct((bsz, 1, d), F32)],
            scratch_shapes=[pltpu.VMEM((nch, hd, hd), F32), pltpu.VMEM((nch, hd, hd), F32)],
            compiler_params=params,
        )(z, z, z, z, lb, do)

    @jax.custom_vjp
    def scan(z, lb):
        return fwd_call(z, lb)

    def scan_fwd(z, lb):
        return fwd_call(z, lb), (z, lb)

    def scan_bwd(res, do):
        z, lb = res
        dq, dv, dff, dfb, dlb = bwd_call(z, lb, do)
        dz = jnp.concatenate([dq.astype(BF16), dv.astype(BF16), jnp.zeros_like(dff), dff, dfb], axis=-1)
        return dz, jnp.sum(dlb, axis=0)

    scan.defvjp(scan_fwd, scan_bwd)
    return scan


def _hg_readout(o, gate, g_norm):
    bsz, tlen, d = gate.shape
    nh = d // HG_HEAD_DIM
    rb = TB * nh

    def fn(x, g, gt):
        return x * lax.rsqrt(jnp.mean(x * x, axis=-1, keepdims=True) + EPS) * gt * (g * jax.nn.sigmoid(g))

    spec = pl.BlockSpec((None, rb, HG_HEAD_DIM), lambda b, t: (b, t, 0))
    gspec = pl.BlockSpec((rb, HG_HEAD_DIM), lambda b, t: (0, 0))
    grid = (bsz, tlen // TB)

    def fwd_call(x, g, gt):
        def body(x_ref, g_ref, gt_ref, y_ref):
            y_ref[...] = fn(x_ref[...], g_ref[...].astype(F32), gt_ref[...]).astype(y_ref.dtype)

        return pl.pallas_call(
            body, name="hg_readout_fwd", grid=grid, in_specs=[spec, spec, gspec], out_specs=spec,
            out_shape=jax.ShapeDtypeStruct(g.shape, BF16), compiler_params=_cparams(("parallel", "parallel")),
        )(x, g, gt)

    def bwd_call(x, g, gt, dy):
        def body(x_ref, g_ref, gt_ref, dy_ref, dx_ref, dg_ref, dgt_ref):
            _, vjp = jax.vjp(fn, x_ref[...], g_ref[...].astype(F32), gt_ref[...])
            dx, dg, dgt = vjp(dy_ref[...].astype(F32))
            dx_ref[...] = dx
            dg_ref[...] = dg.astype(dg_ref.dtype)

            @pl.when((pl.program_id(0) == 0) & (pl.program_id(1) == 0))
            def _():
                dgt_ref[...] = jnp.zeros_like(dgt_ref)

            dgt_ref[...] += dgt

        return pl.pallas_call(
            body, name="hg_readout_bwd", grid=grid, in_specs=[spec, spec, gspec, spec],
            out_specs=[spec, spec, gspec],
            out_shape=[jax.ShapeDtypeStruct(x.shape, F32), jax.ShapeDtypeStruct(g.shape, BF16),
                       jax.ShapeDtypeStruct(gt.shape, F32)],
            compiler_params=_cparams(("arbitrary", "arbitrary")),
        )(x, g, gt, dy)

    @jax.custom_vjp
    def op(x, g, gt):
        return fwd_call(x, g, gt)

    def op_fwd(x, g, gt):
        return fwd_call(x, g, gt), (x, g, gt)

    def op_bwd(res, dy):
        return tuple(bwd_call(*res, dy))

    op.defvjp(op_fwd, op_bwd)
    heads = lambda t: t.reshape(bsz, tlen * nh, HG_HEAD_DIM)
    gt = jnp.tile(g_norm.reshape(nh, HG_HEAD_DIM), (TB, 1))
    return op(heads(o), heads(gate), gt).reshape(bsz, tlen, d)


def _hgrn_core(z, lower_bound, g_norm, nctx):
    d = g_norm.shape[-1]
    o = _hg_scan(nctx)(z, lower_bound.reshape(1, d))
    return _hg_readout(o, z[..., 2 * d:3 * d], g_norm)


S5_LC = 16


def _s5_mats(lam_re, lam_im, log_dt, b_re, b_im, c_re, c_im, rev):
    hi = lax.Precision.HIGHEST
    ng, ns = lam_re.shape
    lc, gs = S5_LC, S5_GROUP
    lam_re = jnp.minimum(lam_re, -1e-4)
    dt = jnp.exp(log_dt)[:, None]
    k = jnp.arange(lc + 1, dtype=F32)[:, None, None]
    mag, ang = jnp.exp(lam_re * dt * k), lam_im * dt * k
    p_re, p_im = mag * jnp.cos(ang), mag * jnp.sin(ang)
    a_re, a_im = p_re[1], p_im[1]
    den = lam_re * lam_re + lam_im * lam_im
    f_re = ((a_re - 1) * lam_re + a_im * lam_im) / den
    f_im = (a_im * lam_re - (a_re - 1) * lam_im) / den
    bb_re = f_re[..., None] * b_re - f_im[..., None] * b_im
    bb_im = f_re[..., None] * b_im + f_im[..., None] * b_re
    w_re = c_re[None] * p_re[:, :, None, :] - c_im[None] * p_im[:, :, None, :]
    w_im = c_re[None] * p_im[:, :, None, :] + c_im[None] * p_re[:, :, None, :]
    kk = (jnp.einsum('kgcn,gnd->kgcd', w_re[:lc], bb_re, precision=hi)
          - jnp.einsum('kgcn,gnd->kgcd', w_im[:lc], bb_im, precision=hi))
    t = jnp.arange(lc)
    lag = t[None, :] - t[:, None] if rev else t[:, None] - t[None, :]
    m = jnp.where((lag >= 0)[:, :, None, None, None], kk[jnp.maximum(lag, 0)], 0.0)
    mt = m.transpose(2, 1, 4, 0, 3).reshape(ng, lc * gs, lc * gs)
    left = t if rev else lc - 1 - t
    pw_re, pw_im = p_re[left], p_im[left]
    pr = pw_re[..., None] * bb_re[None] - pw_im[..., None] * bb_im[None]
    pi = pw_re[..., None] * bb_im[None] + pw_im[..., None] * bb_re[None]
    pt = jnp.concatenate([pr, pi], axis=2).transpose(1, 0, 3, 2).reshape(ng, lc * gs, 2 * ns)
    since = lc - t if rev else t + 1
    q = jnp.concatenate([w_re[since], -w_im[since]], axis=-1)
    qt = q.transpose(1, 3, 0, 2).reshape(ng, 2 * ns, lc * gs)
    a16 = jnp.concatenate([p_re[lc], p_im[lc]], axis=-1)
    return mt, pt, qt, a16


def _s5_bmm(terms, out_dtype, name, sum_dirs=False):
    ng = terms[0][0].shape[-3]
    ops, dlist = [], []
    for a, b, dn in terms:
        ops += [a, b]
        dlist.append(dn)
    (ca,), (cb,) = dlist[0]
    om, on = terms[0][0].shape[-2:][1 - ca], terms[0][1].shape[-2:][1 - cb]

    def spec(o):
        if o.ndim == 4:
            return pl.BlockSpec((None, None) + o.shape[2:], lambda g, d: (d, g, 0, 0))
        return pl.BlockSpec((None,) + o.shape[1:], lambda g, d: (g, 0, 0))

    def body(*refs):
        acc = None
        for j, dn in enumerate(dlist):
            a, b = refs[2 * j][...].astype(BF16), refs[2 * j + 1][...].astype(BF16)
            r = lax.dot_general(a, b, (dn, ((), ())), preferred_element_type=F32)
            acc = r if acc is None else acc + r
        o_ref = refs[-1]
        if sum_dirs:
            @pl.when(pl.program_id(1) == 0)
            def _():
                o_ref[...] = acc.astype(o_ref.dtype)

            @pl.when(pl.program_id(1) != 0)
            def _():
                o_ref[...] = (o_ref[...].astype(F32) + acc).astype(o_ref.dtype)
        else:
            o_ref[...] = acc.astype(o_ref.dtype)

    if sum_dirs:
        out_spec = pl.BlockSpec((None, om, on), lambda g, d: (g, 0, 0))
        out_shape = jax.ShapeDtypeStruct((ng, om, on), out_dtype)
    else:
        out_spec = pl.BlockSpec((None, None, om, on), lambda g, d: (d, g, 0, 0))
        out_shape = jax.ShapeDtypeStruct((2, ng, om, on), out_dtype)
    return pl.pallas_call(
        body, name=name, grid=(ng, 2), in_specs=[spec(o) for o in ops], out_specs=out_spec, out_shape=out_shape,
        compiler_params=_cparams(("parallel", "arbitrary" if sum_dirs else "parallel")),
    )(*ops)


def _s5_row_block(rows):
    return 32 if rows % 32 == 0 else rows


def _s5_chunk_order(j, d, nc, ncc):
    return jnp.where(d == 0, j, jnp.where(j < ncc, ncc - 1 - j, nc - 1 + ncc - j))


def _s5_scan_fwd(z, a1, a2, ncc, name):
    nd, nc, rows, lanes = z.shape
    rb = _s5_row_block(rows)

    def body(z_ref, a1_ref, a2_ref, x_ref):
        a1v, a2v = a1_ref[...], a2_ref[...]
        d = pl.program_id(0)

        def step(j, x):
            c = _s5_chunk_order(j, d, nc, ncc)
            x_ref[c] = x
            return a1v * x + a2v * pltpu.roll(x, lanes // 2, axis=1) + z_ref[c]

        lax.fori_loop(0, nc, step, jnp.zeros((rb, lanes), F32))

    blk = pl.BlockSpec((None, nc, rb, lanes), lambda d, r: (d, 0, r, 0))
    par = pl.BlockSpec((None, rb, lanes), lambda d, r: (d, r, 0))
    return pl.pallas_call(
        body, name=name, grid=(nd, rows // rb), in_specs=[blk, par, par], out_specs=blk,
        out_shape=jax.ShapeDtypeStruct(z.shape, F32), compiler_params=_cparams(("parallel", "parallel")),
    )(z, a1, a2)


def _s5_scan_bwd(dxp, xp, a1, a2b, ncc, name):
    nd, nc, rows, lanes = dxp.shape
    rb = _s5_row_block(rows)

    def body(dxp_ref, xp_ref, a1_ref, a2_ref, dz_ref, p1_ref, p2_ref):
        a1v, a2v = a1_ref[...], a2_ref[...]
        zero = jnp.zeros((rb, lanes), F32)
        d = pl.program_id(0)

        def step(i, carry):
            g_next, nxt, p1, p2 = carry
            c = _s5_chunk_order(nc - 1 - i, d, nc, ncc)
            g = nxt + a1v * g_next + a2v * pltpu.roll(g_next, lanes // 2, axis=1)
            dz_ref[c] = g
            x = xp_ref[c]
            return g, dxp_ref[c], p1 + x * g, p2 + pltpu.roll(x, lanes // 2, axis=1) * g

        _, _, p1, p2 = lax.fori_loop(0, nc, step, (zero, zero, zero, zero))
        p1_ref[...] = p1
        p2_ref[...] = p2

    blk = pl.BlockSpec((None, nc, rb, lanes), lambda d, r: (d, 0, r, 0))
    par = pl.BlockSpec((None, rb, lanes), lambda d, r: (d, r, 0))
    return pl.pallas_call(
        body, name=name, grid=(nd, rows // rb), in_specs=[blk, blk, par, par], out_specs=[blk, par, par],
        out_shape=[jax.ShapeDtypeStruct(dxp.shape, F32), jax.ShapeDtypeStruct((nd, rows, lanes), F32),
                   jax.ShapeDtypeStruct((nd, rows, lanes), F32)],
        compiler_params=_cparams(("parallel", "parallel")),
    )(dxp, xp, a1, a2b)


def _s5_rows(t, bsz):
    nd, ng, m, k = t.shape
    return t.reshape(nd, ng, bsz, m // bsz, k).transpose(0, 3, 2, 1, 4).reshape(nd, m // bsz, bsz * ng, k)


def _s5_groups(t, bsz):
    nd, nc, rows, k = t.shape
    return t.reshape(nd, nc, bsz, rows // bsz, k).transpose(0, 3, 2, 1, 4).reshape(nd, rows // bsz, bsz * nc, k)


def _s5_coeffs(a16, bsz):
    half = a16.shape[-1] // 2
    re, im = a16[..., :half], a16[..., half:]
    tile = lambda v: jnp.tile(v, (1, bsz, 1))
    return tile(jnp.concatenate([re, re], -1)), tile(jnp.concatenate([-im, im], -1)), tile(jnp.concatenate([im, -im], -1))


def _s5_apply(bsz, ncc):
    nn, nt, tn = ((1,), (0,)), ((1,), (1,)), ((0,), (0,))

    def run(u, mt, pt, qt, a16):
        a1, a2, _ = _s5_coeffs(a16, bsz)
        z = _s5_bmm([(u, pt, nn)], F32, "s5_z")
        xp = _s5_scan_fwd(_s5_rows(z, bsz), a1, a2, ncc, "s5_scan")
        xg = _s5_groups(xp, bsz).astype(BF16)
        y = _s5_bmm([(u, mt, nn), (xg, qt, nn)], F32, "s5_y", sum_dirs=True)
        return y, (xp, xg)

    @jax.custom_vjp
    def apply(u, mt, pt, qt, a16):
        return run(u, mt.astype(BF16), pt.astype(BF16), qt.astype(BF16), a16)[0]

    def apply_fwd(u, mt, pt, qt, a16):
        mtb, ptb, qtb = mt.astype(BF16), pt.astype(BF16), qt.astype(BF16)
        y, (xp, xg) = run(u, mtb, ptb, qtb, a16)
        return y, (u, mtb, ptb, qtb, a16, xp, xg)

    def apply_bwd(res, dy):
        u, mtb, ptb, qtb, a16, xp, xg = res
        a1, _, a2b = _s5_coeffs(a16, bsz)
        dyb = dy.astype(BF16)
        dmt = _s5_bmm([(u, dyb, tn)], F32, "s5_dmt", sum_dirs=True) * 0.5
        dqt = _s5_bmm([(xg, dyb, tn)], F32, "s5_dqt")
        dxp = _s5_bmm([(dyb, qtb, nt)], F32, "s5_dxp")
        dz, p1, p2 = _s5_scan_bwd(_s5_rows(dxp, bsz), xp, a1, a2b, ncc, "s5_scan_bwd")
        dzg = _s5_groups(dz, bsz).astype(BF16)
        dpt = _s5_bmm([(u, dzg, tn)], F32, "s5_dpt")
        du = _s5_bmm([(dyb, mtb, nt), (dzg, ptb, nt)], BF16, "s5_du", sum_dirs=True)
        half = a16.shape[-1] // 2
        p1 = jnp.sum(p1.reshape(2, bsz, -1, 2 * half), axis=1)
        p2 = jnp.sum(p2.reshape(2, bsz, -1, 2 * half), axis=1)
        da16 = jnp.concatenate([p1[..., :half] + p1[..., half:], p2[..., half:] - p2[..., :half]], axis=-1)
        return du, jnp.stack([dmt, dmt]), dpt, dqt, da16

    apply.defvjp(apply_fwd, apply_bwd)
    return apply


def _s5_core(a, p, nctx):
    bsz, tlen, d = a.shape
    ng, lc, gs = d // S5_GROUP, S5_LC, S5_GROUP
    mats = [_s5_mats(p["s5_lam_re"][k], p["s5_lam_im"][k], p["s5_log_dt"][k], p["s5_b_re"], p["s5_b_im"],
                     p["s5_c_re"][k], p["s5_c_im"][k], k == 1) for k in range(2)]
    mt, pt, qt, a16 = (jnp.stack([mats[0][j], mats[1][j]]) for j in range(4))
    u = a.reshape(bsz, tlen // lc, lc, ng, gs).transpose(3, 0, 1, 2, 4).reshape(ng, bsz * (tlen // lc), lc * gs)
    y = _s5_apply(bsz, nctx // lc)(u, mt, pt, qt, a16)
    y = y.reshape(ng, bsz, tlen // lc, lc, gs).transpose(1, 2, 3, 0, 4).reshape(bsz, tlen, d)
    return jax.nn.gelu(p["s5_d"] * a.astype(F32) + y)


NA_LANES = 256
NA_MASKED = -1e30


def _na_tables(rpb):
    hi = lax.Precision.HIGHEST
    nh = rpb.shape[0]
    q = jnp.arange(GRID_W)
    kc = jnp.arange(GRID_W)
    q_start = jnp.clip(q - NA_COLS // 2, 0, GRID_W - NA_COLS)
    inwin = (kc[None, :] >= q_start[:, None]) & (kc[None, :] < q_start[:, None] + NA_COLS)
    dc = kc[None, :] - q[:, None] + NA_COLS - 1
    onehot = ((dc[:, :, None] == jnp.arange(2 * NA_COLS - 1)) & inwin[:, :, None]).astype(F32)
    a = jnp.arange(NA_ROWS)[None, :] - jnp.arange(NA_ROWS)[:, None] + NA_ROWS - 1
    tab = jnp.einsum('hskc,qlc->hsqkl', rpb[:, a, :], onehot, precision=hi)
    tab = jnp.where(inwin[None, None, :, None, :], tab, NA_MASKED)
    return tab.reshape(nh, NA_ROWS, GRID_W, NA_ROWS * GRID_W)


def _na_math(q2, kw, vw, kc, vc, bias, dh):
    scale = dh ** -0.5
    lane_head = lax.broadcasted_iota(jnp.int32, (1, NA_LANES), 1) // dh
    nt = (((1,), (1,)), ((), ()))
    kwb, vwb, kcb, vcb = (t.astype(BF16) for t in (kw, vw, kc, vc))
    out = jnp.zeros(q2.shape, F32)
    for j in range(NA_LANES // dh):
        mine = lane_head == j
        qh = jnp.where(mine, q2, 0.0).astype(BF16)
        s_loc = lax.dot_general(qh, kwb, nt, preferred_element_type=F32) * scale + bias[j]
        s_ctx = lax.dot_general(qh, kcb, nt, preferred_element_type=F32) * scale
        m = jnp.maximum(jnp.max(s_loc, axis=-1, keepdims=True), jnp.max(s_ctx, axis=-1, keepdims=True))
        m = lax.stop_gradient(m)
        p_loc, p_ctx = jnp.exp(s_loc - m), jnp.exp(s_ctx - m)
        den = jnp.sum(p_loc, axis=-1, keepdims=True) + jnp.sum(p_ctx, axis=-1, keepdims=True)
        inv = 1.0 / den
        o = (jnp.dot((p_loc * inv).astype(BF16), vwb, preferred_element_type=F32)
             + jnp.dot((p_ctx * inv).astype(BF16), vcb, preferred_element_type=F32))
        out = out + jnp.where(mine, o, 0.0)
    return out


def _na_attention(nctx, nh):
    def geometry(z):
        bsz, tlen, d3 = z.shape
        d = d3 // 3
        rows = (tlen - nctx) // GRID_W
        return bsz, tlen, d, rows, d // nh, d // NA_LANES

    def key_row0(r, rows):
        return jnp.clip(r - NA_ROWS // 2, 0, rows - NA_ROWS)

    def specs(z):
        bsz, tlen, d, rows, dh, nlb = geometry(z)
        hpb = NA_LANES // dh
        qs = pl.BlockSpec((None, GRID_W, NA_LANES), lambda b, h, r: (b, nctx // GRID_W + r, h))
        ks = pl.BlockSpec((None, tlen, NA_LANES), lambda b, h, r: (b, 0, nlb + h))
        vs = pl.BlockSpec((None, tlen, NA_LANES), lambda b, h, r: (b, 0, 2 * nlb + h))
        bs = pl.BlockSpec((hpb, None, GRID_W, NA_ROWS * GRID_W), lambda b, h, r: (h, r - key_row0(r, rows), 0, 0))
        os_ = pl.BlockSpec((None, GRID_W, NA_LANES), lambda b, h, r: (b, r, h))
        return qs, ks, vs, bs, os_

    def window(r, rows):
        return pl.ds(pl.multiple_of(nctx + key_row0(r, rows) * GRID_W, GRID_W), NA_ROWS * GRID_W)

    def fwd_call(z, bias):
        bsz, tlen, d, rows, dh, nlb = geometry(z)
        qs, ks, vs, bs, os_ = specs(z)

        def body(q_ref, k_ref, v_ref, b_ref, o_ref):
            win = window(pl.program_id(2), rows)
            o_ref[...] = _na_math(q_ref[...].astype(F32), k_ref[win, :], v_ref[win, :], k_ref[0:nctx, :],
                                  v_ref[0:nctx, :], b_ref[...], dh).astype(o_ref.dtype)

        return pl.pallas_call(
            body, name="na_fwd", grid=(bsz, nlb, rows), in_specs=[qs, ks, vs, bs], out_specs=os_,
            out_shape=jax.ShapeDtypeStruct((bsz, tlen - nctx, d), BF16),
            compiler_params=_cparams(("parallel", "parallel", "arbitrary")),
        )(z, z, z, bias)

    def bwd_call(z, bias, do):
        bsz, tlen, d, rows, dh, nlb = geometry(z)
        hpb = NA_LANES // dh
        qs, ks, vs, bs, os_ = specs(z)

        def body(q_ref, k_ref, v_ref, b_ref, do_ref, dq_ref, dk_ref, dv_ref, db_ref):
            r = pl.program_id(2)
            win = window(r, rows)

            @pl.when(r == 0)
            def _():
                dk_ref[...] = jnp.zeros_like(dk_ref)
                dv_ref[...] = jnp.zeros_like(dv_ref)

            prim = (q_ref[...].astype(F32), k_ref[win, :].astype(F32), v_ref[win, :].astype(F32),
                    k_ref[0:nctx, :].astype(F32), v_ref[0:nctx, :].astype(F32), b_ref[...])
            _, vjp = jax.vjp(functools.partial(_na_math, dh=dh), *prim)
            dq, dkw, dvw, dkc, dvc, db = vjp(do_ref[...].astype(F32))
            dq_ref[...] = dq.astype(dq_ref.dtype)
            dk_ref[win, :] += dkw
            dv_ref[win, :] += dvw
            dk_ref[0:nctx, :] += dkc
            dv_ref[0:nctx, :] += dvc
            prev = jnp.maximum(r - 1, 0)
            first = (r == 0) | ((r - key_row0(r, rows)) != (prev - key_row0(prev, rows)))

            @pl.when(first)
            def _():
                db_ref[...] = db

            @pl.when(jnp.logical_not(first))
            def _():
                db_ref[...] += db

        acc = pl.BlockSpec((None, tlen, NA_LANES), lambda b, h, r: (b, 0, h))
        dbs = pl.BlockSpec((None, hpb, None, GRID_W, NA_ROWS * GRID_W),
                           lambda b, h, r: (b, h, r - key_row0(r, rows), 0, 0))
        return pl.pallas_call(
            body, name="na_bwd", grid=(bsz, nlb, rows), in_specs=[qs, ks, vs, bs, os_],
            out_specs=[os_, acc, acc, dbs],
            out_shape=[jax.ShapeDtypeStruct((bsz, tlen - nctx, d), BF16), jax.ShapeDtypeStruct((bsz, tlen, d), F32),
                       jax.ShapeDtypeStruct((bsz, tlen, d), F32), jax.ShapeDtypeStruct((bsz,) + bias.shape, F32)],
            compiler_params=_cparams(("parallel", "parallel", "arbitrary")),
        )(z, z, z, bias, do)

    @jax.custom_vjp
    def attend(z, bias):
        return fwd_call(z, bias)

    def attend_fwd(z, bias):
        return fwd_call(z, bias), (z, bias)

    def attend_bwd(res, do):
        z, bias = res
        dq, dk, dv, db = bwd_call(z, bias, do)
        dq = jnp.pad(dq, ((0, 0), (nctx, 0), (0, 0)))
        dz = jnp.concatenate([dq, dk.astype(BF16), dv.astype(BF16)], axis=-1)
        return dz, _sum_leading(db.reshape(db.shape[0], -1, NA_ROWS * GRID_W), "na_sum_dbias").reshape(bias.shape)

    attend.defvjp(attend_fwd, attend_bwd)
    return attend


def _na_core(z, rpb, nctx):
    o = _na_attention(nctx, rpb.shape[0])(z, _na_tables(rpb))
    return jnp.pad(o, ((0, 0), (nctx, 0), (0, 0)))


def _forward(x, ctx, mod, p):
    nctx = ctx.shape[1]
    h = jnp.concatenate([ctx, x], axis=1)
    lb_all = jnp.cumsum(jax.nn.softmax(p["hg_lower_bound"], axis=0), axis=0)
    lb_all = lb_all - lb_all[0]
    gains = p["norm_gains"]
    pre = _rowwise(_fn_pre, [BF16], "pre", 2)
    (a,) = pre([h], gains[0, 0:1], mod[0][:, :, 0:2])
    for i in range(DEPTH):
        tag = f"l{i}"
        if i == 0:
            z = _linear(False, tag + "_sc_in")(a, p["sc_w_in"][0])
            yc = _conv_core(z, p["sc_conv"][0], nctx)
            y = _linear(False, tag + "_sc_out")(yc.astype(BF16), p["sc_w_out"][0])
        elif i == 1:
            z = _linear(False, tag + "_hg_in")(a, p["hg_w_in"][0])
            yc = _hgrn_core(z, lb_all[i], p["hg_norm"][0], nctx)
            y = _linear(False, tag + "_hg_out")(yc, p["hg_w_out"][0])
        elif i == 2:
            sp = {k: v[0] for k, v in p.items() if k.startswith("s5_") and k != "s5_w_glu"}
            gz = _s5_core(a, sp, nctx)
            vg = _linear(False, tag + "_s5_glu")(gz.astype(BF16), p["s5_w_glu"][0]).astype(F32)
            d = gz.shape[-1]
            y = (vg[..., :d] * jax.nn.sigmoid(vg[..., d:])).astype(BF16)
        else:
            z = _linear(False, tag + "_na_qkv")(a, p["na_w_qkv"][0])
            yc = _na_core(z, p["na_rpb"][0], nctx)
            y = _linear(False, tag + "_na_out")(yc, p["na_w_out"][0])
        h, a2 = _rowwise(_fn_post_pre, [F32, BF16], tag + "_mix_post", 2)(
            [h, y], gains[i, 1:3], mod[i][:, :, 2:5])
        u = _linear(False, tag + "_mlp_in")(a2, p["mlp_w_in"][i])
        f = _linear(True, tag + "_mlp_out")(u, p["mlp_w_out"][i])
        if i + 1 < DEPTH:
            cols = jnp.stack([gains[i, 3], gains[i + 1, 0]])
            segs = jnp.concatenate([mod[i][:, :, 5:6], mod[i + 1][:, :, 0:2]], axis=2)
            h, a = _rowwise(_fn_post_pre, [F32, BF16], tag + "_mlp_post", 2)([h, f], cols, segs)
        else:
            (h,) = _rowwise(_fn_post, [F32], tag + "_mlp_post", 2)([h, f], gains[i, 3:4], mod[i][:, :, 5:6])
    return h[:, nctx:]


def _local_step(x, ctx, tgt, mod, p):
    y, vjp = jax.vjp(lambda x_, mod_, p_: _forward(x_, ctx, mod_, p_), x, mod, p)
    lblk, dy = _loss_head(y, tgt)
    gx, dmod, gp = vjp(dy)
    return lblk, gx, dmod, gp


PACK_COLS = 512


def _pack_shard(ws):
    return jnp.concatenate([ws[n].reshape(-1, PACK_COLS) for n in BIG], axis=0)


def _unpack_full(buf, shard_shapes):
    out, r = {}, 0
    for n in BIG:
        s = shard_shapes[n]
        nr = math.prod(s) // PACK_COLS
        parts = buf[:, r:r + nr].reshape((N_SHARD,) + s)
        axis = 1 if n in COL_SHARDED else 0
        out[n] = [jnp.concatenate([parts[k, l] for k in range(N_SHARD)], axis=axis) for l in range(s[0])]
        r += nr
    return out


def _pack_grads(gp, shard_shapes):
    per = []
    for k in range(N_SHARD):
        rows = []
        for n in BIG:
            s = shard_shapes[n]
            axis = 1 if n in COL_SHARDED else 0
            width = s[1 + axis]
            for g in gp[n]:
                rows.append(lax.slice_in_dim(g, k * width, (k + 1) * width, axis=axis).reshape(-1, PACK_COLS))
        per.append(jnp.concatenate(rows, axis=0))
    return jnp.stack(per)


def _unpack_shard(buf, shard_shapes):
    out, r = {}, 0
    for n in BIG:
        s = shard_shapes[n]
        nr = math.prod(s) // PACK_COLS
        out[n] = buf[r:r + nr].reshape(s)
        r += nr
    return out


def _shard_cols(a, k, width):
    return lax.dynamic_slice_in_dim(a, k * width, width, axis=a.ndim - 1)


def kernel(x, c, ctx, c_ctx, ada_w, ada_b, norm_gains, mlp_w_in, mlp_w_out, sc_w_in, sc_conv, sc_w_out, hg_w_in, hg_lower_bound, hg_norm, hg_w_out, s5_lam_re, s5_lam_im, s5_log_dt, s5_b_re, s5_b_im, s5_c_re, s5_c_im, s5_d, s5_w_glu, na_w_qkv, na_rpb, na_w_out, loss_target, m_c_ctx, m_ada_w, m_ada_b, m_norm_gains, m_mlp_w_in, m_mlp_w_out, m_sc_w_in, m_sc_conv, m_sc_w_out, m_hg_w_in, m_hg_lower_bound, m_hg_norm, m_hg_w_out, m_s5_lam_re, m_s5_lam_im, m_s5_log_dt, m_s5_b_re, m_s5_b_im, m_s5_c_re, m_s5_c_im, m_s5_d, m_s5_w_glu, m_na_w_qkv, m_na_rpb, m_na_w_out, v_c_ctx, v_ada_w, v_ada_b, v_norm_gains, v_mlp_w_in, v_mlp_w_out, v_sc_w_in, v_sc_conv, v_sc_w_out, v_hg_w_in, v_hg_lower_bound, v_hg_norm, v_hg_w_out, v_s5_lam_re, v_s5_lam_im, v_s5_log_dt, v_s5_b_re, v_s5_b_im, v_s5_c_re, v_s5_c_im, v_s5_d, v_s5_w_glu, v_na_w_qkv, v_na_rpb, v_na_w_out):
    w = dict(c_ctx=c_ctx, ada_w=ada_w, ada_b=ada_b, norm_gains=norm_gains, mlp_w_in=mlp_w_in, mlp_w_out=mlp_w_out,
             sc_w_in=sc_w_in, sc_conv=sc_conv, sc_w_out=sc_w_out, hg_w_in=hg_w_in, hg_lower_bound=hg_lower_bound,
             hg_norm=hg_norm, hg_w_out=hg_w_out, s5_lam_re=s5_lam_re, s5_lam_im=s5_lam_im, s5_log_dt=s5_log_dt,
             s5_b_re=s5_b_re, s5_b_im=s5_b_im, s5_c_re=s5_c_re, s5_c_im=s5_c_im, s5_d=s5_d, s5_w_glu=s5_w_glu,
             na_w_qkv=na_w_qkv, na_rpb=na_rpb, na_w_out=na_w_out)
    mom_m = dict(zip(WEIGHTS, [m_c_ctx, m_ada_w, m_ada_b, m_norm_gains, m_mlp_w_in, m_mlp_w_out, m_sc_w_in, m_sc_conv,
                               m_sc_w_out, m_hg_w_in, m_hg_lower_bound, m_hg_norm, m_hg_w_out, m_s5_lam_re, m_s5_lam_im,
                               m_s5_log_dt, m_s5_b_re, m_s5_b_im, m_s5_c_re, m_s5_c_im, m_s5_d, m_s5_w_glu, m_na_w_qkv,
                               m_na_rpb, m_na_w_out]))
    mom_v = dict(zip(WEIGHTS, [v_c_ctx, v_ada_w, v_ada_b, v_norm_gains, v_mlp_w_in, v_mlp_w_out, v_sc_w_in, v_sc_conv,
                               v_sc_w_out, v_hg_w_in, v_hg_lower_bound, v_hg_norm, v_hg_w_out, v_s5_lam_re, v_s5_lam_im,
                               v_s5_log_dt, v_s5_b_re, v_s5_b_im, v_s5_c_re, v_s5_c_im, v_s5_d, v_s5_w_glu, v_na_w_qkv,
                               v_na_rpb, v_na_w_out]))
    bsz, _, d = x.shape
    ax, ay, ac = lax.axis_index("x"), lax.axis_index("y"), lax.axis_index("c")
    chip = 2 * ax + ay
    dev = 2 * chip + ac
    n_dev = 2 * N_SHARD
    dsh = d // N_SHARD

    shard_shapes = {n: w[n].shape for n in BIG}
    packed = _gather_shards(_pack_shard({n: w[n].astype(BF16) for n in BIG}), "gather_weights")
    full = {n: [a.astype(F32) for a in v] for n, v in _unpack_full(packed, shard_shapes).items()}

    small_shapes = [c.shape] + [w[n].shape for n in SMALL_SHARDED]
    buf_a = _all_gather8(_pad_rows(_pack_rows([c] + [w[n] for n in SMALL_SHARDED])), "gather_small")
    per_dev = [_unpack_rows(buf_a[k], small_shapes) for k in range(n_dev)]
    c_all = jnp.concatenate([per_dev[k][0] for k in range(n_dev)], axis=0)
    for j, n in enumerate(SMALL_SHARDED):
        full[n] = jnp.concatenate([per_dev[2 * s][1 + j] for s in range(N_SHARD)], axis=-1)
    for n in SMALL_REPL:
        full[n] = w[n]

    n_all = c_all.shape[0]
    s_rows = 32
    cond = jnp.concatenate([c_all, c_ctx[None]], axis=0)
    s_all = jnp.pad(jax.nn.silu(cond), ((0, s_rows - n_all - 1), (0, 0))).astype(BF16)
    mod_part = _ada_fwd(s_all, ada_w)
    nsh = mod_part.shape[-1]
    buf_b = _all_gather8(_pad_rows(mod_part.reshape(-1, LANES)), "gather_mod")
    nrow_b = mod_part.size // LANES
    mod_raw = jnp.concatenate([buf_b[2 * s, :nrow_b].reshape(mod_part.shape) for s in range(N_SHARD)], axis=-1)
    mod_raw = mod_raw + ada_b[:, None, :]
    mod_lat = lax.dynamic_slice_in_dim(mod_raw, dev * bsz, bsz, axis=1).reshape(DEPTH, bsz, 1, N_MOD, d)
    mod_ctx = jnp.broadcast_to(mod_raw[:, n_all].reshape(DEPTH, 1, 1, N_MOD, d), (DEPTH, bsz, 1, N_MOD, d))
    mod = jnp.concatenate([mod_ctx, mod_lat], axis=2)

    lblk, grad_x, dmod, gp = _local_step(x, ctx, loss_target, mod, full)

    dmod_rows = jnp.concatenate([dmod[:, :, 1].reshape(DEPTH, bsz, N_MOD * d),
                                 jnp.sum(dmod[:, :, 0], axis=1).reshape(DEPTH, 1, N_MOD * d)], axis=1)
    c_list = [dmod_rows] + [gp[n] for n in SMALL_SHARDED + SMALL_REPL] + [jnp.sum(lblk).reshape(1)]
    c_shapes = [a.shape for a in c_list]
    buf_c = _all_gather8(_pad_rows(_pack_rows(c_list)), "gather_grads")
    sum_c = _sum_leading(buf_c, "sum_grads")
    summed = _unpack_rows(sum_c, c_shapes)
    grads = {}
    for j, n in enumerate(SMALL_SHARDED):
        grads[n] = _shard_cols(summed[1 + j], chip, w[n].shape[-1])
    for j, n in enumerate(SMALL_REPL):
        grads[n] = summed[1 + len(SMALL_SHARDED) + j]
    loss = summed[-1][0]

    dmod_dev = [_unpack_rows(buf_c[k], c_shapes[:1])[0] for k in range(n_dev)]
    dm_lat = jnp.concatenate([t[:, :bsz] for t in dmod_dev], axis=1)
    dm_ctx = summed[0][:, bsz:bsz + 1]
    dm_all = jnp.concatenate([dm_lat, dm_ctx], axis=1)
    grads["ada_b"] = _sum_leading(jnp.moveaxis(dm_all, 1, 0).reshape(n_all + 1, -1, LANES), "sum_ada_b").reshape(ada_b.shape)
    dm_sh = jnp.pad(_shard_cols(dm_all, chip, nsh), ((0, 0), (0, s_rows - n_all - 1), (0, 0)))
    grads["ada_w"] = _ada_dw(s_all, dm_sh)
    ds_part = _ada_ds(dm_sh, ada_w)
    buf_d = _all_gather8(_pad_rows(ds_part[n_all:n_all + 1]), "gather_dcond")
    ds_ctx = _sum_leading(jnp.stack([buf_d[2 * s] for s in range(N_SHARD)]), "sum_dcond")[0]
    grads["c_ctx"] = jax.vjp(jax.nn.silu, c_ctx)[1](ds_ctx)[0]

    g_pack = _pack_grads(gp, shard_shapes)
    c_idx = jnp.reshape(ac, (1,)).astype(jnp.int32)
    s_idx = jnp.reshape(chip, (1,)).astype(jnp.int32)
    part, part_wire = _add_own_half(g_pack, _swap_other_half(g_pack, "rs_swap_half"), c_idx, "rs_add_sibling")
    mine = _add_arrivals(part, _send_to_chips(part_wire, "rs_send_chips"), s_idx, "rs_add_chips")
    grads.update(_unpack_shard(_join_halves(mine, "rs_join_halves"), shard_shapes))

    delta, new_m, new_v = {}, {}, {}
    for n in BIG + ["ada_w"]:
        delta[n], new_m[n], new_v[n] = _adamw(w[n], grads[n], mom_m[n], mom_v[n], "adamw_" + n)
    small = [n for n in WEIGHTS if n not in BIG and n != "ada_w"]
    shapes = [w[n].shape for n in small]
    packs = [_pad_rows(_pack_rows([src[n] for n in small])) for src in (w, grads, mom_m, mom_v)]
    outs = _adamw(*packs, "adamw_small")
    for tgt, buf in zip((delta, new_m, new_v), outs):
        for n, a in zip(small, _unpack_rows(buf, shapes)):
            tgt[n] = a
    return (loss, grad_x, *[grads[n] for n in WEIGHTS], *[delta[n] for n in WEIGHTS],
            *[new_m[n] for n in WEIGHTS], *[new_v[n] for n in WEIGHTS])
```

```python
import functools
import math

import jax
import jax.numpy as jnp
from jax import lax
from jax.experimental import pallas as pl
from jax.experimental.pallas import tpu as pltpu

F32 = jnp.float32
BF16 = jnp.bfloat16
MESH = pl.DeviceIdType.MESH
EPS = 1e-6
TB = 256
LANES = 1024
VMEM_LIMIT = 48 * 1024 * 1024
N_SHARD = 4
DEPTH = 4
N_MOD = 6
GRID_W = 64
HG_HEAD_DIM = 128
HG_CHUNK = 32
S5_GROUP = 16
NA_ROWS = 8
NA_COLS = 16
ADAM_LR, ADAM_B1, ADAM_B2, ADAM_EPS, ADAM_WD, ADAM_STEP = 0.001, 0.9, 0.999, 1e-08, 0.01, 10

BIG = ["mlp_w_in", "mlp_w_out", "sc_w_in", "sc_w_out", "hg_w_in", "hg_w_out", "s5_w_glu", "na_w_qkv", "na_w_out"]
COL_SHARDED = {"mlp_w_in", "sc_w_in", "hg_w_in", "s5_w_glu", "na_w_qkv"}
SMALL_SHARDED = ["norm_gains", "sc_conv", "hg_norm", "s5_d"]
SMALL_REPL = ["hg_lower_bound", "s5_lam_re", "s5_lam_im", "s5_log_dt", "s5_b_re", "s5_b_im", "s5_c_re", "s5_c_im", "na_rpb"]
WEIGHTS = ["c_ctx", "ada_w", "ada_b", "norm_gains", "mlp_w_in", "mlp_w_out", "sc_w_in", "sc_conv", "sc_w_out", "hg_w_in",
           "hg_lower_bound", "hg_norm", "hg_w_out", "s5_lam_re", "s5_lam_im", "s5_log_dt", "s5_b_re", "s5_b_im", "s5_c_re",
           "s5_c_im", "s5_d", "s5_w_glu", "na_w_qkv", "na_rpb", "na_w_out"]


def _cparams(sem=None):
    return pltpu.CompilerParams(dimension_semantics=sem, vmem_limit_bytes=VMEM_LIMIT)


def _rowwise(fn, out_dtypes, name, nseg):
    def seg_index(t):
        return jnp.minimum(t, nseg - 1)

    def in_specs(rows, colp, segp):
        rs = [pl.BlockSpec((None, TB, r.shape[-1]), lambda b, t: (b, t, 0)) for r in rows]
        cs = pl.BlockSpec(colp.shape, lambda b, t: (0, 0))
        ss = pl.BlockSpec((None, None) + segp.shape[2:], lambda b, t: (b, seg_index(t), 0, 0))
        return rs + [cs, ss]

    def load(refs, n, nc, ns):
        rows = [r[...].astype(F32) for r in refs[:n]]
        cols = [refs[n][k:k + 1, :] for k in range(nc)]
        segs = [refs[n + 1][k:k + 1, :] for k in range(ns)]
        return rows, cols, segs

    def out_blocks(rows, colp, segp):
        one = jax.ShapeDtypeStruct((1, colp.shape[-1]), F32)
        return jax.eval_shape(fn, [jax.ShapeDtypeStruct((TB, r.shape[-1]), F32) for r in rows],
                              [one] * colp.shape[0], [one] * segp.shape[2])

    def fwd_call(rows, colp, segp):
        bsz, tlen, _ = rows[0].shape
        n, nc, ns = len(rows), colp.shape[0], segp.shape[2]
        blk = out_blocks(rows, colp, segp)

        def body(*refs):
            vals = fn(*load(refs, n, nc, ns))
            for o, v in zip(refs[n + 2:], vals):
                o[...] = v.astype(o.dtype)

        return pl.pallas_call(
            body, name=name + "_fwd", grid=(bsz, tlen // TB), in_specs=in_specs(rows, colp, segp),
            out_specs=[pl.BlockSpec((None, TB, o.shape[-1]), lambda b, t: (b, t, 0)) for o in blk],
            out_shape=[jax.ShapeDtypeStruct((bsz, tlen, o.shape[-1]), dt) for o, dt in zip(blk, out_dtypes)],
            compiler_params=_cparams(("parallel", "parallel")),
        )(*rows, colp, segp)

    def bwd_call(rows, colp, segp, cts):
        bsz, tlen, _ = rows[0].shape
        n, nc, ns, m = len(rows), colp.shape[0], segp.shape[2], len(cts)

        def body(*refs):
            b, t = pl.program_id(0), pl.program_id(1)
            prim = load(refs, n, nc, ns)
            ct = tuple(r[...].astype(F32) for r in refs[n + 2:n + 2 + m])
            _, vjp = jax.vjp(fn, *prim)
            drows, dcols, dsegs = vjp(ct)
            outs = refs[n + 2 + m:]
            for o, v in zip(outs[:n], drows):
                o[...] = v.astype(o.dtype)
            dcol_ref, dseg_ref = outs[n], outs[n + 1]

            @pl.when((b == 0) & (t == 0))
            def _():
                dcol_ref[...] = jnp.zeros_like(dcol_ref)

            for k, v in enumerate(dcols):
                dcol_ref[k:k + 1, :] += v

            @pl.when(t < nseg)
            def _():
                for k, v in enumerate(dsegs):
                    dseg_ref[k:k + 1, :] = v

            @pl.when(t >= nseg)
            def _():
                for k, v in enumerate(dsegs):
                    dseg_ref[k:k + 1, :] += v

        row_specs = [pl.BlockSpec((None, TB, r.shape[-1]), lambda b, t: (b, t, 0)) for r in rows]
        ct_specs = [pl.BlockSpec((None, TB, c.shape[-1]), lambda b, t: (b, t, 0)) for c in cts]
        return pl.pallas_call(
            body, name=name + "_bwd", grid=(bsz, tlen // TB),
            in_specs=in_specs(rows, colp, segp) + ct_specs,
            out_specs=row_specs + [pl.BlockSpec(colp.shape, lambda b, t: (0, 0)),
                                   pl.BlockSpec((None, None) + segp.shape[2:], lambda b, t: (b, seg_index(t), 0, 0))],
            out_shape=[jax.ShapeDtypeStruct(r.shape, r.dtype) for r in rows]
            + [jax.ShapeDtypeStruct(colp.shape, F32), jax.ShapeDtypeStruct(segp.shape, F32)],
            compiler_params=_cparams(("arbitrary", "arbitrary")),
        )(*rows, colp, segp, *cts)

    @jax.custom_vjp
    def op(rows, colp, segp):
        return tuple(fwd_call(rows, colp, segp))

    def op_fwd(rows, colp, segp):
        return tuple(fwd_call(rows, colp, segp)), (rows, colp, segp)

    def op_bwd(res, cts):
        rows, colp, segp = res
        outs = bwd_call(rows, colp, segp, list(cts))
        return list(outs[:len(rows)]), outs[len(rows)], outs[len(rows) + 1]

    op.defvjp(op_fwd, op_bwd)
    return op


def _rms(x, g):
    return x * lax.rsqrt(jnp.mean(x * x, axis=-1, keepdims=True) + EPS) * g


def _fn_pre(rows, cols, segs):
    return (_rms(rows[0], cols[0]) * (1.0 + segs[1]) + segs[0],)


def _fn_post_pre(rows, cols, segs):
    h2 = rows[0] + segs[0] * _rms(rows[1], cols[0])
    return h2, _rms(h2, cols[1]) * (1.0 + segs[2]) + segs[1]


def _fn_post(rows, cols, segs):
    return (rows[0] + segs[0] * _rms(rows[1], cols[0]),)


def _col_tile(n, limit):
    for t in range(min(n, limit), 127, -128):
        if n % t == 0 and t % 128 == 0:
            return t
    return n


def _relu2(x):
    r = jnp.maximum(x, 0.0)
    return r * r


def _mm(x, wb, act, out_dtype, name):
    m, k = x.shape
    n = wb.shape[1]
    tm = 512 if m % 512 == 0 else m
    tn = _col_tile(n, 1024 if k <= 1024 else 512)

    def body(x_ref, w_ref, o_ref):
        xv = x_ref[...]
        if act:
            xv = _relu2(xv.astype(F32))
        o_ref[...] = jnp.dot(xv.astype(BF16), w_ref[...], preferred_element_type=F32).astype(o_ref.dtype)

    return pl.pallas_call(
        body, name=name, grid=(n // tn, m // tm),
        in_specs=[pl.BlockSpec((tm, k), lambda j, i: (i, 0)), pl.BlockSpec((k, tn), lambda j, i: (0, j))],
        out_specs=pl.BlockSpec((tm, tn), lambda j, i: (i, j)),
        out_shape=jax.ShapeDtypeStruct((m, n), out_dtype),
        compiler_params=_cparams(("parallel", "parallel")),
    )(x, wb)


def _mm_dx(dy, wb, x, act, name):
    m, n = dy.shape
    k = wb.shape[0]
    tm = 256 if m % 256 == 0 else m

    def body(dy_ref, w_ref, x_ref, o_ref):
        acc = lax.dot_general(dy_ref[...].astype(BF16), w_ref[...], (((1,), (1,)), ((), ())),
                              preferred_element_type=F32)
        if act:
            acc = acc * (2.0 * jnp.maximum(x_ref[...].astype(F32), 0.0))
        o_ref[...] = acc.astype(o_ref.dtype)

    return pl.pallas_call(
        body, name=name, grid=(m // tm,),
        in_specs=[pl.BlockSpec((tm, n), lambda i: (i, 0)), pl.BlockSpec((k, n), lambda i: (0, 0)),
                  pl.BlockSpec((tm, k), lambda i: (i, 0))],
        out_specs=pl.BlockSpec((tm, k), lambda i: (i, 0)),
        out_shape=jax.ShapeDtypeStruct((m, k), x.dtype),
        compiler_params=_cparams(("parallel",)),
    )(dy, wb, x)


def _mm_dw(x, dy, act, name):
    m, k = x.shape
    n = dy.shape[1]
    tm = 512 if m % 512 == 0 else m
    tk, tn = _col_tile(k, 1024), _col_tile(n, 1024)

    def body(x_ref, dy_ref, o_ref):
        @pl.when(pl.program_id(2) == 0)
        def _():
            o_ref[...] = jnp.zeros_like(o_ref)

        xv = x_ref[...]
        if act:
            xv = _relu2(xv.astype(F32))
        o_ref[...] += lax.dot_general(xv.astype(BF16), dy_ref[...].astype(BF16), (((0,), (0,)), ((), ())),
                                      preferred_element_type=F32)

    return pl.pallas_call(
        body, name=name, grid=(k // tk, n // tn, m // tm),
        in_specs=[pl.BlockSpec((tm, tk), lambda a, b, i: (i, a)), pl.BlockSpec((tm, tn), lambda a, b, i: (i, b))],
        out_specs=pl.BlockSpec((tk, tn), lambda a, b, i: (a, b)),
        out_shape=jax.ShapeDtypeStruct((k, n), F32),
        compiler_params=_cparams(("parallel", "parallel", "arbitrary")),
    )(x, dy)


def _linear(act, name, out_dtype=BF16):
    def run(x, wb):
        y = _mm(x.reshape(-1, x.shape[-1]), wb, act, out_dtype, name + "_fwd")
        return y.reshape(x.shape[:-1] + (wb.shape[1],))

    @jax.custom_vjp
    def lin(x, w):
        return run(x, w.astype(BF16))

    def lin_fwd(x, w):
        wb = w.astype(BF16)
        return run(x, wb), (x, wb)

    def lin_bwd(res, dy):
        x, wb = res
        x2, dy2 = x.reshape(-1, x.shape[-1]), dy.reshape(-1, dy.shape[-1])
        dx = _mm_dx(dy2, wb, x2, act, name + "_dx").reshape(x.shape)
        return dx, _mm_dw(x2, dy2, act, name + "_dw")

    lin.defvjp(lin_fwd, lin_bwd)
    return lin


def _loss_head(y, tgt):
    bsz, seq, d = y.shape

    def body(y_ref, t_ref, l_ref, d_ref):
        err = y_ref[...] - t_ref[...]
        d_ref[...] = err * (1.0 / d)
        l_ref[...] = jnp.full(l_ref.shape, 0.5 / d, F32) * jnp.sum(err * err)

    spec = pl.BlockSpec((None, TB, d), lambda b, t: (b, t, 0))
    lblk, dy = pl.pallas_call(
        body, name="loss_head", grid=(bsz, seq // TB), in_specs=[spec, spec],
        out_specs=[pl.BlockSpec((None, None, 8, 128), lambda b, t: (b, t, 0, 0)), spec],
        out_shape=[jax.ShapeDtypeStruct((bsz, seq // TB, 8, 128), F32), jax.ShapeDtypeStruct(y.shape, F32)],
        compiler_params=_cparams(("parallel", "parallel")),
    )(y, tgt)
    return lblk[:, :, 0, 0], dy


def _row_tile(rows, limit=512):
    for tr in range(min(rows, limit), 7, -1):
        if rows % tr == 0 and tr % 8 == 0:
            return tr
    return rows


def _adamw(w, g, m, v, name):
    shape = w.shape
    cols = shape[-1]
    w2, g2, m2, v2 = (a.reshape(-1, cols) for a in (w, g, m, v))
    rows = w2.shape[0]
    tr = _row_tile(rows, max(8, (1 << 19) // cols))
    c1, c2 = 1.0 - ADAM_B1 ** ADAM_STEP, 1.0 - ADAM_B2 ** ADAM_STEP

    def body(w_ref, g_ref, m_ref, v_ref, d_ref, mo_ref, vo_ref):
        gv = g_ref[...]
        mn = ADAM_B1 * m_ref[...] + (1.0 - ADAM_B1) * gv
        vn = ADAM_B2 * v_ref[...] + (1.0 - ADAM_B2) * (gv * gv)
        d_ref[...] = -ADAM_LR * ((mn / c1) / (jnp.sqrt(vn / c2) + ADAM_EPS) + ADAM_WD * w_ref[...])
        mo_ref[...] = mn
        vo_ref[...] = vn

    spec = pl.BlockSpec((tr, cols), lambda i: (i, 0))
    outs = pl.pallas_call(
        body, name=name, grid=(rows // tr,), in_specs=[spec] * 4, out_specs=[spec] * 3,
        out_shape=[jax.ShapeDtypeStruct((rows, cols), F32)] * 3, compiler_params=_cparams(("parallel",)),
    )(w2, g2, m2, v2)
    return tuple(o.reshape(shape) for o in outs)


def _sum_leading(a, name):
    n, rows, cols = a.shape
    tr = _row_tile(rows, max(8, (1 << 18) // cols))

    def body(a_ref, o_ref):
        acc = a_ref[0]
        for j in range(1, n):
            acc = acc + a_ref[j]
        o_ref[...] = acc

    return pl.pallas_call(
        body, name=name, grid=(rows // tr,), in_specs=[pl.BlockSpec((n, tr, cols), lambda i: (0, i, 0))],
        out_specs=pl.BlockSpec((tr, cols), lambda i: (i, 0)), out_shape=jax.ShapeDtypeStruct((rows, cols), F32),
        compiler_params=_cparams(("parallel",)),
    )(a)


def _pack_rows(arrs):
    flat = [a.reshape(-1).astype(F32) for a in arrs]
    flat = [jnp.pad(f, (0, (-f.shape[0]) % LANES)) for f in flat]
    return jnp.concatenate(flat).reshape(-1, LANES)


def _unpack_rows(buf, shapes):
    out, r = [], 0
    for s in shapes:
        n = math.prod(s)
        nr = -(-n // LANES)
        out.append(buf[r:r + nr].reshape(-1)[:n].reshape(s))
        r += nr
    return out


def _pad_rows(buf, mult=8):
    return jnp.pad(buf, ((0, (-buf.shape[0]) % mult), (0, 0)))


def _ada_fwd(s, w):
    nl, d, n = w.shape
    r = s.shape[0]

    def body(s_ref, w_ref, o_ref):
        o_ref[...] = jnp.dot(s_ref[...], w_ref[...].astype(BF16), preferred_element_type=F32)

    return pl.pallas_call(
        body, name="ada_fwd", grid=(nl,),
        in_specs=[pl.BlockSpec((r, d), lambda i: (0, 0)), pl.BlockSpec((None, d, n), lambda i: (i, 0, 0))],
        out_specs=pl.BlockSpec((None, r, n), lambda i: (i, 0, 0)), out_shape=jax.ShapeDtypeStruct((nl, r, n), F32),
        compiler_params=_cparams(("parallel",)),
    )(s, w)


def _ada_dw(s, dm):
    nl, r, n = dm.shape
    d = s.shape[1]

    def body(s_ref, dm_ref, o_ref):
        o_ref[...] = lax.dot_general(s_ref[...], dm_ref[...].astype(BF16), (((0,), (0,)), ((), ())),
                                     preferred_element_type=F32)

    return pl.pallas_call(
        body, name="ada_dw", grid=(nl,),
        in_specs=[pl.BlockSpec((r, d), lambda i: (0, 0)), pl.BlockSpec((None, r, n), lambda i: (i, 0, 0))],
        out_specs=pl.BlockSpec((None, d, n), lambda i: (i, 0, 0)), out_shape=jax.ShapeDtypeStruct((nl, d, n), F32),
        compiler_params=_cparams(("parallel",)),
    )(s, dm)


def _ada_ds(dm, w):
    nl, r, n = dm.shape
    d = w.shape[1]

    def body(dm_ref, w_ref, o_ref):
        @pl.when(pl.program_id(0) == 0)
        def _():
            o_ref[...] = jnp.zeros_like(o_ref)

        o_ref[...] += lax.dot_general(dm_ref[...].astype(BF16), w_ref[...].astype(BF16), (((1,), (1,)), ((), ())),
                                      preferred_element_type=F32)

    return pl.pallas_call(
        body, name="ada_ds", grid=(nl,),
        in_specs=[pl.BlockSpec((None, r, n), lambda i: (i, 0, 0)), pl.BlockSpec((None, d, n), lambda i: (i, 0, 0))],
        out_specs=pl.BlockSpec((r, d), lambda i: (0, 0)), out_shape=jax.ShapeDtypeStruct((r, d), F32),
        compiler_params=_cparams(("arbitrary",)),
    )(dm, w)


_ANY = pl.BlockSpec(memory_space=pl.ANY)


def _position():
    return lax.axis_index("x"), lax.axis_index("y"), lax.axis_index("c")


def _all_gather8(block, name):
    m, n = block.shape

    def body(x_ref, out_ref, send_sems, recv_sems, local_sem):
        x, y, c = _position()
        me, sibling = (x, y, c), (x, y, 1 - c)
        chips = [(1 - x, y), (x, 1 - y), (1 - x, 1 - y)]

        def slot(px, py, pc):
            return out_ref.at[4 * px + 2 * py + pc]

        def copy(k, blk, to, src=None):
            return pltpu.make_async_remote_copy(
                src_ref=slot(*blk) if src is None else src, dst_ref=slot(*blk), send_sem=send_sems.at[k],
                recv_sem=recv_sems.at[k], device_id=to, device_id_type=MESH)

        mine = pltpu.make_async_copy(x_ref, slot(*me), local_sem)
        mine.start()
        first = [copy(0, me, sibling, src=x_ref)]
        first += [copy(1 + j, me, (*chip, c), src=x_ref) for j, chip in enumerate(chips)]
        for cp in first:
            cp.start()
        passed = [copy(4 + j, (*chip, c), sibling) for j, chip in enumerate(chips)]
        for j, chip in enumerate(chips):
            copy(1 + j, (*chip, c), me).wait_recv()
            passed[j].start()
        copy(0, sibling, me).wait_recv()
        for j, chip in enumerate(chips):
            copy(4 + j, (*chip, 1 - c), me).wait_recv()
        for cp in first + passed:
            cp.wait_send()
        mine.wait()

    return pl.pallas_call(
        body, name=name, out_shape=jax.ShapeDtypeStruct((8, m, n), block.dtype), in_specs=[_ANY], out_specs=_ANY,
        scratch_shapes=[pltpu.SemaphoreType.DMA((7,)), pltpu.SemaphoreType.DMA((7,)), pltpu.SemaphoreType.DMA],
    )(block)


def _gather_shards(shard, name):
    rows, cols = shard.shape
    half = rows // 2

    def body(x_ref, out_ref, send_sems, recv_sems):
        x, y, c = _position()
        sibling = (x, y, 1 - c)
        chips = [(1 - x, y), (x, 1 - y), (1 - x, 1 - y)]

        def part(px, py, pc):
            return out_ref.at[2 * px + py, pl.ds(pc * half, half), :]

        def copy(k, blk, to, src=None):
            return pltpu.make_async_remote_copy(
                src_ref=part(*blk) if src is None else src, dst_ref=part(*blk), send_sem=send_sems.at[k],
                recv_sem=recv_sems.at[k], device_id=to, device_id_type=MESH)

        my_half = x_ref.at[pl.ds(c * half, half), :]
        first = [copy(j, (x, y, c), (*chip, c), src=my_half) for j, chip in enumerate(chips)]
        for cp in first:
            cp.start()
        passed = [copy(3 + j, (*chip, c), sibling) for j, chip in enumerate(chips)]
        for j, chip in enumerate(chips):
            copy(j, (*chip, c), sibling).wait_recv()
            passed[j].start()
        for j, chip in enumerate(chips):
            copy(3 + j, (*chip, 1 - c), sibling).wait_recv()
        for cp in first + passed:
            cp.wait_send()

    return pl.pallas_call(
        body, name=name, out_shape=jax.ShapeDtypeStruct((N_SHARD, rows, cols), shard.dtype), in_specs=[_ANY],
        out_specs=_ANY, scratch_shapes=[pltpu.SemaphoreType.DMA((6,)), pltpu.SemaphoreType.DMA((6,))],
    )(shard)


def _swap_other_half(g, name):
    ns, rows, cols = g.shape
    half = rows // 2

    def body(g_ref, out_ref, send_sem, recv_sem):
        x, y, c = _position()
        cp = pltpu.make_async_remote_copy(
            src_ref=g_ref.at[:, pl.ds((1 - c) * half, half), :], dst_ref=out_ref, send_sem=send_sem,
            recv_sem=recv_sem, device_id=(x, y, 1 - c), device_id_type=MESH)
        cp.start()
        cp.wait()

    return pl.pallas_call(
        body, name=name, out_shape=jax.ShapeDtypeStruct((ns, half, cols), g.dtype), in_specs=[_ANY], out_specs=_ANY,
        scratch_shapes=[pltpu.SemaphoreType.DMA, pltpu.SemaphoreType.DMA],
    )(g)


def _add_own_half(g, r, c_idx, name):
    ns, rows, cols = g.shape
    half = rows // 2
    tr = _row_tile(half, max(8, (1 << 19) // cols))
    nb = half // tr

    def body(c_ref, g_ref, r_ref, o_ref, ob_ref):
        acc = g_ref[...] + r_ref[...]
        o_ref[...] = acc
        ob_ref[...] = acc.astype(BF16)

    out = pl.BlockSpec((None, tr, cols), lambda s, i, c_ref: (s, i, 0))
    return pl.pallas_call(
        body, name=name,
        grid_spec=pltpu.PrefetchScalarGridSpec(
            num_scalar_prefetch=1, grid=(ns, nb),
            in_specs=[pl.BlockSpec((None, tr, cols), lambda s, i, c_ref: (s, c_ref[0] * nb + i, 0)), out],
            out_specs=[out, out]),
        out_shape=[jax.ShapeDtypeStruct((ns, half, cols), F32), jax.ShapeDtypeStruct((ns, half, cols), BF16)],
        compiler_params=_cparams(("parallel", "parallel")),
    )(c_idx, g, r)


def _send_to_chips(a, name):
    ns, half, cols = a.shape

    def body(a_ref, out_ref, send_sems, recv_sems):
        x, y, c = _position()
        chips = [(1 - x, y), (x, 1 - y), (1 - x, 1 - y)]
        cps = [pltpu.make_async_remote_copy(
            src_ref=a_ref.at[2 * px + py], dst_ref=out_ref.at[j], send_sem=send_sems.at[j], recv_sem=recv_sems.at[j],
            device_id=(px, py, c), device_id_type=MESH) for j, (px, py) in enumerate(chips)]
        for cp in cps:
            cp.start()
        for cp in cps:
            cp.wait()

    return pl.pallas_call(
        body, name=name, out_shape=jax.ShapeDtypeStruct((3, half, cols), a.dtype), in_specs=[_ANY], out_specs=_ANY,
        scratch_shapes=[pltpu.SemaphoreType.DMA((3,)), pltpu.SemaphoreType.DMA((3,))],
    )(a)


def _add_arrivals(a, r, s_idx, name):
    ns, half, cols = a.shape
    tr = _row_tile(half, max(8, (1 << 18) // cols))

    def body(s_ref, a_ref, r_ref, o_ref):
        o_ref[...] = ((a_ref[...] + r_ref[0].astype(F32)) + r_ref[1].astype(F32)) + r_ref[2].astype(F32)

    return pl.pallas_call(
        body, name=name,
        grid_spec=pltpu.PrefetchScalarGridSpec(
            num_scalar_prefetch=1, grid=(half // tr,),
            in_specs=[pl.BlockSpec((None, tr, cols), lambda i, s_ref: (s_ref[0], i, 0)),
                      pl.BlockSpec((3, tr, cols), lambda i, s_ref: (0, i, 0))],
            out_specs=pl.BlockSpec((tr, cols), lambda i, s_ref: (i, 0))),
        out_shape=jax.ShapeDtypeStruct((half, cols), F32), compiler_params=_cparams(("parallel",)),
    )(s_idx, a, r)


def _swap_reduced_half(f, name):
    def body(f_ref, out_ref, send_sem, recv_sem):
        x, y, c = _position()
        cp = pltpu.make_async_remote_copy(src_ref=f_ref, dst_ref=out_ref, send_sem=send_sem, recv_sem=recv_sem,
                                          device_id=(x, y, 1 - c), device_id_type=MESH)
        cp.start()
        cp.wait()

    return pl.pallas_call(
        body, name=name, out_shape=jax.ShapeDtypeStruct(f.shape, f.dtype), in_specs=[_ANY], out_specs=_ANY,
        scratch_shapes=[pltpu.SemaphoreType.DMA, pltpu.SemaphoreType.DMA],
    )(f)


def _conv_core(z, conv_w, nctx):
    d = conv_w.shape[-1]
    zf = z.astype(F32)
    b_gate, c_gate, v = zf[..., :d], zf[..., d:2 * d], zf[..., 2 * d:]
    u = c_gate * v

    def conv(s):
        p = jnp.pad(s, ((0, 0), (1, 1), (0, 0)))
        return conv_w[0] * p[:, :-2] + conv_w[1] * p[:, 1:-1] + conv_w[2] * p[:, 2:]

    return b_gate * jnp.concatenate([conv(u[:, :nctx]), conv(u[:, nctx:])], axis=1)


HG_BLOCK = 8
HG_UNROLL = 2
HG_VMEM_LIMIT = 56 * 1024 * 1024


def _hg_cumsum(x, rev):
    n = HG_CHUNK
    nrow = x.shape[0]

    def run(v, backwards):
        pos = lax.broadcasted_iota(jnp.int32, (nrow, 1), 0) % n
        k = 1
        while k < n:
            if backwards:
                v = v + jnp.where(pos + k < n, pltpu.roll(v, nrow - k, axis=0), 0.0)
            else:
                v = v + jnp.where(pos >= k, pltpu.roll(v, k, axis=0), 0.0)
            k *= 2
        return v

    @jax.custom_vjp
    def cs(v):
        return run(v, rev)

    cs.defvjp(lambda v: (run(v, rev), None), lambda _, g: (run(g, not rev),))
    return cs(x)


def _hg_local(q, v, tf, lb, rev):
    n = HG_CHUNK
    nrow = q.shape[0]
    nb = nrow // n
    f = lb + (1.0 - lb) * jax.nn.sigmoid(tf)
    kk = 1.0 - f
    b = _hg_cumsum(jnp.log(f), rev)
    pos = lax.broadcasted_iota(jnp.int32, (nb, n, 1), 1)
    b3 = b.reshape(nb, n, -1)
    mid = n - n // 2 if rev else n // 2 - 1
    last = 0 if rev else n - 1
    b_mid = jnp.sum(jnp.where(pos == mid, b3, 0.0), axis=1, keepdims=True)
    b_last = jnp.sum(jnp.where(pos == last, b3, 0.0), axis=1, keepdims=True)
    q3, k3, v3 = q.reshape(nb, n, -1), kk.reshape(nb, n, -1), v.reshape(nb, n, -1)
    qs = (q3 * jnp.exp(b3 - b_mid)).astype(BF16)
    ks = (k3 * jnp.exp(b_mid - b3)).astype(BF16)
    sc = jnp.einsum('ctk,csk->cts', qs, ks, preferred_element_type=F32)
    row = lax.broadcasted_iota(jnp.int32, (1, n, n), 1)
    col = lax.broadcasted_iota(jnp.int32, (1, n, n), 2)
    sc = jnp.where((col >= row) if rev else (col <= row), sc, 0.0).astype(BF16)
    o_intra = jnp.einsum('cts,csv->ctv', sc, v3.astype(BF16), preferred_element_type=F32)
    qe = q3 * jnp.exp(b3)
    ks2 = k3 * jnp.exp(b_last - b3)
    return o_intra.reshape(nrow, -1), qe.reshape(nrow, -1), ks2.reshape(nrow, -1), jnp.exp(b_last).reshape(nb, -1)


def _hg_scan(nctx):
    hd, n = HG_HEAD_DIM, HG_CHUNK
    rb = HG_BLOCK * n

    def geometry(z):
        bsz, tlen, d5 = z.shape
        return bsz, tlen, d5 // 5, (d5 // 5) // hd, tlen // n, nctx // n

    def rev_chunk(j, nch, ncc):
        return jnp.where(j < ncc, ncc - 1 - j, nch - 1 + ncc - j)

    def rows(c):
        return pl.ds(pl.multiple_of(c * n, n), n)

    seq_params = pltpu.CompilerParams(dimension_semantics=("parallel", "parallel"), vmem_limit_bytes=HG_VMEM_LIMIT)
    nt, tn = (((1,), (1,)), ((), ())), (((0,), (0,)), ((), ()))

    def local_specs(z):
        bsz, tlen, d, nh, nch, ncc = geometry(z)
        col = lambda k: pl.BlockSpec((None, rb, hd), lambda b, h, t: (b, t, k * nh + h))
        blk = pl.BlockSpec((None, rb, hd), lambda b, h, t: (b, t, h))
        dec = pl.BlockSpec((None, HG_BLOCK, hd), lambda b, h, t: (b, t, h))
        lbs = pl.BlockSpec((1, hd), lambda b, h, t: (0, h))
        return [col(0), col(1), col(3), col(4), lbs], blk, dec

    def local_fwd(z, lb):
        bsz, tlen, d, nh, nch, ncc = geometry(z)
        ins, blk, dec = local_specs(z)

        def body(q_ref, v_ref, ff_ref, fb_ref, lb_ref, o_ref, qf_ref, kf_ref, df_ref, qb_ref, kb_ref, db_ref):
            q, v, lbv = q_ref[...].astype(F32), v_ref[...].astype(F32), lb_ref[...]
            of, qe, ks, dc = _hg_local(q, v, ff_ref[...].astype(F32), lbv, False)
            qf_ref[...], kf_ref[...], df_ref[...] = qe.astype(BF16), ks.astype(BF16), dc
            ob, qe, ks, dc = _hg_local(q, v, fb_ref[...].astype(F32), lbv, True)
            qb_ref[...], kb_ref[...], db_ref[...] = qe.astype(BF16), ks.astype(BF16), dc
            o_ref[...] = of + ob

        act = jax.ShapeDtypeStruct((bsz, tlen, d), BF16)
        dcs = jax.ShapeDtypeStruct((bsz, nch, d), F32)
        return pl.pallas_call(
            body, name="hg_local", grid=(bsz, nh, tlen // rb), in_specs=ins,
            out_specs=[blk, blk, blk, dec, blk, blk, dec],
            out_shape=[jax.ShapeDtypeStruct((bsz, tlen, d), F32), act, act, dcs, act, act, dcs],
            compiler_params=_cparams(("parallel", "parallel", "parallel")),
        )(z, z, z, z, lb)

    def local_bwd(z, lb, do, dv_in, dqf, dkf, ddf, dqb, dkb, ddb):
        bsz, tlen, d, nh, nch, ncc = geometry(z)
        ins, blk, dec = local_specs(z)

        def body(q_ref, v_ref, ff_ref, fb_ref, lb_ref, do_ref, dvi_ref, dqf_ref, dkf_ref, ddf_ref, dqb_ref, dkb_ref,
                 ddb_ref, dq_ref, dv_ref, dff_ref, dfb_ref, dlb_ref):
            q, v, lbv = q_ref[...].astype(F32), v_ref[...].astype(F32), lb_ref[...]
            dov = do_ref[...]
            dq, dv, dlb = jnp.zeros_like(q), dvi_ref[...], jnp.zeros_like(lbv)
            for rev, f_ref, df_ref, cts in ((False, ff_ref, dff_ref, (dqf_ref, dkf_ref, ddf_ref)),
                                            (True, fb_ref, dfb_ref, (dqb_ref, dkb_ref, ddb_ref))):
                _, vjp = jax.vjp(functools.partial(_hg_local, rev=rev), q, v, f_ref[...].astype(F32), lbv)
                g = vjp((dov, cts[0][...].astype(F32), cts[1][...].astype(F32), cts[2][...]))
                dq, dv, dlb = dq + g[0], dv + g[1], dlb + g[3]
                df_ref[...] = g[2].astype(df_ref.dtype)
            dq_ref[...] = dq.astype(dq_ref.dtype)
            dv_ref[...] = dv.astype(dv_ref.dtype)

            @pl.when(pl.program_id(2) == 0)
            def _():
                dlb_ref[...] = dlb

            @pl.when(pl.program_id(2) != 0)
            def _():
                dlb_ref[...] += dlb

        act = jax.ShapeDtypeStruct((bsz, tlen, d), BF16)
        return pl.pallas_call(
            body, name="hg_local_bwd", grid=(bsz, nh, tlen // rb),
            in_specs=ins + [blk, blk, blk, blk, dec, blk, blk, dec],
            out_specs=[blk, blk, blk, blk, pl.BlockSpec((None, 1, hd), lambda b, h, t: (b, 0, h))],
            out_shape=[act, act, act, act, jax.ShapeDtypeStruct((bsz, 1, d), F32)],
            compiler_params=_cparams(("parallel", "parallel", "arbitrary")),
        )(z, z, z, z, lb, do, dv_in, dqf, dkf, ddf, dqb, dkb, ddb)

    def head_spec(z, k=None):
        bsz, tlen, d, nh, nch, ncc = geometry(z)
        if k is None:
            return pl.BlockSpec((None, tlen, hd), lambda b, h: (b, 0, h))
        return pl.BlockSpec((None, tlen, hd), lambda b, h: (b, 0, k * nh + h))

    def dec_spec(z):
        bsz, tlen, d, nh, nch, ncc = geometry(z)
        return pl.BlockSpec((None, nch, hd), lambda b, h: (b, 0, h))

    def state_fwd(z, o_in, qf, kf, df, qb, kb, db):
        bsz, tlen, d, nh, nch, ncc = geometry(z)
        hs, ds = head_spec(z), dec_spec(z)

        def body(v_ref, oi_ref, qf_ref, kf_ref, df_ref, qb_ref, kb_ref, db_ref, o_ref):
            o_ref[...] = oi_ref[...]
            chains = ((False, qf_ref, kf_ref, df_ref), (True, qb_ref, kb_ref, db_ref))

            def step(j, carry):
                out = []
                for (rev, q_ref, k_ref, d_ref), st in zip(chains, carry):
                    c = rev_chunk(j, nch, ncc) if rev else j
                    sl = rows(c)
                    o_ref[sl, :] += lax.dot_general(q_ref[sl, :], st.astype(BF16), nt, preferred_element_type=F32)
                    out.append(st * d_ref[pl.ds(c, 1), :] + lax.dot_general(v_ref[sl, :], k_ref[sl, :], tn,
                                                                              preferred_element_type=F32))
                return tuple(out)

            zero = jnp.zeros((hd, hd), F32)
            lax.fori_loop(0, nch, step, (zero, zero), unroll=HG_UNROLL)

        return pl.pallas_call(
            body, name="hg_state", grid=(bsz, nh), in_specs=[head_spec(z, 1), hs, hs, hs, ds, hs, hs, ds],
            out_specs=hs, out_shape=jax.ShapeDtypeStruct((bsz, tlen, d), F32), compiler_params=seq_params,
        )(z, o_in, qf, kf, df, qb, kb, db)

    def state_bwd(z, do, qf, kf, df, qb, kb, db):
        bsz, tlen, d, nh, nch, ncc = geometry(z)
        hs, ds = head_spec(z), dec_spec(z)

        def body(v_ref, do_ref, qf_ref, kf_ref, df_ref, qb_ref, kb_ref, db_ref,
                 dv_ref, dqf_ref, dkf_ref, ddf_ref, dqb_ref, dkb_ref, ddb_ref, stf_ref, stb_ref):
            chains = ((False, qf_ref, kf_ref, df_ref, dqf_ref, dkf_ref, ddf_ref, stf_ref),
                      (True, qb_ref, kb_ref, db_ref, dqb_ref, dkb_ref, ddb_ref, stb_ref))

            def fstep(j, carry):
                out = []
                for (rev, q_ref, k_ref, d_ref, _, _, _, st_ref), st in zip(chains, carry):
                    c = rev_chunk(j, nch, ncc) if rev else j
                    sl = rows(c)
                    st_ref[j] = st
                    out.append(st * d_ref[pl.ds(c, 1), :] + lax.dot_general(v_ref[sl, :], k_ref[sl, :], tn,
                                                                              preferred_element_type=F32))
                return tuple(out)

            zero = jnp.zeros((hd, hd), F32)
            lax.fori_loop(0, nch, fstep, (zero, zero), unroll=HG_UNROLL)
            dv_ref[...] = jnp.zeros_like(dv_ref)

            def bstep(i, carry):
                j = nch - 1 - i
                out = []
                for (rev, q_ref, k_ref, d_ref, dq_ref, dk_ref, dd_ref, st_ref), dst in zip(chains, carry):
                    c = rev_chunk(j, nch, ncc) if rev else j
                    sl = rows(c)
                    st = st_ref[j]
                    dob = do_ref[sl, :].astype(BF16)
                    dstb = dst.astype(BF16)
                    dec = d_ref[pl.ds(c, 1), :]
                    dq_ref[sl, :] = jnp.dot(dob, st.astype(BF16), preferred_element_type=F32).astype(dq_ref.dtype)
                    dk_ref[sl, :] = jnp.dot(v_ref[sl, :], dstb, preferred_element_type=F32).astype(dk_ref.dtype)
                    dv_ref[sl, :] += lax.dot_general(k_ref[sl, :], dstb, nt, preferred_element_type=F32)
                    dd_ref[pl.ds(c, 1), :] = jnp.sum(dst * st, axis=0, keepdims=True)
                    out.append(dst * dec + lax.dot_general(dob, q_ref[sl, :], tn, preferred_element_type=F32))
                return tuple(out)

            lax.fori_loop(0, nch, bstep, (zero, zero), unroll=HG_UNROLL)

        act = jax.ShapeDtypeStruct((bsz, tlen, d), BF16)
        dcs = jax.ShapeDtypeStruct((bsz, nch, d), F32)
        return pl.pallas_call(
            body, name="hg_state_bwd", grid=(bsz, nh), in_specs=[head_spec(z, 1), hs, hs, hs, ds, hs, hs, ds],
            out_specs=[hs, hs, hs, ds, hs, hs, ds],
            out_shape=[jax.ShapeDtypeStruct((bsz, tlen, d), F32), act, act, dcs, act, act, dcs],
            scratch_shapes=[pltpu.VMEM((nch, hd, hd), F32), pltpu.VMEM((nch, hd, hd), F32)],
            compiler_params=seq_params,
        )(z, do, qf, kf, df, qb, kb, db)

    @jax.custom_vjp
    def scan(z, lb):
        return state_fwd(z, *local_fwd(z, lb))

    def scan_fwd(z, lb):
        loc = local_fwd(z, lb)
        return state_fwd(z, *loc), (z, lb, loc[1:])

    def scan_bwd(res, do):
        z, lb, loc = res
        dv_in, *dstate = state_bwd(z, do, *loc)
        dq, dv, dff, dfb, dlb = local_bwd(z, lb, do, dv_in, *dstate)
        dz = jnp.concatenate([dq, dv, jnp.zeros_like(dff), dff, dfb], axis=-1)
        return dz, jnp.sum(dlb, axis=0)

    scan.defvjp(scan_fwd, scan_bwd)
    return scan


def _hg_readout(o, gate, g_norm):
    bsz, tlen, d = gate.shape
    nh = d // HG_HEAD_DIM
    rb = TB * nh

    def fn(x, g, gt):
        return x * lax.rsqrt(jnp.mean(x * x, axis=-1, keepdims=True) + EPS) * gt * (g * jax.nn.sigmoid(g))

    spec = pl.BlockSpec((None, rb, HG_HEAD_DIM), lambda b, t: (b, t, 0))
    gspec = pl.BlockSpec((rb, HG_HEAD_DIM), lambda b, t: (0, 0))
    grid = (bsz, tlen // TB)

    def fwd_call(x, g, gt):
        def body(x_ref, g_ref, gt_ref, y_ref):
            y_ref[...] = fn(x_ref[...], g_ref[...].astype(F32), gt_ref[...]).astype(y_ref.dtype)

        return pl.pallas_call(
            body, name="hg_readout_fwd", grid=grid, in_specs=[spec, spec, gspec], out_specs=spec,
            out_shape=jax.ShapeDtypeStruct(g.shape, BF16), compiler_params=_cparams(("parallel", "parallel")),
        )(x, g, gt)

    def bwd_call(x, g, gt, dy):
        def body(x_ref, g_ref, gt_ref, dy_ref, dx_ref, dg_ref, dgt_ref):
            _, vjp = jax.vjp(fn, x_ref[...], g_ref[...].astype(F32), gt_ref[...])
            dx, dg, dgt = vjp(dy_ref[...].astype(F32))
            dx_ref[...] = dx
            dg_ref[...] = dg.astype(dg_ref.dtype)

            @pl.when((pl.program_id(0) == 0) & (pl.program_id(1) == 0))
            def _():
                dgt_ref[...] = jnp.zeros_like(dgt_ref)

            dgt_ref[...] += dgt

        return pl.pallas_call(
            body, name="hg_readout_bwd", grid=grid, in_specs=[spec, spec, gspec, spec],
            out_specs=[spec, spec, gspec],
            out_shape=[jax.ShapeDtypeStruct(x.shape, F32), jax.ShapeDtypeStruct(g.shape, BF16),
                       jax.ShapeDtypeStruct(gt.shape, F32)],
            compiler_params=_cparams(("arbitrary", "arbitrary")),
        )(x, g, gt, dy)

    @jax.custom_vjp
    def op(x, g, gt):
        return fwd_call(x, g, gt)

    def op_fwd(x, g, gt):
        return fwd_call(x, g, gt), (x, g, gt)

    def op_bwd(res, dy):
        return tuple(bwd_call(*res, dy))

    op.defvjp(op_fwd, op_bwd)
    heads = lambda t: t.reshape(bsz, tlen * nh, HG_HEAD_DIM)
    gt = jnp.tile(g_norm.reshape(nh, HG_HEAD_DIM), (TB, 1))
    return op(heads(o), heads(gate), gt).reshape(bsz, tlen, d)


def _hgrn_core(z, lower_bound, g_norm, nctx):
    d = g_norm.shape[-1]
    o = _hg_scan(nctx)(z, lower_bound.reshape(1, d))
    return _hg_readout(o, z[..., 2 * d:3 * d], g_norm)


S5_LC = 16


def _s5_mats(lam_re, lam_im, log_dt, b_re, b_im, c_re, c_im, rev):
    hi = lax.Precision.HIGHEST
    ng, ns = lam_re.shape
    lc, gs = S5_LC, S5_GROUP
    lam_re = jnp.minimum(lam_re, -1e-4)
    dt = jnp.exp(log_dt)[:, None]
    k = jnp.arange(lc + 1, dtype=F32)[:, None, None]
    mag, ang = jnp.exp(lam_re * dt * k), lam_im * dt * k
    p_re, p_im = mag * jnp.cos(ang), mag * jnp.sin(ang)
    a_re, a_im = p_re[1], p_im[1]
    den = lam_re * lam_re + lam_im * lam_im
    f_re = ((a_re - 1) * lam_re + a_im * lam_im) / den
    f_im = (a_im * lam_re - (a_re - 1) * lam_im) / den
    bb_re = f_re[..., None] * b_re - f_im[..., None] * b_im
    bb_im = f_re[..., None] * b_im + f_im[..., None] * b_re
    w_re = c_re[None] * p_re[:, :, None, :] - c_im[None] * p_im[:, :, None, :]
    w_im = c_re[None] * p_im[:, :, None, :] + c_im[None] * p_re[:, :, None, :]
    kk = (jnp.einsum('kgcn,gnd->kgcd', w_re[:lc], bb_re, precision=hi)
          - jnp.einsum('kgcn,gnd->kgcd', w_im[:lc], bb_im, precision=hi))
    t = jnp.arange(lc)
    lag = t[None, :] - t[:, None] if rev else t[:, None] - t[None, :]
    m = jnp.where((lag >= 0)[:, :, None, None, None], kk[jnp.maximum(lag, 0)], 0.0)
    mt = m.transpose(2, 1, 4, 0, 3).reshape(ng, lc * gs, lc * gs)
    left = t if rev else lc - 1 - t
    pw_re, pw_im = p_re[left], p_im[left]
    pr = pw_re[..., None] * bb_re[None] - pw_im[..., None] * bb_im[None]
    pi = pw_re[..., None] * bb_im[None] + pw_im[..., None] * bb_re[None]
    pt = jnp.concatenate([pr, pi], axis=2).transpose(1, 0, 3, 2).reshape(ng, lc * gs, 2 * ns)
    since = lc - t if rev else t + 1
    q = jnp.concatenate([w_re[since], -w_im[since]], axis=-1)
    qt = q.transpose(1, 3, 0, 2).reshape(ng, 2 * ns, lc * gs)
    a16 = jnp.concatenate([p_re[lc], p_im[lc]], axis=-1)
    return mt, pt, qt, a16


def _s5_bmm(terms, out_dtype, name, sum_dirs=False):
    ng = terms[0][0].shape[-3]
    ops, dlist = [], []
    for a, b, dn in terms:
        ops += [a, b]
        dlist.append(dn)
    (ca,), (cb,) = dlist[0]
    om, on = terms[0][0].shape[-2:][1 - ca], terms[0][1].shape[-2:][1 - cb]

    def spec(o):
        if o.ndim == 4:
            return pl.BlockSpec((None, None) + o.shape[2:], lambda g, d: (d, g, 0, 0))
        return pl.BlockSpec((None,) + o.shape[1:], lambda g, d: (g, 0, 0))

    def body(*refs):
        acc = None
        for j, dn in enumerate(dlist):
            a, b = refs[2 * j][...].astype(BF16), refs[2 * j + 1][...].astype(BF16)
            r = lax.dot_general(a, b, (dn, ((), ())), preferred_element_type=F32)
            acc = r if acc is None else acc + r
        o_ref = refs[-1]
        if sum_dirs:
            @pl.when(pl.program_id(1) == 0)
            def _():
                o_ref[...] = acc.astype(o_ref.dtype)

            @pl.when(pl.program_id(1) != 0)
            def _():
                o_ref[...] = (o_ref[...].astype(F32) + acc).astype(o_ref.dtype)
        else:
            o_ref[...] = acc.astype(o_ref.dtype)

    if sum_dirs:
        out_spec = pl.BlockSpec((None, om, on), lambda g, d: (g, 0, 0))
        out_shape = jax.ShapeDtypeStruct((ng, om, on), out_dtype)
    else:
        out_spec = pl.BlockSpec((None, None, om, on), lambda g, d: (d, g, 0, 0))
        out_shape = jax.ShapeDtypeStruct((2, ng, om, on), out_dtype)
    return pl.pallas_call(
        body, name=name, grid=(ng, 2), in_specs=[spec(o) for o in ops], out_specs=out_spec, out_shape=out_shape,
        compiler_params=_cparams(("parallel", "arbitrary" if sum_dirs else "parallel")),
    )(*ops)


def _s5_row_block(rows):
    return 32 if rows % 32 == 0 else rows


def _s5_chunk_order(j, d, nc, ncc):
    return jnp.where(d == 0, j, jnp.where(j < ncc, ncc - 1 - j, nc - 1 + ncc - j))


def _s5_scan_fwd(z, a1, a2, ncc, name):
    nd, nc, rows, lanes = z.shape
    rb = _s5_row_block(rows)

    def body(z_ref, a1_ref, a2_ref, x_ref):
        a1v, a2v = a1_ref[...], a2_ref[...]
        d = pl.program_id(0)

        def step(j, x):
            c = _s5_chunk_order(j, d, nc, ncc)
            x_ref[c] = x
            return a1v * x + a2v * pltpu.roll(x, lanes // 2, axis=1) + z_ref[c]

        lax.fori_loop(0, nc, step, jnp.zeros((rb, lanes), F32))

    blk = pl.BlockSpec((None, nc, rb, lanes), lambda d, r: (d, 0, r, 0))
    par = pl.BlockSpec((None, rb, lanes), lambda d, r: (d, r, 0))
    return pl.pallas_call(
        body, name=name, grid=(nd, rows // rb), in_specs=[blk, par, par], out_specs=blk,
        out_shape=jax.ShapeDtypeStruct(z.shape, F32), compiler_params=_cparams(("parallel", "parallel")),
    )(z, a1, a2)


def _s5_scan_bwd(dxp, xp, a1, a2b, ncc, name):
    nd, nc, rows, lanes = dxp.shape
    rb = _s5_row_block(rows)

    def body(dxp_ref, xp_ref, a1_ref, a2_ref, dz_ref, p1_ref, p2_ref):
        a1v, a2v = a1_ref[...], a2_ref[...]
        zero = jnp.zeros((rb, lanes), F32)
        d = pl.program_id(0)

        def step(i, carry):
            g_next, nxt, p1, p2 = carry
            c = _s5_chunk_order(nc - 1 - i, d, nc, ncc)
            g = nxt + a1v * g_next + a2v * pltpu.roll(g_next, lanes // 2, axis=1)
            dz_ref[c] = g
            x = xp_ref[c]
            return g, dxp_ref[c], p1 + x * g, p2 + pltpu.roll(x, lanes // 2, axis=1) * g

        _, _, p1, p2 = lax.fori_loop(0, nc, step, (zero, zero, zero, zero))
        p1_ref[...] = p1
        p2_ref[...] = p2

    blk = pl.BlockSpec((None, nc, rb, lanes), lambda d, r: (d, 0, r, 0))
    par = pl.BlockSpec((None, rb, lanes), lambda d, r: (d, r, 0))
    return pl.pallas_call(
        body, name=name, grid=(nd, rows // rb), in_specs=[blk, blk, par, par], out_specs=[blk, par, par],
        out_shape=[jax.ShapeDtypeStruct(dxp.shape, F32), jax.ShapeDtypeStruct((nd, rows, lanes), F32),
                   jax.ShapeDtypeStruct((nd, rows, lanes), F32)],
        compiler_params=_cparams(("parallel", "parallel")),
    )(dxp, xp, a1, a2b)


def _s5_rows(t, bsz):
    nd, ng, m, k = t.shape
    return t.reshape(nd, ng, bsz, m // bsz, k).transpose(0, 3, 2, 1, 4).reshape(nd, m // bsz, bsz * ng, k)


def _s5_groups(t, bsz):
    nd, nc, rows, k = t.shape
    return t.reshape(nd, nc, bsz, rows // bsz, k).transpose(0, 3, 2, 1, 4).reshape(nd, rows // bsz, bsz * nc, k)


def _s5_coeffs(a16, bsz):
    half = a16.shape[-1] // 2
    re, im = a16[..., :half], a16[..., half:]
    tile = lambda v: jnp.tile(v, (1, bsz, 1))
    return tile(jnp.concatenate([re, re], -1)), tile(jnp.concatenate([-im, im], -1)), tile(jnp.concatenate([im, -im], -1))


def _s5_apply(bsz, ncc):
    nn, nt, tn = ((1,), (0,)), ((1,), (1,)), ((0,), (0,))

    def run(u, mt, pt, qt, a16):
        a1, a2, _ = _s5_coeffs(a16, bsz)
        z = _s5_bmm([(u, pt, nn)], F32, "s5_z")
        xp = _s5_scan_fwd(_s5_rows(z, bsz), a1, a2, ncc, "s5_scan")
        xg = _s5_groups(xp, bsz).astype(BF16)
        y = _s5_bmm([(u, mt, nn), (xg, qt, nn)], F32, "s5_y", sum_dirs=True)
        return y, (xp, xg)

    @jax.custom_vjp
    def apply(u, mt, pt, qt, a16):
        return run(u, mt.astype(BF16), pt.astype(BF16), qt.astype(BF16), a16)[0]

    def apply_fwd(u, mt, pt, qt, a16):
        mtb, ptb, qtb = mt.astype(BF16), pt.astype(BF16), qt.astype(BF16)
        y, (xp, xg) = run(u, mtb, ptb, qtb, a16)
        return y, (u, mtb, ptb, qtb, a16, xp, xg)

    def apply_bwd(res, dy):
        u, mtb, ptb, qtb, a16, xp, xg = res
        a1, _, a2b = _s5_coeffs(a16, bsz)
        dyb = dy.astype(BF16)
        dmt = _s5_bmm([(u, dyb, tn)], F32, "s5_dmt", sum_dirs=True) * 0.5
        dqt = _s5_bmm([(xg, dyb, tn)], F32, "s5_dqt")
        dxp = _s5_bmm([(dyb, qtb, nt)], F32, "s5_dxp")
        dz, p1, p2 = _s5_scan_bwd(_s5_rows(dxp, bsz), xp, a1, a2b, ncc, "s5_scan_bwd")
        dzg = _s5_groups(dz, bsz).astype(BF16)
        dpt = _s5_bmm([(u, dzg, tn)], F32, "s5_dpt")
        du = _s5_bmm([(dyb, mtb, nt), (dzg, ptb, nt)], BF16, "s5_du", sum_dirs=True)
        half = a16.shape[-1] // 2
        p1 = jnp.sum(p1.reshape(2, bsz, -1, 2 * half), axis=1)
        p2 = jnp.sum(p2.reshape(2, bsz, -1, 2 * half), axis=1)
        da16 = jnp.concatenate([p1[..., :half] + p1[..., half:], p2[..., half:] - p2[..., :half]], axis=-1)
        return du, jnp.stack([dmt, dmt]), dpt, dqt, da16

    apply.defvjp(apply_fwd, apply_bwd)
    return apply


def _s5_core(a, p, nctx):
    bsz, tlen, d = a.shape
    ng, lc, gs = d // S5_GROUP, S5_LC, S5_GROUP
    mats = [_s5_mats(p["s5_lam_re"][k], p["s5_lam_im"][k], p["s5_log_dt"][k], p["s5_b_re"], p["s5_b_im"],
                     p["s5_c_re"][k], p["s5_c_im"][k], k == 1) for k in range(2)]
    mt, pt, qt, a16 = (jnp.stack([mats[0][j], mats[1][j]]) for j in range(4))
    u = a.reshape(bsz, tlen // lc, lc, ng, gs).transpose(3, 0, 1, 2, 4).reshape(ng, bsz * (tlen // lc), lc * gs)
    y = _s5_apply(bsz, nctx // lc)(u, mt, pt, qt, a16)
    y = y.reshape(ng, bsz, tlen // lc, lc, gs).transpose(1, 2, 3, 0, 4).reshape(bsz, tlen, d)
    return jax.nn.gelu(p["s5_d"] * a.astype(F32) + y)


NA_LANES = 256
NA_MASKED = -1e30


def _na_tables(rpb):
    hi = lax.Precision.HIGHEST
    nh = rpb.shape[0]
    q = jnp.arange(GRID_W)
    kc = jnp.arange(GRID_W)
    q_start = jnp.clip(q - NA_COLS // 2, 0, GRID_W - NA_COLS)
    inwin = (kc[None, :] >= q_start[:, None]) & (kc[None, :] < q_start[:, None] + NA_COLS)
    dc = kc[None, :] - q[:, None] + NA_COLS - 1
    onehot = ((dc[:, :, None] == jnp.arange(2 * NA_COLS - 1)) & inwin[:, :, None]).astype(F32)
    a = jnp.arange(NA_ROWS)[None, :] - jnp.arange(NA_ROWS)[:, None] + NA_ROWS - 1
    tab = jnp.einsum('hskc,qlc->hsqkl', rpb[:, a, :], onehot, precision=hi)
    tab = jnp.where(inwin[None, None, :, None, :], tab, NA_MASKED)
    return tab.reshape(nh, NA_ROWS, GRID_W, NA_ROWS * GRID_W)


def _na_math(q2, kw, vw, kc, vc, bias, dh):
    scale = dh ** -0.5
    nq, nhb = q2.shape[0], NA_LANES // dh
    lane_head = lax.broadcasted_iota(jnp.int32, (1, NA_LANES), 1) // dh
    nt = (((1,), (1,)), ((), ()))
    kwb, vwb, kcb, vcb = (t.astype(BF16) for t in (kw, vw, kc, vc))
    qs = jnp.concatenate([jnp.where(lane_head == j, q2, 0.0) for j in range(nhb)], axis=0).astype(BF16)
    s_loc = lax.dot_general(qs, kwb, nt, preferred_element_type=F32) * scale + bias.reshape(nhb * nq, -1)
    s_ctx = lax.dot_general(qs, kcb, nt, preferred_element_type=F32) * scale
    m = jnp.maximum(jnp.max(s_loc, axis=-1, keepdims=True), jnp.max(s_ctx, axis=-1, keepdims=True))
    m = lax.stop_gradient(m)
    p_loc, p_ctx = jnp.exp(s_loc - m), jnp.exp(s_ctx - m)
    inv = 1.0 / (jnp.sum(p_loc, axis=-1, keepdims=True) + jnp.sum(p_ctx, axis=-1, keepdims=True))
    o_all = (jnp.dot((p_loc * inv).astype(BF16), vwb, preferred_element_type=F32)
             + jnp.dot((p_ctx * inv).astype(BF16), vcb, preferred_element_type=F32))
    out = jnp.zeros(q2.shape, F32)
    for j in range(nhb):
        out = out + jnp.where(lane_head == j, o_all[j * nq:(j + 1) * nq], 0.0)
    return out


def _na_attention(nctx, nh):
    def geometry(z):
        bsz, tlen, d3 = z.shape
        d = d3 // 3
        rows = (tlen - nctx) // GRID_W
        return bsz, tlen, d, rows, d // nh, d // NA_LANES

    def key_row0(r, rows):
        return jnp.clip(r - NA_ROWS // 2, 0, rows - NA_ROWS)

    def specs(z):
        bsz, tlen, d, rows, dh, nlb = geometry(z)
        hpb = NA_LANES // dh
        qs = pl.BlockSpec((None, GRID_W, NA_LANES), lambda b, h, r: (b, nctx // GRID_W + r, h))
        ks = pl.BlockSpec((None, tlen, NA_LANES), lambda b, h, r: (b, 0, nlb + h))
        vs = pl.BlockSpec((None, tlen, NA_LANES), lambda b, h, r: (b, 0, 2 * nlb + h))
        bs = pl.BlockSpec((hpb, None, GRID_W, NA_ROWS * GRID_W), lambda b, h, r: (h, r - key_row0(r, rows), 0, 0))
        os_ = pl.BlockSpec((None, GRID_W, NA_LANES), lambda b, h, r: (b, r, h))
        return qs, ks, vs, bs, os_

    def window(r, rows):
        return pl.ds(pl.multiple_of(nctx + key_row0(r, rows) * GRID_W, GRID_W), NA_ROWS * GRID_W)

    def fwd_call(z, bias):
        bsz, tlen, d, rows, dh, nlb = geometry(z)
        qs, ks, vs, bs, os_ = specs(z)

        def body(q_ref, k_ref, v_ref, b_ref, o_ref):
            win = window(pl.program_id(2), rows)
            o_ref[...] = _na_math(q_ref[...].astype(F32), k_ref[win, :], v_ref[win, :], k_ref[0:nctx, :],
                                  v_ref[0:nctx, :], b_ref[...], dh).astype(o_ref.dtype)

        return pl.pallas_call(
            body, name="na_fwd", grid=(bsz, nlb, rows), in_specs=[qs, ks, vs, bs], out_specs=os_,
            out_shape=jax.ShapeDtypeStruct((bsz, tlen - nctx, d), BF16),
            compiler_params=_cparams(("parallel", "parallel", "arbitrary")),
        )(z, z, z, bias)

    def bwd_call(z, bias, do):
        bsz, tlen, d, rows, dh, nlb = geometry(z)
        hpb = NA_LANES // dh
        qs, ks, vs, bs, os_ = specs(z)

        def body(q_ref, k_ref, v_ref, b_ref, do_ref, dq_ref, dk_ref, dv_ref, db_ref):
            r = pl.program_id(2)
            win = window(r, rows)

            @pl.when(r == 0)
            def _():
                dk_ref[...] = jnp.zeros_like(dk_ref)
                dv_ref[...] = jnp.zeros_like(dv_ref)

            prim = (q_ref[...].astype(F32), k_ref[win, :].astype(F32), v_ref[win, :].astype(F32),
                    k_ref[0:nctx, :].astype(F32), v_ref[0:nctx, :].astype(F32), b_ref[...])
            _, vjp = jax.vjp(functools.partial(_na_math, dh=dh), *prim)
            dq, dkw, dvw, dkc, dvc, db = vjp(do_ref[...].astype(F32))
            dq_ref[...] = dq.astype(dq_ref.dtype)
            dk_ref[win, :] += dkw
            dv_ref[win, :] += dvw
            dk_ref[0:nctx, :] += dkc
            dv_ref[0:nctx, :] += dvc
            prev = jnp.maximum(r - 1, 0)
            first = (r == 0) | ((r - key_row0(r, rows)) != (prev - key_row0(prev, rows)))

            @pl.when(first)
            def _():
                db_ref[...] = db

            @pl.when(jnp.logical_not(first))
            def _():
                db_ref[...] += db

        acc = pl.BlockSpec((None, tlen, NA_LANES), lambda b, h, r: (b, 0, h))
        dbs = pl.BlockSpec((None, hpb, None, GRID_W, NA_ROWS * GRID_W),
                           lambda b, h, r: (b, h, r - key_row0(r, rows), 0, 0))
        return pl.pallas_call(
            body, name="na_bwd", grid=(bsz, nlb, rows), in_specs=[qs, ks, vs, bs, os_],
            out_specs=[os_, acc, acc, dbs],
            out_shape=[jax.ShapeDtypeStruct((bsz, tlen - nctx, d), BF16), jax.ShapeDtypeStruct((bsz, tlen, d), F32),
                       jax.ShapeDtypeStruct((bsz, tlen, d), F32), jax.ShapeDtypeStruct((bsz,) + bias.shape, F32)],
            compiler_params=_cparams(("parallel", "parallel", "arbitrary")),
        )(z, z, z, bias, do)

    @jax.custom_vjp
    def attend(z, bias):
        return fwd_call(z, bias)

    def attend_fwd(z, bias):
        return fwd_call(z, bias), (z, bias)

    def attend_bwd(res, do):
        z, bias = res
        dq, dk, dv, db = bwd_call(z, bias, do)
        dq = jnp.pad(dq, ((0, 0), (nctx, 0), (0, 0)))
        dz = jnp.concatenate([dq, dk.astype(BF16), dv.astype(BF16)], axis=-1)
        return dz, _sum_leading(db.reshape(db.shape[0], -1, NA_ROWS * GRID_W), "na_sum_dbias").reshape(bias.shape)

    attend.defvjp(attend_fwd, attend_bwd)
    return attend


def _na_core(z, rpb, nctx):
    o = _na_attention(nctx, rpb.shape[0])(z, _na_tables(rpb))
    return jnp.pad(o, ((0, 0), (nctx, 0), (0, 0)))


def _forward(x, ctx, mod, p):
    nctx = ctx.shape[1]
    h = jnp.concatenate([ctx, x], axis=1)
    lb_all = jnp.cumsum(jax.nn.softmax(p["hg_lower_bound"], axis=0), axis=0)
    lb_all = lb_all - lb_all[0]
    gains = p["norm_gains"]
    pre = _rowwise(_fn_pre, [BF16], "pre", 2)
    (a,) = pre([h], gains[0, 0:1], mod[0][:, :, 0:2])
    for i in range(DEPTH):
        tag = f"l{i}"
        if i == 0:
            z = _linear(False, tag + "_sc_in")(a, p["sc_w_in"][0])
            yc = _conv_core(z, p["sc_conv"][0], nctx)
            y = _linear(False, tag + "_sc_out")(yc.astype(BF16), p["sc_w_out"][0])
        elif i == 1:
            z = _linear(False, tag + "_hg_in")(a, p["hg_w_in"][0])
            yc = _hgrn_core(z, lb_all[i], p["hg_norm"][0], nctx)
            y = _linear(False, tag + "_hg_out")(yc, p["hg_w_out"][0])
        elif i == 2:
            sp = {k: v[0] for k, v in p.items() if k.startswith("s5_") and k != "s5_w_glu"}
            gz = _s5_core(a, sp, nctx)
            vg = _linear(False, tag + "_s5_glu")(gz.astype(BF16), p["s5_w_glu"][0]).astype(F32)
            d = gz.shape[-1]
            y = (vg[..., :d] * jax.nn.sigmoid(vg[..., d:])).astype(BF16)
        else:
            z = _linear(False, tag + "_na_qkv")(a, p["na_w_qkv"][0])
            yc = _na_core(z, p["na_rpb"][0], nctx)
            y = _linear(False, tag + "_na_out")(yc, p["na_w_out"][0])
        h, a2 = _rowwise(_fn_post_pre, [F32, BF16], tag + "_mix_post", 2)(
            [h, y], gains[i, 1:3], mod[i][:, :, 2:5])
        u = _linear(False, tag + "_mlp_in")(a2, p["mlp_w_in"][i])
        f = _linear(True, tag + "_mlp_out")(u, p["mlp_w_out"][i])
        if i + 1 < DEPTH:
            cols = jnp.stack([gains[i, 3], gains[i + 1, 0]])
            segs = jnp.concatenate([mod[i][:, :, 5:6], mod[i + 1][:, :, 0:2]], axis=2)
            h, a = _rowwise(_fn_post_pre, [F32, BF16], tag + "_mlp_post", 2)([h, f], cols, segs)
        else:
            (h,) = _rowwise(_fn_post, [F32], tag + "_mlp_post", 2)([h, f], gains[i, 3:4], mod[i][:, :, 5:6])
    return h[:, nctx:]


def _local_step(x, ctx, tgt, mod, p):
    y, vjp = jax.vjp(lambda x_, mod_, p_: _forward(x_, ctx, mod_, p_), x, mod, p)
    lblk, dy = _loss_head(y, tgt)
    gx, dmod, gp = vjp(dy)
    return lblk, gx, dmod, gp


PACK_COLS = 512


def _pack_shard(ws):
    return jnp.concatenate([ws[n].reshape(-1, PACK_COLS) for n in BIG], axis=0)


def _unpack_full(buf, shard_shapes):
    out, r = {}, 0
    for n in BIG:
        s = shard_shapes[n]
        nr = math.prod(s) // PACK_COLS
        parts = buf[:, r:r + nr].reshape((N_SHARD,) + s)
        axis = 1 if n in COL_SHARDED else 0
        out[n] = [jnp.concatenate([parts[k, l] for k in range(N_SHARD)], axis=axis) for l in range(s[0])]
        r += nr
    return out


def _pack_grads(gp, shard_shapes):
    per = []
    for k in range(N_SHARD):
        rows = []
        for n in BIG:
            s = shard_shapes[n]
            axis = 1 if n in COL_SHARDED else 0
            width = s[1 + axis]
            for g in gp[n]:
                rows.append(lax.slice_in_dim(g, k * width, (k + 1) * width, axis=axis).reshape(-1, PACK_COLS))
        per.append(jnp.concatenate(rows, axis=0))
    return jnp.stack(per)


def _unpack_shard(buf, shard_shapes):
    out, r = {}, 0
    for n in BIG:
        s = shard_shapes[n]
        nr = math.prod(s) // PACK_COLS
        out[n] = buf[r:r + nr].reshape(s)
        r += nr
    return out


def _shard_cols(a, k, width):
    return lax.dynamic_slice_in_dim(a, k * width, width, axis=a.ndim - 1)


def kernel(x, c, ctx, c_ctx, ada_w, ada_b, norm_gains, mlp_w_in, mlp_w_out, sc_w_in, sc_conv, sc_w_out, hg_w_in, hg_lower_bound, hg_norm, hg_w_out, s5_lam_re, s5_lam_im, s5_log_dt, s5_b_re, s5_b_im, s5_c_re, s5_c_im, s5_d, s5_w_glu, na_w_qkv, na_rpb, na_w_out, loss_target, m_c_ctx, m_ada_w, m_ada_b, m_norm_gains, m_mlp_w_in, m_mlp_w_out, m_sc_w_in, m_sc_conv, m_sc_w_out, m_hg_w_in, m_hg_lower_bound, m_hg_norm, m_hg_w_out, m_s5_lam_re, m_s5_lam_im, m_s5_log_dt, m_s5_b_re, m_s5_b_im, m_s5_c_re, m_s5_c_im, m_s5_d, m_s5_w_glu, m_na_w_qkv, m_na_rpb, m_na_w_out, v_c_ctx, v_ada_w, v_ada_b, v_norm_gains, v_mlp_w_in, v_mlp_w_out, v_sc_w_in, v_sc_conv, v_sc_w_out, v_hg_w_in, v_hg_lower_bound, v_hg_norm, v_hg_w_out, v_s5_lam_re, v_s5_lam_im, v_s5_log_dt, v_s5_b_re, v_s5_b_im, v_s5_c_re, v_s5_c_im, v_s5_d, v_s5_w_glu, v_na_w_qkv, v_na_rpb, v_na_w_out):
    w = dict(c_ctx=c_ctx, ada_w=ada_w, ada_b=ada_b, norm_gains=norm_gains, mlp_w_in=mlp_w_in, mlp_w_out=mlp_w_out,
             sc_w_in=sc_w_in, sc_conv=sc_conv, sc_w_out=sc_w_out, hg_w_in=hg_w_in, hg_lower_bound=hg_lower_bound,
             hg_norm=hg_norm, hg_w_out=hg_w_out, s5_lam_re=s5_lam_re, s5_lam_im=s5_lam_im, s5_log_dt=s5_log_dt,
             s5_b_re=s5_b_re, s5_b_im=s5_b_im, s5_c_re=s5_c_re, s5_c_im=s5_c_im, s5_d=s5_d, s5_w_glu=s5_w_glu,
             na_w_qkv=na_w_qkv, na_rpb=na_rpb, na_w_out=na_w_out)
    mom_m = dict(zip(WEIGHTS, [m_c_ctx, m_ada_w, m_ada_b, m_norm_gains, m_mlp_w_in, m_mlp_w_out, m_sc_w_in, m_sc_conv,
                               m_sc_w_out, m_hg_w_in, m_hg_lower_bound, m_hg_norm, m_hg_w_out, m_s5_lam_re, m_s5_lam_im,
                               m_s5_log_dt, m_s5_b_re, m_s5_b_im, m_s5_c_re, m_s5_c_im, m_s5_d, m_s5_w_glu, m_na_w_qkv,
                               m_na_rpb, m_na_w_out]))
    mom_v = dict(zip(WEIGHTS, [v_c_ctx, v_ada_w, v_ada_b, v_norm_gains, v_mlp_w_in, v_mlp_w_out, v_sc_w_in, v_sc_conv,
                               v_sc_w_out, v_hg_w_in, v_hg_lower_bound, v_hg_norm, v_hg_w_out, v_s5_lam_re, v_s5_lam_im,
                               v_s5_log_dt, v_s5_b_re, v_s5_b_im, v_s5_c_re, v_s5_c_im, v_s5_d, v_s5_w_glu, v_na_w_qkv,
                               v_na_rpb, v_na_w_out]))
    bsz, _, d = x.shape
    ax, ay, ac = lax.axis_index("x"), lax.axis_index("y"), lax.axis_index("c")
    chip = 2 * ax + ay
    dev = 2 * chip + ac
    n_dev = 2 * N_SHARD
    dsh = d // N_SHARD

    shard_shapes = {n: w[n].shape for n in BIG}
    own = _pack_shard({n: w[n].astype(BF16) for n in BIG})
    packed = lax.dynamic_update_slice(_gather_shards(own, "gather_weights"), own[None], (chip, 0, 0))
    full = {n: [a.astype(F32) for a in v] for n, v in _unpack_full(packed, shard_shapes).items()}

    small_shapes = [c.shape] + [w[n].shape for n in SMALL_SHARDED]
    buf_a = _all_gather8(_pad_rows(_pack_rows([c] + [w[n] for n in SMALL_SHARDED])), "gather_small")
    per_dev = [_unpack_rows(buf_a[k], small_shapes) for k in range(n_dev)]
    c_all = jnp.concatenate([per_dev[k][0] for k in range(n_dev)], axis=0)
    for j, n in enumerate(SMALL_SHARDED):
        full[n] = jnp.concatenate([per_dev[2 * s][1 + j] for s in range(N_SHARD)], axis=-1)
    for n in SMALL_REPL:
        full[n] = w[n]

    n_all = c_all.shape[0]
    s_rows = 32
    cond = jnp.concatenate([c_all, c_ctx[None]], axis=0)
    s_all = jnp.pad(jax.nn.silu(cond), ((0, s_rows - n_all - 1), (0, 0))).astype(BF16)
    mod_part = _ada_fwd(s_all, ada_w)
    nsh = mod_part.shape[-1]
    buf_b = _all_gather8(_pad_rows(mod_part.reshape(-1, LANES)), "gather_mod")
    nrow_b = mod_part.size // LANES
    mod_raw = jnp.concatenate([buf_b[2 * s, :nrow_b].reshape(mod_part.shape) for s in range(N_SHARD)], axis=-1)
    mod_raw = mod_raw + ada_b[:, None, :]
    mod_lat = lax.dynamic_slice_in_dim(mod_raw, dev * bsz, bsz, axis=1).reshape(DEPTH, bsz, 1, N_MOD, d)
    mod_ctx = jnp.broadcast_to(mod_raw[:, n_all].reshape(DEPTH, 1, 1, N_MOD, d), (DEPTH, bsz, 1, N_MOD, d))
    mod = jnp.concatenate([mod_ctx, mod_lat], axis=2)

    lblk, grad_x, dmod, gp = _local_step(x, ctx, loss_target, mod, full)

    dmod_rows = jnp.concatenate([dmod[:, :, 1].reshape(DEPTH, bsz, N_MOD * d),
                                 jnp.sum(dmod[:, :, 0], axis=1).reshape(DEPTH, 1, N_MOD * d)], axis=1)
    c_list = [dmod_rows] + [gp[n] for n in SMALL_SHARDED + SMALL_REPL] + [jnp.sum(lblk).reshape(1)]
    c_shapes = [a.shape for a in c_list]
    buf_c = _all_gather8(_pad_rows(_pack_rows(c_list)), "gather_grads")
    sum_c = _sum_leading(buf_c, "sum_grads")
    summed = _unpack_rows(sum_c, c_shapes)
    grads = {}
    for j, n in enumerate(SMALL_SHARDED):
        grads[n] = _shard_cols(summed[1 + j], chip, w[n].shape[-1])
    for j, n in enumerate(SMALL_REPL):
        grads[n] = summed[1 + len(SMALL_SHARDED) + j]
    loss = summed[-1][0]

    dmod_dev = [_unpack_rows(buf_c[k], c_shapes[:1])[0] for k in range(n_dev)]
    dm_lat = jnp.concatenate([t[:, :bsz] for t in dmod_dev], axis=1)
    dm_ctx = summed[0][:, bsz:bsz + 1]
    dm_all = jnp.concatenate([dm_lat, dm_ctx], axis=1)
    grads["ada_b"] = _sum_leading(jnp.moveaxis(dm_all, 1, 0).reshape(n_all + 1, -1, LANES), "sum_ada_b").reshape(ada_b.shape)
    dm_sh = jnp.pad(_shard_cols(dm_all, chip, nsh), ((0, 0), (0, s_rows - n_all - 1), (0, 0)))
    grads["ada_w"] = _ada_dw(s_all, dm_sh)
    ds_part = _ada_ds(dm_sh, ada_w)
    buf_d = _all_gather8(_pad_rows(ds_part[n_all:n_all + 1]), "gather_dcond")
    ds_ctx = _sum_leading(jnp.stack([buf_d[2 * s] for s in range(N_SHARD)]), "sum_dcond")[0]
    grads["c_ctx"] = jax.vjp(jax.nn.silu, c_ctx)[1](ds_ctx)[0]

    g_pack = _pack_grads(gp, shard_shapes)
    c_idx = jnp.reshape(ac, (1,)).astype(jnp.int32)
    s_idx = jnp.reshape(chip, (1,)).astype(jnp.int32)
    part, part_wire = _add_own_half(g_pack, _swap_other_half(g_pack, "rs_swap_half"), c_idx, "rs_add_sibling")
    mine = _add_arrivals(part, _send_to_chips(part_wire, "rs_send_chips"), s_idx, "rs_add_chips")
    other = _swap_reduced_half(mine, "rs_swap_reduced")
    joined = jnp.concatenate([jnp.where(ac == 0, mine, other), jnp.where(ac == 0, other, mine)], axis=0)
    grads.update(_unpack_shard(joined, shard_shapes))

    delta, new_m, new_v = {}, {}, {}
    for n in BIG + ["ada_w"]:
        delta[n], new_m[n], new_v[n] = _adamw(w[n], grads[n], mom_m[n], mom_v[n], "adamw_" + n)
    small = [n for n in WEIGHTS if n not in BIG and n != "ada_w"]
    shapes = [w[n].shape for n in small]
    packs = [_pad_rows(_pack_rows([src[n] for n in small])) for src in (w, grads, mom_m, mom_v)]
    outs = _adamw(*packs, "adamw_small")
    for tgt, buf in zip((delta, new_m, new_v), outs):
        for n, a in zip(small, _unpack_rows(buf, shapes)):
            tgt[n] = a
    return (loss, grad_x, *[grads[n] for n in WEIGHTS], *[delta[n] for n in WEIGHTS],
            *[new_m[n] for n in WEIGHTS], *[new_v[n] for n in WEIGHTS])
```

```python
import functools
import math

import jax
import jax.numpy as jnp
from jax import lax
from jax.experimental import pallas as pl
from jax.experimental.pallas import tpu as pltpu

F32 = jnp.float32
BF16 = jnp.bfloat16
MESH = pl.DeviceIdType.MESH
EPS = 1e-6
TB = 256
LANES = 1024
VMEM_LIMIT = 48 * 1024 * 1024
N_SHARD = 4
DEPTH = 4
N_MOD = 6
GRID_W = 64
HG_HEAD_DIM = 128
HG_CHUNK = 32
S5_GROUP = 16
NA_ROWS = 8
NA_COLS = 16
ADAM_LR, ADAM_B1, ADAM_B2, ADAM_EPS, ADAM_WD, ADAM_STEP = 0.001, 0.9, 0.999, 1e-08, 0.01, 10

BIG = ["mlp_w_in", "mlp_w_out", "sc_w_in", "sc_w_out", "hg_w_in", "hg_w_out", "s5_w_glu", "na_w_qkv", "na_w_out"]
COL_SHARDED = {"mlp_w_in", "sc_w_in", "hg_w_in", "s5_w_glu", "na_w_qkv"}
SMALL_SHARDED = ["norm_gains", "sc_conv", "hg_norm", "s5_d"]
SMALL_REPL = ["hg_lower_bound", "s5_lam_re", "s5_lam_im", "s5_log_dt", "s5_b_re", "s5_b_im", "s5_c_re", "s5_c_im", "na_rpb"]
WEIGHTS = ["c_ctx", "ada_w", "ada_b", "norm_gains", "mlp_w_in", "mlp_w_out", "sc_w_in", "sc_conv", "sc_w_out", "hg_w_in",
           "hg_lower_bound", "hg_norm", "hg_w_out", "s5_lam_re", "s5_lam_im", "s5_log_dt", "s5_b_re", "s5_b_im", "s5_c_re",
           "s5_c_im", "s5_d", "s5_w_glu", "na_w_qkv", "na_rpb", "na_w_out"]


def _cparams(sem=None):
    return pltpu.CompilerParams(dimension_semantics=sem, vmem_limit_bytes=VMEM_LIMIT)


def _rowwise(fn, out_dtypes, name, nseg):
    def seg_index(t):
        return jnp.minimum(t, nseg - 1)

    def in_specs(rows, colp, segp):
        rs = [pl.BlockSpec((None, TB, r.shape[-1]), lambda b, t: (b, t, 0)) for r in rows]
        cs = pl.BlockSpec(colp.shape, lambda b, t: (0, 0))
        ss = pl.BlockSpec((None, None) + segp.shape[2:], lambda b, t: (b, seg_index(t), 0, 0))
        return rs + [cs, ss]

    def load(refs, n, nc, ns):
        rows = [r[...].astype(F32) for r in refs[:n]]
        cols = [refs[n][k:k + 1, :] for k in range(nc)]
        segs = [refs[n + 1][k:k + 1, :] for k in range(ns)]
        return rows, cols, segs

    def out_blocks(rows, colp, segp):
        one = jax.ShapeDtypeStruct((1, colp.shape[-1]), F32)
        return jax.eval_shape(fn, [jax.ShapeDtypeStruct((TB, r.shape[-1]), F32) for r in rows],
                              [one] * colp.shape[0], [one] * segp.shape[2])

    def fwd_call(rows, colp, segp):
        bsz, tlen, _ = rows[0].shape
        n, nc, ns = len(rows), colp.shape[0], segp.shape[2]
        blk = out_blocks(rows, colp, segp)

        def body(*refs):
            vals = fn(*load(refs, n, nc, ns))
            for o, v in zip(refs[n + 2:], vals):
                o[...] = v.astype(o.dtype)

        return pl.pallas_call(
            body, name=name + "_fwd", grid=(bsz, tlen // TB), in_specs=in_specs(rows, colp, segp),
            out_specs=[pl.BlockSpec((None, TB, o.shape[-1]), lambda b, t: (b, t, 0)) for o in blk],
            out_shape=[jax.ShapeDtypeStruct((bsz, tlen, o.shape[-1]), dt) for o, dt in zip(blk, out_dtypes)],
            compiler_params=_cparams(("parallel", "parallel")),
        )(*rows, colp, segp)

    def bwd_call(rows, colp, segp, cts):
        bsz, tlen, _ = rows[0].shape
        n, nc, ns, m = len(rows), colp.shape[0], segp.shape[2], len(cts)

        def body(*refs):
            b, t = pl.program_id(0), pl.program_id(1)
            prim = load(refs, n, nc, ns)
            ct = tuple(r[...].astype(F32) for r in refs[n + 2:n + 2 + m])
            _, vjp = jax.vjp(fn, *prim)
            drows, dcols, dsegs = vjp(ct)
            outs = refs[n + 2 + m:]
            for o, v in zip(outs[:n], drows):
                o[...] = v.astype(o.dtype)
            dcol_ref, dseg_ref = outs[n], outs[n + 1]

            @pl.when((b == 0) & (t == 0))
            def _():
                dcol_ref[...] = jnp.zeros_like(dcol_ref)

            for k, v in enumerate(dcols):
                dcol_ref[k:k + 1, :] += v

            @pl.when(t < nseg)
            def _():
                for k, v in enumerate(dsegs):
                    dseg_ref[k:k + 1, :] = v

            @pl.when(t >= nseg)
            def _():
                for k, v in enumerate(dsegs):
                    dseg_ref[k:k + 1, :] += v

        row_specs = [pl.BlockSpec((None, TB, r.shape[-1]), lambda b, t: (b, t, 0)) for r in rows]
        ct_specs = [pl.BlockSpec((None, TB, c.shape[-1]), lambda b, t: (b, t, 0)) for c in cts]
        return pl.pallas_call(
            body, name=name + "_bwd", grid=(bsz, tlen // TB),
            in_specs=in_specs(rows, colp, segp) + ct_specs,
            out_specs=row_specs + [pl.BlockSpec(colp.shape, lambda b, t: (0, 0)),
                                   pl.BlockSpec((None, None) + segp.shape[2:], lambda b, t: (b, seg_index(t), 0, 0))],
            out_shape=[jax.ShapeDtypeStruct(r.shape, r.dtype) for r in rows]
            + [jax.ShapeDtypeStruct(colp.shape, F32), jax.ShapeDtypeStruct(segp.shape, F32)],
            compiler_params=_cparams(("arbitrary", "arbitrary")),
        )(*rows, colp, segp, *cts)

    @jax.custom_vjp
    def op(rows, colp, segp):
        return tuple(fwd_call(rows, colp, segp))

    def op_fwd(rows, colp, segp):
        return tuple(fwd_call(rows, colp, segp)), (rows, colp, segp)

    def op_bwd(res, cts):
        rows, colp, segp = res
        outs = bwd_call(rows, colp, segp, list(cts))
        return list(outs[:len(rows)]), outs[len(rows)], outs[len(rows) + 1]

    op.defvjp(op_fwd, op_bwd)
    return op


def _rms(x, g):
    return x * lax.rsqrt(jnp.mean(x * x, axis=-1, keepdims=True) + EPS) * g


def _fn_pre(rows, cols, segs):
    return (_rms(rows[0], cols[0]) * (1.0 + segs[1]) + segs[0],)


def _fn_post_pre(rows, cols, segs):
    h2 = rows[0] + segs[0] * _rms(rows[1], cols[0])
    return h2, _rms(h2, cols[1]) * (1.0 + segs[2]) + segs[1]


def _fn_post(rows, cols, segs):
    return (rows[0] + segs[0] * _rms(rows[1], cols[0]),)


def _m_tile(m, limit):
    for t in range(min(m, limit) // 16 * 16, 15, -16):
        if m % t == 0:
            return t
    return m


def _col_tile(n, limit):
    for t in range(min(n, limit), 127, -128):
        if n % t == 0 and t % 128 == 0:
            return t
    return n


def _relu2(x):
    r = jnp.maximum(x, 0.0)
    return r * r


def _mm(x, wb, act, out_dtype, name):
    m, k = x.shape
    n = wb.shape[1]
    tm = _m_tile(m, 1088)
    tn = _col_tile(n, 1024 if k <= 1024 else 512)

    def body(x_ref, w_ref, o_ref):
        xv = x_ref[...]
        if act:
            xv = _relu2(xv.astype(F32))
        o_ref[...] = jnp.dot(xv.astype(BF16), w_ref[...], preferred_element_type=F32).astype(o_ref.dtype)

    return pl.pallas_call(
        body, name=name, grid=(n // tn, m // tm),
        in_specs=[pl.BlockSpec((tm, k), lambda j, i: (i, 0)), pl.BlockSpec((k, tn), lambda j, i: (0, j))],
        out_specs=pl.BlockSpec((tm, tn), lambda j, i: (i, j)),
        out_shape=jax.ShapeDtypeStruct((m, n), out_dtype),
        compiler_params=_cparams(("parallel", "parallel")),
    )(x, wb)


def _mm_dx(dy, wb, x, act, name):
    m, n = dy.shape
    k = wb.shape[0]
    tm = _m_tile(m, 544)

    def body(dy_ref, w_ref, x_ref, o_ref):
        acc = lax.dot_general(dy_ref[...].astype(BF16), w_ref[...], (((1,), (1,)), ((), ())),
                              preferred_element_type=F32)
        if act:
            acc = acc * (2.0 * jnp.maximum(x_ref[...].astype(F32), 0.0))
        o_ref[...] = acc.astype(o_ref.dtype)

    return pl.pallas_call(
        body, name=name, grid=(m // tm,),
        in_specs=[pl.BlockSpec((tm, n), lambda i: (i, 0)), pl.BlockSpec((k, n), lambda i: (0, 0)),
                  pl.BlockSpec((tm, k), lambda i: (i, 0))],
        out_specs=pl.BlockSpec((tm, k), lambda i: (i, 0)),
        out_shape=jax.ShapeDtypeStruct((m, k), x.dtype),
        compiler_params=_cparams(("parallel",)),
    )(dy, wb, x)


def _mm_dw(x, dy, act, name):
    m, k = x.shape
    n = dy.shape[1]
    tm = _m_tile(m, 1088)
    tk, tn = _col_tile(k, 1024), _col_tile(n, 1024)

    def body(x_ref, dy_ref, o_ref):
        @pl.when(pl.program_id(2) == 0)
        def _():
            o_ref[...] = jnp.zeros_like(o_ref)

        xv = x_ref[...]
        if act:
            xv = _relu2(xv.astype(F32))
        o_ref[...] += lax.dot_general(xv.astype(BF16), dy_ref[...].astype(BF16), (((0,), (0,)), ((), ())),
                                      preferred_element_type=F32)

    return pl.pallas_call(
        body, name=name, grid=(k // tk, n // tn, m // tm),
        in_specs=[pl.BlockSpec((tm, tk), lambda a, b, i: (i, a)), pl.BlockSpec((tm, tn), lambda a, b, i: (i, b))],
        out_specs=pl.BlockSpec((tk, tn), lambda a, b, i: (a, b)),
        out_shape=jax.ShapeDtypeStruct((k, n), F32),
        compiler_params=_cparams(("parallel", "parallel", "arbitrary")),
    )(x, dy)


def _linear(act, name, out_dtype=BF16):
    def run(x, wb):
        y = _mm(x.reshape(-1, x.shape[-1]), wb, act, out_dtype, name + "_fwd")
        return y.reshape(x.shape[:-1] + (wb.shape[1],))

    @jax.custom_vjp
    def lin(x, w):
        return run(x, w.astype(BF16))

    def lin_fwd(x, w):
        wb = w.astype(BF16)
        return run(x, wb), (x, wb)

    def lin_bwd(res, dy):
        x, wb = res
        x2, dy2 = x.reshape(-1, x.shape[-1]), dy.reshape(-1, dy.shape[-1])
        dx = _mm_dx(dy2, wb, x2, act, name + "_dx").reshape(x.shape)
        return dx, _mm_dw(x2, dy2, act, name + "_dw")

    lin.defvjp(lin_fwd, lin_bwd)
    return lin


def _loss_head(y, tgt):
    bsz, seq, d = y.shape

    def body(y_ref, t_ref, l_ref, d_ref):
        err = y_ref[...] - t_ref[...]
        d_ref[...] = err * (1.0 / d)
        l_ref[...] = jnp.full(l_ref.shape, 0.5 / d, F32) * jnp.sum(err * err)

    spec = pl.BlockSpec((None, TB, d), lambda b, t: (b, t, 0))
    lblk, dy = pl.pallas_call(
        body, name="loss_head", grid=(bsz, seq // TB), in_specs=[spec, spec],
        out_specs=[pl.BlockSpec((None, None, 8, 128), lambda b, t: (b, t, 0, 0)), spec],
        out_shape=[jax.ShapeDtypeStruct((bsz, seq // TB, 8, 128), F32), jax.ShapeDtypeStruct(y.shape, F32)],
        compiler_params=_cparams(("parallel", "parallel")),
    )(y, tgt)
    return lblk[:, :, 0, 0], dy


def _row_tile(rows, limit=512):
    for tr in range(min(rows, limit), 7, -1):
        if rows % tr == 0 and tr % 8 == 0:
            return tr
    return rows


def _adamw(w, g, m, v, name):
    shape = w.shape
    cols = shape[-1]
    w2, g2, m2, v2 = (a.reshape(-1, cols) for a in (w, g, m, v))
    rows = w2.shape[0]
    tr = _row_tile(rows, max(8, (1 << 19) // cols))
    c1, c2 = 1.0 - ADAM_B1 ** ADAM_STEP, 1.0 - ADAM_B2 ** ADAM_STEP

    def body(w_ref, g_ref, m_ref, v_ref, d_ref, mo_ref, vo_ref):
        gv = g_ref[...]
        mn = ADAM_B1 * m_ref[...] + (1.0 - ADAM_B1) * gv
        vn = ADAM_B2 * v_ref[...] + (1.0 - ADAM_B2) * (gv * gv)
        d_ref[...] = -ADAM_LR * ((mn / c1) / (jnp.sqrt(vn / c2) + ADAM_EPS) + ADAM_WD * w_ref[...])
        mo_ref[...] = mn
        vo_ref[...] = vn

    spec = pl.BlockSpec((tr, cols), lambda i: (i, 0))
    outs = pl.pallas_call(
        body, name=name, grid=(rows // tr,), in_specs=[spec] * 4, out_specs=[spec] * 3,
        out_shape=[jax.ShapeDtypeStruct((rows, cols), F32)] * 3, compiler_params=_cparams(("parallel",)),
    )(w2, g2, m2, v2)
    return tuple(o.reshape(shape) for o in outs)


def _sum_leading(a, name):
    n, rows, cols = a.shape
    tr = _row_tile(rows, max(8, (1 << 18) // cols))

    def body(a_ref, o_ref):
        acc = a_ref[0]
        for j in range(1, n):
            acc = acc + a_ref[j]
        o_ref[...] = acc

    return pl.pallas_call(
        body, name=name, grid=(rows // tr,), in_specs=[pl.BlockSpec((n, tr, cols), lambda i: (0, i, 0))],
        out_specs=pl.BlockSpec((tr, cols), lambda i: (i, 0)), out_shape=jax.ShapeDtypeStruct((rows, cols), F32),
        compiler_params=_cparams(("parallel",)),
    )(a)


def _pack_rows(arrs):
    flat = [a.reshape(-1).astype(F32) for a in arrs]
    flat = [jnp.pad(f, (0, (-f.shape[0]) % LANES)) for f in flat]
    return jnp.concatenate(flat).reshape(-1, LANES)


def _unpack_rows(buf, shapes):
    out, r = [], 0
    for s in shapes:
        n = math.prod(s)
        nr = -(-n // LANES)
        out.append(buf[r:r + nr].reshape(-1)[:n].reshape(s))
        r += nr
    return out


def _pad_rows(buf, mult=8):
    return jnp.pad(buf, ((0, (-buf.shape[0]) % mult), (0, 0)))


def _ada_fwd(s, w):
    nl, d, n = w.shape
    r = s.shape[0]

    def body(s_ref, w_ref, o_ref):
        o_ref[...] = jnp.dot(s_ref[...], w_ref[...].astype(BF16), preferred_element_type=F32)

    return pl.pallas_call(
        body, name="ada_fwd", grid=(nl,),
        in_specs=[pl.BlockSpec((r, d), lambda i: (0, 0)), pl.BlockSpec((None, d, n), lambda i: (i, 0, 0))],
        out_specs=pl.BlockSpec((None, r, n), lambda i: (i, 0, 0)), out_shape=jax.ShapeDtypeStruct((nl, r, n), F32),
        compiler_params=_cparams(("parallel",)),
    )(s, w)


def _ada_dw(s, dm):
    nl, r, n = dm.shape
    d = s.shape[1]

    def body(s_ref, dm_ref, o_ref):
        o_ref[...] = lax.dot_general(s_ref[...], dm_ref[...].astype(BF16), (((0,), (0,)), ((), ())),
                                     preferred_element_type=F32)

    return pl.pallas_call(
        body, name="ada_dw", grid=(nl,),
        in_specs=[pl.BlockSpec((r, d), lambda i: (0, 0)), pl.BlockSpec((None, r, n), lambda i: (i, 0, 0))],
        out_specs=pl.BlockSpec((None, d, n), lambda i: (i, 0, 0)), out_shape=jax.ShapeDtypeStruct((nl, d, n), F32),
        compiler_params=_cparams(("parallel",)),
    )(s, dm)


def _ada_ds(dm, w):
    nl, r, n = dm.shape
    d = w.shape[1]

    def body(dm_ref, w_ref, o_ref):
        @pl.when(pl.program_id(0) == 0)
        def _():
            o_ref[...] = jnp.zeros_like(o_ref)

        o_ref[...] += lax.dot_general(dm_ref[...].astype(BF16), w_ref[...].astype(BF16), (((1,), (1,)), ((), ())),
                                      preferred_element_type=F32)

    return pl.pallas_call(
        body, name="ada_ds", grid=(nl,),
        in_specs=[pl.BlockSpec((None, r, n), lambda i: (i, 0, 0)), pl.BlockSpec((None, d, n), lambda i: (i, 0, 0))],
        out_specs=pl.BlockSpec((r, d), lambda i: (0, 0)), out_shape=jax.ShapeDtypeStruct((r, d), F32),
        compiler_params=_cparams(("arbitrary",)),
    )(dm, w)


_ANY = pl.BlockSpec(memory_space=pl.ANY)


def _position():
    return lax.axis_index("x"), lax.axis_index("y"), lax.axis_index("c")


def _all_gather8(block, name):
    m, n = block.shape

    def body(x_ref, out_ref, send_sems, recv_sems, local_sem):
        x, y, c = _position()
        me, sibling = (x, y, c), (x, y, 1 - c)
        chips = [(1 - x, y), (x, 1 - y), (1 - x, 1 - y)]

        def slot(px, py, pc):
            return out_ref.at[4 * px + 2 * py + pc]

        def copy(k, blk, to, src=None):
            return pltpu.make_async_remote_copy(
                src_ref=slot(*blk) if src is None else src, dst_ref=slot(*blk), send_sem=send_sems.at[k],
                recv_sem=recv_sems.at[k], device_id=to, device_id_type=MESH)

        mine = pltpu.make_async_copy(x_ref, slot(*me), local_sem)
        mine.start()
        first = [copy(0, me, sibling, src=x_ref)]
        first += [copy(1 + j, me, (*chip, c), src=x_ref) for j, chip in enumerate(chips)]
        for cp in first:
            cp.start()
        passed = [copy(4 + j, (*chip, c), sibling) for j, chip in enumerate(chips)]
        for j, chip in enumerate(chips):
            copy(1 + j, (*chip, c), me).wait_recv()
            passed[j].start()
        copy(0, sibling, me).wait_recv()
        for j, chip in enumerate(chips):
            copy(4 + j, (*chip, 1 - c), me).wait_recv()
        for cp in first + passed:
            cp.wait_send()
        mine.wait()

    return pl.pallas_call(
        body, name=name, out_shape=jax.ShapeDtypeStruct((8, m, n), block.dtype), in_specs=[_ANY], out_specs=_ANY,
        scratch_shapes=[pltpu.SemaphoreType.DMA((7,)), pltpu.SemaphoreType.DMA((7,)), pltpu.SemaphoreType.DMA],
    )(block)


def _gather_shards(shard, name):
    rows, cols = shard.shape
    half = rows // 2

    def body(x_ref, out_ref, send_sems, recv_sems):
        x, y, c = _position()
        sibling = (x, y, 1 - c)
        chips = [(1 - x, y), (x, 1 - y), (1 - x, 1 - y)]

        def part(px, py, pc):
            return out_ref.at[2 * px + py, pl.ds(pc * half, half), :]

        def copy(k, blk, to, src=None):
            return pltpu.make_async_remote_copy(
                src_ref=part(*blk) if src is None else src, dst_ref=part(*blk), send_sem=send_sems.at[k],
                recv_sem=recv_sems.at[k], device_id=to, device_id_type=MESH)

        my_half = x_ref.at[pl.ds(c * half, half), :]
        first = [copy(j, (x, y, c), (*chip, c), src=my_half) for j, chip in enumerate(chips)]
        for cp in first:
            cp.start()
        passed = [copy(3 + j, (*chip, c), sibling) for j, chip in enumerate(chips)]
        for j, chip in enumerate(chips):
            copy(j, (*chip, c), sibling).wait_recv()
            passed[j].start()
        for j, chip in enumerate(chips):
            copy(3 + j, (*chip, 1 - c), sibling).wait_recv()
        for cp in first + passed:
            cp.wait_send()

    return pl.pallas_call(
        body, name=name, out_shape=jax.ShapeDtypeStruct((N_SHARD, rows, cols), shard.dtype), in_specs=[_ANY],
        out_specs=_ANY, scratch_shapes=[pltpu.SemaphoreType.DMA((6,)), pltpu.SemaphoreType.DMA((6,))],
    )(shard)


def _swap_other_half(g, name):
    ns, rows, cols = g.shape
    half = rows // 2

    def body(g_ref, out_ref, send_sem, recv_sem):
        x, y, c = _position()
        cp = pltpu.make_async_remote_copy(
            src_ref=g_ref.at[:, pl.ds((1 - c) * half, half), :], dst_ref=out_ref, send_sem=send_sem,
            recv_sem=recv_sem, device_id=(x, y, 1 - c), device_id_type=MESH)
        cp.start()
        cp.wait()

    return pl.pallas_call(
        body, name=name, out_shape=jax.ShapeDtypeStruct((ns, half, cols), g.dtype), in_specs=[_ANY], out_specs=_ANY,
        scratch_shapes=[pltpu.SemaphoreType.DMA, pltpu.SemaphoreType.DMA],
    )(g)


def _add_own_half(g, r, c_idx, name):
    ns, rows, cols = g.shape
    half = rows // 2
    tr = _row_tile(half, max(8, (1 << 19) // cols))
    nb = half // tr

    def body(c_ref, g_ref, r_ref, o_ref, ob_ref):
        acc = g_ref[...] + r_ref[...]
        o_ref[...] = acc
        ob_ref[...] = acc.astype(BF16)

    out = pl.BlockSpec((None, tr, cols), lambda s, i, c_ref: (s, i, 0))
    return pl.pallas_call(
        body, name=name,
        grid_spec=pltpu.PrefetchScalarGridSpec(
            num_scalar_prefetch=1, grid=(ns, nb),
            in_specs=[pl.BlockSpec((None, tr, cols), lambda s, i, c_ref: (s, c_ref[0] * nb + i, 0)), out],
            out_specs=[out, out]),
        out_shape=[jax.ShapeDtypeStruct((ns, half, cols), F32), jax.ShapeDtypeStruct((ns, half, cols), BF16)],
        compiler_params=_cparams(("parallel", "parallel")),
    )(c_idx, g, r)


def _send_to_chips(a, name):
    ns, half, cols = a.shape

    def body(a_ref, out_ref, send_sems, recv_sems):
        x, y, c = _position()
        chips = [(1 - x, y), (x, 1 - y), (1 - x, 1 - y)]
        cps = [pltpu.make_async_remote_copy(
            src_ref=a_ref.at[2 * px + py], dst_ref=out_ref.at[j], send_sem=send_sems.at[j], recv_sem=recv_sems.at[j],
            device_id=(px, py, c), device_id_type=MESH) for j, (px, py) in enumerate(chips)]
        for cp in cps:
            cp.start()
        for cp in cps:
            cp.wait()

    return pl.pallas_call(
        body, name=name, out_shape=jax.ShapeDtypeStruct((3, half, cols), a.dtype), in_specs=[_ANY], out_specs=_ANY,
        scratch_shapes=[pltpu.SemaphoreType.DMA((3,)), pltpu.SemaphoreType.DMA((3,))],
    )(a)


def _add_arrivals(a, r, s_idx, name):
    ns, half, cols = a.shape
    tr = _row_tile(half, max(8, (1 << 18) // cols))

    def body(s_ref, a_ref, r_ref, o_ref):
        o_ref[...] = ((a_ref[...] + r_ref[0].astype(F32)) + r_ref[1].astype(F32)) + r_ref[2].astype(F32)

    return pl.pallas_call(
        body, name=name,
        grid_spec=pltpu.PrefetchScalarGridSpec(
            num_scalar_prefetch=1, grid=(half // tr,),
            in_specs=[pl.BlockSpec((None, tr, cols), lambda i, s_ref: (s_ref[0], i, 0)),
                      pl.BlockSpec((3, tr, cols), lambda i, s_ref: (0, i, 0))],
            out_specs=pl.BlockSpec((tr, cols), lambda i, s_ref: (i, 0))),
        out_shape=jax.ShapeDtypeStruct((half, cols), F32), compiler_params=_cparams(("parallel",)),
    )(s_idx, a, r)


def _swap_reduced_half(f, name):
    def body(f_ref, out_ref, send_sem, recv_sem):
        x, y, c = _position()
        cp = pltpu.make_async_remote_copy(src_ref=f_ref, dst_ref=out_ref, send_sem=send_sem, recv_sem=recv_sem,
                                          device_id=(x, y, 1 - c), device_id_type=MESH)
        cp.start()
        cp.wait()

    return pl.pallas_call(
        body, name=name, out_shape=jax.ShapeDtypeStruct(f.shape, f.dtype), in_specs=[_ANY], out_specs=_ANY,
        scratch_shapes=[pltpu.SemaphoreType.DMA, pltpu.SemaphoreType.DMA],
    )(f)


def _conv_core(z, conv_w, nctx):
    bsz, tlen, d3 = z.shape
    d = d3 // 3
    nblk = tlen // TB
    first_lat = nctx // TB
    hr = 8
    per = TB // hr

    def split(t):
        return t[:, :d].astype(F32), t[:, d:2 * d].astype(F32), t[:, 2 * d:].astype(F32)

    def halo_valid(t):
        prev_ok = (t != 0) & (t != first_lat)
        next_ok = (t != first_lat - 1) & (t != nblk - 1)
        return prev_ok, next_ok

    def shifted(u, prev_row, next_row):
        ridx = lax.broadcasted_iota(jnp.int32, (TB, 1), 0)
        up = jnp.where(ridx == 0, prev_row, pltpu.roll(u, 1, axis=0))
        dn = jnp.where(ridx == TB - 1, next_row, pltpu.roll(u, TB - 1, axis=0))
        return up, dn

    main = lambda w_: pl.BlockSpec((None, TB, w_), lambda b, t: (b, t, 0))
    prev = lambda w_: pl.BlockSpec((None, hr, w_), lambda b, t: (b, jnp.maximum(t * per - 1, 0), 0))
    nxt = lambda w_: pl.BlockSpec((None, hr, w_), lambda b, t: (b, jnp.minimum((t + 1) * per, nblk * per - 1), 0))
    wspec = pl.BlockSpec((3, d), lambda b, t: (0, 0))

    def halo_rows(zp_ref, zn_ref, t):
        prev_ok, next_ok = halo_valid(t)
        _, cgp, vp = split(zp_ref[...])
        _, cgn, vn = split(zn_ref[...])
        up = jnp.where(prev_ok, (cgp * vp)[hr - 1:hr], 0.0)
        un = jnp.where(next_ok, (cgn * vn)[0:1], 0.0)
        return up, un

    def fwd_call(z, w):
        def body(z_ref, zp_ref, zn_ref, w_ref, o_ref):
            bg, cg, v = split(z_ref[...])
            u = cg * v
            up, dn = shifted(u, *halo_rows(zp_ref, zn_ref, pl.program_id(1)))
            o_ref[...] = (bg * (w_ref[0:1, :] * up + w_ref[1:2, :] * u + w_ref[2:3, :] * dn)).astype(o_ref.dtype)

        return pl.pallas_call(
            body, name="conv_fwd", grid=(bsz, nblk), in_specs=[main(d3), prev(d3), nxt(d3), wspec], out_specs=main(d),
            out_shape=jax.ShapeDtypeStruct((bsz, tlen, d), BF16), compiler_params=_cparams(("parallel", "parallel")),
        )(z, z, z, w)

    def bwd_call(z, w, dy):
        def body(z_ref, zp_ref, zn_ref, w_ref, dy_ref, dyp_ref, dyn_ref, dz_ref, dw_ref):
            t = pl.program_id(1)
            prev_ok, next_ok = halo_valid(t)
            bg, cg, v = split(z_ref[...])
            u = cg * v
            up, dn = shifted(u, *halo_rows(zp_ref, zn_ref, t))
            w0, w1, w2 = w_ref[0:1, :], w_ref[1:2, :], w_ref[2:3, :]
            dyv = dy_ref[...].astype(F32)
            dconv = dyv * bg
            bgp = zp_ref[...][:, :d].astype(F32)
            bgn = zn_ref[...][:, :d].astype(F32)
            dc_prev = jnp.where(prev_ok, (dyp_ref[...].astype(F32) * bgp)[hr - 1:hr], 0.0)
            dc_next = jnp.where(next_ok, (dyn_ref[...].astype(F32) * bgn)[0:1], 0.0)
            dc_up, dc_dn = shifted(dconv, dc_prev, dc_next)
            du = w0 * dc_dn + w1 * dconv + w2 * dc_up
            dz_ref[:, 0:d] = (dyv * (w0 * up + w1 * u + w2 * dn)).astype(dz_ref.dtype)
            dz_ref[:, d:2 * d] = (du * v).astype(dz_ref.dtype)
            dz_ref[:, 2 * d:3 * d] = (du * cg).astype(dz_ref.dtype)

            @pl.when((pl.program_id(0) == 0) & (t == 0))
            def _():
                dw_ref[...] = jnp.zeros_like(dw_ref)

            dw_ref[0:1, :] += jnp.sum(dconv * up, axis=0, keepdims=True)
            dw_ref[1:2, :] += jnp.sum(dconv * u, axis=0, keepdims=True)
            dw_ref[2:3, :] += jnp.sum(dconv * dn, axis=0, keepdims=True)

        return pl.pallas_call(
            body, name="conv_bwd", grid=(bsz, nblk),
            in_specs=[main(d3), prev(d3), nxt(d3), wspec, main(d), prev(d), nxt(d)],
            out_specs=[main(d3), wspec],
            out_shape=[jax.ShapeDtypeStruct(z.shape, BF16), jax.ShapeDtypeStruct((3, d), F32)],
            compiler_params=_cparams(("arbitrary", "arbitrary")),
        )(z, z, z, w, dy, dy, dy)

    @jax.custom_vjp
    def op(z, w):
        return fwd_call(z, w)

    def op_fwd(z, w):
        return fwd_call(z, w), (z, w)

    def op_bwd(res, dy):
        return tuple(bwd_call(*res, dy))

    op.defvjp(op_fwd, op_bwd)
    return op(z, conv_w)


HG_BLOCK = 8
HG_UNROLL = 2
HG_VMEM_LIMIT = 56 * 1024 * 1024


def _hg_cumsum(x, rev):
    n = HG_CHUNK
    nrow = x.shape[0]

    def run(v, backwards):
        pos = lax.broadcasted_iota(jnp.int32, (nrow, 1), 0) % n
        k = 1
        while k < n:
            if backwards:
                v = v + jnp.where(pos + k < n, pltpu.roll(v, nrow - k, axis=0), 0.0)
            else:
                v = v + jnp.where(pos >= k, pltpu.roll(v, k, axis=0), 0.0)
            k *= 2
        return v

    @jax.custom_vjp
    def cs(v):
        return run(v, rev)

    cs.defvjp(lambda v: (run(v, rev), None), lambda _, g: (run(g, not rev),))
    return cs(x)


def _hg_local(q, v, tf, lb, rev):
    n = HG_CHUNK
    nrow = q.shape[0]
    nb = nrow // n
    f = lb + (1.0 - lb) * jax.nn.sigmoid(tf)
    kk = 1.0 - f
    b = _hg_cumsum(jnp.log(f), rev)
    pos = lax.broadcasted_iota(jnp.int32, (nb, n, 1), 1)
    b3 = b.reshape(nb, n, -1)
    mid = n - n // 2 if rev else n // 2 - 1
    last = 0 if rev else n - 1
    b_mid = jnp.sum(jnp.where(pos == mid, b3, 0.0), axis=1, keepdims=True)
    b_last = jnp.sum(jnp.where(pos == last, b3, 0.0), axis=1, keepdims=True)
    q3, k3, v3 = q.reshape(nb, n, -1), kk.reshape(nb, n, -1), v.reshape(nb, n, -1)
    qs = (q3 * jnp.exp(b3 - b_mid)).astype(BF16)
    ks = (k3 * jnp.exp(b_mid - b3)).astype(BF16)
    sc = jnp.einsum('ctk,csk->cts', qs, ks, preferred_element_type=F32)
    row = lax.broadcasted_iota(jnp.int32, (1, n, n), 1)
    col = lax.broadcasted_iota(jnp.int32, (1, n, n), 2)
    sc = jnp.where((col >= row) if rev else (col <= row), sc, 0.0).astype(BF16)
    o_intra = jnp.einsum('cts,csv->ctv', sc, v3.astype(BF16), preferred_element_type=F32)
    qe = q3 * jnp.exp(b3)
    ks2 = k3 * jnp.exp(b_last - b3)
    return o_intra.reshape(nrow, -1), qe.reshape(nrow, -1), ks2.reshape(nrow, -1), jnp.exp(b_last).reshape(nb, -1)


def _hg_scan(nctx):
    hd, n = HG_HEAD_DIM, HG_CHUNK
    rb = HG_BLOCK * n

    def geometry(z):
        bsz, tlen, d5 = z.shape
        return bsz, tlen, d5 // 5, (d5 // 5) // hd, tlen // n, nctx // n

    def rev_chunk(j, nch, ncc):
        return jnp.where(j < ncc, ncc - 1 - j, nch - 1 + ncc - j)

    def rows(c):
        return pl.ds(pl.multiple_of(c * n, n), n)

    seq_params = pltpu.CompilerParams(dimension_semantics=("parallel", "parallel"), vmem_limit_bytes=HG_VMEM_LIMIT)
    nt, tn = (((1,), (1,)), ((), ())), (((0,), (0,)), ((), ()))

    def local_specs(z):
        bsz, tlen, d, nh, nch, ncc = geometry(z)
        col = lambda k: pl.BlockSpec((None, rb, hd), lambda b, h, t: (b, t, k * nh + h))
        blk = pl.BlockSpec((None, rb, hd), lambda b, h, t: (b, t, h))
        dec = pl.BlockSpec((None, HG_BLOCK, hd), lambda b, h, t: (b, t, h))
        lbs = pl.BlockSpec((1, hd), lambda b, h, t: (0, h))
        return [col(0), col(1), col(3), col(4), lbs], blk, dec

    def local_fwd(z, lb):
        bsz, tlen, d, nh, nch, ncc = geometry(z)
        ins, blk, dec = local_specs(z)

        def body(q_ref, v_ref, ff_ref, fb_ref, lb_ref, o_ref, qf_ref, kf_ref, df_ref, qb_ref, kb_ref, db_ref):
            q, v, lbv = q_ref[...].astype(F32), v_ref[...].astype(F32), lb_ref[...]
            of, qe, ks, dc = _hg_local(q, v, ff_ref[...].astype(F32), lbv, False)
            qf_ref[...], kf_ref[...], df_ref[...] = qe.astype(BF16), ks.astype(BF16), dc
            ob, qe, ks, dc = _hg_local(q, v, fb_ref[...].astype(F32), lbv, True)
            qb_ref[...], kb_ref[...], db_ref[...] = qe.astype(BF16), ks.astype(BF16), dc
            o_ref[...] = of + ob

        act = jax.ShapeDtypeStruct((bsz, tlen, d), BF16)
        dcs = jax.ShapeDtypeStruct((bsz, nch, d), F32)
        return pl.pallas_call(
            body, name="hg_local", grid=(bsz, nh, tlen // rb), in_specs=ins,
            out_specs=[blk, blk, blk, dec, blk, blk, dec],
            out_shape=[jax.ShapeDtypeStruct((bsz, tlen, d), F32), act, act, dcs, act, act, dcs],
            compiler_params=_cparams(("parallel", "parallel", "parallel")),
        )(z, z, z, z, lb)

    def local_bwd(z, lb, do, dv_in, dqf, dkf, ddf, dqb, dkb, ddb):
        bsz, tlen, d, nh, nch, ncc = geometry(z)
        ins, blk, dec = local_specs(z)

        def body(q_ref, v_ref, ff_ref, fb_ref, lb_ref, do_ref, dvi_ref, dqf_ref, dkf_ref, ddf_ref, dqb_ref, dkb_ref,
                 ddb_ref, dq_ref, dv_ref, dff_ref, dfb_ref, dlb_ref):
            q, v, lbv = q_ref[...].astype(F32), v_ref[...].astype(F32), lb_ref[...]
            dov = do_ref[...]
            dq, dv, dlb = jnp.zeros_like(q), dvi_ref[...], jnp.zeros_like(lbv)
            for rev, f_ref, df_ref, cts in ((False, ff_ref, dff_ref, (dqf_ref, dkf_ref, ddf_ref)),
                                            (True, fb_ref, dfb_ref, (dqb_ref, dkb_ref, ddb_ref))):
                _, vjp = jax.vjp(functools.partial(_hg_local, rev=rev), q, v, f_ref[...].astype(F32), lbv)
                g = vjp((dov, cts[0][...].astype(F32), cts[1][...].astype(F32), cts[2][...]))
                dq, dv, dlb = dq + g[0], dv + g[1], dlb + g[3]
                df_ref[...] = g[2].astype(df_ref.dtype)
            dq_ref[...] = dq.astype(dq_ref.dtype)
            dv_ref[...] = dv.astype(dv_ref.dtype)

            @pl.when(pl.program_id(2) == 0)
            def _():
                dlb_ref[...] = dlb

            @pl.when(pl.program_id(2) != 0)
            def _():
                dlb_ref[...] += dlb

        act = jax.ShapeDtypeStruct((bsz, tlen, d), BF16)
        return pl.pallas_call(
            body, name="hg_local_bwd", grid=(bsz, nh, tlen // rb),
            in_specs=ins + [blk, blk, blk, blk, dec, blk, blk, dec],
            out_specs=[blk, blk, blk, blk, pl.BlockSpec((None, 1, hd), lambda b, h, t: (b, 0, h))],
            out_shape=[act, act, act, act, jax.ShapeDtypeStruct((bsz, 1, d), F32)],
            compiler_params=_cparams(("parallel", "parallel", "arbitrary")),
        )(z, z, z, z, lb, do, dv_in, dqf, dkf, ddf, dqb, dkb, ddb)

    def head_spec(z, k=None):
        bsz, tlen, d, nh, nch, ncc = geometry(z)
        if k is None:
            return pl.BlockSpec((None, tlen, hd), lambda b, h: (b, 0, h))
        return pl.BlockSpec((None, tlen, hd), lambda b, h: (b, 0, k * nh + h))

    def dec_spec(z):
        bsz, tlen, d, nh, nch, ncc = geometry(z)
        return pl.BlockSpec((None, nch, hd), lambda b, h: (b, 0, h))

    def state_fwd(z, o_in, qf, kf, df, qb, kb, db):
        bsz, tlen, d, nh, nch, ncc = geometry(z)
        hs, ds = head_spec(z), dec_spec(z)

        def body(v_ref, oi_ref, qf_ref, kf_ref, df_ref, qb_ref, kb_ref, db_ref, o_ref):
            o_ref[...] = oi_ref[...]
            chains = ((False, qf_ref, kf_ref, df_ref), (True, qb_ref, kb_ref, db_ref))

            def step(j, carry):
                out = []
                for (rev, q_ref, k_ref, d_ref), st in zip(chains, carry):
                    c = rev_chunk(j, nch, ncc) if rev else j
                    sl = rows(c)
                    o_ref[sl, :] += lax.dot_general(q_ref[sl, :], st.astype(BF16), nt, preferred_element_type=F32)
                    out.append(st * d_ref[pl.ds(c, 1), :] + lax.dot_general(v_ref[sl, :], k_ref[sl, :], tn,
                                                                              preferred_element_type=F32))
                return tuple(out)

            zero = jnp.zeros((hd, hd), F32)
            lax.fori_loop(0, nch, step, (zero, zero), unroll=HG_UNROLL)

        return pl.pallas_call(
            body, name="hg_state", grid=(bsz, nh), in_specs=[head_spec(z, 1), hs, hs, hs, ds, hs, hs, ds],
            out_specs=hs, out_shape=jax.ShapeDtypeStruct((bsz, tlen, d), F32), compiler_params=seq_params,
        )(z, o_in, qf, kf, df, qb, kb, db)

    def state_bwd(z, do, qf, kf, df, qb, kb, db):
        bsz, tlen, d, nh, nch, ncc = geometry(z)
        hs, ds = head_spec(z), dec_spec(z)

        def body(v_ref, do_ref, qf_ref, kf_ref, df_ref, qb_ref, kb_ref, db_ref,
                 dv_ref, dqf_ref, dkf_ref, ddf_ref, dqb_ref, dkb_ref, ddb_ref, stf_ref, stb_ref):
            chains = ((False, qf_ref, kf_ref, df_ref, dqf_ref, dkf_ref, ddf_ref, stf_ref),
                      (True, qb_ref, kb_ref, db_ref, dqb_ref, dkb_ref, ddb_ref, stb_ref))

            def fstep(j, carry):
                out = []
                for (rev, q_ref, k_ref, d_ref, _, _, _, st_ref), st in zip(chains, carry):
                    c = rev_chunk(j, nch, ncc) if rev else j
                    sl = rows(c)
                    st_ref[j] = st
                    out.append(st * d_ref[pl.ds(c, 1), :] + lax.dot_general(v_ref[sl, :], k_ref[sl, :], tn,
                                                                              preferred_element_type=F32))
                return tuple(out)

            zero = jnp.zeros((hd, hd), F32)
            lax.fori_loop(0, nch, fstep, (zero, zero), unroll=HG_UNROLL)
            dv_ref[...] = jnp.zeros_like(dv_ref)

            def bstep(i, carry):
                j = nch - 1 - i
                out = []
                for (rev, q_ref, k_ref, d_ref, dq_ref, dk_ref, dd_ref, st_ref), dst in zip(chains, carry):
                    c = rev_chunk(j, nch, ncc) if rev else j
                    sl = rows(c)
                    st = st_ref[j]
                    dob = do_ref[sl, :].astype(BF16)
                    dstb = dst.astype(BF16)
                    dec = d_ref[pl.ds(c, 1), :]
                    dq_ref[sl, :] = jnp.dot(dob, st.astype(BF16), preferred_element_type=F32).astype(dq_ref.dtype)
                    dk_ref[sl, :] = jnp.dot(v_ref[sl, :], dstb, preferred_element_type=F32).astype(dk_ref.dtype)
                    dv_ref[sl, :] += lax.dot_general(k_ref[sl, :], dstb, nt, preferred_element_type=F32)
                    dd_ref[pl.ds(c, 1), :] = jnp.sum(dst * st, axis=0, keepdims=True)
                    out.append(dst * dec + lax.dot_general(dob, q_ref[sl, :], tn, preferred_element_type=F32))
                return tuple(out)

            lax.fori_loop(0, nch, bstep, (zero, zero), unroll=HG_UNROLL)

        act = jax.ShapeDtypeStruct((bsz, tlen, d), BF16)
        dcs = jax.ShapeDtypeStruct((bsz, nch, d), F32)
        return pl.pallas_call(
            body, name="hg_state_bwd", grid=(bsz, nh), in_specs=[head_spec(z, 1), hs, hs, hs, ds, hs, hs, ds],
            out_specs=[hs, hs, hs, ds, hs, hs, ds],
            out_shape=[jax.ShapeDtypeStruct((bsz, tlen, d), F32), act, act, dcs, act, act, dcs],
            scratch_shapes=[pltpu.VMEM((nch, hd, hd), F32), pltpu.VMEM((nch, hd, hd), F32)],
            compiler_params=seq_params,
        )(z, do, qf, kf, df, qb, kb, db)

    @jax.custom_vjp
    def scan(z, lb):
        return state_fwd(z, *local_fwd(z, lb))

    def scan_fwd(z, lb):
        loc = local_fwd(z, lb)
        return state_fwd(z, *loc), (z, lb, loc[1:])

    def scan_bwd(res, do):
        z, lb, loc = res
        dv_in, *dstate = state_bwd(z, do, *loc)
        dq, dv, dff, dfb, dlb = local_bwd(z, lb, do, dv_in, *dstate)
        dz = jnp.concatenate([dq, dv, jnp.zeros_like(dff), dff, dfb], axis=-1)
        return dz, jnp.sum(dlb, axis=0)

    scan.defvjp(scan_fwd, scan_bwd)
    return scan


def _hg_readout(o, gate, g_norm):
    bsz, tlen, d = gate.shape
    nh = d // HG_HEAD_DIM
    rb = TB * nh

    def fn(x, g, gt):
        return x * lax.rsqrt(jnp.mean(x * x, axis=-1, keepdims=True) + EPS) * gt * (g * jax.nn.sigmoid(g))

    spec = pl.BlockSpec((None, rb, HG_HEAD_DIM), lambda b, t: (b, t, 0))
    gspec = pl.BlockSpec((rb, HG_HEAD_DIM), lambda b, t: (0, 0))
    grid = (bsz, tlen // TB)

    def fwd_call(x, g, gt):
        def body(x_ref, g_ref, gt_ref, y_ref):
            y_ref[...] = fn(x_ref[...], g_ref[...].astype(F32), gt_ref[...]).astype(y_ref.dtype)

        return pl.pallas_call(
            body, name="hg_readout_fwd", grid=grid, in_specs=[spec, spec, gspec], out_specs=spec,
            out_shape=jax.ShapeDtypeStruct(g.shape, BF16), compiler_params=_cparams(("parallel", "parallel")),
        )(x, g, gt)

    def bwd_call(x, g, gt, dy):
        def body(x_ref, g_ref, gt_ref, dy_ref, dx_ref, dg_ref, dgt_ref):
            _, vjp = jax.vjp(fn, x_ref[...], g_ref[...].astype(F32), gt_ref[...])
            dx, dg, dgt = vjp(dy_ref[...].astype(F32))
            dx_ref[...] = dx
            dg_ref[...] = dg.astype(dg_ref.dtype)

            @pl.when((pl.program_id(0) == 0) & (pl.program_id(1) == 0))
            def _():
                dgt_ref[...] = jnp.zeros_like(dgt_ref)

            dgt_ref[...] += dgt

        return pl.pallas_call(
            body, name="hg_readout_bwd", grid=grid, in_specs=[spec, spec, gspec, spec],
            out_specs=[spec, spec, gspec],
            out_shape=[jax.ShapeDtypeStruct(x.shape, F32), jax.ShapeDtypeStruct(g.shape, BF16),
                       jax.ShapeDtypeStruct(gt.shape, F32)],
            compiler_params=_cparams(("arbitrary", "arbitrary")),
        )(x, g, gt, dy)

    @jax.custom_vjp
    def op(x, g, gt):
        return fwd_call(x, g, gt)

    def op_fwd(x, g, gt):
        return fwd_call(x, g, gt), (x, g, gt)

    def op_bwd(res, dy):
        return tuple(bwd_call(*res, dy))

    op.defvjp(op_fwd, op_bwd)
    heads = lambda t: t.reshape(bsz, tlen * nh, HG_HEAD_DIM)
    gt = jnp.tile(g_norm.reshape(nh, HG_HEAD_DIM), (TB, 1))
    return op(heads(o), heads(gate), gt).reshape(bsz, tlen, d)


def _hgrn_core(z, lower_bound, g_norm, nctx):
    d = g_norm.shape[-1]
    o = _hg_scan(nctx)(z, lower_bound.reshape(1, d))
    return _hg_readout(o, z[..., 2 * d:3 * d], g_norm)


S5_LC = 16


def _s5_mats(lam_re, lam_im, log_dt, b_re, b_im, c_re, c_im):
    hi = lax.Precision.HIGHEST
    _, ng, ns = lam_re.shape
    lc, gs = S5_LC, S5_GROUP
    lam_re = jnp.minimum(lam_re, -1e-4)
    dt = jnp.exp(log_dt)[:, None, :, None]
    k = jnp.arange(lc + 1, dtype=F32)[None, :, None, None]
    mag, ang = jnp.exp(lam_re[:, None] * dt * k), lam_im[:, None] * dt * k
    p_re, p_im = mag * jnp.cos(ang), mag * jnp.sin(ang)
    a_re, a_im = p_re[:, 1], p_im[:, 1]
    den = lam_re * lam_re + lam_im * lam_im
    f_re = ((a_re - 1) * lam_re + a_im * lam_im) / den
    f_im = (a_im * lam_re - (a_re - 1) * lam_im) / den
    bt_re, bt_im = b_re.transpose(0, 2, 1), b_im.transpose(0, 2, 1)
    bb_re = f_re[:, :, None] * bt_re - f_im[:, :, None] * bt_im
    bb_im = f_re[:, :, None] * bt_im + f_im[:, :, None] * bt_re
    w_re = c_re[:, None] * p_re[:, :, :, None] - c_im[:, None] * p_im[:, :, :, None]
    w_im = c_re[:, None] * p_im[:, :, :, None] + c_im[:, None] * p_re[:, :, :, None]
    kk = (jnp.einsum('rkgcn,rgdn->rkgcd', w_re[:, :lc], bb_re, precision=hi)
          - jnp.einsum('rkgcn,rgdn->rkgcd', w_im[:, :lc], bb_im, precision=hi))
    t = jnp.arange(lc)
    lag = jnp.stack([t[:, None] - t[None, :], t[None, :] - t[:, None]])
    lag_hot = (lag[..., None] == jnp.arange(lc)).astype(F32)
    mt = jnp.einsum('rtsk,rkgcd->rgsdtc', lag_hot, kk, precision=hi).reshape(2, ng, lc * gs, lc * gs)
    left = jnp.stack([lc - 1 - t, t])
    left_hot = (left[..., None] == jnp.arange(lc + 1)).astype(F32)
    pw_re = jnp.einsum('rsk,rkgn->rsgn', left_hot, p_re, precision=hi)
    pw_im = jnp.einsum('rsk,rkgn->rsgn', left_hot, p_im, precision=hi)
    pr = pw_re[:, :, :, None] * bb_re[:, None] - pw_im[:, :, :, None] * bb_im[:, None]
    pi = pw_re[:, :, :, None] * bb_im[:, None] + pw_im[:, :, :, None] * bb_re[:, None]
    pt = jnp.concatenate([pr, pi], axis=-1).transpose(0, 2, 1, 3, 4).reshape(2, ng, lc * gs, 2 * ns)
    since = jnp.stack([t + 1, lc - t])
    since_hot = (since[..., None] == jnp.arange(lc + 1)).astype(F32)
    q = jnp.concatenate([jnp.einsum('rtk,rkgcn->rtgcn', since_hot, w_re, precision=hi),
                         -jnp.einsum('rtk,rkgcn->rtgcn', since_hot, w_im, precision=hi)], axis=-1)
    qt = q.transpose(0, 2, 4, 1, 3).reshape(2, ng, 2 * ns, lc * gs)
    a16 = jnp.concatenate([p_re[:, lc], p_im[:, lc]], axis=-1)
    return mt, pt, qt, a16


def _s5_bmm(terms, out_dtype, name, sum_dirs=False):
    ng = terms[0][0].shape[-3]
    ops, dlist = [], []
    for a, b, dn in terms:
        ops += [a, b]
        dlist.append(dn)
    (ca,), (cb,) = dlist[0]
    om, on = terms[0][0].shape[-2:][1 - ca], terms[0][1].shape[-2:][1 - cb]

    def spec(o):
        if o.ndim == 4:
            return pl.BlockSpec((None, None) + o.shape[2:], lambda g, d: (d, g, 0, 0))
        return pl.BlockSpec((None,) + o.shape[1:], lambda g, d: (g, 0, 0))

    def body(*refs):
        acc = None
        for j, dn in enumerate(dlist):
            a, b = refs[2 * j][...].astype(BF16), refs[2 * j + 1][...].astype(BF16)
            r = lax.dot_general(a, b, (dn, ((), ())), preferred_element_type=F32)
            acc = r if acc is None else acc + r
        o_ref = refs[-1]
        if sum_dirs:
            @pl.when(pl.program_id(1) == 0)
            def _():
                o_ref[...] = acc.astype(o_ref.dtype)

            @pl.when(pl.program_id(1) != 0)
            def _():
                o_ref[...] = (o_ref[...].astype(F32) + acc).astype(o_ref.dtype)
        else:
            o_ref[...] = acc.astype(o_ref.dtype)

    if sum_dirs:
        out_spec = pl.BlockSpec((None, om, on), lambda g, d: (g, 0, 0))
        out_shape = jax.ShapeDtypeStruct((ng, om, on), out_dtype)
    else:
        out_spec = pl.BlockSpec((None, None, om, on), lambda g, d: (d, g, 0, 0))
        out_shape = jax.ShapeDtypeStruct((2, ng, om, on), out_dtype)
    return pl.pallas_call(
        body, name=name, grid=(ng, 2), in_specs=[spec(o) for o in ops], out_specs=out_spec, out_shape=out_shape,
        compiler_params=_cparams(("parallel", "arbitrary" if sum_dirs else "parallel")),
    )(*ops)


def _s5_row_block(rows):
    return 32 if rows % 32 == 0 else rows


def _s5_chunk_order(j, d, nc, ncc):
    return jnp.where(d == 0, j, jnp.where(j < ncc, ncc - 1 - j, nc - 1 + ncc - j))


def _s5_scan_fwd(z, a1, a2, ncc, name):
    nd, nc, rows, lanes = z.shape
    rb = _s5_row_block(rows)

    def body(z_ref, a1_ref, a2_ref, x_ref):
        a1v, a2v = a1_ref[...], a2_ref[...]
        d = pl.program_id(0)

        def step(j, x):
            c = _s5_chunk_order(j, d, nc, ncc)
            x_ref[c] = x
            return a1v * x + a2v * pltpu.roll(x, lanes // 2, axis=1) + z_ref[c]

        lax.fori_loop(0, nc, step, jnp.zeros((rb, lanes), F32))

    blk = pl.BlockSpec((None, nc, rb, lanes), lambda d, r: (d, 0, r, 0))
    par = pl.BlockSpec((None, rb, lanes), lambda d, r: (d, r, 0))
    return pl.pallas_call(
        body, name=name, grid=(nd, rows // rb), in_specs=[blk, par, par], out_specs=blk,
        out_shape=jax.ShapeDtypeStruct(z.shape, F32), compiler_params=_cparams(("parallel", "parallel")),
    )(z, a1, a2)


def _s5_scan_bwd(dxp, xp, a1, a2b, ncc, name):
    nd, nc, rows, lanes = dxp.shape
    rb = _s5_row_block(rows)

    def body(dxp_ref, xp_ref, a1_ref, a2_ref, dz_ref, p1_ref, p2_ref):
        a1v, a2v = a1_ref[...], a2_ref[...]
        zero = jnp.zeros((rb, lanes), F32)
        d = pl.program_id(0)

        def step(i, carry):
            g_next, nxt, p1, p2 = carry
            c = _s5_chunk_order(nc - 1 - i, d, nc, ncc)
            g = nxt + a1v * g_next + a2v * pltpu.roll(g_next, lanes // 2, axis=1)
            dz_ref[c] = g
            x = xp_ref[c]
            return g, dxp_ref[c], p1 + x * g, p2 + pltpu.roll(x, lanes // 2, axis=1) * g

        _, _, p1, p2 = lax.fori_loop(0, nc, step, (zero, zero, zero, zero))
        p1_ref[...] = p1
        p2_ref[...] = p2

    blk = pl.BlockSpec((None, nc, rb, lanes), lambda d, r: (d, 0, r, 0))
    par = pl.BlockSpec((None, rb, lanes), lambda d, r: (d, r, 0))
    return pl.pallas_call(
        body, name=name, grid=(nd, rows // rb), in_specs=[blk, blk, par, par], out_specs=[blk, par, par],
        out_shape=[jax.ShapeDtypeStruct(dxp.shape, F32), jax.ShapeDtypeStruct((nd, rows, lanes), F32),
                   jax.ShapeDtypeStruct((nd, rows, lanes), F32)],
        compiler_params=_cparams(("parallel", "parallel")),
    )(dxp, xp, a1, a2b)


def _s5_rows(t, bsz):
    nd, ng, m, k = t.shape
    return t.reshape(nd, ng, bsz, m // bsz, k).transpose(0, 3, 2, 1, 4).reshape(nd, m // bsz, bsz * ng, k)


def _s5_groups(t, bsz):
    nd, nc, rows, k = t.shape
    return t.reshape(nd, nc, bsz, rows // bsz, k).transpose(0, 3, 2, 1, 4).reshape(nd, rows // bsz, bsz * nc, k)


def _s5_coeffs(a16, bsz):
    half = a16.shape[-1] // 2
    re, im = a16[..., :half], a16[..., half:]
    tile = lambda v: jnp.tile(v, (1, bsz, 1))
    return tile(jnp.concatenate([re, re], -1)), tile(jnp.concatenate([-im, im], -1)), tile(jnp.concatenate([im, -im], -1))


def _s5_apply(bsz, ncc):
    nn, nt, tn = ((1,), (0,)), ((1,), (1,)), ((0,), (0,))

    def run(u, mt, pt, qt, a16):
        a1, a2, _ = _s5_coeffs(a16, bsz)
        z = _s5_bmm([(u, pt, nn)], F32, "s5_z")
        xp = _s5_scan_fwd(_s5_rows(z, bsz), a1, a2, ncc, "s5_scan")
        xg = _s5_groups(xp, bsz).astype(BF16)
        y = _s5_bmm([(u, mt, nn), (xg, qt, nn)], BF16, "s5_y", sum_dirs=True)
        return y, (xp, xg)

    @jax.custom_vjp
    def apply(u, mt, pt, qt, a16):
        return run(u, mt.astype(BF16), pt.astype(BF16), qt.astype(BF16), a16)[0]

    def apply_fwd(u, mt, pt, qt, a16):
        mtb, ptb, qtb = mt.astype(BF16), pt.astype(BF16), qt.astype(BF16)
        y, (xp, xg) = run(u, mtb, ptb, qtb, a16)
        return y, (u, mtb, ptb, qtb, a16, xp, xg)

    def apply_bwd(res, dy):
        u, mtb, ptb, qtb, a16, xp, xg = res
        a1, _, a2b = _s5_coeffs(a16, bsz)
        dyb = dy.astype(BF16)
        dmt = _s5_bmm([(u, dyb, tn)], F32, "s5_dmt", sum_dirs=True) * 0.5
        dqt = _s5_bmm([(xg, dyb, tn)], F32, "s5_dqt")
        dxp = _s5_bmm([(dyb, qtb, nt)], F32, "s5_dxp")
        dz, p1, p2 = _s5_scan_bwd(_s5_rows(dxp, bsz), xp, a1, a2b, ncc, "s5_scan_bwd")
        dzg = _s5_groups(dz, bsz).astype(BF16)
        dpt = _s5_bmm([(u, dzg, tn)], F32, "s5_dpt")
        du = _s5_bmm([(dyb, mtb, nt), (dzg, ptb, nt)], BF16, "s5_du", sum_dirs=True)
        half = a16.shape[-1] // 2
        p1 = jnp.sum(p1.reshape(2, bsz, -1, 2 * half), axis=1)
        p2 = jnp.sum(p2.reshape(2, bsz, -1, 2 * half), axis=1)
        da16 = jnp.concatenate([p1[..., :half] + p1[..., half:], p2[..., half:] - p2[..., :half]], axis=-1)
        return du, jnp.stack([dmt, dmt]), dpt, dqt, da16

    apply.defvjp(apply_fwd, apply_bwd)
    return apply


def _s5_core(a, p, nctx):
    bsz, tlen, d = a.shape
    ng, lc, gs = d // S5_GROUP, S5_LC, S5_GROUP
    mt, pt, qt, a16 = _s5_mats(p["s5_lam_re"], p["s5_lam_im"], p["s5_log_dt"], p["s5_b_re"], p["s5_b_im"],
                               p["s5_c_re"], p["s5_c_im"])
    u = a.reshape(bsz, tlen // lc, lc, ng, gs).transpose(3, 0, 1, 2, 4).reshape(ng, bsz * (tlen // lc), lc * gs)
    y = _s5_apply(bsz, nctx // lc)(u, mt, pt, qt, a16)
    y = y.reshape(ng, bsz, tlen // lc, lc, gs).transpose(1, 2, 3, 0, 4).reshape(bsz, tlen, d)
    return jax.nn.gelu(p["s5_d"] * a.astype(F32) + y)


NA_LANES = 256
NA_MASKED = -1e30


def _na_tables(rpb):
    hi = lax.Precision.HIGHEST
    nh = rpb.shape[0]
    q = jnp.arange(GRID_W)
    kc = jnp.arange(GRID_W)
    q_start = jnp.clip(q - NA_COLS // 2, 0, GRID_W - NA_COLS)
    inwin = (kc[None, :] >= q_start[:, None]) & (kc[None, :] < q_start[:, None] + NA_COLS)
    dc = kc[None, :] - q[:, None] + NA_COLS - 1
    onehot = ((dc[:, :, None] == jnp.arange(2 * NA_COLS - 1)) & inwin[:, :, None]).astype(F32)
    a = jnp.arange(NA_ROWS)[None, :] - jnp.arange(NA_ROWS)[:, None] + NA_ROWS - 1
    tab = jnp.einsum('hskc,qlc->hsqkl', rpb[:, a, :], onehot, precision=hi)
    tab = jnp.where(inwin[None, None, :, None, :], tab, NA_MASKED)
    return tab.reshape(nh, NA_ROWS, GRID_W, NA_ROWS * GRID_W)


def _na_math(q2, kw, vw, kc, vc, bias, dh):
    scale = dh ** -0.5
    nq, nhb = q2.shape[0], NA_LANES // dh
    lane_head = lax.broadcasted_iota(jnp.int32, (1, NA_LANES), 1) // dh
    nt = (((1,), (1,)), ((), ()))
    kwb, vwb, kcb, vcb = (t.astype(BF16) for t in (kw, vw, kc, vc))
    qs = jnp.concatenate([jnp.where(lane_head == j, q2, 0.0) for j in range(nhb)], axis=0).astype(BF16)
    s_loc = lax.dot_general(qs, kwb, nt, preferred_element_type=F32) * scale + bias.reshape(nhb * nq, -1)
    s_ctx = lax.dot_general(qs, kcb, nt, preferred_element_type=F32) * scale
    m = jnp.maximum(jnp.max(s_loc, axis=-1, keepdims=True), jnp.max(s_ctx, axis=-1, keepdims=True))
    m = lax.stop_gradient(m)
    p_loc, p_ctx = jnp.exp(s_loc - m), jnp.exp(s_ctx - m)
    inv = 1.0 / (jnp.sum(p_loc, axis=-1, keepdims=True) + jnp.sum(p_ctx, axis=-1, keepdims=True))
    o_all = (jnp.dot((p_loc * inv).astype(BF16), vwb, preferred_element_type=F32)
             + jnp.dot((p_ctx * inv).astype(BF16), vcb, preferred_element_type=F32))
    out = jnp.zeros(q2.shape, F32)
    for j in range(nhb):
        out = out + jnp.where(lane_head == j, o_all[j * nq:(j + 1) * nq], 0.0)
    return out


def _na_attention(nctx, nh):
    def geometry(z):
        bsz, tlen, d3 = z.shape
        d = d3 // 3
        rows = (tlen - nctx) // GRID_W
        return bsz, tlen, d, rows, d // nh, d // NA_LANES

    def key_row0(r, rows):
        return jnp.clip(r - NA_ROWS // 2, 0, rows - NA_ROWS)

    def specs(z):
        bsz, tlen, d, rows, dh, nlb = geometry(z)
        hpb = NA_LANES // dh
        qs = pl.BlockSpec((None, GRID_W, NA_LANES), lambda b, h, r: (b, nctx // GRID_W + r, h))
        ks = pl.BlockSpec((None, tlen, NA_LANES), lambda b, h, r: (b, 0, nlb + h))
        vs = pl.BlockSpec((None, tlen, NA_LANES), lambda b, h, r: (b, 0, 2 * nlb + h))
        bs = pl.BlockSpec((hpb, None, GRID_W, NA_ROWS * GRID_W), lambda b, h, r: (h, r - key_row0(r, rows), 0, 0))
        os_ = pl.BlockSpec((None, GRID_W, NA_LANES), lambda b, h, r: (b, r, h))
        return qs, ks, vs, bs, os_

    def window(r, rows):
        return pl.ds(pl.multiple_of(nctx + key_row0(r, rows) * GRID_W, GRID_W), NA_ROWS * GRID_W)

    def fwd_call(z, bias):
        bsz, tlen, d, rows, dh, nlb = geometry(z)
        qs, ks, vs, bs, os_ = specs(z)

        def body(q_ref, k_ref, v_ref, b_ref, o_ref):
            win = window(pl.program_id(2), rows)
            o_ref[...] = _na_math(q_ref[...].astype(F32), k_ref[win, :], v_ref[win, :], k_ref[0:nctx, :],
                                  v_ref[0:nctx, :], b_ref[...], dh).astype(o_ref.dtype)

        return pl.pallas_call(
            body, name="na_fwd", grid=(bsz, nlb, rows), in_specs=[qs, ks, vs, bs], out_specs=os_,
            out_shape=jax.ShapeDtypeStruct((bsz, tlen - nctx, d), BF16),
            compiler_params=_cparams(("parallel", "parallel", "arbitrary")),
        )(z, z, z, bias)

    def bwd_call(z, bias, do):
        bsz, tlen, d, rows, dh, nlb = geometry(z)
        hpb = NA_LANES // dh
        qs, ks, vs, bs, os_ = specs(z)

        def body(q_ref, k_ref, v_ref, b_ref, do_ref, dq_ref, dk_ref, dv_ref, db_ref):
            r = pl.program_id(2)
            win = window(r, rows)

            @pl.when(r == 0)
            def _():
                dk_ref[...] = jnp.zeros_like(dk_ref)
                dv_ref[...] = jnp.zeros_like(dv_ref)

            prim = (q_ref[...].astype(F32), k_ref[win, :].astype(F32), v_ref[win, :].astype(F32),
                    k_ref[0:nctx, :].astype(F32), v_ref[0:nctx, :].astype(F32), b_ref[...])
            _, vjp = jax.vjp(functools.partial(_na_math, dh=dh), *prim)
            dq, dkw, dvw, dkc, dvc, db = vjp(do_ref[...].astype(F32))
            dq_ref[...] = dq.astype(dq_ref.dtype)
            dk_ref[win, :] += dkw
            dv_ref[win, :] += dvw
            dk_ref[0:nctx, :] += dkc
            dv_ref[0:nctx, :] += dvc
            prev = jnp.maximum(r - 1, 0)
            first = (r == 0) | ((r - key_row0(r, rows)) != (prev - key_row0(prev, rows)))

            @pl.when(first)
            def _():
                db_ref[...] = db

            @pl.when(jnp.logical_not(first))
            def _():
                db_ref[...] += db

        acc = pl.BlockSpec((None, tlen, NA_LANES), lambda b, h, r: (b, 0, h))
        dbs = pl.BlockSpec((None, hpb, None, GRID_W, NA_ROWS * GRID_W),
                           lambda b, h, r: (b, h, r - key_row0(r, rows), 0, 0))
        return pl.pallas_call(
            body, name="na_bwd", grid=(bsz, nlb, rows), in_specs=[qs, ks, vs, bs, os_],
            out_specs=[os_, acc, acc, dbs],
            out_shape=[jax.ShapeDtypeStruct((bsz, tlen - nctx, d), BF16), jax.ShapeDtypeStruct((bsz, tlen, d), F32),
                       jax.ShapeDtypeStruct((bsz, tlen, d), F32), jax.ShapeDtypeStruct((bsz,) + bias.shape, F32)],
            compiler_params=_cparams(("parallel", "parallel", "arbitrary")),
        )(z, z, z, bias, do)

    @jax.custom_vjp
    def attend(z, bias):
        return fwd_call(z, bias)

    def attend_fwd(z, bias):
        return fwd_call(z, bias), (z, bias)

    def attend_bwd(res, do):
        z, bias = res
        dq, dk, dv, db = bwd_call(z, bias, do)
        dq = jnp.pad(dq, ((0, 0), (nctx, 0), (0, 0)))
        dz = jnp.concatenate([dq, dk.astype(BF16), dv.astype(BF16)], axis=-1)
        return dz, _sum_leading(db.reshape(db.shape[0], -1, NA_ROWS * GRID_W), "na_sum_dbias").reshape(bias.shape)

    attend.defvjp(attend_fwd, attend_bwd)
    return attend


def _na_core(z, rpb, nctx):
    o = _na_attention(nctx, rpb.shape[0])(z, _na_tables(rpb))
    return jnp.pad(o, ((0, 0), (nctx, 0), (0, 0)))


def _forward(x, ctx, mod, p):
    nctx = ctx.shape[1]
    h = jnp.concatenate([ctx, x], axis=1)
    lb_all = jnp.cumsum(jax.nn.softmax(p["hg_lower_bound"], axis=0), axis=0)
    lb_all = lb_all - lb_all[0]
    gains = p["norm_gains"]
    pre = _rowwise(_fn_pre, [BF16], "pre", 2)
    (a,) = pre([h], gains[0, 0:1], mod[0][:, :, 0:2])
    for i in range(DEPTH):
        tag = f"l{i}"
        if i == 0:
            z = _linear(False, tag + "_sc_in")(a, p["sc_w_in"][0])
            yc = _conv_core(z, p["sc_conv"][0], nctx)
            y = _linear(False, tag + "_sc_out")(yc, p["sc_w_out"][0])
        elif i == 1:
            z = _linear(False, tag + "_hg_in")(a, p["hg_w_in"][0])
            yc = _hgrn_core(z, lb_all[i], p["hg_norm"][0], nctx)
            y = _linear(False, tag + "_hg_out")(yc, p["hg_w_out"][0])
        elif i == 2:
            sp = {k: v[0] for k, v in p.items() if k.startswith("s5_") and k != "s5_w_glu"}
            gz = _s5_core(a, sp, nctx)
            vg = _linear(False, tag + "_s5_glu")(gz.astype(BF16), p["s5_w_glu"][0]).astype(F32)
            d = gz.shape[-1]
            y = (vg[..., :d] * jax.nn.sigmoid(vg[..., d:])).astype(BF16)
        else:
            z = _linear(False, tag + "_na_qkv")(a, p["na_w_qkv"][0])
            yc = _na_core(z, p["na_rpb"][0], nctx)
            y = _linear(False, tag + "_na_out")(yc, p["na_w_out"][0])
        h, a2 = _rowwise(_fn_post_pre, [F32, BF16], tag + "_mix_post", 2)(
            [h, y], gains[i, 1:3], mod[i][:, :, 2:5])
        u = _linear(False, tag + "_mlp_in")(a2, p["mlp_w_in"][i])
        f = _linear(True, tag + "_mlp_out")(u, p["mlp_w_out"][i])
        if i + 1 < DEPTH:
            cols = jnp.stack([gains[i, 3], gains[i + 1, 0]])
            segs = jnp.concatenate([mod[i][:, :, 5:6], mod[i + 1][:, :, 0:2]], axis=2)
            h, a = _rowwise(_fn_post_pre, [F32, BF16], tag + "_mlp_post", 2)([h, f], cols, segs)
        else:
            (h,) = _rowwise(_fn_post, [F32], tag + "_mlp_post", 2)([h, f], gains[i, 3:4], mod[i][:, :, 5:6])
    return h[:, nctx:]


def _local_step(x, ctx, tgt, mod, p):
    y, vjp = jax.vjp(lambda x_, mod_, p_: _forward(x_, ctx, mod_, p_), x, mod, p)
    lblk, dy = _loss_head(y, tgt)
    gx, dmod, gp = vjp(dy)
    return lblk, gx, dmod, gp


PACK_COLS = 512


def _pack_shard(ws):
    return jnp.concatenate([ws[n].reshape(-1, PACK_COLS) for n in BIG], axis=0)


def _unpack_full(buf, shard_shapes):
    out, r = {}, 0
    for n in BIG:
        s = shard_shapes[n]
        nr = math.prod(s) // PACK_COLS
        parts = buf[:, r:r + nr].reshape((N_SHARD,) + s)
        axis = 1 if n in COL_SHARDED else 0
        out[n] = [jnp.concatenate([parts[k, l] for k in range(N_SHARD)], axis=axis) for l in range(s[0])]
        r += nr
    return out


def _pack_grads(gp, shard_shapes):
    per = []
    for k in range(N_SHARD):
        rows = []
        for n in BIG:
            s = shard_shapes[n]
            axis = 1 if n in COL_SHARDED else 0
            width = s[1 + axis]
            for g in gp[n]:
                rows.append(lax.slice_in_dim(g, k * width, (k + 1) * width, axis=axis).reshape(-1, PACK_COLS))
        per.append(jnp.concatenate(rows, axis=0))
    return jnp.stack(per)


def _unpack_shard(buf, shard_shapes):
    out, r = {}, 0
    for n in BIG:
        s = shard_shapes[n]
        nr = math.prod(s) // PACK_COLS
        out[n] = buf[r:r + nr].reshape(s)
        r += nr
    return out


def _shard_cols(a, k, width):
    return lax.dynamic_slice_in_dim(a, k * width, width, axis=a.ndim - 1)


def kernel(x, c, ctx, c_ctx, ada_w, ada_b, norm_gains, mlp_w_in, mlp_w_out, sc_w_in, sc_conv, sc_w_out, hg_w_in, hg_lower_bound, hg_norm, hg_w_out, s5_lam_re, s5_lam_im, s5_log_dt, s5_b_re, s5_b_im, s5_c_re, s5_c_im, s5_d, s5_w_glu, na_w_qkv, na_rpb, na_w_out, loss_target, m_c_ctx, m_ada_w, m_ada_b, m_norm_gains, m_mlp_w_in, m_mlp_w_out, m_sc_w_in, m_sc_conv, m_sc_w_out, m_hg_w_in, m_hg_lower_bound, m_hg_norm, m_hg_w_out, m_s5_lam_re, m_s5_lam_im, m_s5_log_dt, m_s5_b_re, m_s5_b_im, m_s5_c_re, m_s5_c_im, m_s5_d, m_s5_w_glu, m_na_w_qkv, m_na_rpb, m_na_w_out, v_c_ctx, v_ada_w, v_ada_b, v_norm_gains, v_mlp_w_in, v_mlp_w_out, v_sc_w_in, v_sc_conv, v_sc_w_out, v_hg_w_in, v_hg_lower_bound, v_hg_norm, v_hg_w_out, v_s5_lam_re, v_s5_lam_im, v_s5_log_dt, v_s5_b_re, v_s5_b_im, v_s5_c_re, v_s5_c_im, v_s5_d, v_s5_w_glu, v_na_w_qkv, v_na_rpb, v_na_w_out):
    w = dict(c_ctx=c_ctx, ada_w=ada_w, ada_b=ada_b, norm_gains=norm_gains, mlp_w_in=mlp_w_in, mlp_w_out=mlp_w_out,
             sc_w_in=sc_w_in, sc_conv=sc_conv, sc_w_out=sc_w_out, hg_w_in=hg_w_in, hg_lower_bound=hg_lower_bound,
             hg_norm=hg_norm, hg_w_out=hg_w_out, s5_lam_re=s5_lam_re, s5_lam_im=s5_lam_im, s5_log_dt=s5_log_dt,
             s5_b_re=s5_b_re, s5_b_im=s5_b_im, s5_c_re=s5_c_re, s5_c_im=s5_c_im, s5_d=s5_d, s5_w_glu=s5_w_glu,
             na_w_qkv=na_w_qkv, na_rpb=na_rpb, na_w_out=na_w_out)
    mom_m = dict(zip(WEIGHTS, [m_c_ctx, m_ada_w, m_ada_b, m_norm_gains, m_mlp_w_in, m_mlp_w_out, m_sc_w_in, m_sc_conv,
                               m_sc_w_out, m_hg_w_in, m_hg_lower_bound, m_hg_norm, m_hg_w_out, m_s5_lam_re, m_s5_lam_im,
                               m_s5_log_dt, m_s5_b_re, m_s5_b_im, m_s5_c_re, m_s5_c_im, m_s5_d, m_s5_w_glu, m_na_w_qkv,
                               m_na_rpb, m_na_w_out]))
    mom_v = dict(zip(WEIGHTS, [v_c_ctx, v_ada_w, v_ada_b, v_norm_gains, v_mlp_w_in, v_mlp_w_out, v_sc_w_in, v_sc_conv,
                               v_sc_w_out, v_hg_w_in, v_hg_lower_bound, v_hg_norm, v_hg_w_out, v_s5_lam_re, v_s5_lam_im,
                               v_s5_log_dt, v_s5_b_re, v_s5_b_im, v_s5_c_re, v_s5_c_im, v_s5_d, v_s5_w_glu, v_na_w_qkv,
                               v_na_rpb, v_na_w_out]))
    bsz, _, d = x.shape
    ax, ay, ac = lax.axis_index("x"), lax.axis_index("y"), lax.axis_index("c")
    chip = 2 * ax + ay
    dev = 2 * chip + ac
    n_dev = 2 * N_SHARD
    dsh = d // N_SHARD

    shard_shapes = {n: w[n].shape for n in BIG}
    own = _pack_shard({n: w[n].astype(BF16) for n in BIG})
    packed = lax.dynamic_update_slice(_gather_shards(own, "gather_weights"), own[None], (chip, 0, 0))
    full = {n: [a.astype(F32) for a in v] for n, v in _unpack_full(packed, shard_shapes).items()}

    small_shapes = [c.shape] + [w[n].shape for n in SMALL_SHARDED]
    buf_a = _all_gather8(_pad_rows(_pack_rows([c] + [w[n] for n in SMALL_SHARDED])), "gather_small")
    per_dev = [_unpack_rows(buf_a[k], small_shapes) for k in range(n_dev)]
    c_all = jnp.concatenate([per_dev[k][0] for k in range(n_dev)], axis=0)
    for j, n in enumerate(SMALL_SHARDED):
        full[n] = jnp.concatenate([per_dev[2 * s][1 + j] for s in range(N_SHARD)], axis=-1)
    for n in SMALL_REPL:
        full[n] = w[n]

    n_all = c_all.shape[0]
    s_rows = 32
    cond = jnp.concatenate([c_all, c_ctx[None]], axis=0)
    s_all = jnp.pad(jax.nn.silu(cond), ((0, s_rows - n_all - 1), (0, 0))).astype(BF16)
    mod_part = _ada_fwd(s_all, ada_w)
    nsh = mod_part.shape[-1]
    buf_b = _all_gather8(_pad_rows(mod_part.reshape(-1, LANES)), "gather_mod")
    nrow_b = mod_part.size // LANES
    mod_raw = jnp.concatenate([buf_b[2 * s, :nrow_b].reshape(mod_part.shape) for s in range(N_SHARD)], axis=-1)
    mod_raw = mod_raw + ada_b[:, None, :]
    mod_lat = lax.dynamic_slice_in_dim(mod_raw, dev * bsz, bsz, axis=1).reshape(DEPTH, bsz, 1, N_MOD, d)
    mod_ctx = jnp.broadcast_to(mod_raw[:, n_all].reshape(DEPTH, 1, 1, N_MOD, d), (DEPTH, bsz, 1, N_MOD, d))
    mod = jnp.concatenate([mod_ctx, mod_lat], axis=2)

    lblk, grad_x, dmod, gp = _local_step(x, ctx, loss_target, mod, full)

    dmod_rows = jnp.concatenate([dmod[:, :, 1].reshape(DEPTH, bsz, N_MOD * d),
                                 jnp.sum(dmod[:, :, 0], axis=1).reshape(DEPTH, 1, N_MOD * d)], axis=1)
    c_list = [dmod_rows] + [gp[n] for n in SMALL_SHARDED + SMALL_REPL] + [jnp.sum(lblk).reshape(1)]
    c_shapes = [a.shape for a in c_list]
    buf_c = _all_gather8(_pad_rows(_pack_rows(c_list)), "gather_grads")
    sum_c = _sum_leading(buf_c, "sum_grads")
    summed = _unpack_rows(sum_c, c_shapes)
    grads = {}
    for j, n in enumerate(SMALL_SHARDED):
        grads[n] = _shard_cols(summed[1 + j], chip, w[n].shape[-1])
    for j, n in enumerate(SMALL_REPL):
        grads[n] = summed[1 + len(SMALL_SHARDED) + j]
    loss = summed[-1][0]

    dmod_dev = [_unpack_rows(buf_c[k], c_shapes[:1])[0] for k in range(n_dev)]
    dm_lat = jnp.concatenate([t[:, :bsz] for t in dmod_dev], axis=1)
    dm_ctx = summed[0][:, bsz:bsz + 1]
    dm_all = jnp.concatenate([dm_lat, dm_ctx], axis=1)
    grads["ada_b"] = _sum_leading(jnp.moveaxis(dm_all, 1, 0).reshape(n_all + 1, -1, LANES), "sum_ada_b").reshape(ada_b.shape)
    dm_sh = jnp.pad(_shard_cols(dm_all, chip, nsh), ((0, 0), (0, s_rows - n_all - 1), (0, 0)))
    grads["ada_w"] = _ada_dw(s_all, dm_sh)
    ds_part = _ada_ds(dm_sh, ada_w)
    buf_d = _all_gather8(_pad_rows(ds_part[n_all:n_all + 1]), "gather_dcond")
    ds_ctx = _sum_leading(jnp.stack([buf_d[2 * s] for s in range(N_SHARD)]), "sum_dcond")[0]
    grads["c_ctx"] = jax.vjp(jax.nn.silu, c_ctx)[1](ds_ctx)[0]

    g_pack = _pack_grads(gp, shard_shapes)
    c_idx = jnp.reshape(ac, (1,)).astype(jnp.int32)
    s_idx = jnp.reshape(chip, (1,)).astype(jnp.int32)
    part, part_wire = _add_own_half(g_pack, _swap_other_half(g_pack, "rs_swap_half"), c_idx, "rs_add_sibling")
    mine = _add_arrivals(part, _send_to_chips(part_wire, "rs_send_chips"), s_idx, "rs_add_chips")
    other = _swap_reduced_half(mine, "rs_swap_reduced")
    joined = jnp.concatenate([jnp.where(ac == 0, mine, other), jnp.where(ac == 0, other, mine)], axis=0)
    grads.update(_unpack_shard(joined, shard_shapes))

    delta, new_m, new_v = {}, {}, {}
    for n in BIG + ["ada_w"]:
        delta[n], new_m[n], new_v[n] = _adamw(w[n], grads[n], mom_m[n], mom_v[n], "adamw_" + n)
    small = [n for n in WEIGHTS if n not in BIG and n != "ada_w"]
    shapes = [w[n].shape for n in small]
    packs = [_pad_rows(_pack_rows([src[n] for n in small])) for src in (w, grads, mom_m, mom_v)]
    outs = _adamw(*packs, "adamw_small")
    for tgt, buf in zip((delta, new_m, new_v), outs):
        for n, a in zip(small, _unpack_rows(buf, shapes)):
            tgt[n] = a
    return (loss, grad_x, *[grads[n] for n in WEIGHTS], *[delta[n] for n in WEIGHTS],
            *[new_m[n] for n in WEIGHTS], *[new_v[n] for n in WEIGHTS])
```

```python
import functools
import math

import jax
import jax.numpy as jnp
from jax import lax
from jax.experimental import pallas as pl
from jax.experimental.pallas import tpu as pltpu

F32 = jnp.float32
BF16 = jnp.bfloat16
MESH = pl.DeviceIdType.MESH
EPS = 1e-6
TB = 256
LANES = 1024
VMEM_LIMIT = 48 * 1024 * 1024
N_SHARD = 4
DEPTH = 4
N_MOD = 6
GRID_W = 64
HG_HEAD_DIM = 128
HG_CHUNK = 32
S5_GROUP = 16
NA_ROWS = 8
NA_COLS = 16
ADAM_LR, ADAM_B1, ADAM_B2, ADAM_EPS, ADAM_WD, ADAM_STEP = 0.001, 0.9, 0.999, 1e-08, 0.01, 10

BIG = ["mlp_w_in", "mlp_w_out", "sc_w_in", "sc_w_out", "hg_w_in", "hg_w_out", "s5_w_glu", "na_w_qkv", "na_w_out"]
COL_SHARDED = {"mlp_w_in", "sc_w_in", "hg_w_in", "s5_w_glu", "na_w_qkv"}
SMALL_SHARDED = ["norm_gains", "sc_conv", "hg_norm", "s5_d"]
SMALL_REPL = ["hg_lower_bound", "s5_lam_re", "s5_lam_im", "s5_log_dt", "s5_b_re", "s5_b_im", "s5_c_re", "s5_c_im", "na_rpb"]
WEIGHTS = ["c_ctx", "ada_w", "ada_b", "norm_gains", "mlp_w_in", "mlp_w_out", "sc_w_in", "sc_conv", "sc_w_out", "hg_w_in",
           "hg_lower_bound", "hg_norm", "hg_w_out", "s5_lam_re", "s5_lam_im", "s5_log_dt", "s5_b_re", "s5_b_im", "s5_c_re",
           "s5_c_im", "s5_d", "s5_w_glu", "na_w_qkv", "na_rpb", "na_w_out"]


def _cparams(sem=None):
    return pltpu.CompilerParams(dimension_semantics=sem, vmem_limit_bytes=VMEM_LIMIT)


def _rowwise(fn, out_dtypes, name, nseg):
    def seg_index(t):
        return jnp.minimum(t, nseg - 1)

    def in_specs(rows, colp, segp):
        rs = [pl.BlockSpec((None, TB, r.shape[-1]), lambda b, t: (b, t, 0)) for r in rows]
        cs = pl.BlockSpec(colp.shape, lambda b, t: (0, 0))
        ss = pl.BlockSpec((None, None) + segp.shape[2:], lambda b, t: (b, seg_index(t), 0, 0))
        return rs + [cs, ss]

    def load(refs, n, nc, ns):
        rows = [r[...].astype(F32) for r in refs[:n]]
        cols = [refs[n][k:k + 1, :] for k in range(nc)]
        segs = [refs[n + 1][k:k + 1, :] for k in range(ns)]
        return rows, cols, segs

    def out_blocks(rows, colp, segp):
        one = jax.ShapeDtypeStruct((1, colp.shape[-1]), F32)
        return jax.eval_shape(fn, [jax.ShapeDtypeStruct((TB, r.shape[-1]), F32) for r in rows],
                              [one] * colp.shape[0], [one] * segp.shape[2])

    def fwd_call(rows, colp, segp):
        bsz, tlen, _ = rows[0].shape
        n, nc, ns = len(rows), colp.shape[0], segp.shape[2]
        blk = out_blocks(rows, colp, segp)

        def body(*refs):
            vals = fn(*load(refs, n, nc, ns))
            for o, v in zip(refs[n + 2:], vals):
                o[...] = v.astype(o.dtype)

        return pl.pallas_call(
            body, name=name + "_fwd", grid=(bsz, tlen // TB), in_specs=in_specs(rows, colp, segp),
            out_specs=[pl.BlockSpec((None, TB, o.shape[-1]), lambda b, t: (b, t, 0)) for o in blk],
            out_shape=[jax.ShapeDtypeStruct((bsz, tlen, o.shape[-1]), dt) for o, dt in zip(blk, out_dtypes)],
            compiler_params=_cparams(("parallel", "parallel")),
        )(*rows, colp, segp)

    def bwd_call(rows, colp, segp, cts):
        bsz, tlen, _ = rows[0].shape
        n, nc, ns, m = len(rows), colp.shape[0], segp.shape[2], len(cts)

        def body(*refs):
            b, t = pl.program_id(0), pl.program_id(1)
            prim = load(refs, n, nc, ns)
            ct = tuple(r[...].astype(F32) for r in refs[n + 2:n + 2 + m])
            _, vjp = jax.vjp(fn, *prim)
            drows, dcols, dsegs = vjp(ct)
            outs = refs[n + 2 + m:]
            for o, v in zip(outs[:n], drows):
                o[...] = v.astype(o.dtype)
            dcol_ref, dseg_ref = outs[n], outs[n + 1]

            @pl.when((b == 0) & (t == 0))
            def _():
                dcol_ref[...] = jnp.zeros_like(dcol_ref)

            for k, v in enumerate(dcols):
                dcol_ref[k:k + 1, :] += v

            @pl.when(t < nseg)
            def _():
                for k, v in enumerate(dsegs):
                    dseg_ref[k:k + 1, :] = v

            @pl.when(t >= nseg)
            def _():
                for k, v in enumerate(dsegs):
                    dseg_ref[k:k + 1, :] += v

        row_specs = [pl.BlockSpec((None, TB, r.shape[-1]), lambda b, t: (b, t, 0)) for r in rows]
        ct_specs = [pl.BlockSpec((None, TB, c.shape[-1]), lambda b, t: (b, t, 0)) for c in cts]
        return pl.pallas_call(
            body, name=name + "_bwd", grid=(bsz, tlen // TB),
            in_specs=in_specs(rows, colp, segp) + ct_specs,
            out_specs=row_specs + [pl.BlockSpec(colp.shape, lambda b, t: (0, 0)),
                                   pl.BlockSpec((None, None) + segp.shape[2:], lambda b, t: (b, seg_index(t), 0, 0))],
            out_shape=[jax.ShapeDtypeStruct(r.shape, r.dtype) for r in rows]
            + [jax.ShapeDtypeStruct(colp.shape, F32), jax.ShapeDtypeStruct(segp.shape, F32)],
            compiler_params=_cparams(("arbitrary", "arbitrary")),
        )(*rows, colp, segp, *cts)

    @jax.custom_vjp
    def op(rows, colp, segp):
        return tuple(fwd_call(rows, colp, segp))

    def op_fwd(rows, colp, segp):
        return tuple(fwd_call(rows, colp, segp)), (rows, colp, segp)

    def op_bwd(res, cts):
        rows, colp, segp = res
        outs = bwd_call(rows, colp, segp, list(cts))
        return list(outs[:len(rows)]), outs[len(rows)], outs[len(rows) + 1]

    op.defvjp(op_fwd, op_bwd)
    return op


def _rms(x, g):
    return x * lax.rsqrt(jnp.mean(x * x, axis=-1, keepdims=True) + EPS) * g


def _fn_pre(rows, cols, segs):
    return (_rms(rows[0], cols[0]) * (1.0 + segs[1]) + segs[0],)


def _fn_post_pre(rows, cols, segs):
    h2 = rows[0] + segs[0] * _rms(rows[1], cols[0])
    return h2, _rms(h2, cols[1]) * (1.0 + segs[2]) + segs[1]


def _fn_post(rows, cols, segs):
    return (rows[0] + segs[0] * _rms(rows[1], cols[0]),)


def _m_tile(m, limit):
    for t in range(min(m, limit) // 16 * 16, 15, -16):
        if m % t == 0:
            return t
    return m


def _col_tile(n, limit):
    for t in range(min(n, limit), 127, -128):
        if n % t == 0 and t % 128 == 0:
            return t
    return n


def _relu2(x):
    r = jnp.maximum(x, 0.0)
    return r * r


def _mm(x, wb, act, out_dtype, name):
    m, k = x.shape
    n = wb.shape[1]
    tm = _m_tile(m, 1088)
    tn = _col_tile(n, 1024 if k <= 1024 else 512)

    def body(x_ref, w_ref, o_ref):
        xv = x_ref[...]
        if act:
            xv = _relu2(xv.astype(F32))
        o_ref[...] = jnp.dot(xv.astype(BF16), w_ref[...], preferred_element_type=F32).astype(o_ref.dtype)

    return pl.pallas_call(
        body, name=name, grid=(n // tn, m // tm),
        in_specs=[pl.BlockSpec((tm, k), lambda j, i: (i, 0)), pl.BlockSpec((k, tn), lambda j, i: (0, j))],
        out_specs=pl.BlockSpec((tm, tn), lambda j, i: (i, j)),
        out_shape=jax.ShapeDtypeStruct((m, n), out_dtype),
        compiler_params=_cparams(("parallel", "parallel")),
    )(x, wb)


def _mm_dx(dy, wb, x, act, name):
    m, n = dy.shape
    k = wb.shape[0]
    tm = _m_tile(m, 544)

    def body(dy_ref, w_ref, x_ref, o_ref):
        acc = lax.dot_general(dy_ref[...].astype(BF16), w_ref[...], (((1,), (1,)), ((), ())),
                              preferred_element_type=F32)
        if act:
            acc = acc * (2.0 * jnp.maximum(x_ref[...].astype(F32), 0.0))
        o_ref[...] = acc.astype(o_ref.dtype)

    return pl.pallas_call(
        body, name=name, grid=(m // tm,),
        in_specs=[pl.BlockSpec((tm, n), lambda i: (i, 0)), pl.BlockSpec((k, n), lambda i: (0, 0)),
                  pl.BlockSpec((tm, k), lambda i: (i, 0))],
        out_specs=pl.BlockSpec((tm, k), lambda i: (i, 0)),
        out_shape=jax.ShapeDtypeStruct((m, k), x.dtype),
        compiler_params=_cparams(("parallel",)),
    )(dy, wb, x)


def _mm_dw(x, dy, act, name):
    m, k = x.shape
    n = dy.shape[1]
    tm = _m_tile(m, 1088)
    tk, tn = _col_tile(k, 1024), _col_tile(n, 1024)

    def body(x_ref, dy_ref, o_ref):
        @pl.when(pl.program_id(2) == 0)
        def _():
            o_ref[...] = jnp.zeros_like(o_ref)

        xv = x_ref[...]
        if act:
            xv = _relu2(xv.astype(F32))
        o_ref[...] += lax.dot_general(xv.astype(BF16), dy_ref[...].astype(BF16), (((0,), (0,)), ((), ())),
                                      preferred_element_type=F32)

    return pl.pallas_call(
        body, name=name, grid=(k // tk, n // tn, m // tm),
        in_specs=[pl.BlockSpec((tm, tk), lambda a, b, i: (i, a)), pl.BlockSpec((tm, tn), lambda a, b, i: (i, b))],
        out_specs=pl.BlockSpec((tk, tn), lambda a, b, i: (a, b)),
        out_shape=jax.ShapeDtypeStruct((k, n), F32),
        compiler_params=_cparams(("parallel", "parallel", "arbitrary")),
    )(x, dy)


def _linear(act, name, out_dtype=BF16):
    def run(x, wb):
        y = _mm(x.reshape(-1, x.shape[-1]), wb, act, out_dtype, name + "_fwd")
        return y.reshape(x.shape[:-1] + (wb.shape[1],))

    @jax.custom_vjp
    def lin(x, w):
        return run(x, w.astype(BF16))

    def lin_fwd(x, w):
        wb = w.astype(BF16)
        return run(x, wb), (x, wb)

    def lin_bwd(res, dy):
        x, wb = res
        x2, dy2 = x.reshape(-1, x.shape[-1]), dy.reshape(-1, dy.shape[-1])
        dx = _mm_dx(dy2, wb, x2, act, name + "_dx").reshape(x.shape)
        return dx, _mm_dw(x2, dy2, act, name + "_dw")

    lin.defvjp(lin_fwd, lin_bwd)
    return lin


def _loss_head(y, tgt):
    bsz, seq, d = y.shape

    def body(y_ref, t_ref, l_ref, d_ref):
        err = y_ref[...] - t_ref[...]
        d_ref[...] = err * (1.0 / d)
        l_ref[...] = jnp.full(l_ref.shape, 0.5 / d, F32) * jnp.sum(err * err)

    spec = pl.BlockSpec((None, TB, d), lambda b, t: (b, t, 0))
    lblk, dy = pl.pallas_call(
        body, name="loss_head", grid=(bsz, seq // TB), in_specs=[spec, spec],
        out_specs=[pl.BlockSpec((None, None, 8, 128), lambda b, t: (b, t, 0, 0)), spec],
        out_shape=[jax.ShapeDtypeStruct((bsz, seq // TB, 8, 128), F32), jax.ShapeDtypeStruct(y.shape, F32)],
        compiler_params=_cparams(("parallel", "parallel")),
    )(y, tgt)
    return lblk[:, :, 0, 0], dy


def _row_tile(rows, limit=512):
    for tr in range(min(rows, limit), 7, -1):
        if rows % tr == 0 and tr % 8 == 0:
            return tr
    return rows


def _adamw(w, g, m, v, name):
    shape = w.shape
    cols = shape[-1]
    w2, g2, m2, v2 = (a.reshape(-1, cols) for a in (w, g, m, v))
    rows = w2.shape[0]
    tr = _row_tile(rows, max(8, (1 << 19) // cols))
    c1, c2 = 1.0 - ADAM_B1 ** ADAM_STEP, 1.0 - ADAM_B2 ** ADAM_STEP

    def body(w_ref, g_ref, m_ref, v_ref, d_ref, mo_ref, vo_ref):
        gv = g_ref[...]
        mn = ADAM_B1 * m_ref[...] + (1.0 - ADAM_B1) * gv
        vn = ADAM_B2 * v_ref[...] + (1.0 - ADAM_B2) * (gv * gv)
        d_ref[...] = -ADAM_LR * ((mn / c1) / (jnp.sqrt(vn / c2) + ADAM_EPS) + ADAM_WD * w_ref[...])
        mo_ref[...] = mn
        vo_ref[...] = vn

    spec = pl.BlockSpec((tr, cols), lambda i: (i, 0))
    outs = pl.pallas_call(
        body, name=name, grid=(rows // tr,), in_specs=[spec] * 4, out_specs=[spec] * 3,
        out_shape=[jax.ShapeDtypeStruct((rows, cols), F32)] * 3, compiler_params=_cparams(("parallel",)),
    )(w2, g2, m2, v2)
    return tuple(o.reshape(shape) for o in outs)


def _sum_leading(a, name):
    n, rows, cols = a.shape
    tr = _row_tile(rows, max(8, (1 << 18) // cols))

    def body(a_ref, o_ref):
        acc = a_ref[0]
        for j in range(1, n):
            acc = acc + a_ref[j]
        o_ref[...] = acc

    return pl.pallas_call(
        body, name=name, grid=(rows // tr,), in_specs=[pl.BlockSpec((n, tr, cols), lambda i: (0, i, 0))],
        out_specs=pl.BlockSpec((tr, cols), lambda i: (i, 0)), out_shape=jax.ShapeDtypeStruct((rows, cols), F32),
        compiler_params=_cparams(("parallel",)),
    )(a)


def _pack_rows(arrs):
    flat = [a.reshape(-1).astype(F32) for a in arrs]
    flat = [jnp.pad(f, (0, (-f.shape[0]) % LANES)) for f in flat]
    return jnp.concatenate(flat).reshape(-1, LANES)


def _unpack_rows(buf, shapes):
    out, r = [], 0
    for s in shapes:
        n = math.prod(s)
        nr = -(-n // LANES)
        out.append(buf[r:r + nr].reshape(-1)[:n].reshape(s))
        r += nr
    return out


def _pad_rows(buf, mult=8):
    return jnp.pad(buf, ((0, (-buf.shape[0]) % mult), (0, 0)))


def _ada_fwd(s, w):
    nl, d, n = w.shape
    r = s.shape[0]

    def body(s_ref, w_ref, o_ref):
        o_ref[...] = jnp.dot(s_ref[...], w_ref[...].astype(BF16), preferred_element_type=F32)

    return pl.pallas_call(
        body, name="ada_fwd", grid=(nl,),
        in_specs=[pl.BlockSpec((r, d), lambda i: (0, 0)), pl.BlockSpec((None, d, n), lambda i: (i, 0, 0))],
        out_specs=pl.BlockSpec((None, r, n), lambda i: (i, 0, 0)), out_shape=jax.ShapeDtypeStruct((nl, r, n), F32),
        compiler_params=_cparams(("parallel",)),
    )(s, w)


def _ada_dw(s, dm):
    nl, r, n = dm.shape
    d = s.shape[1]

    def body(s_ref, dm_ref, o_ref):
        o_ref[...] = lax.dot_general(s_ref[...], dm_ref[...].astype(BF16), (((0,), (0,)), ((), ())),
                                     preferred_element_type=F32)

    return pl.pallas_call(
        body, name="ada_dw", grid=(nl,),
        in_specs=[pl.BlockSpec((r, d), lambda i: (0, 0)), pl.BlockSpec((None, r, n), lambda i: (i, 0, 0))],
        out_specs=pl.BlockSpec((None, d, n), lambda i: (i, 0, 0)), out_shape=jax.ShapeDtypeStruct((nl, d, n), F32),
        compiler_params=_cparams(("parallel",)),
    )(s, dm)


def _ada_ds(dm, w):
    nl, r, n = dm.shape
    d = w.shape[1]

    def body(dm_ref, w_ref, o_ref):
        @pl.when(pl.program_id(0) == 0)
        def _():
            o_ref[...] = jnp.zeros_like(o_ref)

        o_ref[...] += lax.dot_general(dm_ref[...].astype(BF16), w_ref[...].astype(BF16), (((1,), (1,)), ((), ())),
                                      preferred_element_type=F32)

    return pl.pallas_call(
        body, name="ada_ds", grid=(nl,),
        in_specs=[pl.BlockSpec((None, r, n), lambda i: (i, 0, 0)), pl.BlockSpec((None, d, n), lambda i: (i, 0, 0))],
        out_specs=pl.BlockSpec((r, d), lambda i: (0, 0)), out_shape=jax.ShapeDtypeStruct((r, d), F32),
        compiler_params=_cparams(("arbitrary",)),
    )(dm, w)


_ANY = pl.BlockSpec(memory_space=pl.ANY)


def _position():
    return lax.axis_index("x"), lax.axis_index("y"), lax.axis_index("c")


def _all_gather8(block, name):
    m, n = block.shape

    def body(x_ref, out_ref, send_sems, recv_sems, local_sem):
        x, y, c = _position()
        me, sibling = (x, y, c), (x, y, 1 - c)
        chips = [(1 - x, y), (x, 1 - y), (1 - x, 1 - y)]

        def slot(px, py, pc):
            return out_ref.at[4 * px + 2 * py + pc]

        def copy(k, blk, to, src=None):
            return pltpu.make_async_remote_copy(
                src_ref=slot(*blk) if src is None else src, dst_ref=slot(*blk), send_sem=send_sems.at[k],
                recv_sem=recv_sems.at[k], device_id=to, device_id_type=MESH)

        mine = pltpu.make_async_copy(x_ref, slot(*me), local_sem)
        mine.start()
        first = [copy(0, me, sibling, src=x_ref)]
        first += [copy(1 + j, me, (*chip, c), src=x_ref) for j, chip in enumerate(chips)]
        for cp in first:
            cp.start()
        passed = [copy(4 + j, (*chip, c), sibling) for j, chip in enumerate(chips)]
        for j, chip in enumerate(chips):
            copy(1 + j, (*chip, c), me).wait_recv()
            passed[j].start()
        copy(0, sibling, me).wait_recv()
        for j, chip in enumerate(chips):
            copy(4 + j, (*chip, 1 - c), me).wait_recv()
        for cp in first + passed:
            cp.wait_send()
        mine.wait()

    return pl.pallas_call(
        body, name=name, out_shape=jax.ShapeDtypeStruct((8, m, n), block.dtype), in_specs=[_ANY], out_specs=_ANY,
        scratch_shapes=[pltpu.SemaphoreType.DMA((7,)), pltpu.SemaphoreType.DMA((7,)), pltpu.SemaphoreType.DMA],
    )(block)


def _gather_shards(shard, name):
    rows, cols = shard.shape
    half = rows // 2

    def body(x_ref, out_ref, send_sems, recv_sems):
        x, y, c = _position()
        sibling = (x, y, 1 - c)
        chips = [(1 - x, y), (x, 1 - y), (1 - x, 1 - y)]

        def part(px, py, pc):
            return out_ref.at[2 * px + py, pl.ds(pc * half, half), :]

        def copy(k, blk, to, src=None):
            return pltpu.make_async_remote_copy(
                src_ref=part(*blk) if src is None else src, dst_ref=part(*blk), send_sem=send_sems.at[k],
                recv_sem=recv_sems.at[k], device_id=to, device_id_type=MESH)

        my_half = x_ref.at[pl.ds(c * half, half), :]
        first = [copy(j, (x, y, c), (*chip, c), src=my_half) for j, chip in enumerate(chips)]
        for cp in first:
            cp.start()
        passed = [copy(3 + j, (*chip, c), sibling) for j, chip in enumerate(chips)]
        for j, chip in enumerate(chips):
            copy(j, (*chip, c), sibling).wait_recv()
            passed[j].start()
        for j, chip in enumerate(chips):
            copy(3 + j, (*chip, 1 - c), sibling).wait_recv()
        for cp in first + passed:
            cp.wait_send()

    return pl.pallas_call(
        body, name=name, out_shape=jax.ShapeDtypeStruct((N_SHARD, rows, cols), shard.dtype), in_specs=[_ANY],
        out_specs=_ANY, scratch_shapes=[pltpu.SemaphoreType.DMA((6,)), pltpu.SemaphoreType.DMA((6,))],
    )(shard)


def _swap_other_half(g, name):
    ns, rows, cols = g.shape
    half = rows // 2

    def body(g_ref, out_ref, send_sem, recv_sem):
        x, y, c = _position()
        cp = pltpu.make_async_remote_copy(
            src_ref=g_ref.at[:, pl.ds((1 - c) * half, half), :], dst_ref=out_ref, send_sem=send_sem,
            recv_sem=recv_sem, device_id=(x, y, 1 - c), device_id_type=MESH)
        cp.start()
        cp.wait()

    return pl.pallas_call(
        body, name=name, out_shape=jax.ShapeDtypeStruct((ns, half, cols), g.dtype), in_specs=[_ANY], out_specs=_ANY,
        scratch_shapes=[pltpu.SemaphoreType.DMA, pltpu.SemaphoreType.DMA],
    )(g)


def _add_own_half(g, r, c_idx, name):
    ns, rows, cols = g.shape
    half = rows // 2
    tr = _row_tile(half, max(8, (1 << 19) // cols))
    nb = half // tr

    def body(c_ref, g_ref, r_ref, o_ref, ob_ref):
        acc = g_ref[...] + r_ref[...]
        o_ref[...] = acc
        ob_ref[...] = acc.astype(BF16)

    out = pl.BlockSpec((None, tr, cols), lambda s, i, c_ref: (s, i, 0))
    return pl.pallas_call(
        body, name=name,
        grid_spec=pltpu.PrefetchScalarGridSpec(
            num_scalar_prefetch=1, grid=(ns, nb),
            in_specs=[pl.BlockSpec((None, tr, cols), lambda s, i, c_ref: (s, c_ref[0] * nb + i, 0)), out],
            out_specs=[out, out]),
        out_shape=[jax.ShapeDtypeStruct((ns, half, cols), F32), jax.ShapeDtypeStruct((ns, half, cols), BF16)],
        compiler_params=_cparams(("parallel", "parallel")),
    )(c_idx, g, r)


def _send_to_chips(a, name):
    ns, half, cols = a.shape

    def body(a_ref, out_ref, send_sems, recv_sems):
        x, y, c = _position()
        chips = [(1 - x, y), (x, 1 - y), (1 - x, 1 - y)]
        cps = [pltpu.make_async_remote_copy(
            src_ref=a_ref.at[2 * px + py], dst_ref=out_ref.at[j], send_sem=send_sems.at[j], recv_sem=recv_sems.at[j],
            device_id=(px, py, c), device_id_type=MESH) for j, (px, py) in enumerate(chips)]
        for cp in cps:
            cp.start()
        for cp in cps:
            cp.wait()

    return pl.pallas_call(
        body, name=name, out_shape=jax.ShapeDtypeStruct((3, half, cols), a.dtype), in_specs=[_ANY], out_specs=_ANY,
        scratch_shapes=[pltpu.SemaphoreType.DMA((3,)), pltpu.SemaphoreType.DMA((3,))],
    )(a)


def _add_arrivals(a, r, s_idx, name):
    ns, half, cols = a.shape
    tr = _row_tile(half, max(8, (1 << 18) // cols))

    def body(s_ref, a_ref, r_ref, o_ref):
        o_ref[...] = ((a_ref[...] + r_ref[0].astype(F32)) + r_ref[1].astype(F32)) + r_ref[2].astype(F32)

    return pl.pallas_call(
        body, name=name,
        grid_spec=pltpu.PrefetchScalarGridSpec(
            num_scalar_prefetch=1, grid=(half // tr,),
            in_specs=[pl.BlockSpec((None, tr, cols), lambda i, s_ref: (s_ref[0], i, 0)),
                      pl.BlockSpec((3, tr, cols), lambda i, s_ref: (0, i, 0))],
            out_specs=pl.BlockSpec((tr, cols), lambda i, s_ref: (i, 0))),
        out_shape=jax.ShapeDtypeStruct((half, cols), F32), compiler_params=_cparams(("parallel",)),
    )(s_idx, a, r)


def _swap_reduced_half(f, name):
    def body(f_ref, out_ref, send_sem, recv_sem):
        x, y, c = _position()
        cp = pltpu.make_async_remote_copy(src_ref=f_ref, dst_ref=out_ref, send_sem=send_sem, recv_sem=recv_sem,
                                          device_id=(x, y, 1 - c), device_id_type=MESH)
        cp.start()
        cp.wait()

    return pl.pallas_call(
        body, name=name, out_shape=jax.ShapeDtypeStruct(f.shape, f.dtype), in_specs=[_ANY], out_specs=_ANY,
        scratch_shapes=[pltpu.SemaphoreType.DMA, pltpu.SemaphoreType.DMA],
    )(f)


def _conv_core(z, conv_w, nctx):
    bsz, tlen, d3 = z.shape
    d = d3 // 3
    nblk = tlen // TB
    first_lat = nctx // TB
    hr = 8
    per = TB // hr

    def split(t):
        return t[:, :d].astype(F32), t[:, d:2 * d].astype(F32), t[:, 2 * d:].astype(F32)

    def halo_valid(t):
        prev_ok = (t != 0) & (t != first_lat)
        next_ok = (t != first_lat - 1) & (t != nblk - 1)
        return prev_ok, next_ok

    def shifted(u, prev_row, next_row):
        ridx = lax.broadcasted_iota(jnp.int32, (TB, 1), 0)
        up = jnp.where(ridx == 0, prev_row, pltpu.roll(u, 1, axis=0))
        dn = jnp.where(ridx == TB - 1, next_row, pltpu.roll(u, TB - 1, axis=0))
        return up, dn

    main = lambda w_: pl.BlockSpec((None, TB, w_), lambda b, t: (b, t, 0))
    prev = lambda w_: pl.BlockSpec((None, hr, w_), lambda b, t: (b, jnp.maximum(t * per - 1, 0), 0))
    nxt = lambda w_: pl.BlockSpec((None, hr, w_), lambda b, t: (b, jnp.minimum((t + 1) * per, nblk * per - 1), 0))
    wspec = pl.BlockSpec((3, d), lambda b, t: (0, 0))

    def halo_rows(zp_ref, zn_ref, t):
        prev_ok, next_ok = halo_valid(t)
        _, cgp, vp = split(zp_ref[...])
        _, cgn, vn = split(zn_ref[...])
        up = jnp.where(prev_ok, (cgp * vp)[hr - 1:hr], 0.0)
        un = jnp.where(next_ok, (cgn * vn)[0:1], 0.0)
        return up, un

    def fwd_call(z, w):
        def body(z_ref, zp_ref, zn_ref, w_ref, o_ref):
            bg, cg, v = split(z_ref[...])
            u = cg * v
            up, dn = shifted(u, *halo_rows(zp_ref, zn_ref, pl.program_id(1)))
            o_ref[...] = (bg * (w_ref[0:1, :] * up + w_ref[1:2, :] * u + w_ref[2:3, :] * dn)).astype(o_ref.dtype)

        return pl.pallas_call(
            body, name="conv_fwd", grid=(bsz, nblk), in_specs=[main(d3), prev(d3), nxt(d3), wspec], out_specs=main(d),
            out_shape=jax.ShapeDtypeStruct((bsz, tlen, d), BF16), compiler_params=_cparams(("parallel", "parallel")),
        )(z, z, z, w)

    def bwd_call(z, w, dy):
        def body(z_ref, zp_ref, zn_ref, w_ref, dy_ref, dyp_ref, dyn_ref, dz_ref, dw_ref):
            t = pl.program_id(1)
            prev_ok, next_ok = halo_valid(t)
            bg, cg, v = split(z_ref[...])
            u = cg * v
            up, dn = shifted(u, *halo_rows(zp_ref, zn_ref, t))
            w0, w1, w2 = w_ref[0:1, :], w_ref[1:2, :], w_ref[2:3, :]
            dyv = dy_ref[...].astype(F32)
            dconv = dyv * bg
            bgp = zp_ref[...][:, :d].astype(F32)
            bgn = zn_ref[...][:, :d].astype(F32)
            dc_prev = jnp.where(prev_ok, (dyp_ref[...].astype(F32) * bgp)[hr - 1:hr], 0.0)
            dc_next = jnp.where(next_ok, (dyn_ref[...].astype(F32) * bgn)[0:1], 0.0)
            dc_up, dc_dn = shifted(dconv, dc_prev, dc_next)
            du = w0 * dc_dn + w1 * dconv + w2 * dc_up
            dz_ref[:, 0:d] = (dyv * (w0 * up + w1 * u + w2 * dn)).astype(dz_ref.dtype)
            dz_ref[:, d:2 * d] = (du * v).astype(dz_ref.dtype)
            dz_ref[:, 2 * d:3 * d] = (du * cg).astype(dz_ref.dtype)

            @pl.when((pl.program_id(0) == 0) & (t == 0))
            def _():
                dw_ref[...] = jnp.zeros_like(dw_ref)

            dw_ref[0:1, :] += jnp.sum(dconv * up, axis=0, keepdims=True)
            dw_ref[1:2, :] += jnp.sum(dconv * u, axis=0, keepdims=True)
            dw_ref[2:3, :] += jnp.sum(dconv * dn, axis=0, keepdims=True)

        return pl.pallas_call(
            body, name="conv_bwd", grid=(bsz, nblk),
            in_specs=[main(d3), prev(d3), nxt(d3), wspec, main(d), prev(d), nxt(d)],
            out_specs=[main(d3), wspec],
            out_shape=[jax.ShapeDtypeStruct(z.shape, BF16), jax.ShapeDtypeStruct((3, d), F32)],
            compiler_params=_cparams(("arbitrary", "arbitrary")),
        )(z, z, z, w, dy, dy, dy)

    @jax.custom_vjp
    def op(z, w):
        return fwd_call(z, w)

    def op_fwd(z, w):
        return fwd_call(z, w), (z, w)

    def op_bwd(res, dy):
        return tuple(bwd_call(*res, dy))

    op.defvjp(op_fwd, op_bwd)
    return op(z, conv_w)


HG_BLOCK = 8
HG_UNROLL = 4
HG_VMEM_LIMIT = 56 * 1024 * 1024


def _hg_cumsum(x, rev):
    n = HG_CHUNK
    nrow = x.shape[0]

    def run(v, backwards):
        pos = lax.broadcasted_iota(jnp.int32, (nrow, 1), 0) % n
        k = 1
        while k < n:
            if backwards:
                v = v + jnp.where(pos + k < n, pltpu.roll(v, nrow - k, axis=0), 0.0)
            else:
                v = v + jnp.where(pos >= k, pltpu.roll(v, k, axis=0), 0.0)
            k *= 2
        return v

    @jax.custom_vjp
    def cs(v):
        return run(v, rev)

    cs.defvjp(lambda v: (run(v, rev), None), lambda _, g: (run(g, not rev),))
    return cs(x)


def _hg_local(q, v, tf, lb, rev):
    n = HG_CHUNK
    nrow = q.shape[0]
    nb = nrow // n
    f = lb + (1.0 - lb) * jax.nn.sigmoid(tf)
    kk = 1.0 - f
    b = _hg_cumsum(jnp.log(f), rev)
    pos = lax.broadcasted_iota(jnp.int32, (nb, n, 1), 1)
    b3 = b.reshape(nb, n, -1)
    mid = n - n // 2 if rev else n // 2 - 1
    last = 0 if rev else n - 1
    b_mid = jnp.sum(jnp.where(pos == mid, b3, 0.0), axis=1, keepdims=True)
    b_last = jnp.sum(jnp.where(pos == last, b3, 0.0), axis=1, keepdims=True)
    q3, k3, v3 = q.reshape(nb, n, -1), kk.reshape(nb, n, -1), v.reshape(nb, n, -1)
    qs = (q3 * jnp.exp(b3 - b_mid)).astype(BF16)
    ks = (k3 * jnp.exp(b_mid - b3)).astype(BF16)
    sc = jnp.einsum('ctk,csk->cts', qs, ks, preferred_element_type=F32)
    row = lax.broadcasted_iota(jnp.int32, (1, n, n), 1)
    col = lax.broadcasted_iota(jnp.int32, (1, n, n), 2)
    sc = jnp.where((col >= row) if rev else (col <= row), sc, 0.0).astype(BF16)
    o_intra = jnp.einsum('cts,csv->ctv', sc, v3.astype(BF16), preferred_element_type=F32)
    qe = q3 * jnp.exp(b3)
    ks2 = k3 * jnp.exp(b_last - b3)
    return o_intra.reshape(nrow, -1), qe.reshape(nrow, -1), ks2.reshape(nrow, -1), jnp.exp(b_last).reshape(nb, -1)


def _hg_scan(nctx):
    hd, n = HG_HEAD_DIM, HG_CHUNK
    rb = HG_BLOCK * n

    def geometry(z):
        bsz, tlen, d5 = z.shape
        return bsz, tlen, d5 // 5, (d5 // 5) // hd, tlen // n, nctx // n

    def rev_chunk(j, nch, ncc):
        return jnp.where(j < ncc, ncc - 1 - j, nch - 1 + ncc - j)

    def rows(c):
        return pl.ds(pl.multiple_of(c * n, n), n)

    seq_params = pltpu.CompilerParams(dimension_semantics=("parallel", "parallel"), vmem_limit_bytes=HG_VMEM_LIMIT)
    nt, tn = (((1,), (1,)), ((), ())), (((0,), (0,)), ((), ()))

    def local_specs(z):
        bsz, tlen, d, nh, nch, ncc = geometry(z)
        col = lambda k: pl.BlockSpec((None, rb, hd), lambda b, h, t: (b, t, k * nh + h))
        blk = pl.BlockSpec((None, rb, hd), lambda b, h, t: (b, t, h))
        dec = pl.BlockSpec((None, HG_BLOCK, hd), lambda b, h, t: (b, t, h))
        lbs = pl.BlockSpec((1, hd), lambda b, h, t: (0, h))
        return [col(0), col(1), col(3), col(4), lbs], blk, dec

    def local_fwd(z, lb):
        bsz, tlen, d, nh, nch, ncc = geometry(z)
        ins, blk, dec = local_specs(z)

        def body(q_ref, v_ref, ff_ref, fb_ref, lb_ref, o_ref, qf_ref, kf_ref, df_ref, qb_ref, kb_ref, db_ref):
            q, v, lbv = q_ref[...].astype(F32), v_ref[...].astype(F32), lb_ref[...]
            of, qe, ks, dc = _hg_local(q, v, ff_ref[...].astype(F32), lbv, False)
            qf_ref[...], kf_ref[...], df_ref[...] = qe.astype(BF16), ks.astype(BF16), dc
            ob, qe, ks, dc = _hg_local(q, v, fb_ref[...].astype(F32), lbv, True)
            qb_ref[...], kb_ref[...], db_ref[...] = qe.astype(BF16), ks.astype(BF16), dc
            o_ref[...] = of + ob

        act = jax.ShapeDtypeStruct((bsz, tlen, d), BF16)
        dcs = jax.ShapeDtypeStruct((bsz, nch, d), F32)
        return pl.pallas_call(
            body, name="hg_local", grid=(bsz, nh, tlen // rb), in_specs=ins,
            out_specs=[blk, blk, blk, dec, blk, blk, dec],
            out_shape=[jax.ShapeDtypeStruct((bsz, tlen, d), F32), act, act, dcs, act, act, dcs],
            compiler_params=_cparams(("parallel", "parallel", "parallel")),
        )(z, z, z, z, lb)

    def local_bwd(z, lb, do, dv_in, dqf, dkf, ddf, dqb, dkb, ddb):
        bsz, tlen, d, nh, nch, ncc = geometry(z)
        ins, blk, dec = local_specs(z)

        def body(q_ref, v_ref, ff_ref, fb_ref, lb_ref, do_ref, dvi_ref, dqf_ref, dkf_ref, ddf_ref, dqb_ref, dkb_ref,
                 ddb_ref, dq_ref, dv_ref, dff_ref, dfb_ref, dlb_ref):
            q, v, lbv = q_ref[...].astype(F32), v_ref[...].astype(F32), lb_ref[...]
            dov = do_ref[...]
            dq, dv, dlb = jnp.zeros_like(q), dvi_ref[...], jnp.zeros_like(lbv)
            for rev, f_ref, df_ref, cts in ((False, ff_ref, dff_ref, (dqf_ref, dkf_ref, ddf_ref)),
                                            (True, fb_ref, dfb_ref, (dqb_ref, dkb_ref, ddb_ref))):
                _, vjp = jax.vjp(functools.partial(_hg_local, rev=rev), q, v, f_ref[...].astype(F32), lbv)
                g = vjp((dov, cts[0][...].astype(F32), cts[1][...].astype(F32), cts[2][...]))
                dq, dv, dlb = dq + g[0], dv + g[1], dlb + g[3]
                df_ref[...] = g[2].astype(df_ref.dtype)
            dq_ref[...] = dq.astype(dq_ref.dtype)
            dv_ref[...] = dv.astype(dv_ref.dtype)

            @pl.when(pl.program_id(2) == 0)
            def _():
                dlb_ref[...] = dlb

            @pl.when(pl.program_id(2) != 0)
            def _():
                dlb_ref[...] += dlb

        act = jax.ShapeDtypeStruct((bsz, tlen, d), BF16)
        return pl.pallas_call(
            body, name="hg_local_bwd", grid=(bsz, nh, tlen // rb),
            in_specs=ins + [blk, blk, blk, blk, dec, blk, blk, dec],
            out_specs=[blk, blk, blk, blk, pl.BlockSpec((None, 1, hd), lambda b, h, t: (b, 0, h))],
            out_shape=[act, act, act, act, jax.ShapeDtypeStruct((bsz, 1, d), F32)],
            compiler_params=_cparams(("parallel", "parallel", "arbitrary")),
        )(z, z, z, z, lb, do, dv_in, dqf, dkf, ddf, dqb, dkb, ddb)

    def head_spec(z, k=None):
        bsz, tlen, d, nh, nch, ncc = geometry(z)
        if k is None:
            return pl.BlockSpec((None, tlen, hd), lambda b, h: (b, 0, h))
        return pl.BlockSpec((None, tlen, hd), lambda b, h: (b, 0, k * nh + h))

    def dec_spec(z):
        bsz, tlen, d, nh, nch, ncc = geometry(z)
        return pl.BlockSpec((None, nch, hd), lambda b, h: (b, 0, h))

    def state_fwd(z, o_in, qf, kf, df, qb, kb, db):
        bsz, tlen, d, nh, nch, ncc = geometry(z)
        hs, ds = head_spec(z), dec_spec(z)

        def body(v_ref, oi_ref, qf_ref, kf_ref, df_ref, qb_ref, kb_ref, db_ref, o_ref):
            o_ref[...] = oi_ref[...]
            chains = ((False, qf_ref, kf_ref, df_ref), (True, qb_ref, kb_ref, db_ref))

            def step(j, carry):
                out = []
                for (rev, q_ref, k_ref, d_ref), st in zip(chains, carry):
                    c = rev_chunk(j, nch, ncc) if rev else j
                    sl = rows(c)
                    o_ref[sl, :] += lax.dot_general(q_ref[sl, :], st.astype(BF16), nt, preferred_element_type=F32)
                    out.append(st * d_ref[pl.ds(c, 1), :] + lax.dot_general(v_ref[sl, :], k_ref[sl, :], tn,
                                                                              preferred_element_type=F32))
                return tuple(out)

            zero = jnp.zeros((hd, hd), F32)
            lax.fori_loop(0, nch, step, (zero, zero), unroll=HG_UNROLL)

        return pl.pallas_call(
            body, name="hg_state", grid=(bsz, nh), in_specs=[head_spec(z, 1), hs, hs, hs, ds, hs, hs, ds],
            out_specs=hs, out_shape=jax.ShapeDtypeStruct((bsz, tlen, d), F32), compiler_params=seq_params,
        )(z, o_in, qf, kf, df, qb, kb, db)

    def state_bwd(z, do, qf, kf, df, qb, kb, db):
        bsz, tlen, d, nh, nch, ncc = geometry(z)
        hs, ds = head_spec(z), dec_spec(z)

        def body(v_ref, do_ref, qf_ref, kf_ref, df_ref, qb_ref, kb_ref, db_ref,
                 dv_ref, dqf_ref, dkf_ref, ddf_ref, dqb_ref, dkb_ref, ddb_ref, stf_ref, stb_ref):
            chains = ((False, qf_ref, kf_ref, df_ref, dqf_ref, dkf_ref, ddf_ref, stf_ref),
                      (True, qb_ref, kb_ref, db_ref, dqb_ref, dkb_ref, ddb_ref, stb_ref))

            def fstep(j, carry):
                out = []
                for (rev, q_ref, k_ref, d_ref, _, _, _, st_ref), st in zip(chains, carry):
                    c = rev_chunk(j, nch, ncc) if rev else j
                    sl = rows(c)
                    st_ref[j] = st
                    out.append(st * d_ref[pl.ds(c, 1), :] + lax.dot_general(v_ref[sl, :], k_ref[sl, :], tn,
                                                                              preferred_element_type=F32))
                return tuple(out)

            zero = jnp.zeros((hd, hd), F32)
            lax.fori_loop(0, nch, fstep, (zero, zero), unroll=HG_UNROLL)
            dv_ref[...] = jnp.zeros_like(dv_ref)

            def bstep(i, carry):
                j = nch - 1 - i
                out = []
                for (rev, q_ref, k_ref, d_ref, dq_ref, dk_ref, dd_ref, st_ref), dst in zip(chains, carry):
                    c = rev_chunk(j, nch, ncc) if rev else j
                    sl = rows(c)
                    st = st_ref[j]
                    dob = do_ref[sl, :].astype(BF16)
                    dstb = dst.astype(BF16)
                    dec = d_ref[pl.ds(c, 1), :]
                    dq_ref[sl, :] = jnp.dot(dob, st.astype(BF16), preferred_element_type=F32).astype(dq_ref.dtype)
                    dk_ref[sl, :] = jnp.dot(v_ref[sl, :], dstb, preferred_element_type=F32).astype(dk_ref.dtype)
                    dv_ref[sl, :] += lax.dot_general(k_ref[sl, :], dstb, nt, preferred_element_type=F32)
                    dd_ref[pl.ds(c, 1), :] = jnp.sum(dst * st, axis=0, keepdims=True)
                    out.append(dst * dec + lax.dot_general(dob, q_ref[sl, :], tn, preferred_element_type=F32))
                return tuple(out)

            lax.fori_loop(0, nch, bstep, (zero, zero), unroll=HG_UNROLL)

        act = jax.ShapeDtypeStruct((bsz, tlen, d), BF16)
        dcs = jax.ShapeDtypeStruct((bsz, nch, d), F32)
        return pl.pallas_call(
            body, name="hg_state_bwd", grid=(bsz, nh), in_specs=[head_spec(z, 1), hs, hs, hs, ds, hs, hs, ds],
            out_specs=[hs, hs, hs, ds, hs, hs, ds],
            out_shape=[jax.ShapeDtypeStruct((bsz, tlen, d), F32), act, act, dcs, act, act, dcs],
            scratch_shapes=[pltpu.VMEM((nch, hd, hd), F32), pltpu.VMEM((nch, hd, hd), F32)],
            compiler_params=seq_params,
        )(z, do, qf, kf, df, qb, kb, db)

    @jax.custom_vjp
    def scan(z, lb):
        return state_fwd(z, *local_fwd(z, lb))

    def scan_fwd(z, lb):
        loc = local_fwd(z, lb)
        return state_fwd(z, *loc), (z, lb, loc[1:])

    def scan_bwd(res, do):
        z, lb, loc = res
        dv_in, *dstate = state_bwd(z, do, *loc)
        dq, dv, dff, dfb, dlb = local_bwd(z, lb, do, dv_in, *dstate)
        dz = jnp.concatenate([dq, dv, jnp.zeros_like(dff), dff, dfb], axis=-1)
        return dz, jnp.sum(dlb, axis=0)

    scan.defvjp(scan_fwd, scan_bwd)
    return scan


def _hg_readout(o, gate, g_norm):
    bsz, tlen, d = gate.shape
    nh = d // HG_HEAD_DIM
    rb = TB * nh

    def fn(x, g, gt):
        return x * lax.rsqrt(jnp.mean(x * x, axis=-1, keepdims=True) + EPS) * gt * (g * jax.nn.sigmoid(g))

    spec = pl.BlockSpec((None, rb, HG_HEAD_DIM), lambda b, t: (b, t, 0))
    gspec = pl.BlockSpec((rb, HG_HEAD_DIM), lambda b, t: (0, 0))
    grid = (bsz, tlen // TB)

    def fwd_call(x, g, gt):
        def body(x_ref, g_ref, gt_ref, y_ref):
            y_ref[...] = fn(x_ref[...], g_ref[...].astype(F32), gt_ref[...]).astype(y_ref.dtype)

        return pl.pallas_call(
            body, name="hg_readout_fwd", grid=grid, in_specs=[spec, spec, gspec], out_specs=spec,
            out_shape=jax.ShapeDtypeStruct(g.shape, BF16), compiler_params=_cparams(("parallel", "parallel")),
        )(x, g, gt)

    def bwd_call(x, g, gt, dy):
        def body(x_ref, g_ref, gt_ref, dy_ref, dx_ref, dg_ref, dgt_ref):
            _, vjp = jax.vjp(fn, x_ref[...], g_ref[...].astype(F32), gt_ref[...])
            dx, dg, dgt = vjp(dy_ref[...].astype(F32))
            dx_ref[...] = dx
            dg_ref[...] = dg.astype(dg_ref.dtype)

            @pl.when((pl.program_id(0) == 0) & (pl.program_id(1) == 0))
            def _():
                dgt_ref[...] = jnp.zeros_like(dgt_ref)

            dgt_ref[...] += dgt

        return pl.pallas_call(
            body, name="hg_readout_bwd", grid=grid, in_specs=[spec, spec, gspec, spec],
            out_specs=[spec, spec, gspec],
            out_shape=[jax.ShapeDtypeStruct(x.shape, F32), jax.ShapeDtypeStruct(g.shape, BF16),
                       jax.ShapeDtypeStruct(gt.shape, F32)],
            compiler_params=_cparams(("arbitrary", "arbitrary")),
        )(x, g, gt, dy)

    @jax.custom_vjp
    def op(x, g, gt):
        return fwd_call(x, g, gt)

    def op_fwd(x, g, gt):
        return fwd_call(x, g, gt), (x, g, gt)

    def op_bwd(res, dy):
        return tuple(bwd_call(*res, dy))

    op.defvjp(op_fwd, op_bwd)
    heads = lambda t: t.reshape(bsz, tlen * nh, HG_HEAD_DIM)
    gt = jnp.tile(g_norm.reshape(nh, HG_HEAD_DIM), (TB, 1))
    return op(heads(o), heads(gate), gt).reshape(bsz, tlen, d)


def _hgrn_core(z, lower_bound, g_norm, nctx):
    d = g_norm.shape[-1]
    o = _hg_scan(nctx)(z, lower_bound.reshape(1, d))
    return _hg_readout(o, z[..., 2 * d:3 * d], g_norm)


S5_LC = 16


def _s5_mats(lam_re, lam_im, log_dt, b_re, b_im, c_re, c_im):
    hi = lax.Precision.HIGHEST
    _, ng, ns = lam_re.shape
    lc, gs = S5_LC, S5_GROUP
    lam_re = jnp.minimum(lam_re, -1e-4)
    dt = jnp.exp(log_dt)[:, None, :, None]
    k = jnp.arange(lc + 1, dtype=F32)[None, :, None, None]
    mag, ang = jnp.exp(lam_re[:, None] * dt * k), lam_im[:, None] * dt * k
    p_re, p_im = mag * jnp.cos(ang), mag * jnp.sin(ang)
    a_re, a_im = p_re[:, 1], p_im[:, 1]
    den = lam_re * lam_re + lam_im * lam_im
    f_re = ((a_re - 1) * lam_re + a_im * lam_im) / den
    f_im = (a_im * lam_re - (a_re - 1) * lam_im) / den
    bt_re, bt_im = b_re.transpose(0, 2, 1), b_im.transpose(0, 2, 1)
    bb_re = f_re[:, :, None] * bt_re - f_im[:, :, None] * bt_im
    bb_im = f_re[:, :, None] * bt_im + f_im[:, :, None] * bt_re
    w_re = c_re[:, None] * p_re[:, :, :, None] - c_im[:, None] * p_im[:, :, :, None]
    w_im = c_re[:, None] * p_im[:, :, :, None] + c_im[:, None] * p_re[:, :, :, None]
    kk = jnp.sum(w_re[:, :lc, :, :, None, :] * bb_re[:, None, :, None, :, :]
                 - w_im[:, :lc, :, :, None, :] * bb_im[:, None, :, None, :, :], axis=-1)
    t = jnp.arange(lc)
    lag = jnp.stack([t[:, None] - t[None, :], t[None, :] - t[:, None]])
    lag_hot = (lag[..., None] == jnp.arange(lc)).astype(F32)
    mt = jnp.einsum('rtsk,rkgcd->rgsdtc', lag_hot, kk, precision=hi).reshape(2, ng, lc * gs, lc * gs)
    left = jnp.stack([lc - 1 - t, t])
    left_hot = (left[..., None] == jnp.arange(lc + 1)).astype(F32)
    pw_re = jnp.einsum('rsk,rkgn->rsgn', left_hot, p_re, precision=hi)
    pw_im = jnp.einsum('rsk,rkgn->rsgn', left_hot, p_im, precision=hi)
    pr = pw_re[:, :, :, None] * bb_re[:, None] - pw_im[:, :, :, None] * bb_im[:, None]
    pi = pw_re[:, :, :, None] * bb_im[:, None] + pw_im[:, :, :, None] * bb_re[:, None]
    pt = jnp.concatenate([pr, pi], axis=-1).transpose(0, 2, 1, 3, 4).reshape(2, ng, lc * gs, 2 * ns)
    since = jnp.stack([t + 1, lc - t])
    since_hot = (since[..., None] == jnp.arange(lc + 1)).astype(F32)
    q = jnp.concatenate([jnp.einsum('rtk,rkgcn->rtgcn', since_hot, w_re, precision=hi),
                         -jnp.einsum('rtk,rkgcn->rtgcn', since_hot, w_im, precision=hi)], axis=-1)
    qt = q.transpose(0, 2, 4, 1, 3).reshape(2, ng, 2 * ns, lc * gs)
    a16 = jnp.concatenate([p_re[:, lc], p_im[:, lc]], axis=-1)
    return mt, pt, qt, a16


def _s5_bmm(terms, out_dtype, name, sum_dirs=False):
    ng = terms[0][0].shape[-3]
    ops, dlist = [], []
    for a, b, dn in terms:
        ops += [a, b]
        dlist.append(dn)
    (ca,), (cb,) = dlist[0]
    om, on = terms[0][0].shape[-2:][1 - ca], terms[0][1].shape[-2:][1 - cb]

    def spec(o):
        if o.ndim == 4:
            return pl.BlockSpec((None, None) + o.shape[2:], lambda g, d: (d, g, 0, 0))
        return pl.BlockSpec((None,) + o.shape[1:], lambda g, d: (g, 0, 0))

    def body(*refs):
        acc = None
        for j, dn in enumerate(dlist):
            a, b = refs[2 * j][...].astype(BF16), refs[2 * j + 1][...].astype(BF16)
            r = lax.dot_general(a, b, (dn, ((), ())), preferred_element_type=F32)
            acc = r if acc is None else acc + r
        o_ref = refs[-1]
        if sum_dirs:
            @pl.when(pl.program_id(1) == 0)
            def _():
                o_ref[...] = acc.astype(o_ref.dtype)

            @pl.when(pl.program_id(1) != 0)
            def _():
                o_ref[...] = (o_ref[...].astype(F32) + acc).astype(o_ref.dtype)
        else:
            o_ref[...] = acc.astype(o_ref.dtype)

    if sum_dirs:
        out_spec = pl.BlockSpec((None, om, on), lambda g, d: (g, 0, 0))
        out_shape = jax.ShapeDtypeStruct((ng, om, on), out_dtype)
    else:
        out_spec = pl.BlockSpec((None, None, om, on), lambda g, d: (d, g, 0, 0))
        out_shape = jax.ShapeDtypeStruct((2, ng, om, on), out_dtype)
    return pl.pallas_call(
        body, name=name, grid=(ng, 2), in_specs=[spec(o) for o in ops], out_specs=out_spec, out_shape=out_shape,
        compiler_params=_cparams(("parallel", "arbitrary" if sum_dirs else "parallel")),
    )(*ops)


def _s5_row_block(rows):
    return 32 if rows % 32 == 0 else rows


def _s5_chunk_order(j, d, nc, ncc):
    return jnp.where(d == 0, j, jnp.where(j < ncc, ncc - 1 - j, nc - 1 + ncc - j))


def _s5_scan_fwd(z, a1, a2, ncc, name):
    nd, nc, rows, lanes = z.shape
    rb = _s5_row_block(rows)

    def body(z_ref, a1_ref, a2_ref, x_ref):
        a1v, a2v = a1_ref[...], a2_ref[...]
        d = pl.program_id(0)

        def step(j, x):
            c = _s5_chunk_order(j, d, nc, ncc)
            x_ref[c] = x
            return a1v * x + a2v * pltpu.roll(x, lanes // 2, axis=1) + z_ref[c]

        lax.fori_loop(0, nc, step, jnp.zeros((rb, lanes), F32))

    blk = pl.BlockSpec((None, nc, rb, lanes), lambda d, r: (d, 0, r, 0))
    par = pl.BlockSpec((None, rb, lanes), lambda d, r: (d, r, 0))
    return pl.pallas_call(
        body, name=name, grid=(nd, rows // rb), in_specs=[blk, par, par], out_specs=blk,
        out_shape=jax.ShapeDtypeStruct(z.shape, F32), compiler_params=_cparams(("parallel", "parallel")),
    )(z, a1, a2)


def _s5_scan_bwd(dxp, xp, a1, a2b, ncc, name):
    nd, nc, rows, lanes = dxp.shape
    rb = _s5_row_block(rows)

    def body(dxp_ref, xp_ref, a1_ref, a2_ref, dz_ref, p1_ref, p2_ref):
        a1v, a2v = a1_ref[...], a2_ref[...]
        zero = jnp.zeros((rb, lanes), F32)
        d = pl.program_id(0)

        def step(i, carry):
            g_next, nxt, p1, p2 = carry
            c = _s5_chunk_order(nc - 1 - i, d, nc, ncc)
            g = nxt + a1v * g_next + a2v * pltpu.roll(g_next, lanes // 2, axis=1)
            dz_ref[c] = g
            x = xp_ref[c]
            return g, dxp_ref[c], p1 + x * g, p2 + pltpu.roll(x, lanes // 2, axis=1) * g

        _, _, p1, p2 = lax.fori_loop(0, nc, step, (zero, zero, zero, zero))
        p1_ref[...] = p1
        p2_ref[...] = p2

    blk = pl.BlockSpec((None, nc, rb, lanes), lambda d, r: (d, 0, r, 0))
    par = pl.BlockSpec((None, rb, lanes), lambda d, r: (d, r, 0))
    return pl.pallas_call(
        body, name=name, grid=(nd, rows // rb), in_specs=[blk, blk, par, par], out_specs=[blk, par, par],
        out_shape=[jax.ShapeDtypeStruct(dxp.shape, F32), jax.ShapeDtypeStruct((nd, rows, lanes), F32),
                   jax.ShapeDtypeStruct((nd, rows, lanes), F32)],
        compiler_params=_cparams(("parallel", "parallel")),
    )(dxp, xp, a1, a2b)


def _s5_rows(t, bsz):
    nd, ng, m, k = t.shape
    return t.reshape(nd, ng, bsz, m // bsz, k).transpose(0, 3, 2, 1, 4).reshape(nd, m // bsz, bsz * ng, k)


def _s5_groups(t, bsz):
    nd, nc, rows, k = t.shape
    return t.reshape(nd, nc, bsz, rows // bsz, k).transpose(0, 3, 2, 1, 4).reshape(nd, rows // bsz, bsz * nc, k)


def _s5_coeffs(a16, bsz):
    half = a16.shape[-1] // 2
    re, im = a16[..., :half], a16[..., half:]
    tile = lambda v: jnp.tile(v, (1, bsz, 1))
    return tile(jnp.concatenate([re, re], -1)), tile(jnp.concatenate([-im, im], -1)), tile(jnp.concatenate([im, -im], -1))


def _s5_apply(bsz, ncc):
    nn, nt, tn = ((1,), (0,)), ((1,), (1,)), ((0,), (0,))

    def run(u, mt, pt, qt, a16):
        a1, a2, _ = _s5_coeffs(a16, bsz)
        z = _s5_bmm([(u, pt, nn)], F32, "s5_z")
        xp = _s5_scan_fwd(_s5_rows(z, bsz), a1, a2, ncc, "s5_scan")
        xg = _s5_groups(xp, bsz).astype(BF16)
        y = _s5_bmm([(u, mt, nn), (xg, qt, nn)], BF16, "s5_y", sum_dirs=True)
        return y, (xp, xg)

    @jax.custom_vjp
    def apply(u, mt, pt, qt, a16):
        return run(u, mt.astype(BF16), pt.astype(BF16), qt.astype(BF16), a16)[0]

    def apply_fwd(u, mt, pt, qt, a16):
        mtb, ptb, qtb = mt.astype(BF16), pt.astype(BF16), qt.astype(BF16)
        y, (xp, xg) = run(u, mtb, ptb, qtb, a16)
        return y, (u, mtb, ptb, qtb, a16, xp, xg)

    def apply_bwd(res, dy):
        u, mtb, ptb, qtb, a16, xp, xg = res
        a1, _, a2b = _s5_coeffs(a16, bsz)
        dyb = dy.astype(BF16)
        dmt = _s5_bmm([(u, dyb, tn)], F32, "s5_dmt", sum_dirs=True) * 0.5
        dqt = _s5_bmm([(xg, dyb, tn)], F32, "s5_dqt")
        dxp = _s5_bmm([(dyb, qtb, nt)], F32, "s5_dxp")
        dz, p1, p2 = _s5_scan_bwd(_s5_rows(dxp, bsz), xp, a1, a2b, ncc, "s5_scan_bwd")
        dzg = _s5_groups(dz, bsz).astype(BF16)
        dpt = _s5_bmm([(u, dzg, tn)], F32, "s5_dpt")
        du = _s5_bmm([(dyb, mtb, nt), (dzg, ptb, nt)], BF16, "s5_du", sum_dirs=True)
        half = a16.shape[-1] // 2
        p1 = jnp.sum(p1.reshape(2, bsz, -1, 2 * half), axis=1)
        p2 = jnp.sum(p2.reshape(2, bsz, -1, 2 * half), axis=1)
        da16 = jnp.concatenate([p1[..., :half] + p1[..., half:], p2[..., half:] - p2[..., :half]], axis=-1)
        return du, jnp.stack([dmt, dmt]), dpt, dqt, da16

    apply.defvjp(apply_fwd, apply_bwd)
    return apply


def _s5_placement(lanes):
    lc, gs = S5_LC, S5_GROUP
    lane = jnp.arange(lanes)
    col = (lane // gs) * (lc * gs) + lane % gs
    t = jnp.arange(lc)
    return (col[None, :, None] + t[:, None, None] * gs == jnp.arange(lanes * lc)[None, None, :]).astype(BF16)


def _s5_relayout(bsz, tlen, d):
    lc, gs, lanes = S5_LC, S5_GROUP, 128
    nc, ng, gb = tlen // lc, d // S5_GROUP, 128 // S5_GROUP
    width = lanes * lc
    tok = pl.BlockSpec((None, tlen, lanes), lambda b, j: (b, 0, j))
    grp = pl.BlockSpec((gb, nc, lc * gs), lambda b, j: (j, b, 0))
    plc = pl.BlockSpec((lc, lanes, width), lambda b, j: (0, 0, 0))
    grid = (bsz, d // lanes)
    sem = _cparams(("parallel", "parallel"))

    def chunk_call(a):
        def body(a_ref, p_ref, o_ref, f_ref):
            f_ref[...] = a_ref[...].astype(F32)
            acc = None
            for t in range(lc):
                rows = f_ref[pl.ds(t, nc, stride=lc), :].astype(BF16)
                part = jnp.dot(rows, p_ref[t], preferred_element_type=F32)
                acc = part if acc is None else acc + part
            for g in range(gb):
                o_ref[g] = acc[:, g * lc * gs:(g + 1) * lc * gs].astype(o_ref.dtype)

        return pl.pallas_call(
            body, name="s5_chunk", grid=grid, in_specs=[tok, plc], out_specs=grp,
            out_shape=jax.ShapeDtypeStruct((ng, bsz * nc, lc * gs), BF16),
            scratch_shapes=[pltpu.VMEM((tlen, lanes), F32)], compiler_params=sem,
        )(a, _s5_placement(lanes))

    def unchunk_call(y):
        def body(y_ref, p_ref, o_ref, f_ref):
            cat = jnp.concatenate([y_ref[g] for g in range(gb)], axis=1)
            for t in range(lc):
                f_ref[pl.ds(t, nc, stride=lc), :] = lax.dot_general(cat, p_ref[t], (((1,), (1,)), ((), ())),
                                                                    preferred_element_type=F32)
            o_ref[...] = f_ref[...].astype(o_ref.dtype)

        return pl.pallas_call(
            body, name="s5_unchunk", grid=grid, in_specs=[grp, plc], out_specs=tok,
            out_shape=jax.ShapeDtypeStruct((bsz, tlen, d), BF16),
            scratch_shapes=[pltpu.VMEM((tlen, lanes), F32)], compiler_params=sem,
        )(y, _s5_placement(lanes))

    @jax.custom_vjp
    def chunk(a):
        return chunk_call(a)

    chunk.defvjp(lambda a: (chunk_call(a), None), lambda _, g: (unchunk_call(g),))

    @jax.custom_vjp
    def unchunk(y):
        return unchunk_call(y)

    unchunk.defvjp(lambda y: (unchunk_call(y), None), lambda _, g: (chunk_call(g),))
    return chunk, unchunk


def _s5_core(a, p, nctx):
    bsz, tlen, d = a.shape
    mt, pt, qt, a16 = _s5_mats(p["s5_lam_re"], p["s5_lam_im"], p["s5_log_dt"], p["s5_b_re"], p["s5_b_im"],
                               p["s5_c_re"], p["s5_c_im"])
    chunk, unchunk = _s5_relayout(bsz, tlen, d)
    y = unchunk(_s5_apply(bsz, nctx // S5_LC)(chunk(a), mt, pt, qt, a16))
    return jax.nn.gelu(p["s5_d"] * a.astype(F32) + y.astype(F32))


NA_LANES = 256
NA_MASKED = -1e30


def _na_tables(rpb):
    hi = lax.Precision.HIGHEST
    nh = rpb.shape[0]
    q = jnp.arange(GRID_W)
    kc = jnp.arange(GRID_W)
    q_start = jnp.clip(q - NA_COLS // 2, 0, GRID_W - NA_COLS)
    inwin = (kc[None, :] >= q_start[:, None]) & (kc[None, :] < q_start[:, None] + NA_COLS)
    dc = kc[None, :] - q[:, None] + NA_COLS - 1
    onehot = ((dc[:, :, None] == jnp.arange(2 * NA_COLS - 1)) & inwin[:, :, None]).astype(F32)
    a = jnp.arange(NA_ROWS)[None, :] - jnp.arange(NA_ROWS)[:, None] + NA_ROWS - 1
    tab = jnp.einsum('hskc,qlc->hsqkl', rpb[:, a, :], onehot, precision=hi)
    tab = jnp.where(inwin[None, None, :, None, :], tab, NA_MASKED)
    return tab.reshape(nh, NA_ROWS, GRID_W, NA_ROWS * GRID_W)


def _na_math(q2, kw, vw, kc, vc, bias, dh):
    scale = dh ** -0.5
    nq, nhb = q2.shape[0], NA_LANES // dh
    lane_head = lax.broadcasted_iota(jnp.int32, (1, NA_LANES), 1) // dh
    nt = (((1,), (1,)), ((), ()))
    kwb, vwb, kcb, vcb = (t.astype(BF16) for t in (kw, vw, kc, vc))
    qs = jnp.concatenate([jnp.where(lane_head == j, q2, 0.0) for j in range(nhb)], axis=0).astype(BF16)
    s_loc = lax.dot_general(qs, kwb, nt, preferred_element_type=F32) * scale + bias.reshape(nhb * nq, -1)
    s_ctx = lax.dot_general(qs, kcb, nt, preferred_element_type=F32) * scale
    m = jnp.maximum(jnp.max(s_loc, axis=-1, keepdims=True), jnp.max(s_ctx, axis=-1, keepdims=True))
    m = lax.stop_gradient(m)
    p_loc, p_ctx = jnp.exp(s_loc - m), jnp.exp(s_ctx - m)
    inv = 1.0 / (jnp.sum(p_loc, axis=-1, keepdims=True) + jnp.sum(p_ctx, axis=-1, keepdims=True))
    o_all = (jnp.dot((p_loc * inv).astype(BF16), vwb, preferred_element_type=F32)
             + jnp.dot((p_ctx * inv).astype(BF16), vcb, preferred_element_type=F32))
    out = jnp.zeros(q2.shape, F32)
    for j in range(nhb):
        out = out + jnp.where(lane_head == j, o_all[j * nq:(j + 1) * nq], 0.0)
    return out


def _na_attention(nctx, nh):
    def geometry(z):
        bsz, tlen, d3 = z.shape
        d = d3 // 3
        rows = (tlen - nctx) // GRID_W
        return bsz, tlen, d, rows, d // nh, d // NA_LANES

    def key_row0(r, rows):
        return jnp.clip(r - NA_ROWS // 2, 0, rows - NA_ROWS)

    def specs(z):
        bsz, tlen, d, rows, dh, nlb = geometry(z)
        hpb = NA_LANES // dh
        qs = pl.BlockSpec((None, GRID_W, NA_LANES), lambda b, h, r: (b, nctx // GRID_W + r, h))
        ks = pl.BlockSpec((None, tlen, NA_LANES), lambda b, h, r: (b, 0, nlb + h))
        vs = pl.BlockSpec((None, tlen, NA_LANES), lambda b, h, r: (b, 0, 2 * nlb + h))
        bs = pl.BlockSpec((hpb, None, GRID_W, NA_ROWS * GRID_W), lambda b, h, r: (h, r - key_row0(r, rows), 0, 0))
        os_ = pl.BlockSpec((None, GRID_W, NA_LANES), lambda b, h, r: (b, r, h))
        return qs, ks, vs, bs, os_

    def window(r, rows):
        return pl.ds(pl.multiple_of(nctx + key_row0(r, rows) * GRID_W, GRID_W), NA_ROWS * GRID_W)

    def fwd_call(z, bias):
        bsz, tlen, d, rows, dh, nlb = geometry(z)
        qs, ks, vs, bs, os_ = specs(z)

        def body(q_ref, k_ref, v_ref, b_ref, o_ref):
            win = window(pl.program_id(2), rows)
            o_ref[...] = _na_math(q_ref[...].astype(F32), k_ref[win, :], v_ref[win, :], k_ref[0:nctx, :],
                                  v_ref[0:nctx, :], b_ref[...], dh).astype(o_ref.dtype)

        return pl.pallas_call(
            body, name="na_fwd", grid=(bsz, nlb, rows), in_specs=[qs, ks, vs, bs], out_specs=os_,
            out_shape=jax.ShapeDtypeStruct((bsz, tlen - nctx, d), BF16),
            compiler_params=_cparams(("parallel", "parallel", "arbitrary")),
        )(z, z, z, bias)

    def bwd_call(z, bias, do):
        bsz, tlen, d, rows, dh, nlb = geometry(z)
        hpb = NA_LANES // dh
        qs, ks, vs, bs, os_ = specs(z)

        def body(q_ref, k_ref, v_ref, b_ref, do_ref, dq_ref, dk_ref, dv_ref, db_ref):
            r = pl.program_id(2)
            win = window(r, rows)

            @pl.when(r == 0)
            def _():
                dk_ref[...] = jnp.zeros_like(dk_ref)
                dv_ref[...] = jnp.zeros_like(dv_ref)

            prim = (q_ref[...].astype(F32), k_ref[win, :].astype(F32), v_ref[win, :].astype(F32),
                    k_ref[0:nctx, :].astype(F32), v_ref[0:nctx, :].astype(F32), b_ref[...])
            _, vjp = jax.vjp(functools.partial(_na_math, dh=dh), *prim)
            dq, dkw, dvw, dkc, dvc, db = vjp(do_ref[...].astype(F32))
            dq_ref[...] = dq.astype(dq_ref.dtype)
            dk_ref[win, :] += dkw
            dv_ref[win, :] += dvw
            dk_ref[0:nctx, :] += dkc
            dv_ref[0:nctx, :] += dvc
            prev = jnp.maximum(r - 1, 0)
            first = (r == 0) | ((r - key_row0(r, rows)) != (prev - key_row0(prev, rows)))

            @pl.when(first)
            def _():
                db_ref[...] = db

            @pl.when(jnp.logical_not(first))
            def _():
                db_ref[...] += db

        acc = pl.BlockSpec((None, tlen, NA_LANES), lambda b, h, r: (b, 0, h))
        dbs = pl.BlockSpec((None, hpb, None, GRID_W, NA_ROWS * GRID_W),
                           lambda b, h, r: (b, h, r - key_row0(r, rows), 0, 0))
        return pl.pallas_call(
            body, name="na_bwd", grid=(bsz, nlb, rows), in_specs=[qs, ks, vs, bs, os_],
            out_specs=[os_, acc, acc, dbs],
            out_shape=[jax.ShapeDtypeStruct((bsz, tlen - nctx, d), BF16), jax.ShapeDtypeStruct((bsz, tlen, d), F32),
                       jax.ShapeDtypeStruct((bsz, tlen, d), F32), jax.ShapeDtypeStruct((bsz,) + bias.shape, F32)],
            compiler_params=_cparams(("parallel", "parallel", "arbitrary")),
        )(z, z, z, bias, do)

    @jax.custom_vjp
    def attend(z, bias):
        return fwd_call(z, bias)

    def attend_fwd(z, bias):
        return fwd_call(z, bias), (z, bias)

    def attend_bwd(res, do):
        z, bias = res
        dq, dk, dv, db = bwd_call(z, bias, do)
        dq = jnp.pad(dq, ((0, 0), (nctx, 0), (0, 0)))
        dz = jnp.concatenate([dq, dk.astype(BF16), dv.astype(BF16)], axis=-1)
        return dz, _sum_leading(db.reshape(db.shape[0], -1, NA_ROWS * GRID_W), "na_sum_dbias").reshape(bias.shape)

    attend.defvjp(attend_fwd, attend_bwd)
    return attend


def _na_core(z, rpb, nctx):
    o = _na_attention(nctx, rpb.shape[0])(z, _na_tables(rpb))
    return jnp.pad(o, ((0, 0), (nctx, 0), (0, 0)))


def _forward(x, ctx, mod, p):
    nctx = ctx.shape[1]
    h = jnp.concatenate([ctx, x], axis=1)
    lb_all = jnp.cumsum(jax.nn.softmax(p["hg_lower_bound"], axis=0), axis=0)
    lb_all = lb_all - lb_all[0]
    gains = p["norm_gains"]
    pre = _rowwise(_fn_pre, [BF16], "pre", 2)
    (a,) = pre([h], gains[0, 0:1], mod[0][:, :, 0:2])
    for i in range(DEPTH):
        tag = f"l{i}"
        if i == 0:
            z = _linear(False, tag + "_sc_in")(a, p["sc_w_in"][0])
            yc = _conv_core(z, p["sc_conv"][0], nctx)
            y = _linear(False, tag + "_sc_out")(yc, p["sc_w_out"][0])
        elif i == 1:
            z = _linear(False, tag + "_hg_in")(a, p["hg_w_in"][0])
            yc = _hgrn_core(z, lb_all[i], p["hg_norm"][0], nctx)
            y = _linear(False, tag + "_hg_out")(yc, p["hg_w_out"][0])
        elif i == 2:
            sp = {k: v[0] for k, v in p.items() if k.startswith("s5_") and k != "s5_w_glu"}
            gz = _s5_core(a, sp, nctx)
            vg = _linear(False, tag + "_s5_glu")(gz.astype(BF16), p["s5_w_glu"][0]).astype(F32)
            d = gz.shape[-1]
            y = (vg[..., :d] * jax.nn.sigmoid(vg[..., d:])).astype(BF16)
        else:
            z = _linear(False, tag + "_na_qkv")(a, p["na_w_qkv"][0])
            yc = _na_core(z, p["na_rpb"][0], nctx)
            y = _linear(False, tag + "_na_out")(yc, p["na_w_out"][0])
        h, a2 = _rowwise(_fn_post_pre, [F32, BF16], tag + "_mix_post", 2)(
            [h, y], gains[i, 1:3], mod[i][:, :, 2:5])
        u = _linear(False, tag + "_mlp_in")(a2, p["mlp_w_in"][i])
        f = _linear(True, tag + "_mlp_out")(u, p["mlp_w_out"][i])
        if i + 1 < DEPTH:
            cols = jnp.stack([gains[i, 3], gains[i + 1, 0]])
            segs = jnp.concatenate([mod[i][:, :, 5:6], mod[i + 1][:, :, 0:2]], axis=2)
            h, a = _rowwise(_fn_post_pre, [F32, BF16], tag + "_mlp_post", 2)([h, f], cols, segs)
        else:
            (h,) = _rowwise(_fn_post, [F32], tag + "_mlp_post", 2)([h, f], gains[i, 3:4], mod[i][:, :, 5:6])
    return h[:, nctx:]


def _local_step(x, ctx, tgt, mod, p):
    y, vjp = jax.vjp(lambda x_, mod_, p_: _forward(x_, ctx, mod_, p_), x, mod, p)
    lblk, dy = _loss_head(y, tgt)
    gx, dmod, gp = vjp(dy)
    return lblk, gx, dmod, gp


PACK_COLS = 512


def _pack_shard(ws):
    return jnp.concatenate([ws[n].reshape(-1, PACK_COLS) for n in BIG], axis=0)


def _unpack_full(buf, shard_shapes):
    out, r = {}, 0
    for n in BIG:
        s = shard_shapes[n]
        nr = math.prod(s) // PACK_COLS
        parts = buf[:, r:r + nr].reshape((N_SHARD,) + s)
        axis = 1 if n in COL_SHARDED else 0
        out[n] = [jnp.concatenate([parts[k, l] for k in range(N_SHARD)], axis=axis) for l in range(s[0])]
        r += nr
    return out


def _pack_grads(gp, shard_shapes):
    per = []
    for k in range(N_SHARD):
        rows = []
        for n in BIG:
            s = shard_shapes[n]
            axis = 1 if n in COL_SHARDED else 0
            width = s[1 + axis]
            for g in gp[n]:
                rows.append(lax.slice_in_dim(g, k * width, (k + 1) * width, axis=axis).reshape(-1, PACK_COLS))
        per.append(jnp.concatenate(rows, axis=0))
    return jnp.stack(per)


def _unpack_shard(buf, shard_shapes):
    out, r = {}, 0
    for n in BIG:
        s = shard_shapes[n]
        nr = math.prod(s) // PACK_COLS
        out[n] = buf[r:r + nr].reshape(s)
        r += nr
    return out


def _shard_cols(a, k, width):
    return lax.dynamic_slice_in_dim(a, k * width, width, axis=a.ndim - 1)


def kernel(x, c, ctx, c_ctx, ada_w, ada_b, norm_gains, mlp_w_in, mlp_w_out, sc_w_in, sc_conv, sc_w_out, hg_w_in, hg_lower_bound, hg_norm, hg_w_out, s5_lam_re, s5_lam_im, s5_log_dt, s5_b_re, s5_b_im, s5_c_re, s5_c_im, s5_d, s5_w_glu, na_w_qkv, na_rpb, na_w_out, loss_target, m_c_ctx, m_ada_w, m_ada_b, m_norm_gains, m_mlp_w_in, m_mlp_w_out, m_sc_w_in, m_sc_conv, m_sc_w_out, m_hg_w_in, m_hg_lower_bound, m_hg_norm, m_hg_w_out, m_s5_lam_re, m_s5_lam_im, m_s5_log_dt, m_s5_b_re, m_s5_b_im, m_s5_c_re, m_s5_c_im, m_s5_d, m_s5_w_glu, m_na_w_qkv, m_na_rpb, m_na_w_out, v_c_ctx, v_ada_w, v_ada_b, v_norm_gains, v_mlp_w_in, v_mlp_w_out, v_sc_w_in, v_sc_conv, v_sc_w_out, v_hg_w_in, v_hg_lower_bound, v_hg_norm, v_hg_w_out, v_s5_lam_re, v_s5_lam_im, v_s5_log_dt, v_s5_b_re, v_s5_b_im, v_s5_c_re, v_s5_c_im, v_s5_d, v_s5_w_glu, v_na_w_qkv, v_na_rpb, v_na_w_out):
    w = dict(c_ctx=c_ctx, ada_w=ada_w, ada_b=ada_b, norm_gains=norm_gains, mlp_w_in=mlp_w_in, mlp_w_out=mlp_w_out,
             sc_w_in=sc_w_in, sc_conv=sc_conv, sc_w_out=sc_w_out, hg_w_in=hg_w_in, hg_lower_bound=hg_lower_bound,
             hg_norm=hg_norm, hg_w_out=hg_w_out, s5_lam_re=s5_lam_re, s5_lam_im=s5_lam_im, s5_log_dt=s5_log_dt,
             s5_b_re=s5_b_re, s5_b_im=s5_b_im, s5_c_re=s5_c_re, s5_c_im=s5_c_im, s5_d=s5_d, s5_w_glu=s5_w_glu,
             na_w_qkv=na_w_qkv, na_rpb=na_rpb, na_w_out=na_w_out)
    mom_m = dict(zip(WEIGHTS, [m_c_ctx, m_ada_w, m_ada_b, m_norm_gains, m_mlp_w_in, m_mlp_w_out, m_sc_w_in, m_sc_conv,
                               m_sc_w_out, m_hg_w_in, m_hg_lower_bound, m_hg_norm, m_hg_w_out, m_s5_lam_re, m_s5_lam_im,
                               m_s5_log_dt, m_s5_b_re, m_s5_b_im, m_s5_c_re, m_s5_c_im, m_s5_d, m_s5_w_glu, m_na_w_qkv,
                               m_na_rpb, m_na_w_out]))
    mom_v = dict(zip(WEIGHTS, [v_c_ctx, v_ada_w, v_ada_b, v_norm_gains, v_mlp_w_in, v_mlp_w_out, v_sc_w_in, v_sc_conv,
                               v_sc_w_out, v_hg_w_in, v_hg_lower_bound, v_hg_norm, v_hg_w_out, v_s5_lam_re, v_s5_lam_im,
                               v_s5_log_dt, v_s5_b_re, v_s5_b_im, v_s5_c_re, v_s5_c_im, v_s5_d, v_s5_w_glu, v_na_w_qkv,
                               v_na_rpb, v_na_w_out]))
    bsz, _, d = x.shape
    ax, ay, ac = lax.axis_index("x"), lax.axis_index("y"), lax.axis_index("c")
    chip = 2 * ax + ay
    dev = 2 * chip + ac
    n_dev = 2 * N_SHARD
    dsh = d // N_SHARD

    shard_shapes = {n: w[n].shape for n in BIG}
    own = _pack_shard({n: w[n].astype(BF16) for n in BIG})
    packed = lax.dynamic_update_slice(_gather_shards(own, "gather_weights"), own[None], (chip, 0, 0))
    full = {n: [a.astype(F32) for a in v] for n, v in _unpack_full(packed, shard_shapes).items()}

    small_shapes = [c.shape] + [w[n].shape for n in SMALL_SHARDED]
    buf_a = _all_gather8(_pad_rows(_pack_rows([c] + [w[n] for n in SMALL_SHARDED])), "gather_small")
    per_dev = [_unpack_rows(buf_a[k], small_shapes) for k in range(n_dev)]
    c_all = jnp.concatenate([per_dev[k][0] for k in range(n_dev)], axis=0)
    for j, n in enumerate(SMALL_SHARDED):
        full[n] = jnp.concatenate([per_dev[2 * s][1 + j] for s in range(N_SHARD)], axis=-1)
    for n in SMALL_REPL:
        full[n] = w[n]

    n_all = c_all.shape[0]
    s_rows = 32
    cond = jnp.concatenate([c_all, c_ctx[None]], axis=0)
    s_all = jnp.pad(jax.nn.silu(cond), ((0, s_rows - n_all - 1), (0, 0))).astype(BF16)
    mod_part = _ada_fwd(s_all, ada_w)
    nsh = mod_part.shape[-1]
    buf_b = _all_gather8(_pad_rows(mod_part.reshape(-1, LANES)), "gather_mod")
    nrow_b = mod_part.size // LANES
    mod_raw = jnp.concatenate([buf_b[2 * s, :nrow_b].reshape(mod_part.shape) for s in range(N_SHARD)], axis=-1)
    mod_raw = mod_raw + ada_b[:, None, :]
    mod_lat = lax.dynamic_slice_in_dim(mod_raw, dev * bsz, bsz, axis=1).reshape(DEPTH, bsz, 1, N_MOD, d)
    mod_ctx = jnp.broadcast_to(mod_raw[:, n_all].reshape(DEPTH, 1, 1, N_MOD, d), (DEPTH, bsz, 1, N_MOD, d))
    mod = jnp.concatenate([mod_ctx, mod_lat], axis=2)

    lblk, grad_x, dmod, gp = _local_step(x, ctx, loss_target, mod, full)

    dmod_rows = jnp.concatenate([dmod[:, :, 1].reshape(DEPTH, bsz, N_MOD * d),
                                 jnp.sum(dmod[:, :, 0], axis=1).reshape(DEPTH, 1, N_MOD * d)], axis=1)
    c_list = [dmod_rows] + [gp[n] for n in SMALL_SHARDED + SMALL_REPL] + [jnp.sum(lblk).reshape(1)]
    c_shapes = [a.shape for a in c_list]
    buf_c = _all_gather8(_pad_rows(_pack_rows(c_list)), "gather_grads")
    sum_c = _sum_leading(buf_c, "sum_grads")
    summed = _unpack_rows(sum_c, c_shapes)
    grads = {}
    for j, n in enumerate(SMALL_SHARDED):
        grads[n] = _shard_cols(summed[1 + j], chip, w[n].shape[-1])
    for j, n in enumerate(SMALL_REPL):
        grads[n] = summed[1 + len(SMALL_SHARDED) + j]
    loss = summed[-1][0]

    dmod_dev = [_unpack_rows(buf_c[k], c_shapes[:1])[0] for k in range(n_dev)]
    dm_lat = jnp.concatenate([t[:, :bsz] for t in dmod_dev], axis=1)
    dm_ctx = summed[0][:, bsz:bsz + 1]
    dm_all = jnp.concatenate([dm_lat, dm_ctx], axis=1)
    grads["ada_b"] = _sum_leading(jnp.moveaxis(dm_all, 1, 0).reshape(n_all + 1, -1, LANES), "sum_ada_b").reshape(ada_b.shape)
    dm_sh = jnp.pad(_shard_cols(dm_all, chip, nsh), ((0, 0), (0, s_rows - n_all - 1), (0, 0)))
    grads["ada_w"] = _ada_dw(s_all, dm_sh)
    ds_part = _ada_ds(dm_sh, ada_w)
    buf_d = _all_gather8(_pad_rows(ds_part[n_all:n_all + 1]), "gather_dcond")
    ds_ctx = _sum_leading(jnp.stack([buf_d[2 * s] for s in range(N_SHARD)]), "sum_dcond")[0]
    grads["c_ctx"] = jax.vjp(jax.nn.silu, c_ctx)[1](ds_ctx)[0]

    g_pack = _pack_grads(gp, shard_shapes)
    c_idx = jnp.reshape(ac, (1,)).astype(jnp.int32)
    s_idx = jnp.reshape(chip, (1,)).astype(jnp.int32)
    part, part_wire = _add_own_half(g_pack, _swap_other_half(g_pack, "rs_swap_half"), c_idx, "rs_add_sibling")
    mine = _add_arrivals(part, _send_to_chips(part_wire, "rs_send_chips"), s_idx, "rs_add_chips")
    other = _swap_reduced_half(mine, "rs_swap_reduced")
    joined = jnp.concatenate([jnp.where(ac == 0, mine, other), jnp.where(ac == 0, other, mine)], axis=0)
    grads.update(_unpack_shard(joined, shard_shapes))

    delta, new_m, new_v = {}, {}, {}
    for n in BIG + ["ada_w"]:
        delta[n], new_m[n], new_v[n] = _adamw(w[n], grads[n], mom_m[n], mom_v[n], "adamw_" + n)
    small = [n for n in WEIGHTS if n not in BIG and n != "ada_w"]
    shapes = [w[n].shape for n in small]
    packs = [_pad_rows(_pack_rows([src[n] for n in small])) for src in (w, grads, mom_m, mom_v)]
    outs = _adamw(*packs, "adamw_small")
    for tgt, buf in zip((delta, new_m, new_v), outs):
        for n, a in zip(small, _unpack_rows(buf, shapes)):
            tgt[n] = a
    return (loss, grad_x, *[grads[n] for n in WEIGHTS], *[delta[n] for n in WEIGHTS],
            *[new_m[n] for n in WEIGHTS], *[new_v[n] for n in WEIGHTS])
```

```python
import functools
import math

import jax
import jax.numpy as jnp
from jax import lax
from jax.experimental import pallas as pl
from jax.experimental.pallas import tpu as pltpu

F32 = jnp.float32
BF16 = jnp.bfloat16
MESH = pl.DeviceIdType.MESH
EPS = 1e-6
TB = 256
LANES = 1024
VMEM_LIMIT = 48 * 1024 * 1024
N_SHARD = 4
DEPTH = 4
N_MOD = 6
GRID_W = 64
HG_HEAD_DIM = 128
HG_CHUNK = 32
S5_GROUP = 16
NA_ROWS = 8
NA_COLS = 16
ADAM_LR, ADAM_B1, ADAM_B2, ADAM_EPS, ADAM_WD, ADAM_STEP = 0.001, 0.9, 0.999, 1e-08, 0.01, 10

BIG = ["mlp_w_in", "mlp_w_out", "sc_w_in", "sc_w_out", "hg_w_in", "hg_w_out", "s5_w_glu", "na_w_qkv", "na_w_out"]
COL_SHARDED = {"mlp_w_in", "sc_w_in", "hg_w_in", "s5_w_glu", "na_w_qkv"}
SMALL_SHARDED = ["norm_gains", "sc_conv", "hg_norm", "s5_d"]
SMALL_REPL = ["hg_lower_bound", "s5_lam_re", "s5_lam_im", "s5_log_dt", "s5_b_re", "s5_b_im", "s5_c_re", "s5_c_im", "na_rpb"]
WEIGHTS = ["c_ctx", "ada_w", "ada_b", "norm_gains", "mlp_w_in", "mlp_w_out", "sc_w_in", "sc_conv", "sc_w_out", "hg_w_in",
           "hg_lower_bound", "hg_norm", "hg_w_out", "s5_lam_re", "s5_lam_im", "s5_log_dt", "s5_b_re", "s5_b_im", "s5_c_re",
           "s5_c_im", "s5_d", "s5_w_glu", "na_w_qkv", "na_rpb", "na_w_out"]


def _cparams(sem=None):
    return pltpu.CompilerParams(dimension_semantics=sem, vmem_limit_bytes=VMEM_LIMIT)


def _rowwise(fn, out_dtypes, name, nseg):
    def seg_index(t):
        return jnp.minimum(t, nseg - 1)

    def in_specs(rows, colp, segp):
        rs = [pl.BlockSpec((None, TB, r.shape[-1]), lambda b, t: (b, t, 0)) for r in rows]
        cs = pl.BlockSpec(colp.shape, lambda b, t: (0, 0))
        ss = pl.BlockSpec((None, None) + segp.shape[2:], lambda b, t: (b, seg_index(t), 0, 0))
        return rs + [cs, ss]

    def load(refs, n, nc, ns):
        rows = [r[...].astype(F32) for r in refs[:n]]
        cols = [refs[n][k:k + 1, :] for k in range(nc)]
        segs = [refs[n + 1][k:k + 1, :] for k in range(ns)]
        return rows, cols, segs

    def out_blocks(rows, colp, segp):
        one = jax.ShapeDtypeStruct((1, colp.shape[-1]), F32)
        return jax.eval_shape(fn, [jax.ShapeDtypeStruct((TB, r.shape[-1]), F32) for r in rows],
                              [one] * colp.shape[0], [one] * segp.shape[2])

    def fwd_call(rows, colp, segp):
        bsz, tlen, _ = rows[0].shape
        n, nc, ns = len(rows), colp.shape[0], segp.shape[2]
        blk = out_blocks(rows, colp, segp)

        def body(*refs):
            vals = fn(*load(refs, n, nc, ns))
            for o, v in zip(refs[n + 2:], vals):
                o[...] = v.astype(o.dtype)

        return pl.pallas_call(
            body, name=name + "_fwd", grid=(bsz, tlen // TB), in_specs=in_specs(rows, colp, segp),
            out_specs=[pl.BlockSpec((None, TB, o.shape[-1]), lambda b, t: (b, t, 0)) for o in blk],
            out_shape=[jax.ShapeDtypeStruct((bsz, tlen, o.shape[-1]), dt) for o, dt in zip(blk, out_dtypes)],
            compiler_params=_cparams(("parallel", "parallel")),
        )(*rows, colp, segp)

    def bwd_call(rows, colp, segp, cts):
        bsz, tlen, _ = rows[0].shape
        n, nc, ns, m = len(rows), colp.shape[0], segp.shape[2], len(cts)

        def body(*refs):
            b, t = pl.program_id(0), pl.program_id(1)
            prim = load(refs, n, nc, ns)
            ct = tuple(r[...].astype(F32) for r in refs[n + 2:n + 2 + m])
            _, vjp = jax.vjp(fn, *prim)
            drows, dcols, dsegs = vjp(ct)
            outs = refs[n + 2 + m:]
            for o, v in zip(outs[:n], drows):
                o[...] = v.astype(o.dtype)
            dcol_ref, dseg_ref = outs[n], outs[n + 1]

            @pl.when((b == 0) & (t == 0))
            def _():
                dcol_ref[...] = jnp.zeros_like(dcol_ref)

            for k, v in enumerate(dcols):
                dcol_ref[k:k + 1, :] += v

            @pl.when(t < nseg)
            def _():
                for k, v in enumerate(dsegs):
                    dseg_ref[k:k + 1, :] = v

            @pl.when(t >= nseg)
            def _():
                for k, v in enumerate(dsegs):
                    dseg_ref[k:k + 1, :] += v

        row_specs = [pl.BlockSpec((None, TB, r.shape[-1]), lambda b, t: (b, t, 0)) for r in rows]
        ct_specs = [pl.BlockSpec((None, TB, c.shape[-1]), lambda b, t: (b, t, 0)) for c in cts]
        return pl.pallas_call(
            body, name=name + "_bwd", grid=(bsz, tlen // TB),
            in_specs=in_specs(rows, colp, segp) + ct_specs,
            out_specs=row_specs + [pl.BlockSpec(colp.shape, lambda b, t: (0, 0)),
                                   pl.BlockSpec((None, None) + segp.shape[2:], lambda b, t: (b, seg_index(t), 0, 0))],
            out_shape=[jax.ShapeDtypeStruct(r.shape, r.dtype) for r in rows]
            + [jax.ShapeDtypeStruct(colp.shape, F32), jax.ShapeDtypeStruct(segp.shape, F32)],
            compiler_params=_cparams(("arbitrary", "arbitrary")),
        )(*rows, colp, segp, *cts)

    @jax.custom_vjp
    def op(rows, colp, segp):
        return tuple(fwd_call(rows, colp, segp))

    def op_fwd(rows, colp, segp):
        return tuple(fwd_call(rows, colp, segp)), (rows, colp, segp)

    def op_bwd(res, cts):
        rows, colp, segp = res
        outs = bwd_call(rows, colp, segp, list(cts))
        return list(outs[:len(rows)]), outs[len(rows)], outs[len(rows) + 1]

    op.defvjp(op_fwd, op_bwd)
    return op


def _rms(x, g):
    return x * lax.rsqrt(jnp.mean(x * x, axis=-1, keepdims=True) + EPS) * g


def _fn_pre(rows, cols, segs):
    return (_rms(rows[0], cols[0]) * (1.0 + segs[1]) + segs[0],)


def _fn_post_pre(rows, cols, segs):
    h2 = rows[0] + segs[0] * _rms(rows[1], cols[0])
    return h2, _rms(h2, cols[1]) * (1.0 + segs[2]) + segs[1]


def _fn_post(rows, cols, segs):
    return (rows[0] + segs[0] * _rms(rows[1], cols[0]),)


def _m_tile(m, limit):
    for t in range(min(m, limit) // 16 * 16, 15, -16):
        if m % t == 0:
            return t
    return m


def _col_tile(n, limit):
    for t in range(min(n, limit), 127, -128):
        if n % t == 0 and t % 128 == 0:
            return t
    return n


def _relu2(x):
    r = jnp.maximum(x, 0.0)
    return r * r


def _mm(x, wb, act, out_dtype, name):
    m, k = x.shape
    n = wb.shape[1]
    tm = _m_tile(m, 1088)
    tn = _col_tile(n, 1024 if k <= 1024 else 512)

    def body(x_ref, w_ref, o_ref):
        xv = x_ref[...]
        if act:
            xv = _relu2(xv.astype(F32))
        o_ref[...] = jnp.dot(xv.astype(BF16), w_ref[...], preferred_element_type=F32).astype(o_ref.dtype)

    return pl.pallas_call(
        body, name=name, grid=(n // tn, m // tm),
        in_specs=[pl.BlockSpec((tm, k), lambda j, i: (i, 0)), pl.BlockSpec((k, tn), lambda j, i: (0, j))],
        out_specs=pl.BlockSpec((tm, tn), lambda j, i: (i, j)),
        out_shape=jax.ShapeDtypeStruct((m, n), out_dtype),
        compiler_params=_cparams(("parallel", "parallel")),
    )(x, wb)


def _mm_dx(dy, wb, x, act, name):
    m, n = dy.shape
    k = wb.shape[0]
    tm = _m_tile(m, 544)

    def body(dy_ref, w_ref, x_ref, o_ref):
        acc = lax.dot_general(dy_ref[...].astype(BF16), w_ref[...], (((1,), (1,)), ((), ())),
                              preferred_element_type=F32)
        if act:
            acc = acc * (2.0 * jnp.maximum(x_ref[...].astype(F32), 0.0))
        o_ref[...] = acc.astype(o_ref.dtype)

    return pl.pallas_call(
        body, name=name, grid=(m // tm,),
        in_specs=[pl.BlockSpec((tm, n), lambda i: (i, 0)), pl.BlockSpec((k, n), lambda i: (0, 0)),
                  pl.BlockSpec((tm, k), lambda i: (i, 0))],
        out_specs=pl.BlockSpec((tm, k), lambda i: (i, 0)),
        out_shape=jax.ShapeDtypeStruct((m, k), x.dtype),
        compiler_params=_cparams(("parallel",)),
    )(dy, wb, x)


def _mm_dw(x, dy, act, name):
    m, k = x.shape
    n = dy.shape[1]
    tm = _m_tile(m, 1088)
    tk, tn = _col_tile(k, 1024), _col_tile(n, 1024)

    def body(x_ref, dy_ref, o_ref):
        @pl.when(pl.program_id(2) == 0)
        def _():
            o_ref[...] = jnp.zeros_like(o_ref)

        xv = x_ref[...]
        if act:
            xv = _relu2(xv.astype(F32))
        o_ref[...] += lax.dot_general(xv.astype(BF16), dy_ref[...].astype(BF16), (((0,), (0,)), ((), ())),
                                      preferred_element_type=F32)

    return pl.pallas_call(
        body, name=name, grid=(k // tk, n // tn, m // tm),
        in_specs=[pl.BlockSpec((tm, tk), lambda a, b, i: (i, a)), pl.BlockSpec((tm, tn), lambda a, b, i: (i, b))],
        out_specs=pl.BlockSpec((tk, tn), lambda a, b, i: (a, b)),
        out_shape=jax.ShapeDtypeStruct((k, n), F32),
        compiler_params=_cparams(("parallel", "parallel", "arbitrary")),
    )(x, dy)


def _linear(act, name, out_dtype=BF16):
    def run(x, wb):
        y = _mm(x.reshape(-1, x.shape[-1]), wb, act, out_dtype, name + "_fwd")
        return y.reshape(x.shape[:-1] + (wb.shape[1],))

    @jax.custom_vjp
    def lin(x, wb, wg):
        return run(x, wb)

    def lin_fwd(x, wb, wg):
        return run(x, wb), (x, wb)

    def lin_bwd(res, dy):
        x, wb = res
        x2, dy2 = x.reshape(-1, x.shape[-1]), dy.reshape(-1, dy.shape[-1])
        dx = _mm_dx(dy2, wb, x2, act, name + "_dx").reshape(x.shape)
        return dx, jnp.zeros_like(wb), _mm_dw(x2, dy2, act, name + "_dw")

    lin.defvjp(lin_fwd, lin_bwd)
    return lin


def _loss_head(y, tgt):
    bsz, seq, d = y.shape

    def body(y_ref, t_ref, l_ref, d_ref):
        err = y_ref[...] - t_ref[...]
        d_ref[...] = err * (1.0 / d)
        l_ref[...] = jnp.full(l_ref.shape, 0.5 / d, F32) * jnp.sum(err * err)

    spec = pl.BlockSpec((None, TB, d), lambda b, t: (b, t, 0))
    lblk, dy = pl.pallas_call(
        body, name="loss_head", grid=(bsz, seq // TB), in_specs=[spec, spec],
        out_specs=[pl.BlockSpec((None, None, 8, 128), lambda b, t: (b, t, 0, 0)), spec],
        out_shape=[jax.ShapeDtypeStruct((bsz, seq // TB, 8, 128), F32), jax.ShapeDtypeStruct(y.shape, F32)],
        compiler_params=_cparams(("parallel", "parallel")),
    )(y, tgt)
    return lblk[:, :, 0, 0], dy


def _row_tile(rows, limit=512):
    for tr in range(min(rows, limit), 7, -1):
        if rows % tr == 0 and tr % 8 == 0:
            return tr
    return rows


def _adamw(w, g, m, v, name):
    shape = w.shape
    cols = shape[-1]
    w2, g2, m2, v2 = (a.reshape(-1, cols) for a in (w, g, m, v))
    rows = w2.shape[0]
    tr = _row_tile(rows, max(8, (1 << 19) // cols))
    c1, c2 = 1.0 - ADAM_B1 ** ADAM_STEP, 1.0 - ADAM_B2 ** ADAM_STEP

    def body(w_ref, g_ref, m_ref, v_ref, d_ref, mo_ref, vo_ref):
        gv = g_ref[...]
        mn = ADAM_B1 * m_ref[...] + (1.0 - ADAM_B1) * gv
        vn = ADAM_B2 * v_ref[...] + (1.0 - ADAM_B2) * (gv * gv)
        d_ref[...] = -ADAM_LR * ((mn / c1) / (jnp.sqrt(vn / c2) + ADAM_EPS) + ADAM_WD * w_ref[...])
        mo_ref[...] = mn
        vo_ref[...] = vn

    spec = pl.BlockSpec((tr, cols), lambda i: (i, 0))
    outs = pl.pallas_call(
        body, name=name, grid=(rows // tr,), in_specs=[spec] * 4, out_specs=[spec] * 3,
        out_shape=[jax.ShapeDtypeStruct((rows, cols), F32)] * 3, compiler_params=_cparams(("parallel",)),
    )(w2, g2, m2, v2)
    return tuple(o.reshape(shape) for o in outs)


def _sum_leading(a, name):
    n, rows, cols = a.shape
    tr = _row_tile(rows, max(8, (1 << 18) // cols))

    def body(a_ref, o_ref):
        acc = a_ref[0]
        for j in range(1, n):
            acc = acc + a_ref[j]
        o_ref[...] = acc

    return pl.pallas_call(
        body, name=name, grid=(rows // tr,), in_specs=[pl.BlockSpec((n, tr, cols), lambda i: (0, i, 0))],
        out_specs=pl.BlockSpec((tr, cols), lambda i: (i, 0)), out_shape=jax.ShapeDtypeStruct((rows, cols), F32),
        compiler_params=_cparams(("parallel",)),
    )(a)


def _pack_rows(arrs):
    flat = [a.reshape(-1).astype(F32) for a in arrs]
    flat = [jnp.pad(f, (0, (-f.shape[0]) % LANES)) for f in flat]
    return jnp.concatenate(flat).reshape(-1, LANES)


def _unpack_rows(buf, shapes):
    out, r = [], 0
    for s in shapes:
        n = math.prod(s)
        nr = -(-n // LANES)
        out.append(buf[r:r + nr].reshape(-1)[:n].reshape(s))
        r += nr
    return out


def _pad_rows(buf, mult=8):
    return jnp.pad(buf, ((0, (-buf.shape[0]) % mult), (0, 0)))


def _ada_fwd(s, w):
    nl, d, n = w.shape
    r = s.shape[0]

    def body(s_ref, w_ref, o_ref):
        o_ref[...] = jnp.dot(s_ref[...], w_ref[...].astype(BF16), preferred_element_type=F32)

    return pl.pallas_call(
        body, name="ada_fwd", grid=(nl,),
        in_specs=[pl.BlockSpec((r, d), lambda i: (0, 0)), pl.BlockSpec((None, d, n), lambda i: (i, 0, 0))],
        out_specs=pl.BlockSpec((None, r, n), lambda i: (i, 0, 0)), out_shape=jax.ShapeDtypeStruct((nl, r, n), F32),
        compiler_params=_cparams(("parallel",)),
    )(s, w)


def _ada_dw(s, dm):
    nl, r, n = dm.shape
    d = s.shape[1]

    def body(s_ref, dm_ref, o_ref):
        o_ref[...] = lax.dot_general(s_ref[...], dm_ref[...].astype(BF16), (((0,), (0,)), ((), ())),
                                     preferred_element_type=F32)

    return pl.pallas_call(
        body, name="ada_dw", grid=(nl,),
        in_specs=[pl.BlockSpec((r, d), lambda i: (0, 0)), pl.BlockSpec((None, r, n), lambda i: (i, 0, 0))],
        out_specs=pl.BlockSpec((None, d, n), lambda i: (i, 0, 0)), out_shape=jax.ShapeDtypeStruct((nl, d, n), F32),
        compiler_params=_cparams(("parallel",)),
    )(s, dm)


def _ada_ds(dm, w):
    nl, r, n = dm.shape
    d = w.shape[1]

    def body(dm_ref, w_ref, o_ref):
        @pl.when(pl.program_id(0) == 0)
        def _():
            o_ref[...] = jnp.zeros_like(o_ref)

        o_ref[...] += lax.dot_general(dm_ref[...].astype(BF16), w_ref[...].astype(BF16), (((1,), (1,)), ((), ())),
                                      preferred_element_type=F32)

    return pl.pallas_call(
        body, name="ada_ds", grid=(nl,),
        in_specs=[pl.BlockSpec((None, r, n), lambda i: (i, 0, 0)), pl.BlockSpec((None, d, n), lambda i: (i, 0, 0))],
        out_specs=pl.BlockSpec((r, d), lambda i: (0, 0)), out_shape=jax.ShapeDtypeStruct((r, d), F32),
        compiler_params=_cparams(("arbitrary",)),
    )(dm, w)


_ANY = pl.BlockSpec(memory_space=pl.ANY)


def _position():
    return lax.axis_index("x"), lax.axis_index("y"), lax.axis_index("c")


def _all_gather8(block, name):
    m, n = block.shape

    def body(x_ref, out_ref, send_sems, recv_sems, local_sem):
        x, y, c = _position()
        me, sibling = (x, y, c), (x, y, 1 - c)
        chips = [(1 - x, y), (x, 1 - y), (1 - x, 1 - y)]

        def slot(px, py, pc):
            return out_ref.at[4 * px + 2 * py + pc]

        def copy(k, blk, to, src=None):
            return pltpu.make_async_remote_copy(
                src_ref=slot(*blk) if src is None else src, dst_ref=slot(*blk), send_sem=send_sems.at[k],
                recv_sem=recv_sems.at[k], device_id=to, device_id_type=MESH)

        mine = pltpu.make_async_copy(x_ref, slot(*me), local_sem)
        mine.start()
        first = [copy(0, me, sibling, src=x_ref)]
        first += [copy(1 + j, me, (*chip, c), src=x_ref) for j, chip in enumerate(chips)]
        for cp in first:
            cp.start()
        passed = [copy(4 + j, (*chip, c), sibling) for j, chip in enumerate(chips)]
        for j, chip in enumerate(chips):
            copy(1 + j, (*chip, c), me).wait_recv()
            passed[j].start()
        copy(0, sibling, me).wait_recv()
        for j, chip in enumerate(chips):
            copy(4 + j, (*chip, 1 - c), me).wait_recv()
        for cp in first + passed:
            cp.wait_send()
        mine.wait()

    return pl.pallas_call(
        body, name=name, out_shape=jax.ShapeDtypeStruct((8, m, n), block.dtype), in_specs=[_ANY], out_specs=_ANY,
        scratch_shapes=[pltpu.SemaphoreType.DMA((7,)), pltpu.SemaphoreType.DMA((7,)), pltpu.SemaphoreType.DMA],
    )(block)


def _gather_shards(shard, name):
    rows, cols = shard.shape
    half = rows // 2

    def body(x_ref, out_ref, send_sems, recv_sems):
        x, y, c = _position()
        sibling = (x, y, 1 - c)
        chips = [(1 - x, y), (x, 1 - y), (1 - x, 1 - y)]

        def part(px, py, pc):
            return out_ref.at[2 * px + py, pl.ds(pc * half, half), :]

        def copy(k, blk, to, src=None):
            return pltpu.make_async_remote_copy(
                src_ref=part(*blk) if src is None else src, dst_ref=part(*blk), send_sem=send_sems.at[k],
                recv_sem=recv_sems.at[k], device_id=to, device_id_type=MESH)

        my_half = x_ref.at[pl.ds(c * half, half), :]
        first = [copy(j, (x, y, c), (*chip, c), src=my_half) for j, chip in enumerate(chips)]
        for cp in first:
            cp.start()
        passed = [copy(3 + j, (*chip, c), sibling) for j, chip in enumerate(chips)]
        for j, chip in enumerate(chips):
            copy(j, (*chip, c), sibling).wait_recv()
            passed[j].start()
        for j, chip in enumerate(chips):
            copy(3 + j, (*chip, 1 - c), sibling).wait_recv()
        for cp in first + passed:
            cp.wait_send()

    return pl.pallas_call(
        body, name=name, out_shape=jax.ShapeDtypeStruct((N_SHARD, rows, cols), shard.dtype), in_specs=[_ANY],
        out_specs=_ANY, scratch_shapes=[pltpu.SemaphoreType.DMA((6,)), pltpu.SemaphoreType.DMA((6,))],
    )(shard)


def _swap_other_half(g, name):
    ns, rows, cols = g.shape
    half = rows // 2

    def body(g_ref, out_ref, send_sem, recv_sem):
        x, y, c = _position()
        cp = pltpu.make_async_remote_copy(
            src_ref=g_ref.at[:, pl.ds((1 - c) * half, half), :], dst_ref=out_ref, send_sem=send_sem,
            recv_sem=recv_sem, device_id=(x, y, 1 - c), device_id_type=MESH)
        cp.start()
        cp.wait()

    return pl.pallas_call(
        body, name=name, out_shape=jax.ShapeDtypeStruct((ns, half, cols), g.dtype), in_specs=[_ANY], out_specs=_ANY,
        scratch_shapes=[pltpu.SemaphoreType.DMA, pltpu.SemaphoreType.DMA],
    )(g)


def _add_own_half(g, r, c_idx, name):
    ns, rows, cols = g.shape
    half = rows // 2
    tr = _row_tile(half, max(8, (1 << 19) // cols))
    nb = half // tr

    def body(c_ref, g_ref, r_ref, o_ref, ob_ref):
        acc = g_ref[...] + r_ref[...]
        o_ref[...] = acc
        ob_ref[...] = acc.astype(BF16)

    out = pl.BlockSpec((None, tr, cols), lambda s, i, c_ref: (s, i, 0))
    return pl.pallas_call(
        body, name=name,
        grid_spec=pltpu.PrefetchScalarGridSpec(
            num_scalar_prefetch=1, grid=(ns, nb),
            in_specs=[pl.BlockSpec((None, tr, cols), lambda s, i, c_ref: (s, c_ref[0] * nb + i, 0)), out],
            out_specs=[out, out]),
        out_shape=[jax.ShapeDtypeStruct((ns, half, cols), F32), jax.ShapeDtypeStruct((ns, half, cols), BF16)],
        compiler_params=_cparams(("parallel", "parallel")),
    )(c_idx, g, r)


def _send_to_chips(a, name):
    ns, half, cols = a.shape

    def body(a_ref, out_ref, send_sems, recv_sems):
        x, y, c = _position()
        chips = [(1 - x, y), (x, 1 - y), (1 - x, 1 - y)]
        cps = [pltpu.make_async_remote_copy(
            src_ref=a_ref.at[2 * px + py], dst_ref=out_ref.at[j], send_sem=send_sems.at[j], recv_sem=recv_sems.at[j],
            device_id=(px, py, c), device_id_type=MESH) for j, (px, py) in enumerate(chips)]
        for cp in cps:
            cp.start()
        for cp in cps:
            cp.wait()

    return pl.pallas_call(
        body, name=name, out_shape=jax.ShapeDtypeStruct((3, half, cols), a.dtype), in_specs=[_ANY], out_specs=_ANY,
        scratch_shapes=[pltpu.SemaphoreType.DMA((3,)), pltpu.SemaphoreType.DMA((3,))],
    )(a)


def _add_arrivals(a, r, s_idx, name):
    ns, half, cols = a.shape
    tr = _row_tile(half, max(8, (1 << 18) // cols))

    def body(s_ref, a_ref, r_ref, o_ref):
        o_ref[...] = ((a_ref[...] + r_ref[0].astype(F32)) + r_ref[1].astype(F32)) + r_ref[2].astype(F32)

    return pl.pallas_call(
        body, name=name,
        grid_spec=pltpu.PrefetchScalarGridSpec(
            num_scalar_prefetch=1, grid=(half // tr,),
            in_specs=[pl.BlockSpec((None, tr, cols), lambda i, s_ref: (s_ref[0], i, 0)),
                      pl.BlockSpec((3, tr, cols), lambda i, s_ref: (0, i, 0))],
            out_specs=pl.BlockSpec((tr, cols), lambda i, s_ref: (i, 0))),
        out_shape=jax.ShapeDtypeStruct((half, cols), F32), compiler_params=_cparams(("parallel",)),
    )(s_idx, a, r)


def _swap_reduced_half(f, name):
    def body(f_ref, out_ref, send_sem, recv_sem):
        x, y, c = _position()
        cp = pltpu.make_async_remote_copy(src_ref=f_ref, dst_ref=out_ref, send_sem=send_sem, recv_sem=recv_sem,
                                          device_id=(x, y, 1 - c), device_id_type=MESH)
        cp.start()
        cp.wait()

    return pl.pallas_call(
        body, name=name, out_shape=jax.ShapeDtypeStruct(f.shape, f.dtype), in_specs=[_ANY], out_specs=_ANY,
        scratch_shapes=[pltpu.SemaphoreType.DMA, pltpu.SemaphoreType.DMA],
    )(f)


def _conv_core(z, conv_w, nctx):
    bsz, tlen, d3 = z.shape
    d = d3 // 3
    nblk = tlen // TB
    first_lat = nctx // TB
    hr = 8
    per = TB // hr

    def split(t):
        return t[:, :d].astype(F32), t[:, d:2 * d].astype(F32), t[:, 2 * d:].astype(F32)

    def halo_valid(t):
        prev_ok = (t != 0) & (t != first_lat)
        next_ok = (t != first_lat - 1) & (t != nblk - 1)
        return prev_ok, next_ok

    def shifted(u, prev_row, next_row):
        ridx = lax.broadcasted_iota(jnp.int32, (TB, 1), 0)
        up = jnp.where(ridx == 0, prev_row, pltpu.roll(u, 1, axis=0))
        dn = jnp.where(ridx == TB - 1, next_row, pltpu.roll(u, TB - 1, axis=0))
        return up, dn

    main = lambda w_: pl.BlockSpec((None, TB, w_), lambda b, t: (b, t, 0))
    prev = lambda w_: pl.BlockSpec((None, hr, w_), lambda b, t: (b, jnp.maximum(t * per - 1, 0), 0))
    nxt = lambda w_: pl.BlockSpec((None, hr, w_), lambda b, t: (b, jnp.minimum((t + 1) * per, nblk * per - 1), 0))
    wspec = pl.BlockSpec((3, d), lambda b, t: (0, 0))

    def halo_rows(zp_ref, zn_ref, t):
        prev_ok, next_ok = halo_valid(t)
        _, cgp, vp = split(zp_ref[...])
        _, cgn, vn = split(zn_ref[...])
        up = jnp.where(prev_ok, (cgp * vp)[hr - 1:hr], 0.0)
        un = jnp.where(next_ok, (cgn * vn)[0:1], 0.0)
        return up, un

    def fwd_call(z, w):
        def body(z_ref, zp_ref, zn_ref, w_ref, o_ref):
            bg, cg, v = split(z_ref[...])
            u = cg * v
            up, dn = shifted(u, *halo_rows(zp_ref, zn_ref, pl.program_id(1)))
            o_ref[...] = (bg * (w_ref[0:1, :] * up + w_ref[1:2, :] * u + w_ref[2:3, :] * dn)).astype(o_ref.dtype)

        return pl.pallas_call(
            body, name="conv_fwd", grid=(bsz, nblk), in_specs=[main(d3), prev(d3), nxt(d3), wspec], out_specs=main(d),
            out_shape=jax.ShapeDtypeStruct((bsz, tlen, d), BF16), compiler_params=_cparams(("parallel", "parallel")),
        )(z, z, z, w)

    def bwd_call(z, w, dy):
        def body(z_ref, zp_ref, zn_ref, w_ref, dy_ref, dyp_ref, dyn_ref, dz_ref, dw_ref):
            t = pl.program_id(1)
            prev_ok, next_ok = halo_valid(t)
            bg, cg, v = split(z_ref[...])
            u = cg * v
            up, dn = shifted(u, *halo_rows(zp_ref, zn_ref, t))
            w0, w1, w2 = w_ref[0:1, :], w_ref[1:2, :], w_ref[2:3, :]
            dyv = dy_ref[...].astype(F32)
            dconv = dyv * bg
            bgp = zp_ref[...][:, :d].astype(F32)
            bgn = zn_ref[...][:, :d].astype(F32)
            dc_prev = jnp.where(prev_ok, (dyp_ref[...].astype(F32) * bgp)[hr - 1:hr], 0.0)
            dc_next = jnp.where(next_ok, (dyn_ref[...].astype(F32) * bgn)[0:1], 0.0)
            dc_up, dc_dn = shifted(dconv, dc_prev, dc_next)
            du = w0 * dc_dn + w1 * dconv + w2 * dc_up
            dz_ref[:, 0:d] = (dyv * (w0 * up + w1 * u + w2 * dn)).astype(dz_ref.dtype)
            dz_ref[:, d:2 * d] = (du * v).astype(dz_ref.dtype)
            dz_ref[:, 2 * d:3 * d] = (du * cg).astype(dz_ref.dtype)

            @pl.when((pl.program_id(0) == 0) & (t == 0))
            def _():
                dw_ref[...] = jnp.zeros_like(dw_ref)

            dw_ref[0:1, :] += jnp.sum(dconv * up, axis=0, keepdims=True)
            dw_ref[1:2, :] += jnp.sum(dconv * u, axis=0, keepdims=True)
            dw_ref[2:3, :] += jnp.sum(dconv * dn, axis=0, keepdims=True)

        return pl.pallas_call(
            body, name="conv_bwd", grid=(bsz, nblk),
            in_specs=[main(d3), prev(d3), nxt(d3), wspec, main(d), prev(d), nxt(d)],
            out_specs=[main(d3), wspec],
            out_shape=[jax.ShapeDtypeStruct(z.shape, BF16), jax.ShapeDtypeStruct((3, d), F32)],
            compiler_params=_cparams(("arbitrary", "arbitrary")),
        )(z, z, z, w, dy, dy, dy)

    @jax.custom_vjp
    def op(z, w):
        return fwd_call(z, w)

    def op_fwd(z, w):
        return fwd_call(z, w), (z, w)

    def op_bwd(res, dy):
        return tuple(bwd_call(*res, dy))

    op.defvjp(op_fwd, op_bwd)
    return op(z, conv_w)


HG_BLOCK = 8
HG_UNROLL = 4
HG_VMEM_LIMIT = 56 * 1024 * 1024


def _hg_cumsum(x, rev):
    n = HG_CHUNK
    nrow = x.shape[0]

    def run(v, backwards):
        pos = lax.broadcasted_iota(jnp.int32, (nrow, 1), 0) % n
        k = 1
        while k < n:
            if backwards:
                v = v + jnp.where(pos + k < n, pltpu.roll(v, nrow - k, axis=0), 0.0)
            else:
                v = v + jnp.where(pos >= k, pltpu.roll(v, k, axis=0), 0.0)
            k *= 2
        return v

    @jax.custom_vjp
    def cs(v):
        return run(v, rev)

    cs.defvjp(lambda v: (run(v, rev), None), lambda _, g: (run(g, not rev),))
    return cs(x)


def _hg_local(q, v, tf, lb, rev):
    n = HG_CHUNK
    nrow = q.shape[0]
    nb = nrow // n
    f = lb + (1.0 - lb) * jax.nn.sigmoid(tf)
    kk = 1.0 - f
    b = _hg_cumsum(jnp.log(f), rev)
    pos = lax.broadcasted_iota(jnp.int32, (nb, n, 1), 1)
    b3 = b.reshape(nb, n, -1)
    mid = n - n // 2 if rev else n // 2 - 1
    last = 0 if rev else n - 1
    b_mid = jnp.sum(jnp.where(pos == mid, b3, 0.0), axis=1, keepdims=True)
    b_last = jnp.sum(jnp.where(pos == last, b3, 0.0), axis=1, keepdims=True)
    q3, k3, v3 = q.reshape(nb, n, -1), kk.reshape(nb, n, -1), v.reshape(nb, n, -1)
    qs = (q3 * jnp.exp(b3 - b_mid)).astype(BF16)
    ks = (k3 * jnp.exp(b_mid - b3)).astype(BF16)
    sc = jnp.einsum('ctk,csk->cts', qs, ks, preferred_element_type=F32)
    row = lax.broadcasted_iota(jnp.int32, (1, n, n), 1)
    col = lax.broadcasted_iota(jnp.int32, (1, n, n), 2)
    sc = jnp.where((col >= row) if rev else (col <= row), sc, 0.0).astype(BF16)
    o_intra = jnp.einsum('cts,csv->ctv', sc, v3.astype(BF16), preferred_element_type=F32)
    qe = q3 * jnp.exp(b3)
    ks2 = k3 * jnp.exp(b_last - b3)
    return o_intra.reshape(nrow, -1), qe.reshape(nrow, -1), ks2.reshape(nrow, -1), jnp.exp(b_last).reshape(nb, -1)


def _hg_scan(nctx):
    hd, n = HG_HEAD_DIM, HG_CHUNK
    rb = HG_BLOCK * n

    def geometry(z):
        bsz, tlen, d5 = z.shape
        return bsz, tlen, d5 // 5, (d5 // 5) // hd, tlen // n, nctx // n

    def rev_chunk(j, nch, ncc):
        return jnp.where(j < ncc, ncc - 1 - j, nch - 1 + ncc - j)

    def rows(c):
        return pl.ds(pl.multiple_of(c * n, n), n)

    seq_params = pltpu.CompilerParams(dimension_semantics=("parallel", "parallel"), vmem_limit_bytes=HG_VMEM_LIMIT)
    nt, tn = (((1,), (1,)), ((), ())), (((0,), (0,)), ((), ()))

    def local_specs(z):
        bsz, tlen, d, nh, nch, ncc = geometry(z)
        col = lambda k: pl.BlockSpec((None, rb, hd), lambda b, h, t: (b, t, k * nh + h))
        blk = pl.BlockSpec((None, rb, hd), lambda b, h, t: (b, t, h))
        dec = pl.BlockSpec((None, HG_BLOCK, hd), lambda b, h, t: (b, t, h))
        lbs = pl.BlockSpec((1, hd), lambda b, h, t: (0, h))
        return [col(0), col(1), col(3), col(4), lbs], blk, dec

    def local_fwd(z, lb):
        bsz, tlen, d, nh, nch, ncc = geometry(z)
        ins, blk, dec = local_specs(z)

        def body(q_ref, v_ref, ff_ref, fb_ref, lb_ref, o_ref, qf_ref, kf_ref, df_ref, qb_ref, kb_ref, db_ref):
            q, v, lbv = q_ref[...].astype(F32), v_ref[...].astype(F32), lb_ref[...]
            of, qe, ks, dc = _hg_local(q, v, ff_ref[...].astype(F32), lbv, False)
            qf_ref[...], kf_ref[...], df_ref[...] = qe.astype(BF16), ks.astype(BF16), dc
            ob, qe, ks, dc = _hg_local(q, v, fb_ref[...].astype(F32), lbv, True)
            qb_ref[...], kb_ref[...], db_ref[...] = qe.astype(BF16), ks.astype(BF16), dc
            o_ref[...] = of + ob

        act = jax.ShapeDtypeStruct((bsz, tlen, d), BF16)
        dcs = jax.ShapeDtypeStruct((bsz, nch, d), F32)
        return pl.pallas_call(
            body, name="hg_local", grid=(bsz, nh, tlen // rb), in_specs=ins,
            out_specs=[blk, blk, blk, dec, blk, blk, dec],
            out_shape=[jax.ShapeDtypeStruct((bsz, tlen, d), F32), act, act, dcs, act, act, dcs],
            compiler_params=_cparams(("parallel", "parallel", "parallel")),
        )(z, z, z, z, lb)

    def local_bwd(z, lb, do, dv_in, dqf, dkf, ddf, dqb, dkb, ddb):
        bsz, tlen, d, nh, nch, ncc = geometry(z)
        ins, blk, dec = local_specs(z)

        def body(q_ref, v_ref, ff_ref, fb_ref, lb_ref, do_ref, dvi_ref, dqf_ref, dkf_ref, ddf_ref, dqb_ref, dkb_ref,
                 ddb_ref, dq_ref, dv_ref, dff_ref, dfb_ref, dlb_ref):
            q, v, lbv = q_ref[...].astype(F32), v_ref[...].astype(F32), lb_ref[...]
            dov = do_ref[...]
            dq, dv, dlb = jnp.zeros_like(q), dvi_ref[...], jnp.zeros_like(lbv)
            for rev, f_ref, df_ref, cts in ((False, ff_ref, dff_ref, (dqf_ref, dkf_ref, ddf_ref)),
                                            (True, fb_ref, dfb_ref, (dqb_ref, dkb_ref, ddb_ref))):
                _, vjp = jax.vjp(functools.partial(_hg_local, rev=rev), q, v, f_ref[...].astype(F32), lbv)
                g = vjp((dov, cts[0][...].astype(F32), cts[1][...].astype(F32), cts[2][...]))
                dq, dv, dlb = dq + g[0], dv + g[1], dlb + g[3]
                df_ref[...] = g[2].astype(df_ref.dtype)
            dq_ref[...] = dq.astype(dq_ref.dtype)
            dv_ref[...] = dv.astype(dv_ref.dtype)

            @pl.when(pl.program_id(2) == 0)
            def _():
                dlb_ref[...] = dlb

            @pl.when(pl.program_id(2) != 0)
            def _():
                dlb_ref[...] += dlb

        act = jax.ShapeDtypeStruct((bsz, tlen, d), BF16)
        return pl.pallas_call(
            body, name="hg_local_bwd", grid=(bsz, nh, tlen // rb),
            in_specs=ins + [blk, blk, blk, blk, dec, blk, blk, dec],
            out_specs=[blk, blk, blk, blk, pl.BlockSpec((None, 1, hd), lambda b, h, t: (b, 0, h))],
            out_shape=[act, act, act, act, jax.ShapeDtypeStruct((bsz, 1, d), F32)],
            compiler_params=_cparams(("parallel", "parallel", "arbitrary")),
        )(z, z, z, z, lb, do, dv_in, dqf, dkf, ddf, dqb, dkb, ddb)

    def head_spec(z, k=None):
        bsz, tlen, d, nh, nch, ncc = geometry(z)
        if k is None:
            return pl.BlockSpec((None, tlen, hd), lambda b, h: (b, 0, h))
        return pl.BlockSpec((None, tlen, hd), lambda b, h: (b, 0, k * nh + h))

    def dec_spec(z):
        bsz, tlen, d, nh, nch, ncc = geometry(z)
        return pl.BlockSpec((None, nch, hd), lambda b, h: (b, 0, h))

    def state_fwd(z, o_in, qf, kf, df, qb, kb, db):
        bsz, tlen, d, nh, nch, ncc = geometry(z)
        hs, ds = head_spec(z), dec_spec(z)

        def body(v_ref, oi_ref, qf_ref, kf_ref, df_ref, qb_ref, kb_ref, db_ref, o_ref):
            o_ref[...] = oi_ref[...]
            chains = ((False, qf_ref, kf_ref, df_ref), (True, qb_ref, kb_ref, db_ref))

            def step(j, carry):
                out = []
                for (rev, q_ref, k_ref, d_ref), st in zip(chains, carry):
                    c = rev_chunk(j, nch, ncc) if rev else j
                    sl = rows(c)
                    o_ref[sl, :] += lax.dot_general(q_ref[sl, :], st.astype(BF16), nt, preferred_element_type=F32)
                    out.append(st * d_ref[pl.ds(c, 1), :] + lax.dot_general(v_ref[sl, :], k_ref[sl, :], tn,
                                                                              preferred_element_type=F32))
                return tuple(out)

            zero = jnp.zeros((hd, hd), F32)
            lax.fori_loop(0, nch, step, (zero, zero), unroll=HG_UNROLL)

        return pl.pallas_call(
            body, name="hg_state", grid=(bsz, nh), in_specs=[head_spec(z, 1), hs, hs, hs, ds, hs, hs, ds],
            out_specs=hs, out_shape=jax.ShapeDtypeStruct((bsz, tlen, d), F32), compiler_params=seq_params,
        )(z, o_in, qf, kf, df, qb, kb, db)

    def state_bwd(z, do, qf, kf, df, qb, kb, db):
        bsz, tlen, d, nh, nch, ncc = geometry(z)
        hs, ds = head_spec(z), dec_spec(z)

        def body(v_ref, do_ref, qf_ref, kf_ref, df_ref, qb_ref, kb_ref, db_ref,
                 dv_ref, dqf_ref, dkf_ref, ddf_ref, dqb_ref, dkb_ref, ddb_ref, stf_ref, stb_ref):
            chains = ((False, qf_ref, kf_ref, df_ref, dqf_ref, dkf_ref, ddf_ref, stf_ref),
                      (True, qb_ref, kb_ref, db_ref, dqb_ref, dkb_ref, ddb_ref, stb_ref))

            def fstep(j, carry):
                out = []
                for (rev, q_ref, k_ref, d_ref, _, _, _, st_ref), st in zip(chains, carry):
                    c = rev_chunk(j, nch, ncc) if rev else j
                    sl = rows(c)
                    st_ref[j] = st
                    out.append(st * d_ref[pl.ds(c, 1), :] + lax.dot_general(v_ref[sl, :], k_ref[sl, :], tn,
                                                                              preferred_element_type=F32))
                return tuple(out)

            zero = jnp.zeros((hd, hd), F32)
            lax.fori_loop(0, nch, fstep, (zero, zero), unroll=HG_UNROLL)
            dv_ref[...] = jnp.zeros_like(dv_ref)

            def bstep(i, carry):
                j = nch - 1 - i
                out = []
                for (rev, q_ref, k_ref, d_ref, dq_ref, dk_ref, dd_ref, st_ref), dst in zip(chains, carry):
                    c = rev_chunk(j, nch, ncc) if rev else j
                    sl = rows(c)
                    st = st_ref[j]
                    dob = do_ref[sl, :].astype(BF16)
                    dstb = dst.astype(BF16)
                    dec = d_ref[pl.ds(c, 1), :]
                    dq_ref[sl, :] = jnp.dot(dob, st.astype(BF16), preferred_element_type=F32).astype(dq_ref.dtype)
                    dk_ref[sl, :] = jnp.dot(v_ref[sl, :], dstb, preferred_element_type=F32).astype(dk_ref.dtype)
                    dv_ref[sl, :] += lax.dot_general(k_ref[sl, :], dstb, nt, preferred_element_type=F32)
                    dd_ref[pl.ds(c, 1), :] = jnp.sum(dst * st, axis=0, keepdims=True)
                    out.append(dst * dec + lax.dot_general(dob, q_ref[sl, :], tn, preferred_element_type=F32))
                return tuple(out)

            lax.fori_loop(0, nch, bstep, (zero, zero), unroll=HG_UNROLL)

        act = jax.ShapeDtypeStruct((bsz, tlen, d), BF16)
        dcs = jax.ShapeDtypeStruct((bsz, nch, d), F32)
        return pl.pallas_call(
            body, name="hg_state_bwd", grid=(bsz, nh), in_specs=[head_spec(z, 1), hs, hs, hs, ds, hs, hs, ds],
            out_specs=[hs, hs, hs, ds, hs, hs, ds],
            out_shape=[jax.ShapeDtypeStruct((bsz, tlen, d), F32), act, act, dcs, act, act, dcs],
            scratch_shapes=[pltpu.VMEM((nch, hd, hd), F32), pltpu.VMEM((nch, hd, hd), F32)],
            compiler_params=seq_params,
        )(z, do, qf, kf, df, qb, kb, db)

    @jax.custom_vjp
    def scan(z, lb):
        return state_fwd(z, *local_fwd(z, lb))

    def scan_fwd(z, lb):
        loc = local_fwd(z, lb)
        return state_fwd(z, *loc), (z, lb, loc[1:])

    def scan_bwd(res, do):
        z, lb, loc = res
        dv_in, *dstate = state_bwd(z, do, *loc)
        dq, dv, dff, dfb, dlb = local_bwd(z, lb, do, dv_in, *dstate)
        dz = jnp.concatenate([dq, dv, jnp.zeros_like(dff), dff, dfb], axis=-1)
        return dz, jnp.sum(dlb, axis=0)

    scan.defvjp(scan_fwd, scan_bwd)
    return scan


def _hg_readout(o, gate, g_norm):
    bsz, tlen, d = gate.shape
    nh = d // HG_HEAD_DIM
    rb = TB * nh

    def fn(x, g, gt):
        return x * lax.rsqrt(jnp.mean(x * x, axis=-1, keepdims=True) + EPS) * gt * (g * jax.nn.sigmoid(g))

    spec = pl.BlockSpec((None, rb, HG_HEAD_DIM), lambda b, t: (b, t, 0))
    gspec = pl.BlockSpec((rb, HG_HEAD_DIM), lambda b, t: (0, 0))
    grid = (bsz, tlen // TB)

    def fwd_call(x, g, gt):
        def body(x_ref, g_ref, gt_ref, y_ref):
            y_ref[...] = fn(x_ref[...], g_ref[...].astype(F32), gt_ref[...]).astype(y_ref.dtype)

        return pl.pallas_call(
            body, name="hg_readout_fwd", grid=grid, in_specs=[spec, spec, gspec], out_specs=spec,
            out_shape=jax.ShapeDtypeStruct(g.shape, BF16), compiler_params=_cparams(("parallel", "parallel")),
        )(x, g, gt)

    def bwd_call(x, g, gt, dy):
        def body(x_ref, g_ref, gt_ref, dy_ref, dx_ref, dg_ref, dgt_ref):
            _, vjp = jax.vjp(fn, x_ref[...], g_ref[...].astype(F32), gt_ref[...])
            dx, dg, dgt = vjp(dy_ref[...].astype(F32))
            dx_ref[...] = dx
            dg_ref[...] = dg.astype(dg_ref.dtype)

            @pl.when((pl.program_id(0) == 0) & (pl.program_id(1) == 0))
            def _():
                dgt_ref[...] = jnp.zeros_like(dgt_ref)

            dgt_ref[...] += dgt

        return pl.pallas_call(
            body, name="hg_readout_bwd", grid=grid, in_specs=[spec, spec, gspec, spec],
            out_specs=[spec, spec, gspec],
            out_shape=[jax.ShapeDtypeStruct(x.shape, F32), jax.ShapeDtypeStruct(g.shape, BF16),
                       jax.ShapeDtypeStruct(gt.shape, F32)],
            compiler_params=_cparams(("arbitrary", "arbitrary")),
        )(x, g, gt, dy)

    @jax.custom_vjp
    def op(x, g, gt):
        return fwd_call(x, g, gt)

    def op_fwd(x, g, gt):
        return fwd_call(x, g, gt), (x, g, gt)

    def op_bwd(res, dy):
        return tuple(bwd_call(*res, dy))

    op.defvjp(op_fwd, op_bwd)
    heads = lambda t: t.reshape(bsz, tlen * nh, HG_HEAD_DIM)
    gt = jnp.tile(g_norm.reshape(nh, HG_HEAD_DIM), (TB, 1))
    return op(heads(o), heads(gate), gt).reshape(bsz, tlen, d)


def _hgrn_core(z, lower_bound, g_norm, nctx):
    d = g_norm.shape[-1]
    o = _hg_scan(nctx)(z, lower_bound.reshape(1, d))
    return _hg_readout(o, z[..., 2 * d:3 * d], g_norm)


S5_LC = 16


def _s5_mats(lam_re, lam_im, log_dt, b_re, b_im, c_re, c_im):
    hi = lax.Precision.HIGHEST
    _, ng, ns = lam_re.shape
    lc, gs = S5_LC, S5_GROUP
    lam_re = jnp.minimum(lam_re, -1e-4)
    dt = jnp.exp(log_dt)[:, None, :, None]
    k = jnp.arange(lc + 1, dtype=F32)[None, :, None, None]
    mag, ang = jnp.exp(lam_re[:, None] * dt * k), lam_im[:, None] * dt * k
    p_re, p_im = mag * jnp.cos(ang), mag * jnp.sin(ang)
    a_re, a_im = p_re[:, 1], p_im[:, 1]
    den = lam_re * lam_re + lam_im * lam_im
    f_re = ((a_re - 1) * lam_re + a_im * lam_im) / den
    f_im = (a_im * lam_re - (a_re - 1) * lam_im) / den
    bt_re, bt_im = b_re.transpose(0, 2, 1), b_im.transpose(0, 2, 1)
    bb_re = f_re[:, :, None] * bt_re - f_im[:, :, None] * bt_im
    bb_im = f_re[:, :, None] * bt_im + f_im[:, :, None] * bt_re
    w_re = c_re[:, None] * p_re[:, :, :, None] - c_im[:, None] * p_im[:, :, :, None]
    w_im = c_re[:, None] * p_im[:, :, :, None] + c_im[:, None] * p_re[:, :, :, None]
    kk = jnp.sum(w_re[:, :lc, :, :, None, :] * bb_re[:, None, :, None, :, :]
                 - w_im[:, :lc, :, :, None, :] * bb_im[:, None, :, None, :, :], axis=-1)
    t = jnp.arange(lc)
    lag = jnp.stack([t[:, None] - t[None, :], t[None, :] - t[:, None]])
    lag_hot = (lag[..., None] == jnp.arange(lc)).astype(F32)
    mt = jnp.einsum('rtsk,rkgcd->rgsdtc', lag_hot, kk, precision=hi).reshape(2, ng, lc * gs, lc * gs)
    left = jnp.stack([lc - 1 - t, t])
    left_hot = (left[..., None] == jnp.arange(lc + 1)).astype(F32)
    pw_re = jnp.einsum('rsk,rkgn->rsgn', left_hot, p_re, precision=hi)
    pw_im = jnp.einsum('rsk,rkgn->rsgn', left_hot, p_im, precision=hi)
    pr = pw_re[:, :, :, None] * bb_re[:, None] - pw_im[:, :, :, None] * bb_im[:, None]
    pi = pw_re[:, :, :, None] * bb_im[:, None] + pw_im[:, :, :, None] * bb_re[:, None]
    pt = jnp.concatenate([pr, pi], axis=-1).transpose(0, 2, 1, 3, 4).reshape(2, ng, lc * gs, 2 * ns)
    since = jnp.stack([t + 1, lc - t])
    since_hot = (since[..., None] == jnp.arange(lc + 1)).astype(F32)
    q = jnp.concatenate([jnp.einsum('rtk,rkgcn->rtgcn', since_hot, w_re, precision=hi),
                         -jnp.einsum('rtk,rkgcn->rtgcn', since_hot, w_im, precision=hi)], axis=-1)
    qt = q.transpose(0, 2, 4, 1, 3).reshape(2, ng, 2 * ns, lc * gs)
    a16 = jnp.concatenate([p_re[:, lc], p_im[:, lc]], axis=-1)
    return mt, pt, qt, a16


def _s5_bmm(terms, out_dtype, name, sum_dirs=False, ndirs=2):
    ng = terms[0][0].shape[-3]
    ops, dlist = [], []
    for a, b, dn in terms:
        ops += [a, b]
        dlist.append(dn)
    (ca,), (cb,) = dlist[0]
    om, on = terms[0][0].shape[-2:][1 - ca], terms[0][1].shape[-2:][1 - cb]

    def spec(o):
        if o.ndim == 4:
            return pl.BlockSpec((None, None) + o.shape[2:], lambda g, d: (d, g, 0, 0))
        return pl.BlockSpec((None,) + o.shape[1:], lambda g, d: (g, 0, 0))

    def body(*refs):
        acc = None
        for j, dn in enumerate(dlist):
            a, b = refs[2 * j][...].astype(BF16), refs[2 * j + 1][...].astype(BF16)
            r = lax.dot_general(a, b, (dn, ((), ())), preferred_element_type=F32)
            acc = r if acc is None else acc + r
        o_ref = refs[-1]
        if sum_dirs:
            @pl.when(pl.program_id(1) == 0)
            def _():
                o_ref[...] = acc.astype(o_ref.dtype)

            @pl.when(pl.program_id(1) != 0)
            def _():
                o_ref[...] = (o_ref[...].astype(F32) + acc).astype(o_ref.dtype)
        else:
            o_ref[...] = acc.astype(o_ref.dtype)

    if sum_dirs:
        out_spec = pl.BlockSpec((None, om, on), lambda g, d: (g, 0, 0))
        out_shape = jax.ShapeDtypeStruct((ng, om, on), out_dtype)
    else:
        out_spec = pl.BlockSpec((None, None, om, on), lambda g, d: (d, g, 0, 0))
        out_shape = jax.ShapeDtypeStruct((2, ng, om, on), out_dtype)
    return pl.pallas_call(
        body, name=name, grid=(ng, ndirs), in_specs=[spec(o) for o in ops], out_specs=out_spec, out_shape=out_shape,
        compiler_params=_cparams(("parallel", "arbitrary" if sum_dirs else "parallel")),
    )(*ops)


def _s5_row_block(rows):
    return 32 if rows % 32 == 0 else rows


def _s5_chunk_order(j, d, nc, ncc):
    return jnp.where(d == 0, j, jnp.where(j < ncc, ncc - 1 - j, nc - 1 + ncc - j))


def _s5_scan_fwd(z, a1, a2, ncc, name):
    nd, nc, rows, lanes = z.shape
    rb = _s5_row_block(rows)

    def body(z_ref, a1_ref, a2_ref, x_ref):
        a1v, a2v = a1_ref[...], a2_ref[...]
        d = pl.program_id(0)

        def step(j, x):
            c = _s5_chunk_order(j, d, nc, ncc)
            x_ref[c] = x
            return a1v * x + a2v * pltpu.roll(x, lanes // 2, axis=1) + z_ref[c]

        lax.fori_loop(0, nc, step, jnp.zeros((rb, lanes), F32))

    blk = pl.BlockSpec((None, nc, rb, lanes), lambda d, r: (d, 0, r, 0))
    par = pl.BlockSpec((None, rb, lanes), lambda d, r: (d, r, 0))
    return pl.pallas_call(
        body, name=name, grid=(nd, rows // rb), in_specs=[blk, par, par], out_specs=blk,
        out_shape=jax.ShapeDtypeStruct(z.shape, F32), compiler_params=_cparams(("parallel", "parallel")),
    )(z, a1, a2)


def _s5_scan_bwd(dxp, xp, a1, a2b, ncc, name):
    nd, nc, rows, lanes = dxp.shape
    rb = _s5_row_block(rows)

    def body(dxp_ref, xp_ref, a1_ref, a2_ref, dz_ref, p1_ref, p2_ref):
        a1v, a2v = a1_ref[...], a2_ref[...]
        zero = jnp.zeros((rb, lanes), F32)
        d = pl.program_id(0)

        def step(i, carry):
            g_next, nxt, p1, p2 = carry
            c = _s5_chunk_order(nc - 1 - i, d, nc, ncc)
            g = nxt + a1v * g_next + a2v * pltpu.roll(g_next, lanes // 2, axis=1)
            dz_ref[c] = g
            x = xp_ref[c]
            return g, dxp_ref[c], p1 + x * g, p2 + pltpu.roll(x, lanes // 2, axis=1) * g

        _, _, p1, p2 = lax.fori_loop(0, nc, step, (zero, zero, zero, zero))
        p1_ref[...] = p1
        p2_ref[...] = p2

    blk = pl.BlockSpec((None, nc, rb, lanes), lambda d, r: (d, 0, r, 0))
    par = pl.BlockSpec((None, rb, lanes), lambda d, r: (d, r, 0))
    return pl.pallas_call(
        body, name=name, grid=(nd, rows // rb), in_specs=[blk, blk, par, par], out_specs=[blk, par, par],
        out_shape=[jax.ShapeDtypeStruct(dxp.shape, F32), jax.ShapeDtypeStruct((nd, rows, lanes), F32),
                   jax.ShapeDtypeStruct((nd, rows, lanes), F32)],
        compiler_params=_cparams(("parallel", "parallel")),
    )(dxp, xp, a1, a2b)


def _s5_rows(t, bsz):
    nd, ng, m, k = t.shape
    return t.reshape(nd, ng, bsz, m // bsz, k).transpose(0, 3, 2, 1, 4).reshape(nd, m // bsz, bsz * ng, k)


def _s5_groups(t, bsz):
    nd, nc, rows, k = t.shape
    return t.reshape(nd, nc, bsz, rows // bsz, k).transpose(0, 3, 2, 1, 4).reshape(nd, rows // bsz, bsz * nc, k)


def _s5_coeffs(a16, bsz):
    half = a16.shape[-1] // 2
    re, im = a16[..., :half], a16[..., half:]
    tile = lambda v: jnp.tile(v, (1, bsz, 1))
    return tile(jnp.concatenate([re, re], -1)), tile(jnp.concatenate([-im, im], -1)), tile(jnp.concatenate([im, -im], -1))


def _s5_apply(bsz, ncc):
    nn, nt, tn = ((1,), (0,)), ((1,), (1,)), ((0,), (0,))

    def run(u, mt, pt, qt, a16):
        a1, a2, _ = _s5_coeffs(a16, bsz)
        z = _s5_bmm([(u, pt, nn)], F32, "s5_z")
        xp = _s5_scan_fwd(_s5_rows(z, bsz), a1, a2, ncc, "s5_scan")
        xg = _s5_groups(xp, bsz).astype(BF16)
        y = _s5_bmm([(u, mt, nn), (xg, qt, nn)], BF16, "s5_y", sum_dirs=True)
        return y, (xp, xg)

    @jax.custom_vjp
    def apply(u, mt, pt, qt, a16):
        return run(u, mt.astype(BF16), pt.astype(BF16), qt.astype(BF16), a16)[0]

    def apply_fwd(u, mt, pt, qt, a16):
        mtb, ptb, qtb = mt.astype(BF16), pt.astype(BF16), qt.astype(BF16)
        y, (xp, xg) = run(u, mtb, ptb, qtb, a16)
        return y, (u, mtb, ptb, qtb, a16, xp, xg)

    def apply_bwd(res, dy):
        u, mtb, ptb, qtb, a16, xp, xg = res
        a1, _, a2b = _s5_coeffs(a16, bsz)
        dyb = dy.astype(BF16)
        dmt = _s5_bmm([(u, dyb, tn)], F32, "s5_dmt", sum_dirs=True, ndirs=1)
        dqt = _s5_bmm([(xg, dyb, tn)], F32, "s5_dqt")
        dxp = _s5_bmm([(dyb, qtb, nt)], F32, "s5_dxp")
        dz, p1, p2 = _s5_scan_bwd(_s5_rows(dxp, bsz), xp, a1, a2b, ncc, "s5_scan_bwd")
        dzg = _s5_groups(dz, bsz).astype(BF16)
        dpt = _s5_bmm([(u, dzg, tn)], F32, "s5_dpt")
        du = _s5_bmm([(dyb, mtb, nt), (dzg, ptb, nt)], BF16, "s5_du", sum_dirs=True)
        half = a16.shape[-1] // 2
        p1 = jnp.sum(p1.reshape(2, bsz, -1, 2 * half), axis=1)
        p2 = jnp.sum(p2.reshape(2, bsz, -1, 2 * half), axis=1)
        da16 = jnp.concatenate([p1[..., :half] + p1[..., half:], p2[..., half:] - p2[..., :half]], axis=-1)
        return du, jnp.stack([dmt, dmt]), dpt, dqt, da16

    apply.defvjp(apply_fwd, apply_bwd)
    return apply


def _s5_placement(lanes):
    lc, gs = S5_LC, S5_GROUP
    lane = jnp.arange(lanes)
    col = (lane // gs) * (lc * gs) + lane % gs
    t = jnp.arange(lc)
    return (col[None, :, None] + t[:, None, None] * gs == jnp.arange(lanes * lc)[None, None, :]).astype(BF16)


def _s5_relayout(bsz, tlen, d):
    lc, gs, lanes = S5_LC, S5_GROUP, 128
    nc, ng, gb = tlen // lc, d // S5_GROUP, 128 // S5_GROUP
    width = lanes * lc
    tok = pl.BlockSpec((None, tlen, lanes), lambda b, j: (b, 0, j))
    grp = pl.BlockSpec((gb, nc, lc * gs), lambda b, j: (j, b, 0))
    plc = pl.BlockSpec((lc, lanes, width), lambda b, j: (0, 0, 0))
    grid = (bsz, d // lanes)
    sem = _cparams(("parallel", "parallel"))

    def chunk_call(a):
        def body(a_ref, p_ref, o_ref, f_ref):
            f_ref[...] = a_ref[...].astype(F32)
            acc = None
            for t in range(lc):
                rows = f_ref[pl.ds(t, nc, stride=lc), :].astype(BF16)
                part = jnp.dot(rows, p_ref[t], preferred_element_type=F32)
                acc = part if acc is None else acc + part
            for g in range(gb):
                o_ref[g] = acc[:, g * lc * gs:(g + 1) * lc * gs].astype(o_ref.dtype)

        return pl.pallas_call(
            body, name="s5_chunk", grid=grid, in_specs=[tok, plc], out_specs=grp,
            out_shape=jax.ShapeDtypeStruct((ng, bsz * nc, lc * gs), BF16),
            scratch_shapes=[pltpu.VMEM((tlen, lanes), F32)], compiler_params=sem,
        )(a, _s5_placement(lanes))

    def unchunk_call(y):
        def body(y_ref, p_ref, o_ref, f_ref):
            cat = jnp.concatenate([y_ref[g] for g in range(gb)], axis=1)
            for t in range(lc):
                f_ref[pl.ds(t, nc, stride=lc), :] = lax.dot_general(cat, p_ref[t], (((1,), (1,)), ((), ())),
                                                                    preferred_element_type=F32)
            o_ref[...] = f_ref[...].astype(o_ref.dtype)

        return pl.pallas_call(
            body, name="s5_unchunk", grid=grid, in_specs=[grp, plc], out_specs=tok,
            out_shape=jax.ShapeDtypeStruct((bsz, tlen, d), BF16),
            scratch_shapes=[pltpu.VMEM((tlen, lanes), F32)], compiler_params=sem,
        )(y, _s5_placement(lanes))

    @jax.custom_vjp
    def chunk(a):
        return chunk_call(a)

    chunk.defvjp(lambda a: (chunk_call(a), None), lambda _, g: (unchunk_call(g),))

    @jax.custom_vjp
    def unchunk(y):
        return unchunk_call(y)

    unchunk.defvjp(lambda y: (unchunk_call(y), None), lambda _, g: (chunk_call(g),))
    return chunk, unchunk


def _s5_core(a, p, nctx):
    bsz, tlen, d = a.shape
    mt, pt, qt, a16 = _s5_mats(p["s5_lam_re"], p["s5_lam_im"], p["s5_log_dt"], p["s5_b_re"], p["s5_b_im"],
                               p["s5_c_re"], p["s5_c_im"])
    chunk, unchunk = _s5_relayout(bsz, tlen, d)
    y = unchunk(_s5_apply(bsz, nctx // S5_LC)(chunk(a), mt, pt, qt, a16))
    return jax.nn.gelu(p["s5_d"] * a.astype(F32) + y.astype(F32))


NA_LANES = 256
NA_MASKED = -1e30


def _na_tables(rpb):
    hi = lax.Precision.HIGHEST
    nh = rpb.shape[0]
    q = jnp.arange(GRID_W)
    kc = jnp.arange(GRID_W)
    q_start = jnp.clip(q - NA_COLS // 2, 0, GRID_W - NA_COLS)
    inwin = (kc[None, :] >= q_start[:, None]) & (kc[None, :] < q_start[:, None] + NA_COLS)
    dc = kc[None, :] - q[:, None] + NA_COLS - 1
    onehot = ((dc[:, :, None] == jnp.arange(2 * NA_COLS - 1)) & inwin[:, :, None]).astype(F32)
    a = jnp.arange(NA_ROWS)[None, :] - jnp.arange(NA_ROWS)[:, None] + NA_ROWS - 1
    tab = jnp.einsum('hskc,qlc->hsqkl', rpb[:, a, :], onehot, precision=hi)
    tab = jnp.where(inwin[None, None, :, None, :], tab, NA_MASKED)
    return tab.reshape(nh, NA_ROWS, GRID_W, NA_ROWS * GRID_W)


def _na_math(q2, kw, vw, kc, vc, bias, dh):
    scale = dh ** -0.5
    nq, nhb = q2.shape[0], NA_LANES // dh
    lane_head = lax.broadcasted_iota(jnp.int32, (1, NA_LANES), 1) // dh
    nt = (((1,), (1,)), ((), ()))
    kwb, vwb, kcb, vcb = (t.astype(BF16) for t in (kw, vw, kc, vc))
    qs = jnp.concatenate([jnp.where(lane_head == j, q2, 0.0) for j in range(nhb)], axis=0).astype(BF16)
    s_loc = lax.dot_general(qs, kwb, nt, preferred_element_type=F32) * scale + bias.reshape(nhb * nq, -1)
    s_ctx = lax.dot_general(qs, kcb, nt, preferred_element_type=F32) * scale
    m = jnp.maximum(jnp.max(s_loc, axis=-1, keepdims=True), jnp.max(s_ctx, axis=-1, keepdims=True))
    m = lax.stop_gradient(m)
    p_loc, p_ctx = jnp.exp(s_loc - m), jnp.exp(s_ctx - m)
    inv = 1.0 / (jnp.sum(p_loc, axis=-1, keepdims=True) + jnp.sum(p_ctx, axis=-1, keepdims=True))
    o_all = (jnp.dot((p_loc * inv).astype(BF16), vwb, preferred_element_type=F32)
             + jnp.dot((p_ctx * inv).astype(BF16), vcb, preferred_element_type=F32))
    out = jnp.zeros(q2.shape, F32)
    for j in range(nhb):
        out = out + jnp.where(lane_head == j, o_all[j * nq:(j + 1) * nq], 0.0)
    return out


def _na_attention(nctx, nh):
    def geometry(z):
        bsz, tlen, d3 = z.shape
        d = d3 // 3
        rows = (tlen - nctx) // GRID_W
        return bsz, tlen, d, rows, d // nh, d // NA_LANES

    def key_row0(r, rows):
        return jnp.clip(r - NA_ROWS // 2, 0, rows - NA_ROWS)

    def specs(z):
        bsz, tlen, d, rows, dh, nlb = geometry(z)
        hpb = NA_LANES // dh
        qs = pl.BlockSpec((None, GRID_W, NA_LANES), lambda b, h, r: (b, nctx // GRID_W + r, h))
        ks = pl.BlockSpec((None, tlen, NA_LANES), lambda b, h, r: (b, 0, nlb + h))
        vs = pl.BlockSpec((None, tlen, NA_LANES), lambda b, h, r: (b, 0, 2 * nlb + h))
        bs = pl.BlockSpec((hpb, None, GRID_W, NA_ROWS * GRID_W), lambda b, h, r: (h, r - key_row0(r, rows), 0, 0))
        os_ = pl.BlockSpec((None, GRID_W, NA_LANES), lambda b, h, r: (b, r, h))
        return qs, ks, vs, bs, os_

    def window(r, rows):
        return pl.ds(pl.multiple_of(nctx + key_row0(r, rows) * GRID_W, GRID_W), NA_ROWS * GRID_W)

    def fwd_call(z, bias):
        bsz, tlen, d, rows, dh, nlb = geometry(z)
        qs, ks, vs, bs, os_ = specs(z)

        def body(q_ref, k_ref, v_ref, b_ref, o_ref):
            win = window(pl.program_id(2), rows)
            o_ref[...] = _na_math(q_ref[...].astype(F32), k_ref[win, :], v_ref[win, :], k_ref[0:nctx, :],
                                  v_ref[0:nctx, :], b_ref[...], dh).astype(o_ref.dtype)

        return pl.pallas_call(
            body, name="na_fwd", grid=(bsz, nlb, rows), in_specs=[qs, ks, vs, bs], out_specs=os_,
            out_shape=jax.ShapeDtypeStruct((bsz, tlen - nctx, d), BF16),
            compiler_params=_cparams(("parallel", "parallel", "arbitrary")),
        )(z, z, z, bias)

    def bwd_call(z, bias, do):
        bsz, tlen, d, rows, dh, nlb = geometry(z)
        hpb = NA_LANES // dh
        qs, ks, vs, bs, os_ = specs(z)

        def body(q_ref, k_ref, v_ref, b_ref, do_ref, dq_ref, dk_ref, dv_ref, db_ref):
            r = pl.program_id(2)
            win = window(r, rows)

            @pl.when(r == 0)
            def _():
                dk_ref[...] = jnp.zeros_like(dk_ref)
                dv_ref[...] = jnp.zeros_like(dv_ref)

            prim = (q_ref[...].astype(F32), k_ref[win, :].astype(F32), v_ref[win, :].astype(F32),
                    k_ref[0:nctx, :].astype(F32), v_ref[0:nctx, :].astype(F32), b_ref[...])
            _, vjp = jax.vjp(functools.partial(_na_math, dh=dh), *prim)
            dq, dkw, dvw, dkc, dvc, db = vjp(do_ref[...].astype(F32))
            dq_ref[...] = dq.astype(dq_ref.dtype)
            dk_ref[win, :] += dkw
            dv_ref[win, :] += dvw
            dk_ref[0:nctx, :] += dkc
            dv_ref[0:nctx, :] += dvc
            prev = jnp.maximum(r - 1, 0)
            first = (r == 0) | ((r - key_row0(r, rows)) != (prev - key_row0(prev, rows)))

            @pl.when(first)
            def _():
                db_ref[...] = db

            @pl.when(jnp.logical_not(first))
            def _():
                db_ref[...] += db

        acc = pl.BlockSpec((None, tlen, NA_LANES), lambda b, h, r: (b, 0, h))
        dbs = pl.BlockSpec((None, hpb, None, GRID_W, NA_ROWS * GRID_W),
                           lambda b, h, r: (b, h, r - key_row0(r, rows), 0, 0))
        return pl.pallas_call(
            body, name="na_bwd", grid=(bsz, nlb, rows), in_specs=[qs, ks, vs, bs, os_],
            out_specs=[os_, acc, acc, dbs],
            out_shape=[jax.ShapeDtypeStruct((bsz, tlen - nctx, d), BF16), jax.ShapeDtypeStruct((bsz, tlen, d), F32),
                       jax.ShapeDtypeStruct((bsz, tlen, d), F32), jax.ShapeDtypeStruct((bsz,) + bias.shape, F32)],
            compiler_params=_cparams(("parallel", "parallel", "arbitrary")),
        )(z, z, z, bias, do)

    @jax.custom_vjp
    def attend(z, bias):
        return fwd_call(z, bias)

    def attend_fwd(z, bias):
        return fwd_call(z, bias), (z, bias)

    def attend_bwd(res, do):
        z, bias = res
        dq, dk, dv, db = bwd_call(z, bias, do)
        dq = jnp.pad(dq, ((0, 0), (nctx, 0), (0, 0)))
        dz = jnp.concatenate([dq, dk.astype(BF16), dv.astype(BF16)], axis=-1)
        return dz, _sum_leading(db.reshape(db.shape[0], -1, NA_ROWS * GRID_W), "na_sum_dbias").reshape(bias.shape)

    attend.defvjp(attend_fwd, attend_bwd)
    return attend


def _na_core(z, rpb, nctx):
    o = _na_attention(nctx, rpb.shape[0])(z, _na_tables(rpb))
    return jnp.pad(o, ((0, 0), (nctx, 0), (0, 0)))


def _forward(x, ctx, mod, p, wts):
    nctx = ctx.shape[1]
    h = jnp.concatenate([ctx, x], axis=1)
    lb_all = jnp.cumsum(jax.nn.softmax(p["hg_lower_bound"], axis=0), axis=0)
    lb_all = lb_all - lb_all[0]
    gains = p["norm_gains"]
    pre = _rowwise(_fn_pre, [BF16], "pre", 2)
    (a,) = pre([h], gains[0, 0:1], mod[0][:, :, 0:2])
    for i in range(DEPTH):
        tag = f"l{i}"
        if i == 0:
            z = _linear(False, tag + "_sc_in")(a, wts["sc_w_in"][0], p["sc_w_in"][0])
            yc = _conv_core(z, p["sc_conv"][0], nctx)
            y = _linear(False, tag + "_sc_out")(yc, wts["sc_w_out"][0], p["sc_w_out"][0])
        elif i == 1:
            z = _linear(False, tag + "_hg_in")(a, wts["hg_w_in"][0], p["hg_w_in"][0])
            yc = _hgrn_core(z, lb_all[i], p["hg_norm"][0], nctx)
            y = _linear(False, tag + "_hg_out")(yc, wts["hg_w_out"][0], p["hg_w_out"][0])
        elif i == 2:
            sp = {k: v[0] for k, v in p.items() if k.startswith("s5_") and k != "s5_w_glu"}
            gz = _s5_core(a, sp, nctx)
            vg = _linear(False, tag + "_s5_glu")(gz.astype(BF16), wts["s5_w_glu"][0], p["s5_w_glu"][0]).astype(F32)
            d = gz.shape[-1]
            y = (vg[..., :d] * jax.nn.sigmoid(vg[..., d:])).astype(BF16)
        else:
            z = _linear(False, tag + "_na_qkv")(a, wts["na_w_qkv"][0], p["na_w_qkv"][0])
            yc = _na_core(z, p["na_rpb"][0], nctx)
            y = _linear(False, tag + "_na_out")(yc, wts["na_w_out"][0], p["na_w_out"][0])
        h, a2 = _rowwise(_fn_post_pre, [F32, BF16], tag + "_mix_post", 2)(
            [h, y], gains[i, 1:3], mod[i][:, :, 2:5])
        u = _linear(False, tag + "_mlp_in")(a2, wts["mlp_w_in"][i], p["mlp_w_in"][i])
        f = _linear(True, tag + "_mlp_out")(u, wts["mlp_w_out"][i], p["mlp_w_out"][i])
        if i + 1 < DEPTH:
            cols = jnp.stack([gains[i, 3], gains[i + 1, 0]])
            segs = jnp.concatenate([mod[i][:, :, 5:6], mod[i + 1][:, :, 0:2]], axis=2)
            h, a = _rowwise(_fn_post_pre, [F32, BF16], tag + "_mlp_post", 2)([h, f], cols, segs)
        else:
            (h,) = _rowwise(_fn_post, [F32], tag + "_mlp_post", 2)([h, f], gains[i, 3:4], mod[i][:, :, 5:6])
    return h[:, nctx:]


def _local_step(x, ctx, tgt, mod, p, wts):
    y, vjp = jax.vjp(lambda x_, mod_, p_: _forward(x_, ctx, mod_, p_, wts), x, mod, p)
    lblk, dy = _loss_head(y, tgt)
    gx, dmod, gp = vjp(dy)
    return lblk, gx, dmod, gp


PACK_COLS = 512


def _pack_shard(ws):
    return jnp.concatenate([ws[n].reshape(-1, PACK_COLS) for n in BIG], axis=0)


def _unpack_full(buf, shard_shapes):
    out, r = {}, 0
    for n in BIG:
        s = shard_shapes[n]
        nr = math.prod(s) // PACK_COLS
        parts = buf[:, r:r + nr].reshape((N_SHARD,) + s)
        axis = 1 if n in COL_SHARDED else 0
        out[n] = [jnp.concatenate([parts[k, l] for k in range(N_SHARD)], axis=axis) for l in range(s[0])]
        r += nr
    return out


def _pack_grads(gp, shard_shapes):
    per = []
    for k in range(N_SHARD):
        rows = []
        for n in BIG:
            s = shard_shapes[n]
            axis = 1 if n in COL_SHARDED else 0
            width = s[1 + axis]
            for g in gp[n]:
                rows.append(lax.slice_in_dim(g, k * width, (k + 1) * width, axis=axis).reshape(-1, PACK_COLS))
        per.append(jnp.concatenate(rows, axis=0))
    return jnp.stack(per)


def _unpack_shard(buf, shard_shapes):
    out, r = {}, 0
    for n in BIG:
        s = shard_shapes[n]
        nr = math.prod(s) // PACK_COLS
        out[n] = buf[r:r + nr].reshape(s)
        r += nr
    return out


def _shard_cols(a, k, width):
    return lax.dynamic_slice_in_dim(a, k * width, width, axis=a.ndim - 1)


def kernel(x, c, ctx, c_ctx, ada_w, ada_b, norm_gains, mlp_w_in, mlp_w_out, sc_w_in, sc_conv, sc_w_out, hg_w_in, hg_lower_bound, hg_norm, hg_w_out, s5_lam_re, s5_lam_im, s5_log_dt, s5_b_re, s5_b_im, s5_c_re, s5_c_im, s5_d, s5_w_glu, na_w_qkv, na_rpb, na_w_out, loss_target, m_c_ctx, m_ada_w, m_ada_b, m_norm_gains, m_mlp_w_in, m_mlp_w_out, m_sc_w_in, m_sc_conv, m_sc_w_out, m_hg_w_in, m_hg_lower_bound, m_hg_norm, m_hg_w_out, m_s5_lam_re, m_s5_lam_im, m_s5_log_dt, m_s5_b_re, m_s5_b_im, m_s5_c_re, m_s5_c_im, m_s5_d, m_s5_w_glu, m_na_w_qkv, m_na_rpb, m_na_w_out, v_c_ctx, v_ada_w, v_ada_b, v_norm_gains, v_mlp_w_in, v_mlp_w_out, v_sc_w_in, v_sc_conv, v_sc_w_out, v_hg_w_in, v_hg_lower_bound, v_hg_norm, v_hg_w_out, v_s5_lam_re, v_s5_lam_im, v_s5_log_dt, v_s5_b_re, v_s5_b_im, v_s5_c_re, v_s5_c_im, v_s5_d, v_s5_w_glu, v_na_w_qkv, v_na_rpb, v_na_w_out):
    w = dict(c_ctx=c_ctx, ada_w=ada_w, ada_b=ada_b, norm_gains=norm_gains, mlp_w_in=mlp_w_in, mlp_w_out=mlp_w_out,
             sc_w_in=sc_w_in, sc_conv=sc_conv, sc_w_out=sc_w_out, hg_w_in=hg_w_in, hg_lower_bound=hg_lower_bound,
             hg_norm=hg_norm, hg_w_out=hg_w_out, s5_lam_re=s5_lam_re, s5_lam_im=s5_lam_im, s5_log_dt=s5_log_dt,
             s5_b_re=s5_b_re, s5_b_im=s5_b_im, s5_c_re=s5_c_re, s5_c_im=s5_c_im, s5_d=s5_d, s5_w_glu=s5_w_glu,
             na_w_qkv=na_w_qkv, na_rpb=na_rpb, na_w_out=na_w_out)
    mom_m = dict(zip(WEIGHTS, [m_c_ctx, m_ada_w, m_ada_b, m_norm_gains, m_mlp_w_in, m_mlp_w_out, m_sc_w_in, m_sc_conv,
                               m_sc_w_out, m_hg_w_in, m_hg_lower_bound, m_hg_norm, m_hg_w_out, m_s5_lam_re, m_s5_lam_im,
                               m_s5_log_dt, m_s5_b_re, m_s5_b_im, m_s5_c_re, m_s5_c_im, m_s5_d, m_s5_w_glu, m_na_w_qkv,
                               m_na_rpb, m_na_w_out]))
    mom_v = dict(zip(WEIGHTS, [v_c_ctx, v_ada_w, v_ada_b, v_norm_gains, v_mlp_w_in, v_mlp_w_out, v_sc_w_in, v_sc_conv,
                               v_sc_w_out, v_hg_w_in, v_hg_lower_bound, v_hg_norm, v_hg_w_out, v_s5_lam_re, v_s5_lam_im,
                               v_s5_log_dt, v_s5_b_re, v_s5_b_im, v_s5_c_re, v_s5_c_im, v_s5_d, v_s5_w_glu, v_na_w_qkv,
                               v_na_rpb, v_na_w_out]))
    bsz, _, d = x.shape
    ax, ay, ac = lax.axis_index("x"), lax.axis_index("y"), lax.axis_index("c")
    chip = 2 * ax + ay
    dev = 2 * chip + ac
    n_dev = 2 * N_SHARD
    dsh = d // N_SHARD

    shard_shapes = {n: w[n].shape for n in BIG}
    own = _pack_shard({n: w[n].astype(BF16) for n in BIG})
    packed = lax.dynamic_update_slice(_gather_shards(own, "gather_weights"), own[None], (chip, 0, 0))
    wts = _unpack_full(packed, shard_shapes)
    full = {n: [jnp.zeros(a.shape, F32) for a in v] for n, v in wts.items()}

    small_shapes = [c.shape] + [w[n].shape for n in SMALL_SHARDED]
    buf_a = _all_gather8(_pad_rows(_pack_rows([c] + [w[n] for n in SMALL_SHARDED])), "gather_small")
    per_dev = [_unpack_rows(buf_a[k], small_shapes) for k in range(n_dev)]
    c_all = jnp.concatenate([per_dev[k][0] for k in range(n_dev)], axis=0)
    for j, n in enumerate(SMALL_SHARDED):
        full[n] = jnp.concatenate([per_dev[2 * s][1 + j] for s in range(N_SHARD)], axis=-1)
    for n in SMALL_REPL:
        full[n] = w[n]

    n_all = c_all.shape[0]
    s_rows = 32
    cond = jnp.concatenate([c_all, c_ctx[None]], axis=0)
    s_all = jnp.pad(jax.nn.silu(cond), ((0, s_rows - n_all - 1), (0, 0))).astype(BF16)
    mod_part = _ada_fwd(s_all, ada_w)
    nsh = mod_part.shape[-1]
    buf_b = _all_gather8(_pad_rows(mod_part.reshape(-1, LANES)), "gather_mod")
    nrow_b = mod_part.size // LANES
    mod_raw = jnp.concatenate([buf_b[2 * s, :nrow_b].reshape(mod_part.shape) for s in range(N_SHARD)], axis=-1)
    mod_raw = mod_raw + ada_b[:, None, :]
    mod_lat = lax.dynamic_slice_in_dim(mod_raw, dev * bsz, bsz, axis=1).reshape(DEPTH, bsz, 1, N_MOD, d)
    mod_ctx = jnp.broadcast_to(mod_raw[:, n_all].reshape(DEPTH, 1, 1, N_MOD, d), (DEPTH, bsz, 1, N_MOD, d))
    mod = jnp.concatenate([mod_ctx, mod_lat], axis=2)

    lblk, grad_x, dmod, gp = _local_step(x, ctx, loss_target, mod, full, wts)

    dmod_rows = jnp.concatenate([dmod[:, :, 1].reshape(DEPTH, bsz, N_MOD * d),
                                 jnp.sum(dmod[:, :, 0], axis=1).reshape(DEPTH, 1, N_MOD * d)], axis=1)
    c_list = [dmod_rows] + [gp[n] for n in SMALL_SHARDED + SMALL_REPL] + [jnp.sum(lblk).reshape(1)]
    c_shapes = [a.shape for a in c_list]
    buf_c = _all_gather8(_pad_rows(_pack_rows(c_list)), "gather_grads")
    sum_c = _sum_leading(buf_c, "sum_grads")
    summed = _unpack_rows(sum_c, c_shapes)
    grads = {}
    for j, n in enumerate(SMALL_SHARDED):
        grads[n] = _shard_cols(summed[1 + j], chip, w[n].shape[-1])
    for j, n in enumerate(SMALL_REPL):
        grads[n] = summed[1 + len(SMALL_SHARDED) + j]
    loss = summed[-1][0]

    dmod_dev = [_unpack_rows(buf_c[k], c_shapes[:1])[0] for k in range(n_dev)]
    dm_lat = jnp.concatenate([t[:, :bsz] for t in dmod_dev], axis=1)
    dm_ctx = summed[0][:, bsz:bsz + 1]
    dm_all = jnp.concatenate([dm_lat, dm_ctx], axis=1)
    grads["ada_b"] = _sum_leading(jnp.moveaxis(dm_all, 1, 0).reshape(n_all + 1, -1, LANES), "sum_ada_b").reshape(ada_b.shape)
    dm_sh = jnp.pad(_shard_cols(dm_all, chip, nsh), ((0, 0), (0, s_rows - n_all - 1), (0, 0)))
    grads["ada_w"] = _ada_dw(s_all, dm_sh)
    ds_part = _ada_ds(dm_sh, ada_w)
    buf_d = _all_gather8(_pad_rows(ds_part[n_all:n_all + 1]), "gather_dcond")
    ds_ctx = _sum_leading(jnp.stack([buf_d[2 * s] for s in range(N_SHARD)]), "sum_dcond")[0]
    grads["c_ctx"] = jax.vjp(jax.nn.silu, c_ctx)[1](ds_ctx)[0]

    g_pack = _pack_grads(gp, shard_shapes)
    c_idx = jnp.reshape(ac, (1,)).astype(jnp.int32)
    s_idx = jnp.reshape(chip, (1,)).astype(jnp.int32)
    part, part_wire = _add_own_half(g_pack, _swap_other_half(g_pack, "rs_swap_half"), c_idx, "rs_add_sibling")
    mine = _add_arrivals(part, _send_to_chips(part_wire, "rs_send_chips"), s_idx, "rs_add_chips")
    other = _swap_reduced_half(mine, "rs_swap_reduced")
    joined = jnp.concatenate([jnp.where(ac == 0, mine, other), jnp.where(ac == 0, other, mine)], axis=0)
    grads.update(_unpack_shard(joined, shard_shapes))

    delta, new_m, new_v = {}, {}, {}
    for n in BIG + ["ada_w"]:
        delta[n], new_m[n], new_v[n] = _adamw(w[n], grads[n], mom_m[n], mom_v[n], "adamw_" + n)
    small = [n for n in WEIGHTS if n not in BIG and n != "ada_w"]
    shapes = [w[n].shape for n in small]
    packs = [_pad_rows(_pack_rows([src[n] for n in small])) for src in (w, grads, mom_m, mom_v)]
    outs = _adamw(*packs, "adamw_small")
    for tgt, buf in zip((delta, new_m, new_v), outs):
        for n, a in zip(small, _unpack_rows(buf, shapes)):
            tgt[n] = a
    return (loss, grad_x, *[grads[n] for n in WEIGHTS], *[delta[n] for n in WEIGHTS],
            *[new_m[n] for n in WEIGHTS], *[new_v[n] for n in WEIGHTS])
```

```python
import functools
import math

import jax
import jax.numpy as jnp
from jax import lax
from jax.experimental import pallas as pl
from jax.experimental.pallas import tpu as pltpu

F32 = jnp.float32
BF16 = jnp.bfloat16
MESH = pl.DeviceIdType.MESH
EPS = 1e-6
TB = 256
LANES = 1024
VMEM_LIMIT = 48 * 1024 * 1024
N_SHARD = 4
DEPTH = 4
N_MOD = 6
GRID_W = 64
HG_HEAD_DIM = 128
HG_CHUNK = 32
S5_GROUP = 16
NA_ROWS = 8
NA_COLS = 16
ADAM_LR, ADAM_B1, ADAM_B2, ADAM_EPS, ADAM_WD, ADAM_STEP = 0.001, 0.9, 0.999, 1e-08, 0.01, 10

BIG = ["mlp_w_in", "mlp_w_out", "sc_w_in", "sc_w_out", "hg_w_in", "hg_w_out", "s5_w_glu", "na_w_qkv", "na_w_out"]
COL_SHARDED = {"mlp_w_in", "sc_w_in", "hg_w_in", "s5_w_glu", "na_w_qkv"}
SMALL_SHARDED = ["norm_gains", "sc_conv", "hg_norm", "s5_d"]
SMALL_REPL = ["hg_lower_bound", "s5_lam_re", "s5_lam_im", "s5_log_dt", "s5_b_re", "s5_b_im", "s5_c_re", "s5_c_im", "na_rpb"]
WEIGHTS = ["c_ctx", "ada_w", "ada_b", "norm_gains", "mlp_w_in", "mlp_w_out", "sc_w_in", "sc_conv", "sc_w_out", "hg_w_in",
           "hg_lower_bound", "hg_norm", "hg_w_out", "s5_lam_re", "s5_lam_im", "s5_log_dt", "s5_b_re", "s5_b_im", "s5_c_re",
           "s5_c_im", "s5_d", "s5_w_glu", "na_w_qkv", "na_rpb", "na_w_out"]


def _cparams(sem=None):
    return pltpu.CompilerParams(dimension_semantics=sem, vmem_limit_bytes=VMEM_LIMIT)


def _rowwise(fn, out_dtypes, name, nseg):
    def seg_index(t):
        return jnp.minimum(t, nseg - 1)

    def in_specs(rows, colp, segp):
        rs = [pl.BlockSpec((None, TB, r.shape[-1]), lambda b, t: (b, t, 0)) for r in rows]
        cs = pl.BlockSpec(colp.shape, lambda b, t: (0, 0))
        ss = pl.BlockSpec((None, None) + segp.shape[2:], lambda b, t: (b, seg_index(t), 0, 0))
        return rs + [cs, ss]

    def load(refs, n, nc, ns):
        rows = [r[...].astype(F32) for r in refs[:n]]
        cols = [refs[n][k:k + 1, :] for k in range(nc)]
        segs = [refs[n + 1][k:k + 1, :] for k in range(ns)]
        return rows, cols, segs

    def out_blocks(rows, colp, segp):
        one = jax.ShapeDtypeStruct((1, colp.shape[-1]), F32)
        return jax.eval_shape(fn, [jax.ShapeDtypeStruct((TB, r.shape[-1]), F32) for r in rows],
                              [one] * colp.shape[0], [one] * segp.shape[2])

    def fwd_call(rows, colp, segp):
        bsz, tlen, _ = rows[0].shape
        n, nc, ns = len(rows), colp.shape[0], segp.shape[2]
        blk = out_blocks(rows, colp, segp)

        def body(*refs):
            vals = fn(*load(refs, n, nc, ns))
            for o, v in zip(refs[n + 2:], vals):
                o[...] = v.astype(o.dtype)

        return pl.pallas_call(
            body, name=name + "_fwd", grid=(bsz, tlen // TB), in_specs=in_specs(rows, colp, segp),
            out_specs=[pl.BlockSpec((None, TB, o.shape[-1]), lambda b, t: (b, t, 0)) for o in blk],
            out_shape=[jax.ShapeDtypeStruct((bsz, tlen, o.shape[-1]), dt) for o, dt in zip(blk, out_dtypes)],
            compiler_params=_cparams(("parallel", "parallel")),
        )(*rows, colp, segp)

    def bwd_call(rows, colp, segp, cts):
        bsz, tlen, _ = rows[0].shape
        n, nc, ns, m = len(rows), colp.shape[0], segp.shape[2], len(cts)

        def body(*refs):
            b, t = pl.program_id(0), pl.program_id(1)
            prim = load(refs, n, nc, ns)
            ct = tuple(r[...].astype(F32) for r in refs[n + 2:n + 2 + m])
            _, vjp = jax.vjp(fn, *prim)
            drows, dcols, dsegs = vjp(ct)
            outs = refs[n + 2 + m:]
            for o, v in zip(outs[:n], drows):
                o[...] = v.astype(o.dtype)
            dcol_ref, dseg_ref = outs[n], outs[n + 1]

            @pl.when((b == 0) & (t == 0))
            def _():
                dcol_ref[...] = jnp.zeros_like(dcol_ref)

            for k, v in enumerate(dcols):
                dcol_ref[k:k + 1, :] += v

            @pl.when(t < nseg)
            def _():
                for k, v in enumerate(dsegs):
                    dseg_ref[k:k + 1, :] = v

            @pl.when(t >= nseg)
            def _():
                for k, v in enumerate(dsegs):
                    dseg_ref[k:k + 1, :] += v

        row_specs = [pl.BlockSpec((None, TB, r.shape[-1]), lambda b, t: (b, t, 0)) for r in rows]
        ct_specs = [pl.BlockSpec((None, TB, c.shape[-1]), lambda b, t: (b, t, 0)) for c in cts]
        return pl.pallas_call(
            body, name=name + "_bwd", grid=(bsz, tlen // TB),
            in_specs=in_specs(rows, colp, segp) + ct_specs,
            out_specs=row_specs + [pl.BlockSpec(colp.shape, lambda b, t: (0, 0)),
                                   pl.BlockSpec((None, None) + segp.shape[2:], lambda b, t: (b, seg_index(t), 0, 0))],
            out_shape=[jax.ShapeDtypeStruct(r.shape, r.dtype) for r in rows]
            + [jax.ShapeDtypeStruct(colp.shape, F32), jax.ShapeDtypeStruct(segp.shape, F32)],
            compiler_params=_cparams(("arbitrary", "arbitrary")),
        )(*rows, colp, segp, *cts)

    @jax.custom_vjp
    def op(rows, colp, segp):
        return tuple(fwd_call(rows, colp, segp))

    def op_fwd(rows, colp, segp):
        return tuple(fwd_call(rows, colp, segp)), (rows, colp, segp)

    def op_bwd(res, cts):
        rows, colp, segp = res
        outs = bwd_call(rows, colp, segp, list(cts))
        return list(outs[:len(rows)]), outs[len(rows)], outs[len(rows) + 1]

    op.defvjp(op_fwd, op_bwd)
    return op


def _rms(x, g):
    return x * lax.rsqrt(jnp.mean(x * x, axis=-1, keepdims=True) + EPS) * g


def _fn_pre(rows, cols, segs):
    return (_rms(rows[0], cols[0]) * (1.0 + segs[1]) + segs[0],)


def _fn_post_pre(rows, cols, segs):
    h2 = rows[0] + segs[0] * _rms(rows[1], cols[0])
    return h2, _rms(h2, cols[1]) * (1.0 + segs[2]) + segs[1]


def _fn_post(rows, cols, segs):
    return (rows[0] + segs[0] * _rms(rows[1], cols[0]),)


def _m_tile(m, limit):
    for t in range(min(m, limit) // 16 * 16, 15, -16):
        if m % t == 0:
            return t
    return m


def _col_tile(n, limit):
    for t in range(min(n, limit), 127, -128):
        if n % t == 0 and t % 128 == 0:
            return t
    return n


def _relu2(x):
    r = jnp.maximum(x, 0.0)
    return r * r


def _mm(x, wb, act, out_dtype, name):
    m, k = x.shape
    n = wb.shape[1]
    tm = _m_tile(m, 1088)
    tn = _col_tile(n, 1024 if k <= 1024 else 512)

    def body(x_ref, w_ref, o_ref):
        xv = x_ref[...]
        if act:
            xv = _relu2(xv.astype(F32))
        o_ref[...] = jnp.dot(xv.astype(BF16), w_ref[...], preferred_element_type=F32).astype(o_ref.dtype)

    return pl.pallas_call(
        body, name=name, grid=(n // tn, m // tm),
        in_specs=[pl.BlockSpec((tm, k), lambda j, i: (i, 0)), pl.BlockSpec((k, tn), lambda j, i: (0, j))],
        out_specs=pl.BlockSpec((tm, tn), lambda j, i: (i, j)),
        out_shape=jax.ShapeDtypeStruct((m, n), out_dtype),
        compiler_params=_cparams(("parallel", "parallel")),
    )(x, wb)


def _mm_dx(dy, wb, x, act, name):
    m, n = dy.shape
    k = wb.shape[0]
    tm = _m_tile(m, 544)

    def body(dy_ref, w_ref, x_ref, o_ref):
        acc = lax.dot_general(dy_ref[...].astype(BF16), w_ref[...], (((1,), (1,)), ((), ())),
                              preferred_element_type=F32)
        if act:
            acc = acc * (2.0 * jnp.maximum(x_ref[...].astype(F32), 0.0))
        o_ref[...] = acc.astype(o_ref.dtype)

    return pl.pallas_call(
        body, name=name, grid=(m // tm,),
        in_specs=[pl.BlockSpec((tm, n), lambda i: (i, 0)), pl.BlockSpec((k, n), lambda i: (0, 0)),
                  pl.BlockSpec((tm, k), lambda i: (i, 0))],
        out_specs=pl.BlockSpec((tm, k), lambda i: (i, 0)),
        out_shape=jax.ShapeDtypeStruct((m, k), x.dtype),
        compiler_params=_cparams(("parallel",)),
    )(dy, wb, x)


def _mm_dw(x, dy, act, name):
    m, k = x.shape
    n = dy.shape[1]
    tm = _m_tile(m, 1088)
    tk, tn = _col_tile(k, 1024), _col_tile(n, 1024)

    def body(x_ref, dy_ref, o_ref):
        @pl.when(pl.program_id(2) == 0)
        def _():
            o_ref[...] = jnp.zeros_like(o_ref)

        xv = x_ref[...]
        if act:
            xv = _relu2(xv.astype(F32))
        o_ref[...] += lax.dot_general(xv.astype(BF16), dy_ref[...].astype(BF16), (((0,), (0,)), ((), ())),
                                      preferred_element_type=F32)

    return pl.pallas_call(
        body, name=name, grid=(k // tk, n // tn, m // tm),
        in_specs=[pl.BlockSpec((tm, tk), lambda a, b, i: (i, a)), pl.BlockSpec((tm, tn), lambda a, b, i: (i, b))],
        out_specs=pl.BlockSpec((tk, tn), lambda a, b, i: (a, b)),
        out_shape=jax.ShapeDtypeStruct((k, n), F32),
        compiler_params=_cparams(("parallel", "parallel", "arbitrary")),
    )(x, dy)


def _linear(act, name, out_dtype=BF16):
    def run(x, wb):
        y = _mm(x.reshape(-1, x.shape[-1]), wb, act, out_dtype, name + "_fwd")
        return y.reshape(x.shape[:-1] + (wb.shape[1],))

    @jax.custom_vjp
    def lin(x, wb, wg):
        return run(x, wb)

    def lin_fwd(x, wb, wg):
        return run(x, wb), (x, wb)

    def lin_bwd(res, dy):
        x, wb = res
        x2, dy2 = x.reshape(-1, x.shape[-1]), dy.reshape(-1, dy.shape[-1])
        dx = _mm_dx(dy2, wb, x2, act, name + "_dx").reshape(x.shape)
        return dx, jnp.zeros_like(wb), _mm_dw(x2, dy2, act, name + "_dw")

    lin.defvjp(lin_fwd, lin_bwd)
    return lin


def _loss_head(y, tgt):
    bsz, seq, d = y.shape

    def body(y_ref, t_ref, l_ref, d_ref):
        err = y_ref[...] - t_ref[...]
        d_ref[...] = err * (1.0 / d)
        l_ref[...] = jnp.full(l_ref.shape, 0.5 / d, F32) * jnp.sum(err * err)

    spec = pl.BlockSpec((None, TB, d), lambda b, t: (b, t, 0))
    lblk, dy = pl.pallas_call(
        body, name="loss_head", grid=(bsz, seq // TB), in_specs=[spec, spec],
        out_specs=[pl.BlockSpec((None, None, 8, 128), lambda b, t: (b, t, 0, 0)), spec],
        out_shape=[jax.ShapeDtypeStruct((bsz, seq // TB, 8, 128), F32), jax.ShapeDtypeStruct(y.shape, F32)],
        compiler_params=_cparams(("parallel", "parallel")),
    )(y, tgt)
    return lblk[:, :, 0, 0], dy


def _row_tile(rows, limit=512):
    for tr in range(min(rows, limit), 7, -1):
        if rows % tr == 0 and tr % 8 == 0:
            return tr
    return rows


def _adamw(w, g, m, v, name):
    shape = w.shape
    cols = shape[-1]
    w2, g2, m2, v2 = (a.reshape(-1, cols) for a in (w, g, m, v))
    rows = w2.shape[0]
    tr = _row_tile(rows, max(8, (1 << 19) // cols))
    c1, c2 = 1.0 - ADAM_B1 ** ADAM_STEP, 1.0 - ADAM_B2 ** ADAM_STEP

    def body(w_ref, g_ref, m_ref, v_ref, d_ref, mo_ref, vo_ref):
        gv = g_ref[...]
        mn = ADAM_B1 * m_ref[...] + (1.0 - ADAM_B1) * gv
        vn = ADAM_B2 * v_ref[...] + (1.0 - ADAM_B2) * (gv * gv)
        d_ref[...] = -ADAM_LR * ((mn / c1) / (jnp.sqrt(vn / c2) + ADAM_EPS) + ADAM_WD * w_ref[...])
        mo_ref[...] = mn
        vo_ref[...] = vn

    spec = pl.BlockSpec((tr, cols), lambda i: (i, 0))
    outs = pl.pallas_call(
        body, name=name, grid=(rows // tr,), in_specs=[spec] * 4, out_specs=[spec] * 3,
        out_shape=[jax.ShapeDtypeStruct((rows, cols), F32)] * 3, compiler_params=_cparams(("parallel",)),
    )(w2, g2, m2, v2)
    return tuple(o.reshape(shape) for o in outs)


def _sum_leading(a, name):
    n, rows, cols = a.shape
    tr = _row_tile(rows, max(8, (1 << 18) // cols))

    def body(a_ref, o_ref):
        acc = a_ref[0]
        for j in range(1, n):
            acc = acc + a_ref[j]
        o_ref[...] = acc

    return pl.pallas_call(
        body, name=name, grid=(rows // tr,), in_specs=[pl.BlockSpec((n, tr, cols), lambda i: (0, i, 0))],
        out_specs=pl.BlockSpec((tr, cols), lambda i: (i, 0)), out_shape=jax.ShapeDtypeStruct((rows, cols), F32),
        compiler_params=_cparams(("parallel",)),
    )(a)


def _pack_rows(arrs):
    flat = [a.reshape(-1).astype(F32) for a in arrs]
    flat = [jnp.pad(f, (0, (-f.shape[0]) % LANES)) for f in flat]
    return jnp.concatenate(flat).reshape(-1, LANES)


def _unpack_rows(buf, shapes):
    out, r = [], 0
    for s in shapes:
        n = math.prod(s)
        nr = -(-n // LANES)
        out.append(buf[r:r + nr].reshape(-1)[:n].reshape(s))
        r += nr
    return out


def _pad_rows(buf, mult=8):
    return jnp.pad(buf, ((0, (-buf.shape[0]) % mult), (0, 0)))


def _ada_fwd(s, w):
    nl, d, n = w.shape
    r = s.shape[0]

    def body(s_ref, w_ref, o_ref):
        o_ref[...] = jnp.dot(s_ref[...], w_ref[...].astype(BF16), preferred_element_type=F32)

    return pl.pallas_call(
        body, name="ada_fwd", grid=(nl,),
        in_specs=[pl.BlockSpec((r, d), lambda i: (0, 0)), pl.BlockSpec((None, d, n), lambda i: (i, 0, 0))],
        out_specs=pl.BlockSpec((None, r, n), lambda i: (i, 0, 0)), out_shape=jax.ShapeDtypeStruct((nl, r, n), F32),
        compiler_params=_cparams(("parallel",)),
    )(s, w)


def _ada_dw(s, dm):
    nl, r, n = dm.shape
    d = s.shape[1]

    def body(s_ref, dm_ref, o_ref):
        o_ref[...] = lax.dot_general(s_ref[...], dm_ref[...].astype(BF16), (((0,), (0,)), ((), ())),
                                     preferred_element_type=F32)

    return pl.pallas_call(
        body, name="ada_dw", grid=(nl,),
        in_specs=[pl.BlockSpec((r, d), lambda i: (0, 0)), pl.BlockSpec((None, r, n), lambda i: (i, 0, 0))],
        out_specs=pl.BlockSpec((None, d, n), lambda i: (i, 0, 0)), out_shape=jax.ShapeDtypeStruct((nl, d, n), F32),
        compiler_params=_cparams(("parallel",)),
    )(s, dm)


def _ada_ds(dm, w):
    nl, r, n = dm.shape
    d = w.shape[1]

    def body(dm_ref, w_ref, o_ref):
        @pl.when(pl.program_id(0) == 0)
        def _():
            o_ref[...] = jnp.zeros_like(o_ref)

        o_ref[...] += lax.dot_general(dm_ref[...].astype(BF16), w_ref[...].astype(BF16), (((1,), (1,)), ((), ())),
                                      preferred_element_type=F32)

    return pl.pallas_call(
        body, name="ada_ds", grid=(nl,),
        in_specs=[pl.BlockSpec((None, r, n), lambda i: (i, 0, 0)), pl.BlockSpec((None, d, n), lambda i: (i, 0, 0))],
        out_specs=pl.BlockSpec((r, d), lambda i: (0, 0)), out_shape=jax.ShapeDtypeStruct((r, d), F32),
        compiler_params=_cparams(("arbitrary",)),
    )(dm, w)


_ANY = pl.BlockSpec(memory_space=pl.ANY)


def _position():
    return lax.axis_index("x"), lax.axis_index("y"), lax.axis_index("c")


def _all_gather8(block, name):
    m, n = block.shape

    def body(x_ref, out_ref, send_sems, recv_sems, local_sem):
        x, y, c = _position()
        me, sibling = (x, y, c), (x, y, 1 - c)
        chips = [(1 - x, y), (x, 1 - y), (1 - x, 1 - y)]

        def slot(px, py, pc):
            return out_ref.at[4 * px + 2 * py + pc]

        def copy(k, blk, to, src=None):
            return pltpu.make_async_remote_copy(
                src_ref=slot(*blk) if src is None else src, dst_ref=slot(*blk), send_sem=send_sems.at[k],
                recv_sem=recv_sems.at[k], device_id=to, device_id_type=MESH)

        mine = pltpu.make_async_copy(x_ref, slot(*me), local_sem)
        mine.start()
        first = [copy(0, me, sibling, src=x_ref)]
        first += [copy(1 + j, me, (*chip, c), src=x_ref) for j, chip in enumerate(chips)]
        for cp in first:
            cp.start()
        passed = [copy(4 + j, (*chip, c), sibling) for j, chip in enumerate(chips)]
        for j, chip in enumerate(chips):
            copy(1 + j, (*chip, c), me).wait_recv()
            passed[j].start()
        copy(0, sibling, me).wait_recv()
        for j, chip in enumerate(chips):
            copy(4 + j, (*chip, 1 - c), me).wait_recv()
        for cp in first + passed:
            cp.wait_send()
        mine.wait()

    return pl.pallas_call(
        body, name=name, out_shape=jax.ShapeDtypeStruct((8, m, n), block.dtype), in_specs=[_ANY], out_specs=_ANY,
        scratch_shapes=[pltpu.SemaphoreType.DMA((7,)), pltpu.SemaphoreType.DMA((7,)), pltpu.SemaphoreType.DMA],
    )(block)


def _gather_shards(shard, name):
    rows, cols = shard.shape
    half = rows // 2

    def body(x_ref, out_ref, send_sems, recv_sems):
        x, y, c = _position()
        sibling = (x, y, 1 - c)
        chips = [(1 - x, y), (x, 1 - y), (1 - x, 1 - y)]

        def part(px, py, pc):
            return out_ref.at[2 * px + py, pl.ds(pc * half, half), :]

        def copy(k, blk, to, src=None):
            return pltpu.make_async_remote_copy(
                src_ref=part(*blk) if src is None else src, dst_ref=part(*blk), send_sem=send_sems.at[k],
                recv_sem=recv_sems.at[k], device_id=to, device_id_type=MESH)

        my_half = x_ref.at[pl.ds(c * half, half), :]
        first = [copy(j, (x, y, c), (*chip, c), src=my_half) for j, chip in enumerate(chips)]
        for cp in first:
            cp.start()
        passed = [copy(3 + j, (*chip, c), sibling) for j, chip in enumerate(chips)]
        for j, chip in enumerate(chips):
            copy(j, (*chip, c), sibling).wait_recv()
            passed[j].start()
        for j, chip in enumerate(chips):
            copy(3 + j, (*chip, 1 - c), sibling).wait_recv()
        for cp in first + passed:
            cp.wait_send()

    return pl.pallas_call(
        body, name=name, out_shape=jax.ShapeDtypeStruct((N_SHARD, rows, cols), shard.dtype), in_specs=[_ANY],
        out_specs=_ANY, scratch_shapes=[pltpu.SemaphoreType.DMA((6,)), pltpu.SemaphoreType.DMA((6,))],
    )(shard)


def _swap_other_half(g, name):
    ns, rows, cols = g.shape
    half = rows // 2

    def body(g_ref, out_ref, send_sem, recv_sem):
        x, y, c = _position()
        cp = pltpu.make_async_remote_copy(
            src_ref=g_ref.at[:, pl.ds((1 - c) * half, half), :], dst_ref=out_ref, send_sem=send_sem,
            recv_sem=recv_sem, device_id=(x, y, 1 - c), device_id_type=MESH)
        cp.start()
        cp.wait()

    return pl.pallas_call(
        body, name=name, out_shape=jax.ShapeDtypeStruct((ns, half, cols), g.dtype), in_specs=[_ANY], out_specs=_ANY,
        scratch_shapes=[pltpu.SemaphoreType.DMA, pltpu.SemaphoreType.DMA],
    )(g)


def _add_own_half(g, r, c_idx, name):
    ns, rows, cols = g.shape
    half = rows // 2
    tr = _row_tile(half, max(8, (1 << 19) // cols))
    nb = half // tr

    def body(c_ref, g_ref, r_ref, o_ref, ob_ref):
        acc = g_ref[...] + r_ref[...]
        o_ref[...] = acc
        ob_ref[...] = acc.astype(BF16)

    out = pl.BlockSpec((None, tr, cols), lambda s, i, c_ref: (s, i, 0))
    return pl.pallas_call(
        body, name=name,
        grid_spec=pltpu.PrefetchScalarGridSpec(
            num_scalar_prefetch=1, grid=(ns, nb),
            in_specs=[pl.BlockSpec((None, tr, cols), lambda s, i, c_ref: (s, c_ref[0] * nb + i, 0)), out],
            out_specs=[out, out]),
        out_shape=[jax.ShapeDtypeStruct((ns, half, cols), F32), jax.ShapeDtypeStruct((ns, half, cols), BF16)],
        compiler_params=_cparams(("parallel", "parallel")),
    )(c_idx, g, r)


def _send_to_chips(a, name):
    ns, half, cols = a.shape

    def body(a_ref, out_ref, send_sems, recv_sems):
        x, y, c = _position()
        chips = [(1 - x, y), (x, 1 - y), (1 - x, 1 - y)]
        cps = [pltpu.make_async_remote_copy(
            src_ref=a_ref.at[2 * px + py], dst_ref=out_ref.at[j], send_sem=send_sems.at[j], recv_sem=recv_sems.at[j],
            device_id=(px, py, c), device_id_type=MESH) for j, (px, py) in enumerate(chips)]
        for cp in cps:
            cp.start()
        for cp in cps:
            cp.wait()

    return pl.pallas_call(
        body, name=name, out_shape=jax.ShapeDtypeStruct((3, half, cols), a.dtype), in_specs=[_ANY], out_specs=_ANY,
        scratch_shapes=[pltpu.SemaphoreType.DMA((3,)), pltpu.SemaphoreType.DMA((3,))],
    )(a)


def _add_arrivals(a, r, s_idx, name):
    ns, half, cols = a.shape
    tr = _row_tile(half, max(8, (1 << 18) // cols))

    def body(s_ref, a_ref, r_ref, o_ref):
        o_ref[...] = ((a_ref[...] + r_ref[0].astype(F32)) + r_ref[1].astype(F32)) + r_ref[2].astype(F32)

    return pl.pallas_call(
        body, name=name,
        grid_spec=pltpu.PrefetchScalarGridSpec(
            num_scalar_prefetch=1, grid=(half // tr,),
            in_specs=[pl.BlockSpec((None, tr, cols), lambda i, s_ref: (s_ref[0], i, 0)),
                      pl.BlockSpec((3, tr, cols), lambda i, s_ref: (0, i, 0))],
            out_specs=pl.BlockSpec((tr, cols), lambda i, s_ref: (i, 0))),
        out_shape=jax.ShapeDtypeStruct((half, cols), F32), compiler_params=_cparams(("parallel",)),
    )(s_idx, a, r)


def _swap_reduced_half(f, name):
    def body(f_ref, out_ref, send_sem, recv_sem):
        x, y, c = _position()
        cp = pltpu.make_async_remote_copy(src_ref=f_ref, dst_ref=out_ref, send_sem=send_sem, recv_sem=recv_sem,
                                          device_id=(x, y, 1 - c), device_id_type=MESH)
        cp.start()
        cp.wait()

    return pl.pallas_call(
        body, name=name, out_shape=jax.ShapeDtypeStruct(f.shape, f.dtype), in_specs=[_ANY], out_specs=_ANY,
        scratch_shapes=[pltpu.SemaphoreType.DMA, pltpu.SemaphoreType.DMA],
    )(f)


def _conv_core(z, conv_w, nctx):
    bsz, tlen, d3 = z.shape
    d = d3 // 3
    nblk = tlen // TB
    first_lat = nctx // TB
    hr = 8
    per = TB // hr

    def split(t):
        return t[:, :d].astype(F32), t[:, d:2 * d].astype(F32), t[:, 2 * d:].astype(F32)

    def halo_valid(t):
        prev_ok = (t != 0) & (t != first_lat)
        next_ok = (t != first_lat - 1) & (t != nblk - 1)
        return prev_ok, next_ok

    def shifted(u, prev_row, next_row):
        ridx = lax.broadcasted_iota(jnp.int32, (TB, 1), 0)
        up = jnp.where(ridx == 0, prev_row, pltpu.roll(u, 1, axis=0))
        dn = jnp.where(ridx == TB - 1, next_row, pltpu.roll(u, TB - 1, axis=0))
        return up, dn

    main = lambda w_: pl.BlockSpec((None, TB, w_), lambda b, t: (b, t, 0))
    prev = lambda w_: pl.BlockSpec((None, hr, w_), lambda b, t: (b, jnp.maximum(t * per - 1, 0), 0))
    nxt = lambda w_: pl.BlockSpec((None, hr, w_), lambda b, t: (b, jnp.minimum((t + 1) * per, nblk * per - 1), 0))
    wspec = pl.BlockSpec((3, d), lambda b, t: (0, 0))

    def halo_rows(zp_ref, zn_ref, t):
        prev_ok, next_ok = halo_valid(t)
        _, cgp, vp = split(zp_ref[...])
        _, cgn, vn = split(zn_ref[...])
        up = jnp.where(prev_ok, (cgp * vp)[hr - 1:hr], 0.0)
        un = jnp.where(next_ok, (cgn * vn)[0:1], 0.0)
        return up, un

    def fwd_call(z, w):
        def body(z_ref, zp_ref, zn_ref, w_ref, o_ref):
            bg, cg, v = split(z_ref[...])
            u = cg * v
            up, dn = shifted(u, *halo_rows(zp_ref, zn_ref, pl.program_id(1)))
            o_ref[...] = (bg * (w_ref[0:1, :] * up + w_ref[1:2, :] * u + w_ref[2:3, :] * dn)).astype(o_ref.dtype)

        return pl.pallas_call(
            body, name="conv_fwd", grid=(bsz, nblk), in_specs=[main(d3), prev(d3), nxt(d3), wspec], out_specs=main(d),
            out_shape=jax.ShapeDtypeStruct((bsz, tlen, d), BF16), compiler_params=_cparams(("parallel", "parallel")),
        )(z, z, z, w)

    def bwd_call(z, w, dy):
        def body(z_ref, zp_ref, zn_ref, w_ref, dy_ref, dyp_ref, dyn_ref, dz_ref, dw_ref):
            t = pl.program_id(1)
            prev_ok, next_ok = halo_valid(t)
            bg, cg, v = split(z_ref[...])
            u = cg * v
            up, dn = shifted(u, *halo_rows(zp_ref, zn_ref, t))
            w0, w1, w2 = w_ref[0:1, :], w_ref[1:2, :], w_ref[2:3, :]
            dyv = dy_ref[...].astype(F32)
            dconv = dyv * bg
            bgp = zp_ref[...][:, :d].astype(F32)
            bgn = zn_ref[...][:, :d].astype(F32)
            dc_prev = jnp.where(prev_ok, (dyp_ref[...].astype(F32) * bgp)[hr - 1:hr], 0.0)
            dc_next = jnp.where(next_ok, (dyn_ref[...].astype(F32) * bgn)[0:1], 0.0)
            dc_up, dc_dn = shifted(dconv, dc_prev, dc_next)
            du = w0 * dc_dn + w1 * dconv + w2 * dc_up
            dz_ref[:, 0:d] = (dyv * (w0 * up + w1 * u + w2 * dn)).astype(dz_ref.dtype)
            dz_ref[:, d:2 * d] = (du * v).astype(dz_ref.dtype)
            dz_ref[:, 2 * d:3 * d] = (du * cg).astype(dz_ref.dtype)

            @pl.when((pl.program_id(0) == 0) & (t == 0))
            def _():
                dw_ref[...] = jnp.zeros_like(dw_ref)

            dw_ref[0:1, :] += jnp.sum(dconv * up, axis=0, keepdims=True)
            dw_ref[1:2, :] += jnp.sum(dconv * u, axis=0, keepdims=True)
            dw_ref[2:3, :] += jnp.sum(dconv * dn, axis=0, keepdims=True)

        return pl.pallas_call(
            body, name="conv_bwd", grid=(bsz, nblk),
            in_specs=[main(d3), prev(d3), nxt(d3), wspec, main(d), prev(d), nxt(d)],
            out_specs=[main(d3), wspec],
            out_shape=[jax.ShapeDtypeStruct(z.shape, BF16), jax.ShapeDtypeStruct((3, d), F32)],
            compiler_params=_cparams(("arbitrary", "arbitrary")),
        )(z, z, z, w, dy, dy, dy)

    @jax.custom_vjp
    def op(z, w):
        return fwd_call(z, w)

    def op_fwd(z, w):
        return fwd_call(z, w), (z, w)

    def op_bwd(res, dy):
        return tuple(bwd_call(*res, dy))

    op.defvjp(op_fwd, op_bwd)
    return op(z, conv_w)


HG_BLOCK = 8
HG_UNROLL = 4
HG_VMEM_LIMIT = 56 * 1024 * 1024


def _hg_cumsum(x, rev):
    n = HG_CHUNK
    nrow = x.shape[0]

    def run(v, backwards):
        pos = lax.broadcasted_iota(jnp.int32, (nrow, 1), 0) % n
        k = 1
        while k < n:
            if backwards:
                v = v + jnp.where(pos + k < n, pltpu.roll(v, nrow - k, axis=0), 0.0)
            else:
                v = v + jnp.where(pos >= k, pltpu.roll(v, k, axis=0), 0.0)
            k *= 2
        return v

    @jax.custom_vjp
    def cs(v):
        return run(v, rev)

    cs.defvjp(lambda v: (run(v, rev), None), lambda _, g: (run(g, not rev),))
    return cs(x)


def _hg_local(q, v, tf, lb, rev):
    n = HG_CHUNK
    nrow = q.shape[0]
    nb = nrow // n
    f = lb + (1.0 - lb) * jax.nn.sigmoid(tf)
    kk = 1.0 - f
    b = _hg_cumsum(jnp.log(f), rev)
    pos = lax.broadcasted_iota(jnp.int32, (nb, n, 1), 1)
    b3 = b.reshape(nb, n, -1)
    mid = n - n // 2 if rev else n // 2 - 1
    last = 0 if rev else n - 1
    b_mid = jnp.sum(jnp.where(pos == mid, b3, 0.0), axis=1, keepdims=True)
    b_last = jnp.sum(jnp.where(pos == last, b3, 0.0), axis=1, keepdims=True)
    q3, k3, v3 = q.reshape(nb, n, -1), kk.reshape(nb, n, -1), v.reshape(nb, n, -1)
    qs = (q3 * jnp.exp(b3 - b_mid)).astype(BF16)
    ks = (k3 * jnp.exp(b_mid - b3)).astype(BF16)
    sc = jnp.einsum('ctk,csk->cts', qs, ks, preferred_element_type=F32)
    row = lax.broadcasted_iota(jnp.int32, (1, n, n), 1)
    col = lax.broadcasted_iota(jnp.int32, (1, n, n), 2)
    sc = jnp.where((col >= row) if rev else (col <= row), sc, 0.0).astype(BF16)
    o_intra = jnp.einsum('cts,csv->ctv', sc, v3.astype(BF16), preferred_element_type=F32)
    qe = q3 * jnp.exp(b3)
    ks2 = k3 * jnp.exp(b_last - b3)
    return o_intra.reshape(nrow, -1), qe.reshape(nrow, -1), ks2.reshape(nrow, -1), jnp.exp(b_last).reshape(nb, -1)


def _hg_scan(nctx):
    hd, n = HG_HEAD_DIM, HG_CHUNK
    rb = HG_BLOCK * n

    def geometry(z):
        bsz, tlen, d5 = z.shape
        return bsz, tlen, d5 // 5, (d5 // 5) // hd, tlen // n, nctx // n

    def rev_chunk(j, nch, ncc):
        return jnp.where(j < ncc, ncc - 1 - j, nch - 1 + ncc - j)

    def rows(c):
        return pl.ds(pl.multiple_of(c * n, n), n)

    seq_params = pltpu.CompilerParams(dimension_semantics=("parallel", "parallel"), vmem_limit_bytes=HG_VMEM_LIMIT)
    nt, tn = (((1,), (1,)), ((), ())), (((0,), (0,)), ((), ()))

    def local_specs(z):
        bsz, tlen, d, nh, nch, ncc = geometry(z)
        col = lambda k: pl.BlockSpec((None, rb, hd), lambda b, h, t: (b, t, k * nh + h))
        blk = pl.BlockSpec((None, rb, hd), lambda b, h, t: (b, t, h))
        dec = pl.BlockSpec((None, HG_BLOCK, hd), lambda b, h, t: (b, t, h))
        lbs = pl.BlockSpec((1, hd), lambda b, h, t: (0, h))
        return [col(0), col(1), col(3), col(4), lbs], blk, dec

    def local_fwd(z, lb):
        bsz, tlen, d, nh, nch, ncc = geometry(z)
        ins, blk, dec = local_specs(z)

        def body(q_ref, v_ref, ff_ref, fb_ref, lb_ref, o_ref, qf_ref, kf_ref, df_ref, qb_ref, kb_ref, db_ref):
            q, v, lbv = q_ref[...].astype(F32), v_ref[...].astype(F32), lb_ref[...]
            of, qe, ks, dc = _hg_local(q, v, ff_ref[...].astype(F32), lbv, False)
            qf_ref[...], kf_ref[...], df_ref[...] = qe.astype(BF16), ks.astype(BF16), dc
            ob, qe, ks, dc = _hg_local(q, v, fb_ref[...].astype(F32), lbv, True)
            qb_ref[...], kb_ref[...], db_ref[...] = qe.astype(BF16), ks.astype(BF16), dc
            o_ref[...] = of + ob

        act = jax.ShapeDtypeStruct((bsz, tlen, d), BF16)
        dcs = jax.ShapeDtypeStruct((bsz, nch, d), F32)
        return pl.pallas_call(
            body, name="hg_local", grid=(bsz, nh, tlen // rb), in_specs=ins,
            out_specs=[blk, blk, blk, dec, blk, blk, dec],
            out_shape=[jax.ShapeDtypeStruct((bsz, tlen, d), F32), act, act, dcs, act, act, dcs],
            compiler_params=_cparams(("parallel", "parallel", "parallel")),
        )(z, z, z, z, lb)

    def local_bwd(z, lb, do, dv_in, dqf, dkf, ddf, dqb, dkb, ddb):
        bsz, tlen, d, nh, nch, ncc = geometry(z)
        ins, blk, dec = local_specs(z)

        def body(q_ref, v_ref, ff_ref, fb_ref, lb_ref, do_ref, dvi_ref, dqf_ref, dkf_ref, ddf_ref, dqb_ref, dkb_ref,
                 ddb_ref, dq_ref, dv_ref, dff_ref, dfb_ref, dlb_ref):
            q, v, lbv = q_ref[...].astype(F32), v_ref[...].astype(F32), lb_ref[...]
            dov = do_ref[...]
            dq, dv, dlb = jnp.zeros_like(q), dvi_ref[...], jnp.zeros_like(lbv)
            for rev, f_ref, df_ref, cts in ((False, ff_ref, dff_ref, (dqf_ref, dkf_ref, ddf_ref)),
                                            (True, fb_ref, dfb_ref, (dqb_ref, dkb_ref, ddb_ref))):
                _, vjp = jax.vjp(functools.partial(_hg_local, rev=rev), q, v, f_ref[...].astype(F32), lbv)
                g = vjp((dov, cts[0][...].astype(F32), cts[1][...].astype(F32), cts[2][...]))
                dq, dv, dlb = dq + g[0], dv + g[1], dlb + g[3]
                df_ref[...] = g[2].astype(df_ref.dtype)
            dq_ref[...] = dq.astype(dq_ref.dtype)
            dv_ref[...] = dv.astype(dv_ref.dtype)

            @pl.when(pl.program_id(2) == 0)
            def _():
                dlb_ref[...] = dlb

            @pl.when(pl.program_id(2) != 0)
            def _():
                dlb_ref[...] += dlb

        act = jax.ShapeDtypeStruct((bsz, tlen, d), BF16)
        return pl.pallas_call(
            body, name="hg_local_bwd", grid=(bsz, nh, tlen // rb),
            in_specs=ins + [blk, blk, blk, blk, dec, blk, blk, dec],
            out_specs=[blk, blk, blk, blk, pl.BlockSpec((None, 1, hd), lambda b, h, t: (b, 0, h))],
            out_shape=[act, act, act, act, jax.ShapeDtypeStruct((bsz, 1, d), F32)],
            compiler_params=_cparams(("parallel", "parallel", "arbitrary")),
        )(z, z, z, z, lb, do, dv_in, dqf, dkf, ddf, dqb, dkb, ddb)

    def head_spec(z, k=None):
        bsz, tlen, d, nh, nch, ncc = geometry(z)
        if k is None:
            return pl.BlockSpec((None, tlen, hd), lambda b, h: (b, 0, h))
        return pl.BlockSpec((None, tlen, hd), lambda b, h: (b, 0, k * nh + h))

    def dec_spec(z):
        bsz, tlen, d, nh, nch, ncc = geometry(z)
        return pl.BlockSpec((None, nch, hd), lambda b, h: (b, 0, h))

    def state_fwd(z, o_in, qf, kf, df, qb, kb, db):
        bsz, tlen, d, nh, nch, ncc = geometry(z)
        hs, ds = head_spec(z), dec_spec(z)

        def body(v_ref, oi_ref, qf_ref, kf_ref, df_ref, qb_ref, kb_ref, db_ref, o_ref):
            o_ref[...] = oi_ref[...]
            chains = ((False, qf_ref, kf_ref, df_ref), (True, qb_ref, kb_ref, db_ref))

            def step(j, carry):
                out = []
                for (rev, q_ref, k_ref, d_ref), st in zip(chains, carry):
                    c = rev_chunk(j, nch, ncc) if rev else j
                    sl = rows(c)
                    o_ref[sl, :] += lax.dot_general(q_ref[sl, :], st.astype(BF16), nt, preferred_element_type=F32)
                    out.append(st * d_ref[pl.ds(c, 1), :] + lax.dot_general(v_ref[sl, :], k_ref[sl, :], tn,
                                                                              preferred_element_type=F32))
                return tuple(out)

            zero = jnp.zeros((hd, hd), F32)
            lax.fori_loop(0, nch, step, (zero, zero), unroll=HG_UNROLL)

        return pl.pallas_call(
            body, name="hg_state", grid=(bsz, nh), in_specs=[head_spec(z, 1), hs, hs, hs, ds, hs, hs, ds],
            out_specs=hs, out_shape=jax.ShapeDtypeStruct((bsz, tlen, d), F32), compiler_params=seq_params,
        )(z, o_in, qf, kf, df, qb, kb, db)

    def state_bwd(z, do, qf, kf, df, qb, kb, db):
        bsz, tlen, d, nh, nch, ncc = geometry(z)
        hs, ds = head_spec(z), dec_spec(z)

        def body(v_ref, do_ref, qf_ref, kf_ref, df_ref, qb_ref, kb_ref, db_ref,
                 dv_ref, dqf_ref, dkf_ref, ddf_ref, dqb_ref, dkb_ref, ddb_ref, stf_ref, stb_ref):
            chains = ((False, qf_ref, kf_ref, df_ref, dqf_ref, dkf_ref, ddf_ref, stf_ref),
                      (True, qb_ref, kb_ref, db_ref, dqb_ref, dkb_ref, ddb_ref, stb_ref))

            def fstep(j, carry):
                out = []
                for (rev, q_ref, k_ref, d_ref, _, _, _, st_ref), st in zip(chains, carry):
                    c = rev_chunk(j, nch, ncc) if rev else j
                    sl = rows(c)
                    st_ref[j] = st
                    out.append(st * d_ref[pl.ds(c, 1), :] + lax.dot_general(v_ref[sl, :], k_ref[sl, :], tn,
                                                                              preferred_element_type=F32))
                return tuple(out)

            zero = jnp.zeros((hd, hd), F32)
            lax.fori_loop(0, nch, fstep, (zero, zero), unroll=HG_UNROLL)
            dv_ref[...] = jnp.zeros_like(dv_ref)

            def bstep(i, carry):
                j = nch - 1 - i
                out = []
                for (rev, q_ref, k_ref, d_ref, dq_ref, dk_ref, dd_ref, st_ref), dst in zip(chains, carry):
                    c = rev_chunk(j, nch, ncc) if rev else j
                    sl = rows(c)
                    st = st_ref[j]
                    dob = do_ref[sl, :].astype(BF16)
                    dstb = dst.astype(BF16)
                    dec = d_ref[pl.ds(c, 1), :]
                    dq_ref[sl, :] = jnp.dot(dob, st.astype(BF16), preferred_element_type=F32).astype(dq_ref.dtype)
                    dk_ref[sl, :] = jnp.dot(v_ref[sl, :], dstb, preferred_element_type=F32).astype(dk_ref.dtype)
                    dv_ref[sl, :] += lax.dot_general(k_ref[sl, :], dstb, nt, preferred_element_type=F32)
                    dd_ref[pl.ds(c, 1), :] = jnp.sum(dst * st, axis=0, keepdims=True)
                    out.append(dst * dec + lax.dot_general(dob, q_ref[sl, :], tn, preferred_element_type=F32))
                return tuple(out)

            lax.fori_loop(0, nch, bstep, (zero, zero), unroll=HG_UNROLL)

        act = jax.ShapeDtypeStruct((bsz, tlen, d), BF16)
        dcs = jax.ShapeDtypeStruct((bsz, nch, d), F32)
        return pl.pallas_call(
            body, name="hg_state_bwd", grid=(bsz, nh), in_specs=[head_spec(z, 1), hs, hs, hs, ds, hs, hs, ds],
            out_specs=[hs, hs, hs, ds, hs, hs, ds],
            out_shape=[jax.ShapeDtypeStruct((bsz, tlen, d), F32), act, act, dcs, act, act, dcs],
            scratch_shapes=[pltpu.VMEM((nch, hd, hd), F32), pltpu.VMEM((nch, hd, hd), F32)],
            compiler_params=seq_params,
        )(z, do, qf, kf, df, qb, kb, db)

    @jax.custom_vjp
    def scan(z, lb):
        return state_fwd(z, *local_fwd(z, lb))

    def scan_fwd(z, lb):
        loc = local_fwd(z, lb)
        return state_fwd(z, *loc), (z, lb, loc[1:])

    def scan_bwd(res, do):
        z, lb, loc = res
        dv_in, *dstate = state_bwd(z, do, *loc)
        dq, dv, dff, dfb, dlb = local_bwd(z, lb, do, dv_in, *dstate)
        dz = jnp.concatenate([dq, dv, jnp.zeros_like(dff), dff, dfb], axis=-1)
        return dz, jnp.sum(dlb, axis=0)

    scan.defvjp(scan_fwd, scan_bwd)
    return scan


def _hg_readout(o, gate, g_norm):
    bsz, tlen, d = gate.shape
    nh = d // HG_HEAD_DIM
    rb = TB * nh

    def fn(x, g, gt):
        return x * lax.rsqrt(jnp.mean(x * x, axis=-1, keepdims=True) + EPS) * gt * (g * jax.nn.sigmoid(g))

    spec = pl.BlockSpec((None, rb, HG_HEAD_DIM), lambda b, t: (b, t, 0))
    gspec = pl.BlockSpec((rb, HG_HEAD_DIM), lambda b, t: (0, 0))
    grid = (bsz, tlen // TB)

    def fwd_call(x, g, gt):
        def body(x_ref, g_ref, gt_ref, y_ref):
            y_ref[...] = fn(x_ref[...], g_ref[...].astype(F32), gt_ref[...]).astype(y_ref.dtype)

        return pl.pallas_call(
            body, name="hg_readout_fwd", grid=grid, in_specs=[spec, spec, gspec], out_specs=spec,
            out_shape=jax.ShapeDtypeStruct(g.shape, BF16), compiler_params=_cparams(("parallel", "parallel")),
        )(x, g, gt)

    def bwd_call(x, g, gt, dy):
        def body(x_ref, g_ref, gt_ref, dy_ref, dx_ref, dg_ref, dgt_ref):
            _, vjp = jax.vjp(fn, x_ref[...], g_ref[...].astype(F32), gt_ref[...])
            dx, dg, dgt = vjp(dy_ref[...].astype(F32))
            dx_ref[...] = dx
            dg_ref[...] = dg.astype(dg_ref.dtype)

            @pl.when((pl.program_id(0) == 0) & (pl.program_id(1) == 0))
            def _():
                dgt_ref[...] = jnp.zeros_like(dgt_ref)

            dgt_ref[...] += dgt

        return pl.pallas_call(
            body, name="hg_readout_bwd", grid=grid, in_specs=[spec, spec, gspec, spec],
            out_specs=[spec, spec, gspec],
            out_shape=[jax.ShapeDtypeStruct(x.shape, F32), jax.ShapeDtypeStruct(g.shape, BF16),
                       jax.ShapeDtypeStruct(gt.shape, F32)],
            compiler_params=_cparams(("arbitrary", "arbitrary")),
        )(x, g, gt, dy)

    @jax.custom_vjp
    def op(x, g, gt):
        return fwd_call(x, g, gt)

    def op_fwd(x, g, gt):
        return fwd_call(x, g, gt), (x, g, gt)

    def op_bwd(res, dy):
        return tuple(bwd_call(*res, dy))

    op.defvjp(op_fwd, op_bwd)
    heads = lambda t: t.reshape(bsz, tlen * nh, HG_HEAD_DIM)
    gt = jnp.tile(g_norm.reshape(nh, HG_HEAD_DIM), (TB, 1))
    return op(heads(o), heads(gate), gt).reshape(bsz, tlen, d)


def _hgrn_core(z, lower_bound, g_norm, nctx):
    d = g_norm.shape[-1]
    o = _hg_scan(nctx)(z, lower_bound.reshape(1, d))
    return _hg_readout(o, z[..., 2 * d:3 * d], g_norm)


S5_LC = 16


def _s5_mats(lam_re, lam_im, log_dt, b_re, b_im, c_re, c_im):
    hi = lax.Precision.HIGHEST
    _, ng, ns = lam_re.shape
    lc, gs = S5_LC, S5_GROUP
    lam_re = jnp.minimum(lam_re, -1e-4)
    dt = jnp.exp(log_dt)[:, None, :, None]
    k = jnp.arange(lc + 1, dtype=F32)[None, :, None, None]
    mag, ang = jnp.exp(lam_re[:, None] * dt * k), lam_im[:, None] * dt * k
    p_re, p_im = mag * jnp.cos(ang), mag * jnp.sin(ang)
    a_re, a_im = p_re[:, 1], p_im[:, 1]
    den = lam_re * lam_re + lam_im * lam_im
    f_re = ((a_re - 1) * lam_re + a_im * lam_im) / den
    f_im = (a_im * lam_re - (a_re - 1) * lam_im) / den
    bt_re, bt_im = b_re.transpose(0, 2, 1), b_im.transpose(0, 2, 1)
    bb_re = f_re[:, :, None] * bt_re - f_im[:, :, None] * bt_im
    bb_im = f_re[:, :, None] * bt_im + f_im[:, :, None] * bt_re
    w_re = c_re[:, None] * p_re[:, :, :, None] - c_im[:, None] * p_im[:, :, :, None]
    w_im = c_re[:, None] * p_im[:, :, :, None] + c_im[:, None] * p_re[:, :, :, None]
    kk = jnp.sum(w_re[:, :lc, :, :, None, :] * bb_re[:, None, :, None, :, :]
                 - w_im[:, :lc, :, :, None, :] * bb_im[:, None, :, None, :, :], axis=-1)
    t = jnp.arange(lc)
    lag = jnp.stack([t[:, None] - t[None, :], t[None, :] - t[:, None]])
    lag_hot = (lag[..., None] == jnp.arange(lc)).astype(F32)
    mt = jnp.einsum('rtsk,rkgcd->rgsdtc', lag_hot, kk, precision=hi).reshape(2, ng, lc * gs, lc * gs)
    left = jnp.stack([lc - 1 - t, t])
    left_hot = (left[..., None] == jnp.arange(lc + 1)).astype(F32)
    pw_re = jnp.einsum('rsk,rkgn->rsgn', left_hot, p_re, precision=hi)
    pw_im = jnp.einsum('rsk,rkgn->rsgn', left_hot, p_im, precision=hi)
    pr = pw_re[:, :, :, None] * bb_re[:, None] - pw_im[:, :, :, None] * bb_im[:, None]
    pi = pw_re[:, :, :, None] * bb_im[:, None] + pw_im[:, :, :, None] * bb_re[:, None]
    pt = jnp.concatenate([pr, pi], axis=-1).transpose(0, 2, 1, 3, 4).reshape(2, ng, lc * gs, 2 * ns)
    since = jnp.stack([t + 1, lc - t])
    since_hot = (since[..., None] == jnp.arange(lc + 1)).astype(F32)
    q = jnp.concatenate([jnp.einsum('rtk,rkgcn->rtgcn', since_hot, w_re, precision=hi),
                         -jnp.einsum('rtk,rkgcn->rtgcn', since_hot, w_im, precision=hi)], axis=-1)
    qt = q.transpose(0, 2, 4, 1, 3).reshape(2, ng, 2 * ns, lc * gs)
    a16 = jnp.concatenate([p_re[:, lc], p_im[:, lc]], axis=-1)
    return mt, pt, qt, a16


def _s5_bmm(terms, out_dtype, name, sum_dirs=False, ndirs=2):
    ng = terms[0][0].shape[-3]
    ops, dlist = [], []
    for a, b, dn in terms:
        ops += [a, b]
        dlist.append(dn)
    (ca,), (cb,) = dlist[0]
    om, on = terms[0][0].shape[-2:][1 - ca], terms[0][1].shape[-2:][1 - cb]

    def spec(o):
        if o.ndim == 4:
            return pl.BlockSpec((None, None) + o.shape[2:], lambda g, d: (d, g, 0, 0))
        return pl.BlockSpec((None,) + o.shape[1:], lambda g, d: (g, 0, 0))

    def body(*refs):
        acc = None
        for j, dn in enumerate(dlist):
            a, b = refs[2 * j][...].astype(BF16), refs[2 * j + 1][...].astype(BF16)
            r = lax.dot_general(a, b, (dn, ((), ())), preferred_element_type=F32)
            acc = r if acc is None else acc + r
        o_ref = refs[-1]
        if sum_dirs:
            @pl.when(pl.program_id(1) == 0)
            def _():
                o_ref[...] = acc.astype(o_ref.dtype)

            @pl.when(pl.program_id(1) != 0)
            def _():
                o_ref[...] = (o_ref[...].astype(F32) + acc).astype(o_ref.dtype)
        else:
            o_ref[...] = acc.astype(o_ref.dtype)

    if sum_dirs:
        out_spec = pl.BlockSpec((None, om, on), lambda g, d: (g, 0, 0))
        out_shape = jax.ShapeDtypeStruct((ng, om, on), out_dtype)
    else:
        out_spec = pl.BlockSpec((None, None, om, on), lambda g, d: (d, g, 0, 0))
        out_shape = jax.ShapeDtypeStruct((2, ng, om, on), out_dtype)
    return pl.pallas_call(
        body, name=name, grid=(ng, ndirs), in_specs=[spec(o) for o in ops], out_specs=out_spec, out_shape=out_shape,
        compiler_params=_cparams(("parallel", "arbitrary" if sum_dirs else "parallel")),
    )(*ops)


def _s5_row_block(rows, size=32):
    return size if rows % size == 0 else rows


def _s5_chunk_order(j, d, nc, ncc):
    return jnp.where(d == 0, j, jnp.where(j < ncc, ncc - 1 - j, nc - 1 + ncc - j))


def _s5_scan_fwd(z, a1, a2, ncc, name):
    nd, nc, rows, lanes = z.shape
    rb = _s5_row_block(rows, 64)

    def body(z_ref, a1_ref, a2_ref, x_ref):
        a1v, a2v = a1_ref[...], a2_ref[...]
        d = pl.program_id(0)

        def step(j, x):
            c = _s5_chunk_order(j, d, nc, ncc)
            x_ref[c] = x
            return a1v * x + a2v * pltpu.roll(x, lanes // 2, axis=1) + z_ref[c]

        lax.fori_loop(0, nc, step, jnp.zeros((rb, lanes), F32), unroll=2)

    blk = pl.BlockSpec((None, nc, rb, lanes), lambda d, r: (d, 0, r, 0))
    par = pl.BlockSpec((None, rb, lanes), lambda d, r: (d, r, 0))
    return pl.pallas_call(
        body, name=name, grid=(nd, rows // rb), in_specs=[blk, par, par], out_specs=blk,
        out_shape=jax.ShapeDtypeStruct(z.shape, F32), compiler_params=_cparams(("parallel", "parallel")),
    )(z, a1, a2)


def _s5_scan_bwd(dxp, xp, a1, a2b, ncc, name):
    nd, nc, rows, lanes = dxp.shape
    rb = _s5_row_block(rows)

    def body(dxp_ref, xp_ref, a1_ref, a2_ref, dz_ref, p1_ref, p2_ref):
        a1v, a2v = a1_ref[...], a2_ref[...]
        zero = jnp.zeros((rb, lanes), F32)
        d = pl.program_id(0)

        def step(i, carry):
            g_next, nxt, p1, p2 = carry
            c = _s5_chunk_order(nc - 1 - i, d, nc, ncc)
            g = nxt + a1v * g_next + a2v * pltpu.roll(g_next, lanes // 2, axis=1)
            dz_ref[c] = g
            x = xp_ref[c]
            return g, dxp_ref[c], p1 + x * g, p2 + pltpu.roll(x, lanes // 2, axis=1) * g

        _, _, p1, p2 = lax.fori_loop(0, nc, step, (zero, zero, zero, zero), unroll=2)
        p1_ref[...] = p1
        p2_ref[...] = p2

    blk = pl.BlockSpec((None, nc, rb, lanes), lambda d, r: (d, 0, r, 0))
    par = pl.BlockSpec((None, rb, lanes), lambda d, r: (d, r, 0))
    return pl.pallas_call(
        body, name=name, grid=(nd, rows // rb), in_specs=[blk, blk, par, par], out_specs=[blk, par, par],
        out_shape=[jax.ShapeDtypeStruct(dxp.shape, F32), jax.ShapeDtypeStruct((nd, rows, lanes), F32),
                   jax.ShapeDtypeStruct((nd, rows, lanes), F32)],
        compiler_params=_cparams(("parallel", "parallel")),
    )(dxp, xp, a1, a2b)


def _s5_rows(t, bsz):
    nd, ng, m, k = t.shape
    return t.reshape(nd, ng, bsz, m // bsz, k).transpose(0, 3, 2, 1, 4).reshape(nd, m // bsz, bsz * ng, k)


def _s5_groups(t, bsz):
    nd, nc, rows, k = t.shape
    return t.reshape(nd, nc, bsz, rows // bsz, k).transpose(0, 3, 2, 1, 4).reshape(nd, rows // bsz, bsz * nc, k)


def _s5_coeffs(a16, bsz):
    half = a16.shape[-1] // 2
    re, im = a16[..., :half], a16[..., half:]
    tile = lambda v: jnp.tile(v, (1, bsz, 1))
    return tile(jnp.concatenate([re, re], -1)), tile(jnp.concatenate([-im, im], -1)), tile(jnp.concatenate([im, -im], -1))


def _s5_apply(bsz, ncc):
    nn, nt, tn = ((1,), (0,)), ((1,), (1,)), ((0,), (0,))

    def run(u, mt, pt, qt, a16):
        a1, a2, _ = _s5_coeffs(a16, bsz)
        z = _s5_bmm([(u, pt, nn)], F32, "s5_z")
        xp = _s5_scan_fwd(_s5_rows(z, bsz), a1, a2, ncc, "s5_scan")
        xg = _s5_groups(xp, bsz).astype(BF16)
        y = _s5_bmm([(u, mt, nn), (xg, qt, nn)], BF16, "s5_y", sum_dirs=True)
        return y, (xp, xg)

    @jax.custom_vjp
    def apply(u, mt, pt, qt, a16):
        return run(u, mt.astype(BF16), pt.astype(BF16), qt.astype(BF16), a16)[0]

    def apply_fwd(u, mt, pt, qt, a16):
        mtb, ptb, qtb = mt.astype(BF16), pt.astype(BF16), qt.astype(BF16)
        y, (xp, xg) = run(u, mtb, ptb, qtb, a16)
        return y, (u, mtb, ptb, qtb, a16, xp, xg)

    def apply_bwd(res, dy):
        u, mtb, ptb, qtb, a16, xp, xg = res
        a1, _, a2b = _s5_coeffs(a16, bsz)
        dyb = dy.astype(BF16)
        dmt = _s5_bmm([(u, dyb, tn)], F32, "s5_dmt", sum_dirs=True, ndirs=1)
        dqt = _s5_bmm([(xg, dyb, tn)], F32, "s5_dqt")
        dxp = _s5_bmm([(dyb, qtb, nt)], F32, "s5_dxp")
        dz, p1, p2 = _s5_scan_bwd(_s5_rows(dxp, bsz), xp, a1, a2b, ncc, "s5_scan_bwd")
        dzg = _s5_groups(dz, bsz).astype(BF16)
        dpt = _s5_bmm([(u, dzg, tn)], F32, "s5_dpt")
        du = _s5_bmm([(dyb, mtb, nt), (dzg, ptb, nt)], BF16, "s5_du", sum_dirs=True)
        half = a16.shape[-1] // 2
        p1 = jnp.sum(p1.reshape(2, bsz, -1, 2 * half), axis=1)
        p2 = jnp.sum(p2.reshape(2, bsz, -1, 2 * half), axis=1)
        da16 = jnp.concatenate([p1[..., :half] + p1[..., half:], p2[..., half:] - p2[..., :half]], axis=-1)
        return du, jnp.stack([dmt, dmt]), dpt, dqt, da16

    apply.defvjp(apply_fwd, apply_bwd)
    return apply


def _s5_placement(lanes):
    lc, gs = S5_LC, S5_GROUP
    lane = jnp.arange(lanes)
    col = (lane // gs) * (lc * gs) + lane % gs
    t = jnp.arange(lc)
    return (col[None, :, None] + t[:, None, None] * gs == jnp.arange(lanes * lc)[None, None, :]).astype(BF16)


def _s5_relayout(bsz, tlen, d):
    lc, gs, lanes = S5_LC, S5_GROUP, 128
    nc, ng, gb = tlen // lc, d // S5_GROUP, 128 // S5_GROUP
    width = lanes * lc
    tok = pl.BlockSpec((None, tlen, lanes), lambda b, j: (b, 0, j))
    grp = pl.BlockSpec((gb, nc, lc * gs), lambda b, j: (j, b, 0))
    plc = pl.BlockSpec((lc, lanes, width), lambda b, j: (0, 0, 0))
    grid = (bsz, d // lanes)
    sem = _cparams(("parallel", "parallel"))

    def chunk_call(a):
        def body(a_ref, p_ref, o_ref, f_ref):
            f_ref[...] = a_ref[...].astype(F32)
            acc = None
            for t in range(lc):
                rows = f_ref[pl.ds(t, nc, stride=lc), :].astype(BF16)
                part = jnp.dot(rows, p_ref[t], preferred_element_type=F32)
                acc = part if acc is None else acc + part
            for g in range(gb):
                o_ref[g] = acc[:, g * lc * gs:(g + 1) * lc * gs].astype(o_ref.dtype)

        return pl.pallas_call(
            body, name="s5_chunk", grid=grid, in_specs=[tok, plc], out_specs=grp,
            out_shape=jax.ShapeDtypeStruct((ng, bsz * nc, lc * gs), BF16),
            scratch_shapes=[pltpu.VMEM((tlen, lanes), F32)], compiler_params=sem,
        )(a, _s5_placement(lanes))

    def unchunk_call(y):
        def body(y_ref, p_ref, o_ref, f_ref):
            cat = jnp.concatenate([y_ref[g] for g in range(gb)], axis=1)
            for t in range(lc):
                f_ref[pl.ds(t, nc, stride=lc), :] = lax.dot_general(cat, p_ref[t], (((1,), (1,)), ((), ())),
                                                                    preferred_element_type=F32)
            o_ref[...] = f_ref[...].astype(o_ref.dtype)

        return pl.pallas_call(
            body, name="s5_unchunk", grid=grid, in_specs=[grp, plc], out_specs=tok,
            out_shape=jax.ShapeDtypeStruct((bsz, tlen, d), BF16),
            scratch_shapes=[pltpu.VMEM((tlen, lanes), F32)], compiler_params=sem,
        )(y, _s5_placement(lanes))

    @jax.custom_vjp
    def chunk(a):
        return chunk_call(a)

    chunk.defvjp(lambda a: (chunk_call(a), None), lambda _, g: (unchunk_call(g),))

    @jax.custom_vjp
    def unchunk(y):
        return unchunk_call(y)

    unchunk.defvjp(lambda y: (unchunk_call(y), None), lambda _, g: (chunk_call(g),))
    return chunk, unchunk


def _s5_core(a, p, nctx):
    bsz, tlen, d = a.shape
    mt, pt, qt, a16 = _s5_mats(p["s5_lam_re"], p["s5_lam_im"], p["s5_log_dt"], p["s5_b_re"], p["s5_b_im"],
                               p["s5_c_re"], p["s5_c_im"])
    chunk, unchunk = _s5_relayout(bsz, tlen, d)
    y = unchunk(_s5_apply(bsz, nctx // S5_LC)(chunk(a), mt, pt, qt, a16))
    return jax.nn.gelu(p["s5_d"] * a.astype(F32) + y.astype(F32))


NA_LANES = 256
NA_MASKED = -1e30


def _na_tables(rpb):
    hi = lax.Precision.HIGHEST
    nh = rpb.shape[0]
    q = jnp.arange(GRID_W)
    kc = jnp.arange(GRID_W)
    q_start = jnp.clip(q - NA_COLS // 2, 0, GRID_W - NA_COLS)
    inwin = (kc[None, :] >= q_start[:, None]) & (kc[None, :] < q_start[:, None] + NA_COLS)
    dc = kc[None, :] - q[:, None] + NA_COLS - 1
    onehot = ((dc[:, :, None] == jnp.arange(2 * NA_COLS - 1)) & inwin[:, :, None]).astype(F32)
    a = jnp.arange(NA_ROWS)[None, :] - jnp.arange(NA_ROWS)[:, None] + NA_ROWS - 1
    tab = jnp.einsum('hskc,qlc->hsqkl', rpb[:, a, :], onehot, precision=hi)
    tab = jnp.where(inwin[None, None, :, None, :], tab, NA_MASKED)
    return tab.reshape(nh, NA_ROWS, GRID_W, NA_ROWS * GRID_W)


def _na_math(q2, kw, vw, kc, vc, bias, dh):
    scale = dh ** -0.5
    nq, nhb = q2.shape[0], NA_LANES // dh
    lane_head = lax.broadcasted_iota(jnp.int32, (1, NA_LANES), 1) // dh
    nt = (((1,), (1,)), ((), ()))
    kwb, vwb, kcb, vcb = (t.astype(BF16) for t in (kw, vw, kc, vc))
    qs = jnp.concatenate([jnp.where(lane_head == j, q2, 0.0) for j in range(nhb)], axis=0).astype(BF16)
    s_loc = lax.dot_general(qs, kwb, nt, preferred_element_type=F32) * scale + bias.reshape(nhb * nq, -1)
    s_ctx = lax.dot_general(qs, kcb, nt, preferred_element_type=F32) * scale
    m = jnp.maximum(jnp.max(s_loc, axis=-1, keepdims=True), jnp.max(s_ctx, axis=-1, keepdims=True))
    m = lax.stop_gradient(m)
    p_loc, p_ctx = jnp.exp(s_loc - m), jnp.exp(s_ctx - m)
    inv = 1.0 / (jnp.sum(p_loc, axis=-1, keepdims=True) + jnp.sum(p_ctx, axis=-1, keepdims=True))
    o_all = (jnp.dot((p_loc * inv).astype(BF16), vwb, preferred_element_type=F32)
             + jnp.dot((p_ctx * inv).astype(BF16), vcb, preferred_element_type=F32))
    out = jnp.zeros(q2.shape, F32)
    for j in range(nhb):
        out = out + jnp.where(lane_head == j, o_all[j * nq:(j + 1) * nq], 0.0)
    return out


def _na_attention(nctx, nh):
    def geometry(z):
        bsz, tlen, d3 = z.shape
        d = d3 // 3
        rows = (tlen - nctx) // GRID_W
        return bsz, tlen, d, rows, d // nh, d // NA_LANES

    def key_row0(r, rows):
        return jnp.clip(r - NA_ROWS // 2, 0, rows - NA_ROWS)

    def specs(z):
        bsz, tlen, d, rows, dh, nlb = geometry(z)
        hpb = NA_LANES // dh
        qs = pl.BlockSpec((None, GRID_W, NA_LANES), lambda b, h, r: (b, nctx // GRID_W + r, h))
        ks = pl.BlockSpec((None, tlen, NA_LANES), lambda b, h, r: (b, 0, nlb + h))
        vs = pl.BlockSpec((None, tlen, NA_LANES), lambda b, h, r: (b, 0, 2 * nlb + h))
        bs = pl.BlockSpec((hpb, None, GRID_W, NA_ROWS * GRID_W), lambda b, h, r: (h, r - key_row0(r, rows), 0, 0))
        os_ = pl.BlockSpec((None, GRID_W, NA_LANES), lambda b, h, r: (b, r, h))
        return qs, ks, vs, bs, os_

    def window(r, rows):
        return pl.ds(pl.multiple_of(nctx + key_row0(r, rows) * GRID_W, GRID_W), NA_ROWS * GRID_W)

    def fwd_call(z, bias):
        bsz, tlen, d, rows, dh, nlb = geometry(z)
        qs, ks, vs, bs, os_ = specs(z)

        def body(q_ref, k_ref, v_ref, b_ref, o_ref):
            win = window(pl.program_id(2), rows)
            o_ref[...] = _na_math(q_ref[...].astype(F32), k_ref[win, :], v_ref[win, :], k_ref[0:nctx, :],
                                  v_ref[0:nctx, :], b_ref[...], dh).astype(o_ref.dtype)

        return pl.pallas_call(
            body, name="na_fwd", grid=(bsz, nlb, rows), in_specs=[qs, ks, vs, bs], out_specs=os_,
            out_shape=jax.ShapeDtypeStruct((bsz, tlen - nctx, d), BF16),
            compiler_params=_cparams(("parallel", "parallel", "arbitrary")),
        )(z, z, z, bias)

    def bwd_call(z, bias, do):
        bsz, tlen, d, rows, dh, nlb = geometry(z)
        hpb = NA_LANES // dh
        qs, ks, vs, bs, os_ = specs(z)

        def body(q_ref, k_ref, v_ref, b_ref, do_ref, dq_ref, dk_ref, dv_ref, db_ref):
            r = pl.program_id(2)
            win = window(r, rows)

            @pl.when(r == 0)
            def _():
                dk_ref[...] = jnp.zeros_like(dk_ref)
                dv_ref[...] = jnp.zeros_like(dv_ref)

            prim = (q_ref[...].astype(F32), k_ref[win, :].astype(F32), v_ref[win, :].astype(F32),
                    k_ref[0:nctx, :].astype(F32), v_ref[0:nctx, :].astype(F32), b_ref[...])
            _, vjp = jax.vjp(functools.partial(_na_math, dh=dh), *prim)
            dq, dkw, dvw, dkc, dvc, db = vjp(do_ref[...].astype(F32))
            dq_ref[...] = dq.astype(dq_ref.dtype)
            dk_ref[win, :] += dkw
            dv_ref[win, :] += dvw
            dk_ref[0:nctx, :] += dkc
            dv_ref[0:nctx, :] += dvc
            prev = jnp.maximum(r - 1, 0)
            first = (r == 0) | ((r - key_row0(r, rows)) != (prev - key_row0(prev, rows)))

            @pl.when(first)
            def _():
                db_ref[...] = db

            @pl.when(jnp.logical_not(first))
            def _():
                db_ref[...] += db

        acc = pl.BlockSpec((None, tlen, NA_LANES), lambda b, h, r: (b, 0, h))
        dbs = pl.BlockSpec((None, hpb, None, GRID_W, NA_ROWS * GRID_W),
                           lambda b, h, r: (b, h, r - key_row0(r, rows), 0, 0))
        return pl.pallas_call(
            body, name="na_bwd", grid=(bsz, nlb, rows), in_specs=[qs, ks, vs, bs, os_],
            out_specs=[os_, acc, acc, dbs],
            out_shape=[jax.ShapeDtypeStruct((bsz, tlen - nctx, d), BF16), jax.ShapeDtypeStruct((bsz, tlen, d), F32),
                       jax.ShapeDtypeStruct((bsz, tlen, d), F32), jax.ShapeDtypeStruct((bsz,) + bias.shape, F32)],
            compiler_params=_cparams(("parallel", "parallel", "arbitrary")),
        )(z, z, z, bias, do)

    @jax.custom_vjp
    def attend(z, bias):
        return fwd_call(z, bias)

    def attend_fwd(z, bias):
        return fwd_call(z, bias), (z, bias)

    def attend_bwd(res, do):
        z, bias = res
        dq, dk, dv, db = bwd_call(z, bias, do)
        dq = jnp.pad(dq, ((0, 0), (nctx, 0), (0, 0)))
        dz = jnp.concatenate([dq, dk.astype(BF16), dv.astype(BF16)], axis=-1)
        return dz, _sum_leading(db.reshape(db.shape[0], -1, NA_ROWS * GRID_W), "na_sum_dbias").reshape(bias.shape)

    attend.defvjp(attend_fwd, attend_bwd)
    return attend


def _na_core(z, rpb, nctx):
    o = _na_attention(nctx, rpb.shape[0])(z, _na_tables(rpb))
    return jnp.pad(o, ((0, 0), (nctx, 0), (0, 0)))


def _forward(x, ctx, mod, p, wts):
    nctx = ctx.shape[1]
    h = jnp.concatenate([ctx, x], axis=1)
    lb_all = jnp.cumsum(jax.nn.softmax(p["hg_lower_bound"], axis=0), axis=0)
    lb_all = lb_all - lb_all[0]
    gains = p["norm_gains"]
    pre = _rowwise(_fn_pre, [BF16], "pre", 2)
    (a,) = pre([h], gains[0, 0:1], mod[0][:, :, 0:2])
    for i in range(DEPTH):
        tag = f"l{i}"
        if i == 0:
            z = _linear(False, tag + "_sc_in")(a, wts["sc_w_in"][0], p["sc_w_in"][0])
            yc = _conv_core(z, p["sc_conv"][0], nctx)
            y = _linear(False, tag + "_sc_out")(yc, wts["sc_w_out"][0], p["sc_w_out"][0])
        elif i == 1:
            z = _linear(False, tag + "_hg_in")(a, wts["hg_w_in"][0], p["hg_w_in"][0])
            yc = _hgrn_core(z, lb_all[i], p["hg_norm"][0], nctx)
            y = _linear(False, tag + "_hg_out")(yc, wts["hg_w_out"][0], p["hg_w_out"][0])
        elif i == 2:
            sp = {k: v[0] for k, v in p.items() if k.startswith("s5_") and k != "s5_w_glu"}
            gz = _s5_core(a, sp, nctx)
            vg = _linear(False, tag + "_s5_glu")(gz.astype(BF16), wts["s5_w_glu"][0], p["s5_w_glu"][0]).astype(F32)
            d = gz.shape[-1]
            y = (vg[..., :d] * jax.nn.sigmoid(vg[..., d:])).astype(BF16)
        else:
            z = _linear(False, tag + "_na_qkv")(a, wts["na_w_qkv"][0], p["na_w_qkv"][0])
            yc = _na_core(z, p["na_rpb"][0], nctx)
            y = _linear(False, tag + "_na_out")(yc, wts["na_w_out"][0], p["na_w_out"][0])
        h, a2 = _rowwise(_fn_post_pre, [F32, BF16], tag + "_mix_post", 2)(
            [h, y], gains[i, 1:3], mod[i][:, :, 2:5])
        u = _linear(False, tag + "_mlp_in")(a2, wts["mlp_w_in"][i], p["mlp_w_in"][i])
        f = _linear(True, tag + "_mlp_out")(u, wts["mlp_w_out"][i], p["mlp_w_out"][i])
        if i + 1 < DEPTH:
            cols = jnp.stack([gains[i, 3], gains[i + 1, 0]])
            segs = jnp.concatenate([mod[i][:, :, 5:6], mod[i + 1][:, :, 0:2]], axis=2)
            h, a = _rowwise(_fn_post_pre, [F32, BF16], tag + "_mlp_post", 2)([h, f], cols, segs)
        else:
            (h,) = _rowwise(_fn_post, [F32], tag + "_mlp_post", 2)([h, f], gains[i, 3:4], mod[i][:, :, 5:6])
    return h[:, nctx:]


def _local_step(x, ctx, tgt, mod, p, wts):
    y, vjp = jax.vjp(lambda x_, mod_, p_: _forward(x_, ctx, mod_, p_, wts), x, mod, p)
    lblk, dy = _loss_head(y, tgt)
    gx, dmod, gp = vjp(dy)
    return lblk, gx, dmod, gp


PACK_COLS = 512


def _pack_shard(ws):
    return jnp.concatenate([ws[n].reshape(-1, PACK_COLS) for n in BIG], axis=0)


def _unpack_full(buf, shard_shapes):
    out, r = {}, 0
    for n in BIG:
        s = shard_shapes[n]
        nr = math.prod(s) // PACK_COLS
        parts = buf[:, r:r + nr].reshape((N_SHARD,) + s)
        axis = 1 if n in COL_SHARDED else 0
        out[n] = [jnp.concatenate([parts[k, l] for k in range(N_SHARD)], axis=axis) for l in range(s[0])]
        r += nr
    return out


def _pack_grads(gp, shard_shapes):
    per = []
    for k in range(N_SHARD):
        rows = []
        for n in BIG:
            s = shard_shapes[n]
            axis = 1 if n in COL_SHARDED else 0
            width = s[1 + axis]
            for g in gp[n]:
                rows.append(lax.slice_in_dim(g, k * width, (k + 1) * width, axis=axis).reshape(-1, PACK_COLS))
        per.append(jnp.concatenate(rows, axis=0))
    return jnp.stack(per)


def _unpack_shard(buf, shard_shapes):
    out, r = {}, 0
    for n in BIG:
        s = shard_shapes[n]
        nr = math.prod(s) // PACK_COLS
        out[n] = buf[r:r + nr].reshape(s)
        r += nr
    return out


def _shard_cols(a, k, width):
    return lax.dynamic_slice_in_dim(a, k * width, width, axis=a.ndim - 1)


def kernel(x, c, ctx, c_ctx, ada_w, ada_b, norm_gains, mlp_w_in, mlp_w_out, sc_w_in, sc_conv, sc_w_out, hg_w_in, hg_lower_bound, hg_norm, hg_w_out, s5_lam_re, s5_lam_im, s5_log_dt, s5_b_re, s5_b_im, s5_c_re, s5_c_im, s5_d, s5_w_glu, na_w_qkv, na_rpb, na_w_out, loss_target, m_c_ctx, m_ada_w, m_ada_b, m_norm_gains, m_mlp_w_in, m_mlp_w_out, m_sc_w_in, m_sc_conv, m_sc_w_out, m_hg_w_in, m_hg_lower_bound, m_hg_norm, m_hg_w_out, m_s5_lam_re, m_s5_lam_im, m_s5_log_dt, m_s5_b_re, m_s5_b_im, m_s5_c_re, m_s5_c_im, m_s5_d, m_s5_w_glu, m_na_w_qkv, m_na_rpb, m_na_w_out, v_c_ctx, v_ada_w, v_ada_b, v_norm_gains, v_mlp_w_in, v_mlp_w_out, v_sc_w_in, v_sc_conv, v_sc_w_out, v_hg_w_in, v_hg_lower_bound, v_hg_norm, v_hg_w_out, v_s5_lam_re, v_s5_lam_im, v_s5_log_dt, v_s5_b_re, v_s5_b_im, v_s5_c_re, v_s5_c_im, v_s5_d, v_s5_w_glu, v_na_w_qkv, v_na_rpb, v_na_w_out):
    w = dict(c_ctx=c_ctx, ada_w=ada_w, ada_b=ada_b, norm_gains=norm_gains, mlp_w_in=mlp_w_in, mlp_w_out=mlp_w_out,
             sc_w_in=sc_w_in, sc_conv=sc_conv, sc_w_out=sc_w_out, hg_w_in=hg_w_in, hg_lower_bound=hg_lower_bound,
             hg_norm=hg_norm, hg_w_out=hg_w_out, s5_lam_re=s5_lam_re, s5_lam_im=s5_lam_im, s5_log_dt=s5_log_dt,
             s5_b_re=s5_b_re, s5_b_im=s5_b_im, s5_c_re=s5_c_re, s5_c_im=s5_c_im, s5_d=s5_d, s5_w_glu=s5_w_glu,
             na_w_qkv=na_w_qkv, na_rpb=na_rpb, na_w_out=na_w_out)
    mom_m = dict(zip(WEIGHTS, [m_c_ctx, m_ada_w, m_ada_b, m_norm_gains, m_mlp_w_in, m_mlp_w_out, m_sc_w_in, m_sc_conv,
                               m_sc_w_out, m_hg_w_in, m_hg_lower_bound, m_hg_norm, m_hg_w_out, m_s5_lam_re, m_s5_lam_im,
                               m_s5_log_dt, m_s5_b_re, m_s5_b_im, m_s5_c_re, m_s5_c_im, m_s5_d, m_s5_w_glu, m_na_w_qkv,
                               m_na_rpb, m_na_w_out]))
    mom_v = dict(zip(WEIGHTS, [v_c_ctx, v_ada_w, v_ada_b, v_norm_gains, v_mlp_w_in, v_mlp_w_out, v_sc_w_in, v_sc_conv,
                               v_sc_w_out, v_hg_w_in, v_hg_lower_bound, v_hg_norm, v_hg_w_out, v_s5_lam_re, v_s5_lam_im,
                               v_s5_log_dt, v_s5_b_re, v_s5_b_im, v_s5_c_re, v_s5_c_im, v_s5_d, v_s5_w_glu, v_na_w_qkv,
                               v_na_rpb, v_na_w_out]))
    bsz, _, d = x.shape
    ax, ay, ac = lax.axis_index("x"), lax.axis_index("y"), lax.axis_index("c")
    chip = 2 * ax + ay
    dev = 2 * chip + ac
    n_dev = 2 * N_SHARD
    dsh = d // N_SHARD

    shard_shapes = {n: w[n].shape for n in BIG}
    own = _pack_shard({n: w[n].astype(BF16) for n in BIG})
    packed = lax.dynamic_update_slice(_gather_shards(own, "gather_weights"), own[None], (chip, 0, 0))
    wts = _unpack_full(packed, shard_shapes)
    full = {n: [jnp.zeros(a.shape, F32) for a in v] for n, v in wts.items()}

    small_shapes = [c.shape] + [w[n].shape for n in SMALL_SHARDED]
    buf_a = _all_gather8(_pad_rows(_pack_rows([c] + [w[n] for n in SMALL_SHARDED])), "gather_small")
    per_dev = [_unpack_rows(buf_a[k], small_shapes) for k in range(n_dev)]
    c_all = jnp.concatenate([per_dev[k][0] for k in range(n_dev)], axis=0)
    for j, n in enumerate(SMALL_SHARDED):
        full[n] = jnp.concatenate([per_dev[2 * s][1 + j] for s in range(N_SHARD)], axis=-1)
    for n in SMALL_REPL:
        full[n] = w[n]

    n_all = c_all.shape[0]
    s_rows = 32
    cond = jnp.concatenate([c_all, c_ctx[None]], axis=0)
    s_all = jnp.pad(jax.nn.silu(cond), ((0, s_rows - n_all - 1), (0, 0))).astype(BF16)
    mod_part = _ada_fwd(s_all, ada_w)
    nsh = mod_part.shape[-1]
    buf_b = _all_gather8(_pad_rows(mod_part.reshape(-1, LANES)), "gather_mod")
    nrow_b = mod_part.size // LANES
    mod_raw = jnp.concatenate([buf_b[2 * s, :nrow_b].reshape(mod_part.shape) for s in range(N_SHARD)], axis=-1)
    mod_raw = mod_raw + ada_b[:, None, :]
    mod_lat = lax.dynamic_slice_in_dim(mod_raw, dev * bsz, bsz, axis=1).reshape(DEPTH, bsz, 1, N_MOD, d)
    mod_ctx = jnp.broadcast_to(mod_raw[:, n_all].reshape(DEPTH, 1, 1, N_MOD, d), (DEPTH, bsz, 1, N_MOD, d))
    mod = jnp.concatenate([mod_ctx, mod_lat], axis=2)

    lblk, grad_x, dmod, gp = _local_step(x, ctx, loss_target, mod, full, wts)

    dmod_rows = jnp.concatenate([dmod[:, :, 1].reshape(DEPTH, bsz, N_MOD * d),
                                 jnp.sum(dmod[:, :, 0], axis=1).reshape(DEPTH, 1, N_MOD * d)], axis=1)
    c_list = [dmod_rows] + [gp[n] for n in SMALL_SHARDED + SMALL_REPL] + [jnp.sum(lblk).reshape(1)]
    c_shapes = [a.shape for a in c_list]
    buf_c = _all_gather8(_pad_rows(_pack_rows(c_list)), "gather_grads")
    sum_c = _sum_leading(buf_c, "sum_grads")
    summed = _unpack_rows(sum_c, c_shapes)
    grads = {}
    for j, n in enumerate(SMALL_SHARDED):
        grads[n] = _shard_cols(summed[1 + j], chip, w[n].shape[-1])
    for j, n in enumerate(SMALL_REPL):
        grads[n] = summed[1 + len(SMALL_SHARDED) + j]
    loss = summed[-1][0]

    dmod_dev = [_unpack_rows(buf_c[k], c_shapes[:1])[0] for k in range(n_dev)]
    dm_lat = jnp.concatenate([t[:, :bsz] for t in dmod_dev], axis=1)
    dm_ctx = summed[0][:, bsz:bsz + 1]
    dm_all = jnp.concatenate([dm_lat, dm_ctx], axis=1)
    grads["ada_b"] = _sum_leading(jnp.moveaxis(dm_all, 1, 0).reshape(n_all + 1, -1, LANES), "sum_ada_b").reshape(ada_b.shape)
    dm_sh = jnp.pad(_shard_cols(dm_all, chip, nsh), ((0, 0), (0, s_rows - n_all - 1), (0, 0)))
    grads["ada_w"] = _ada_dw(s_all, dm_sh)
    ds_part = _ada_ds(dm_sh, ada_w)
    buf_d = _all_gather8(_pad_rows(ds_part[n_all:n_all + 1]), "gather_dcond")
    ds_ctx = _sum_leading(jnp.stack([buf_d[2 * s] for s in range(N_SHARD)]), "sum_dcond")[0]
    grads["c_ctx"] = jax.vjp(jax.nn.silu, c_ctx)[1](ds_ctx)[0]

    g_pack = _pack_grads(gp, shard_shapes)
    c_idx = jnp.reshape(ac, (1,)).astype(jnp.int32)
    s_idx = jnp.reshape(chip, (1,)).astype(jnp.int32)
    part, part_wire = _add_own_half(g_pack, _swap_other_half(g_pack, "rs_swap_half"), c_idx, "rs_add_sibling")
    mine = _add_arrivals(part, _send_to_chips(part_wire, "rs_send_chips"), s_idx, "rs_add_chips")
    other = _swap_reduced_half(mine, "rs_swap_reduced")
    joined = jnp.concatenate([jnp.where(ac == 0, mine, other), jnp.where(ac == 0, other, mine)], axis=0)
    grads.update(_unpack_shard(joined, shard_shapes))

    delta, new_m, new_v = {}, {}, {}
    for n in BIG + ["ada_w"]:
        delta[n], new_m[n], new_v[n] = _adamw(w[n], grads[n], mom_m[n], mom_v[n], "adamw_" + n)
    small = [n for n in WEIGHTS if n not in BIG and n != "ada_w"]
    shapes = [w[n].shape for n in small]
    packs = [_pad_rows(_pack_rows([src[n] for n in small])) for src in (w, grads, mom_m, mom_v)]
    outs = _adamw(*packs, "adamw_small")
    for tgt, buf in zip((delta, new_m, new_v), outs):
        for n, a in zip(small, _unpack_rows(buf, shapes)):
            tgt[n] = a
    return (loss, grad_x, *[grads[n] for n in WEIGHTS], *[delta[n] for n in WEIGHTS],
            *[new_m[n] for n in WEIGHTS], *[new_v[n] for n in WEIGHTS])
```

```python
import functools
import math

import jax
import jax.numpy as jnp
from jax import lax
from jax.experimental import pallas as pl
from jax.experimental.pallas import tpu as pltpu

F32 = jnp.float32
BF16 = jnp.bfloat16
MESH = pl.DeviceIdType.MESH
EPS = 1e-6
TB = 256
LANES = 1024
VMEM_LIMIT = 48 * 1024 * 1024
N_SHARD = 4
DEPTH = 4
N_MOD = 6
GRID_W = 64
HG_HEAD_DIM = 128
HG_CHUNK = 32
S5_GROUP = 16
NA_ROWS = 8
NA_COLS = 16
ADAM_LR, ADAM_B1, ADAM_B2, ADAM_EPS, ADAM_WD, ADAM_STEP = 0.001, 0.9, 0.999, 1e-08, 0.01, 10

BIG = ["mlp_w_in", "mlp_w_out", "sc_w_in", "sc_w_out", "hg_w_in", "hg_w_out", "s5_w_glu", "na_w_qkv", "na_w_out"]
COL_SHARDED = {"mlp_w_in", "sc_w_in", "hg_w_in", "s5_w_glu", "na_w_qkv"}
SMALL_SHARDED = ["norm_gains", "sc_conv", "hg_norm", "s5_d"]
SMALL_REPL = ["hg_lower_bound", "s5_lam_re", "s5_lam_im", "s5_log_dt", "s5_b_re", "s5_b_im", "s5_c_re", "s5_c_im", "na_rpb"]
WEIGHTS = ["c_ctx", "ada_w", "ada_b", "norm_gains", "mlp_w_in", "mlp_w_out", "sc_w_in", "sc_conv", "sc_w_out", "hg_w_in",
           "hg_lower_bound", "hg_norm", "hg_w_out", "s5_lam_re", "s5_lam_im", "s5_log_dt", "s5_b_re", "s5_b_im", "s5_c_re",
           "s5_c_im", "s5_d", "s5_w_glu", "na_w_qkv", "na_rpb", "na_w_out"]


def _cparams(sem=None):
    return pltpu.CompilerParams(dimension_semantics=sem, vmem_limit_bytes=VMEM_LIMIT)


def _rowwise(fn, out_dtypes, name, nseg):
    def seg_index(t):
        return jnp.minimum(t, nseg - 1)

    def in_specs(rows, colp, segp):
        rs = [pl.BlockSpec((None, TB, r.shape[-1]), lambda b, t: (b, t, 0)) for r in rows]
        cs = pl.BlockSpec(colp.shape, lambda b, t: (0, 0))
        ss = pl.BlockSpec((None, None) + segp.shape[2:], lambda b, t: (b, seg_index(t), 0, 0))
        return rs + [cs, ss]

    def load(refs, n, nc, ns):
        rows = [r[...].astype(F32) for r in refs[:n]]
        cols = [refs[n][k:k + 1, :] for k in range(nc)]
        segs = [refs[n + 1][k:k + 1, :] for k in range(ns)]
        return rows, cols, segs

    def out_blocks(rows, colp, segp):
        one = jax.ShapeDtypeStruct((1, colp.shape[-1]), F32)
        return jax.eval_shape(fn, [jax.ShapeDtypeStruct((TB, r.shape[-1]), F32) for r in rows],
                              [one] * colp.shape[0], [one] * segp.shape[2])

    def fwd_call(rows, colp, segp):
        bsz, tlen, _ = rows[0].shape
        n, nc, ns = len(rows), colp.shape[0], segp.shape[2]
        blk = out_blocks(rows, colp, segp)

        def body(*refs):
            vals = fn(*load(refs, n, nc, ns))
            for o, v in zip(refs[n + 2:], vals):
                o[...] = v.astype(o.dtype)

        return pl.pallas_call(
            body, name=name + "_fwd", grid=(bsz, tlen // TB), in_specs=in_specs(rows, colp, segp),
            out_specs=[pl.BlockSpec((None, TB, o.shape[-1]), lambda b, t: (b, t, 0)) for o in blk],
            out_shape=[jax.ShapeDtypeStruct((bsz, tlen, o.shape[-1]), dt) for o, dt in zip(blk, out_dtypes)],
            compiler_params=_cparams(("parallel", "parallel")),
        )(*rows, colp, segp)

    def bwd_call(rows, colp, segp, cts):
        bsz, tlen, _ = rows[0].shape
        n, nc, ns, m = len(rows), colp.shape[0], segp.shape[2], len(cts)

        def body(*refs):
            b, t = pl.program_id(0), pl.program_id(1)
            prim = load(refs, n, nc, ns)
            ct = tuple(r[...].astype(F32) for r in refs[n + 2:n + 2 + m])
            _, vjp = jax.vjp(fn, *prim)
            drows, dcols, dsegs = vjp(ct)
            outs = refs[n + 2 + m:]
            for o, v in zip(outs[:n], drows):
                o[...] = v.astype(o.dtype)
            dcol_ref, dseg_ref = outs[n], outs[n + 1]

            @pl.when((b == 0) & (t == 0))
            def _():
                dcol_ref[...] = jnp.zeros_like(dcol_ref)

            for k, v in enumerate(dcols):
                dcol_ref[k:k + 1, :] += v

            @pl.when(t < nseg)
            def _():
                for k, v in enumerate(dsegs):
                    dseg_ref[k:k + 1, :] = v

            @pl.when(t >= nseg)
            def _():
                for k, v in enumerate(dsegs):
                    dseg_ref[k:k + 1, :] += v

        row_specs = [pl.BlockSpec((None, TB, r.shape[-1]), lambda b, t: (b, t, 0)) for r in rows]
        ct_specs = [pl.BlockSpec((None, TB, c.shape[-1]), lambda b, t: (b, t, 0)) for c in cts]
        return pl.pallas_call(
            body, name=name + "_bwd", grid=(bsz, tlen // TB),
            in_specs=in_specs(rows, colp, segp) + ct_specs,
            out_specs=row_specs + [pl.BlockSpec(colp.shape, lambda b, t: (0, 0)),
                                   pl.BlockSpec((None, None) + segp.shape[2:], lambda b, t: (b, seg_index(t), 0, 0))],
            out_shape=[jax.ShapeDtypeStruct(r.shape, r.dtype) for r in rows]
            + [jax.ShapeDtypeStruct(colp.shape, F32), jax.ShapeDtypeStruct(segp.shape, F32)],
            compiler_params=_cparams(("arbitrary", "arbitrary")),
        )(*rows, colp, segp, *cts)

    @jax.custom_vjp
    def op(rows, colp, segp):
        return tuple(fwd_call(rows, colp, segp))

    def op_fwd(rows, colp, segp):
        return tuple(fwd_call(rows, colp, segp)), (rows, colp, segp)

    def op_bwd(res, cts):
        rows, colp, segp = res
        outs = bwd_call(rows, colp, segp, list(cts))
        return list(outs[:len(rows)]), outs[len(rows)], outs[len(rows) + 1]

    op.defvjp(op_fwd, op_bwd)
    return op


def _rms(x, g):
    return x * lax.rsqrt(jnp.mean(x * x, axis=-1, keepdims=True) + EPS) * g


def _fn_pre(rows, cols, segs):
    return (_rms(rows[0], cols[0]) * (1.0 + segs[1]) + segs[0],)


def _fn_post_pre(rows, cols, segs):
    h2 = rows[0] + segs[0] * _rms(rows[1], cols[0])
    return h2, _rms(h2, cols[1]) * (1.0 + segs[2]) + segs[1]


def _fn_post(rows, cols, segs):
    return (rows[0] + segs[0] * _rms(rows[1], cols[0]),)


def _m_tile(m, limit):
    for t in range(min(m, limit) // 16 * 16, 15, -16):
        if m % t == 0:
            return t
    return m


def _col_tile(n, limit):
    for t in range(min(n, limit), 127, -128):
        if n % t == 0 and t % 128 == 0:
            return t
    return n


def _relu2(x):
    r = jnp.maximum(x, 0.0)
    return r * r


def _mm(x, wb, act, out_dtype, name):
    m, k = x.shape
    n = wb.shape[1]
    tm = _m_tile(m, 1088)
    tn = _col_tile(n, 2048 if k <= 1024 else 512)

    def body(x_ref, w_ref, o_ref):
        xv = x_ref[...]
        if act:
            xv = _relu2(xv.astype(F32))
        o_ref[...] = jnp.dot(xv.astype(BF16), w_ref[...], preferred_element_type=F32).astype(o_ref.dtype)

    return pl.pallas_call(
        body, name=name, grid=(n // tn, m // tm),
        in_specs=[pl.BlockSpec((tm, k), lambda j, i: (i, 0)), pl.BlockSpec((k, tn), lambda j, i: (0, j))],
        out_specs=pl.BlockSpec((tm, tn), lambda j, i: (i, j)),
        out_shape=jax.ShapeDtypeStruct((m, n), out_dtype),
        compiler_params=_cparams(("parallel", "parallel")),
    )(x, wb)


def _mm_dx(dy, wb, x, act, name):
    m, n = dy.shape
    k = wb.shape[0]
    tm = _m_tile(m, 544)

    def body(dy_ref, w_ref, x_ref, o_ref):
        acc = lax.dot_general(dy_ref[...].astype(BF16), w_ref[...], (((1,), (1,)), ((), ())),
                              preferred_element_type=F32)
        if act:
            acc = acc * (2.0 * jnp.maximum(x_ref[...].astype(F32), 0.0))
        o_ref[...] = acc.astype(o_ref.dtype)

    return pl.pallas_call(
        body, name=name, grid=(m // tm,),
        in_specs=[pl.BlockSpec((tm, n), lambda i: (i, 0)), pl.BlockSpec((k, n), lambda i: (0, 0)),
                  pl.BlockSpec((tm, k), lambda i: (i, 0))],
        out_specs=pl.BlockSpec((tm, k), lambda i: (i, 0)),
        out_shape=jax.ShapeDtypeStruct((m, k), x.dtype),
        compiler_params=_cparams(("parallel",)),
    )(dy, wb, x)


def _mm_dw(x, dy, act, name):
    m, k = x.shape
    n = dy.shape[1]
    tm = _m_tile(m, 1088)
    tk, tn = _col_tile(k, 1024), _col_tile(n, 1024)

    def body(x_ref, dy_ref, o_ref):
        @pl.when(pl.program_id(2) == 0)
        def _():
            o_ref[...] = jnp.zeros_like(o_ref)

        xv = x_ref[...]
        if act:
            xv = _relu2(xv.astype(F32))
        o_ref[...] += lax.dot_general(xv.astype(BF16), dy_ref[...].astype(BF16), (((0,), (0,)), ((), ())),
                                      preferred_element_type=F32)

    return pl.pallas_call(
        body, name=name, grid=(k // tk, n // tn, m // tm),
        in_specs=[pl.BlockSpec((tm, tk), lambda a, b, i: (i, a)), pl.BlockSpec((tm, tn), lambda a, b, i: (i, b))],
        out_specs=pl.BlockSpec((tk, tn), lambda a, b, i: (a, b)),
        out_shape=jax.ShapeDtypeStruct((k, n), F32),
        compiler_params=_cparams(("parallel", "parallel", "arbitrary")),
    )(x, dy)


def _linear(act, name, out_dtype=BF16):
    def run(x, wb):
        y = _mm(x.reshape(-1, x.shape[-1]), wb, act, out_dtype, name + "_fwd")
        return y.reshape(x.shape[:-1] + (wb.shape[1],))

    @jax.custom_vjp
    def lin(x, wb, wg):
        return run(x, wb)

    def lin_fwd(x, wb, wg):
        return run(x, wb), (x, wb)

    def lin_bwd(res, dy):
        x, wb = res
        x2, dy2 = x.reshape(-1, x.shape[-1]), dy.reshape(-1, dy.shape[-1])
        dx = _mm_dx(dy2, wb, x2, act, name + "_dx").reshape(x.shape)
        return dx, jnp.zeros_like(wb), _mm_dw(x2, dy2, act, name + "_dw")

    lin.defvjp(lin_fwd, lin_bwd)
    return lin


def _loss_head(y, tgt):
    bsz, seq, d = y.shape

    def body(y_ref, t_ref, l_ref, d_ref):
        err = y_ref[...] - t_ref[...]
        d_ref[...] = err * (1.0 / d)
        l_ref[...] = jnp.full(l_ref.shape, 0.5 / d, F32) * jnp.sum(err * err)

    spec = pl.BlockSpec((None, TB, d), lambda b, t: (b, t, 0))
    lblk, dy = pl.pallas_call(
        body, name="loss_head", grid=(bsz, seq // TB), in_specs=[spec, spec],
        out_specs=[pl.BlockSpec((None, None, 8, 128), lambda b, t: (b, t, 0, 0)), spec],
        out_shape=[jax.ShapeDtypeStruct((bsz, seq // TB, 8, 128), F32), jax.ShapeDtypeStruct(y.shape, F32)],
        compiler_params=_cparams(("parallel", "parallel")),
    )(y, tgt)
    return lblk[:, :, 0, 0], dy


def _row_tile(rows, limit=512):
    for tr in range(min(rows, limit), 7, -1):
        if rows % tr == 0 and tr % 8 == 0:
            return tr
    return rows


def _adamw(w, g, m, v, name):
    shape = w.shape
    cols = shape[-1]
    w2, g2, m2, v2 = (a.reshape(-1, cols) for a in (w, g, m, v))
    rows = w2.shape[0]
    tr = _row_tile(rows, max(8, (1 << 19) // cols))
    c1, c2 = 1.0 - ADAM_B1 ** ADAM_STEP, 1.0 - ADAM_B2 ** ADAM_STEP

    def body(w_ref, g_ref, m_ref, v_ref, d_ref, mo_ref, vo_ref):
        gv = g_ref[...]
        mn = ADAM_B1 * m_ref[...] + (1.0 - ADAM_B1) * gv
        vn = ADAM_B2 * v_ref[...] + (1.0 - ADAM_B2) * (gv * gv)
        d_ref[...] = -ADAM_LR * ((mn / c1) / (jnp.sqrt(vn / c2) + ADAM_EPS) + ADAM_WD * w_ref[...])
        mo_ref[...] = mn
        vo_ref[...] = vn

    spec = pl.BlockSpec((tr, cols), lambda i: (i, 0))
    outs = pl.pallas_call(
        body, name=name, grid=(rows // tr,), in_specs=[spec] * 4, out_specs=[spec] * 3,
        out_shape=[jax.ShapeDtypeStruct((rows, cols), F32)] * 3, compiler_params=_cparams(("parallel",)),
    )(w2, g2, m2, v2)
    return tuple(o.reshape(shape) for o in outs)


def _sum_leading(a, name):
    n, rows, cols = a.shape
    tr = _row_tile(rows, max(8, (1 << 18) // cols))

    def body(a_ref, o_ref):
        acc = a_ref[0]
        for j in range(1, n):
            acc = acc + a_ref[j]
        o_ref[...] = acc

    return pl.pallas_call(
        body, name=name, grid=(rows // tr,), in_specs=[pl.BlockSpec((n, tr, cols), lambda i: (0, i, 0))],
        out_specs=pl.BlockSpec((tr, cols), lambda i: (i, 0)), out_shape=jax.ShapeDtypeStruct((rows, cols), F32),
        compiler_params=_cparams(("parallel",)),
    )(a)


def _pack_rows(arrs):
    flat = [a.reshape(-1).astype(F32) for a in arrs]
    flat = [jnp.pad(f, (0, (-f.shape[0]) % LANES)) for f in flat]
    return jnp.concatenate(flat).reshape(-1, LANES)


def _unpack_rows(buf, shapes):
    out, r = [], 0
    for s in shapes:
        n = math.prod(s)
        nr = -(-n // LANES)
        out.append(buf[r:r + nr].reshape(-1)[:n].reshape(s))
        r += nr
    return out


def _pad_rows(buf, mult=8):
    return jnp.pad(buf, ((0, (-buf.shape[0]) % mult), (0, 0)))


def _ada_fwd(s, w):
    nl, d, n = w.shape
    r = s.shape[0]

    def body(s_ref, w_ref, o_ref):
        o_ref[...] = jnp.dot(s_ref[...], w_ref[...].astype(BF16), preferred_element_type=F32)

    return pl.pallas_call(
        body, name="ada_fwd", grid=(nl,),
        in_specs=[pl.BlockSpec((r, d), lambda i: (0, 0)), pl.BlockSpec((None, d, n), lambda i: (i, 0, 0))],
        out_specs=pl.BlockSpec((None, r, n), lambda i: (i, 0, 0)), out_shape=jax.ShapeDtypeStruct((nl, r, n), F32),
        compiler_params=_cparams(("parallel",)),
    )(s, w)


def _ada_dw(s, dm):
    nl, r, n = dm.shape
    d = s.shape[1]

    def body(s_ref, dm_ref, o_ref):
        o_ref[...] = lax.dot_general(s_ref[...], dm_ref[...].astype(BF16), (((0,), (0,)), ((), ())),
                                     preferred_element_type=F32)

    return pl.pallas_call(
        body, name="ada_dw", grid=(nl,),
        in_specs=[pl.BlockSpec((r, d), lambda i: (0, 0)), pl.BlockSpec((None, r, n), lambda i: (i, 0, 0))],
        out_specs=pl.BlockSpec((None, d, n), lambda i: (i, 0, 0)), out_shape=jax.ShapeDtypeStruct((nl, d, n), F32),
        compiler_params=_cparams(("parallel",)),
    )(s, dm)


def _ada_ds(dm, w):
    nl, r, n = dm.shape
    d = w.shape[1]

    def body(dm_ref, w_ref, o_ref):
        @pl.when(pl.program_id(0) == 0)
        def _():
            o_ref[...] = jnp.zeros_like(o_ref)

        o_ref[...] += lax.dot_general(dm_ref[...].astype(BF16), w_ref[...].astype(BF16), (((1,), (1,)), ((), ())),
                                      preferred_element_type=F32)

    return pl.pallas_call(
        body, name="ada_ds", grid=(nl,),
        in_specs=[pl.BlockSpec((None, r, n), lambda i: (i, 0, 0)), pl.BlockSpec((None, d, n), lambda i: (i, 0, 0))],
        out_specs=pl.BlockSpec((r, d), lambda i: (0, 0)), out_shape=jax.ShapeDtypeStruct((r, d), F32),
        compiler_params=_cparams(("arbitrary",)),
    )(dm, w)


_ANY = pl.BlockSpec(memory_space=pl.ANY)


def _position():
    return lax.axis_index("x"), lax.axis_index("y"), lax.axis_index("c")


def _all_gather8(block, name):
    m, n = block.shape

    def body(x_ref, out_ref, send_sems, recv_sems, local_sem):
        x, y, c = _position()
        me, sibling = (x, y, c), (x, y, 1 - c)
        chips = [(1 - x, y), (x, 1 - y), (1 - x, 1 - y)]

        def slot(px, py, pc):
            return out_ref.at[4 * px + 2 * py + pc]

        def copy(k, blk, to, src=None):
            return pltpu.make_async_remote_copy(
                src_ref=slot(*blk) if src is None else src, dst_ref=slot(*blk), send_sem=send_sems.at[k],
                recv_sem=recv_sems.at[k], device_id=to, device_id_type=MESH)

        mine = pltpu.make_async_copy(x_ref, slot(*me), local_sem)
        mine.start()
        first = [copy(0, me, sibling, src=x_ref)]
        first += [copy(1 + j, me, (*chip, c), src=x_ref) for j, chip in enumerate(chips)]
        for cp in first:
            cp.start()
        passed = [copy(4 + j, (*chip, c), sibling) for j, chip in enumerate(chips)]
        for j, chip in enumerate(chips):
            copy(1 + j, (*chip, c), me).wait_recv()
            passed[j].start()
        copy(0, sibling, me).wait_recv()
        for j, chip in enumerate(chips):
            copy(4 + j, (*chip, 1 - c), me).wait_recv()
        for cp in first + passed:
            cp.wait_send()
        mine.wait()

    return pl.pallas_call(
        body, name=name, out_shape=jax.ShapeDtypeStruct((8, m, n), block.dtype), in_specs=[_ANY], out_specs=_ANY,
        scratch_shapes=[pltpu.SemaphoreType.DMA((7,)), pltpu.SemaphoreType.DMA((7,)), pltpu.SemaphoreType.DMA],
    )(block)


def _gather_shards(shard, name):
    rows, cols = shard.shape
    half = rows // 2

    def body(x_ref, out_ref, send_sems, recv_sems):
        x, y, c = _position()
        sibling = (x, y, 1 - c)
        chips = [(1 - x, y), (x, 1 - y), (1 - x, 1 - y)]

        def part(px, py, pc):
            return out_ref.at[2 * px + py, pl.ds(pc * half, half), :]

        def copy(k, blk, to, src=None):
            return pltpu.make_async_remote_copy(
                src_ref=part(*blk) if src is None else src, dst_ref=part(*blk), send_sem=send_sems.at[k],
                recv_sem=recv_sems.at[k], device_id=to, device_id_type=MESH)

        my_half = x_ref.at[pl.ds(c * half, half), :]
        first = [copy(j, (x, y, c), (*chip, c), src=my_half) for j, chip in enumerate(chips)]
        for cp in first:
            cp.start()
        passed = [copy(3 + j, (*chip, c), sibling) for j, chip in enumerate(chips)]
        for j, chip in enumerate(chips):
            copy(j, (*chip, c), sibling).wait_recv()
            passed[j].start()
        for j, chip in enumerate(chips):
            copy(3 + j, (*chip, 1 - c), sibling).wait_recv()
        for cp in first + passed:
            cp.wait_send()

    return pl.pallas_call(
        body, name=name, out_shape=jax.ShapeDtypeStruct((N_SHARD, rows, cols), shard.dtype), in_specs=[_ANY],
        out_specs=_ANY, scratch_shapes=[pltpu.SemaphoreType.DMA((6,)), pltpu.SemaphoreType.DMA((6,))],
    )(shard)


def _swap_other_half(g, name):
    ns, rows, cols = g.shape
    half = rows // 2

    def body(g_ref, out_ref, send_sem, recv_sem):
        x, y, c = _position()
        cp = pltpu.make_async_remote_copy(
            src_ref=g_ref.at[:, pl.ds((1 - c) * half, half), :], dst_ref=out_ref, send_sem=send_sem,
            recv_sem=recv_sem, device_id=(x, y, 1 - c), device_id_type=MESH)
        cp.start()
        cp.wait()

    return pl.pallas_call(
        body, name=name, out_shape=jax.ShapeDtypeStruct((ns, half, cols), g.dtype), in_specs=[_ANY], out_specs=_ANY,
        scratch_shapes=[pltpu.SemaphoreType.DMA, pltpu.SemaphoreType.DMA],
    )(g)


def _add_own_half(g, r, c_idx, name):
    ns, rows, cols = g.shape
    half = rows // 2
    tr = _row_tile(half, max(8, (1 << 19) // cols))
    nb = half // tr

    def body(c_ref, g_ref, r_ref, o_ref, ob_ref):
        acc = g_ref[...] + r_ref[...]
        o_ref[...] = acc
        ob_ref[...] = acc.astype(BF16)

    out = pl.BlockSpec((None, tr, cols), lambda s, i, c_ref: (s, i, 0))
    return pl.pallas_call(
        body, name=name,
        grid_spec=pltpu.PrefetchScalarGridSpec(
            num_scalar_prefetch=1, grid=(ns, nb),
            in_specs=[pl.BlockSpec((None, tr, cols), lambda s, i, c_ref: (s, c_ref[0] * nb + i, 0)), out],
            out_specs=[out, out]),
        out_shape=[jax.ShapeDtypeStruct((ns, half, cols), F32), jax.ShapeDtypeStruct((ns, half, cols), BF16)],
        compiler_params=_cparams(("parallel", "parallel")),
    )(c_idx, g, r)


def _send_to_chips(a, name):
    ns, half, cols = a.shape

    def body(a_ref, out_ref, send_sems, recv_sems):
        x, y, c = _position()
        chips = [(1 - x, y), (x, 1 - y), (1 - x, 1 - y)]
        cps = [pltpu.make_async_remote_copy(
            src_ref=a_ref.at[2 * px + py], dst_ref=out_ref.at[j], send_sem=send_sems.at[j], recv_sem=recv_sems.at[j],
            device_id=(px, py, c), device_id_type=MESH) for j, (px, py) in enumerate(chips)]
        for cp in cps:
            cp.start()
        for cp in cps:
            cp.wait()

    return pl.pallas_call(
        body, name=name, out_shape=jax.ShapeDtypeStruct((3, half, cols), a.dtype), in_specs=[_ANY], out_specs=_ANY,
        scratch_shapes=[pltpu.SemaphoreType.DMA((3,)), pltpu.SemaphoreType.DMA((3,))],
    )(a)


def _add_arrivals(a, r, s_idx, name):
    ns, half, cols = a.shape
    tr = _row_tile(half, max(8, (1 << 18) // cols))

    def body(s_ref, a_ref, r_ref, o_ref):
        o_ref[...] = ((a_ref[...] + r_ref[0].astype(F32)) + r_ref[1].astype(F32)) + r_ref[2].astype(F32)

    return pl.pallas_call(
        body, name=name,
        grid_spec=pltpu.PrefetchScalarGridSpec(
            num_scalar_prefetch=1, grid=(half // tr,),
            in_specs=[pl.BlockSpec((None, tr, cols), lambda i, s_ref: (s_ref[0], i, 0)),
                      pl.BlockSpec((3, tr, cols), lambda i, s_ref: (0, i, 0))],
            out_specs=pl.BlockSpec((tr, cols), lambda i, s_ref: (i, 0))),
        out_shape=jax.ShapeDtypeStruct((half, cols), F32), compiler_params=_cparams(("parallel",)),
    )(s_idx, a, r)


def _swap_reduced_half(f, name):
    def body(f_ref, out_ref, send_sem, recv_sem):
        x, y, c = _position()
        cp = pltpu.make_async_remote_copy(src_ref=f_ref, dst_ref=out_ref, send_sem=send_sem, recv_sem=recv_sem,
                                          device_id=(x, y, 1 - c), device_id_type=MESH)
        cp.start()
        cp.wait()

    return pl.pallas_call(
        body, name=name, out_shape=jax.ShapeDtypeStruct(f.shape, f.dtype), in_specs=[_ANY], out_specs=_ANY,
        scratch_shapes=[pltpu.SemaphoreType.DMA, pltpu.SemaphoreType.DMA],
    )(f)


def _conv_core(z, conv_w, nctx):
    bsz, tlen, d3 = z.shape
    d = d3 // 3
    nblk = tlen // TB
    first_lat = nctx // TB
    hr = 8
    per = TB // hr

    def split(t):
        return t[:, :d].astype(F32), t[:, d:2 * d].astype(F32), t[:, 2 * d:].astype(F32)

    def halo_valid(t):
        prev_ok = (t != 0) & (t != first_lat)
        next_ok = (t != first_lat - 1) & (t != nblk - 1)
        return prev_ok, next_ok

    def shifted(u, prev_row, next_row):
        ridx = lax.broadcasted_iota(jnp.int32, (TB, 1), 0)
        up = jnp.where(ridx == 0, prev_row, pltpu.roll(u, 1, axis=0))
        dn = jnp.where(ridx == TB - 1, next_row, pltpu.roll(u, TB - 1, axis=0))
        return up, dn

    main = lambda w_: pl.BlockSpec((None, TB, w_), lambda b, t: (b, t, 0))
    prev = lambda w_: pl.BlockSpec((None, hr, w_), lambda b, t: (b, jnp.maximum(t * per - 1, 0), 0))
    nxt = lambda w_: pl.BlockSpec((None, hr, w_), lambda b, t: (b, jnp.minimum((t + 1) * per, nblk * per - 1), 0))
    wspec = pl.BlockSpec((3, d), lambda b, t: (0, 0))

    def halo_rows(zp_ref, zn_ref, t):
        prev_ok, next_ok = halo_valid(t)
        _, cgp, vp = split(zp_ref[...])
        _, cgn, vn = split(zn_ref[...])
        up = jnp.where(prev_ok, (cgp * vp)[hr - 1:hr], 0.0)
        un = jnp.where(next_ok, (cgn * vn)[0:1], 0.0)
        return up, un

    def fwd_call(z, w):
        def body(z_ref, zp_ref, zn_ref, w_ref, o_ref):
            bg, cg, v = split(z_ref[...])
            u = cg * v
            up, dn = shifted(u, *halo_rows(zp_ref, zn_ref, pl.program_id(1)))
            o_ref[...] = (bg * (w_ref[0:1, :] * up + w_ref[1:2, :] * u + w_ref[2:3, :] * dn)).astype(o_ref.dtype)

        return pl.pallas_call(
            body, name="conv_fwd", grid=(bsz, nblk), in_specs=[main(d3), prev(d3), nxt(d3), wspec], out_specs=main(d),
            out_shape=jax.ShapeDtypeStruct((bsz, tlen, d), BF16), compiler_params=_cparams(("parallel", "parallel")),
        )(z, z, z, w)

    def bwd_call(z, w, dy):
        def body(z_ref, zp_ref, zn_ref, w_ref, dy_ref, dyp_ref, dyn_ref, dz_ref, dw_ref):
            t = pl.program_id(1)
            prev_ok, next_ok = halo_valid(t)
            bg, cg, v = split(z_ref[...])
            u = cg * v
            up, dn = shifted(u, *halo_rows(zp_ref, zn_ref, t))
            w0, w1, w2 = w_ref[0:1, :], w_ref[1:2, :], w_ref[2:3, :]
            dyv = dy_ref[...].astype(F32)
            dconv = dyv * bg
            bgp = zp_ref[...][:, :d].astype(F32)
            bgn = zn_ref[...][:, :d].astype(F32)
            dc_prev = jnp.where(prev_ok, (dyp_ref[...].astype(F32) * bgp)[hr - 1:hr], 0.0)
            dc_next = jnp.where(next_ok, (dyn_ref[...].astype(F32) * bgn)[0:1], 0.0)
            dc_up, dc_dn = shifted(dconv, dc_prev, dc_next)
            du = w0 * dc_dn + w1 * dconv + w2 * dc_up
            dz_ref[:, 0:d] = (dyv * (w0 * up + w1 * u + w2 * dn)).astype(dz_ref.dtype)
            dz_ref[:, d:2 * d] = (du * v).astype(dz_ref.dtype)
            dz_ref[:, 2 * d:3 * d] = (du * cg).astype(dz_ref.dtype)

            @pl.when((pl.program_id(0) == 0) & (t == 0))
            def _():
                dw_ref[...] = jnp.zeros_like(dw_ref)

            dw_ref[0:1, :] += jnp.sum(dconv * up, axis=0, keepdims=True)
            dw_ref[1:2, :] += jnp.sum(dconv * u, axis=0, keepdims=True)
            dw_ref[2:3, :] += jnp.sum(dconv * dn, axis=0, keepdims=True)

        return pl.pallas_call(
            body, name="conv_bwd", grid=(bsz, nblk),
            in_specs=[main(d3), prev(d3), nxt(d3), wspec, main(d), prev(d), nxt(d)],
            out_specs=[main(d3), wspec],
            out_shape=[jax.ShapeDtypeStruct(z.shape, BF16), jax.ShapeDtypeStruct((3, d), F32)],
            compiler_params=_cparams(("arbitrary", "arbitrary")),
        )(z, z, z, w, dy, dy, dy)

    @jax.custom_vjp
    def op(z, w):
        return fwd_call(z, w)

    def op_fwd(z, w):
        return fwd_call(z, w), (z, w)

    def op_bwd(res, dy):
        return tuple(bwd_call(*res, dy))

    op.defvjp(op_fwd, op_bwd)
    return op(z, conv_w)


HG_BLOCK = 8
HG_UNROLL = 4
HG_VMEM_LIMIT = 56 * 1024 * 1024


def _hg_cumsum(x, rev):
    n = HG_CHUNK
    nrow = x.shape[0]

    def run(v, backwards):
        pos = lax.broadcasted_iota(jnp.int32, (nrow, 1), 0) % n
        k = 1
        while k < n:
            if backwards:
                v = v + jnp.where(pos + k < n, pltpu.roll(v, nrow - k, axis=0), 0.0)
            else:
                v = v + jnp.where(pos >= k, pltpu.roll(v, k, axis=0), 0.0)
            k *= 2
        return v

    @jax.custom_vjp
    def cs(v):
        return run(v, rev)

    cs.defvjp(lambda v: (run(v, rev), None), lambda _, g: (run(g, not rev),))
    return cs(x)


def _hg_local(q, v, tf, lb, rev):
    n = HG_CHUNK
    nrow = q.shape[0]
    nb = nrow // n
    f = lb + (1.0 - lb) * jax.nn.sigmoid(tf)
    kk = 1.0 - f
    b = _hg_cumsum(jnp.log(f), rev)
    pos = lax.broadcasted_iota(jnp.int32, (nb, n, 1), 1)
    b3 = b.reshape(nb, n, -1)
    mid = n - n // 2 if rev else n // 2 - 1
    last = 0 if rev else n - 1
    b_mid = jnp.sum(jnp.where(pos == mid, b3, 0.0), axis=1, keepdims=True)
    b_last = jnp.sum(jnp.where(pos == last, b3, 0.0), axis=1, keepdims=True)
    q3, k3, v3 = q.reshape(nb, n, -1), kk.reshape(nb, n, -1), v.reshape(nb, n, -1)
    qs = (q3 * jnp.exp(b3 - b_mid)).astype(BF16)
    ks = (k3 * jnp.exp(b_mid - b3)).astype(BF16)
    sc = jnp.einsum('ctk,csk->cts', qs, ks, preferred_element_type=F32)
    row = lax.broadcasted_iota(jnp.int32, (1, n, n), 1)
    col = lax.broadcasted_iota(jnp.int32, (1, n, n), 2)
    sc = jnp.where((col >= row) if rev else (col <= row), sc, 0.0).astype(BF16)
    o_intra = jnp.einsum('cts,csv->ctv', sc, v3.astype(BF16), preferred_element_type=F32)
    qe = q3 * jnp.exp(b3)
    ks2 = k3 * jnp.exp(b_last - b3)
    return o_intra.reshape(nrow, -1), qe.reshape(nrow, -1), ks2.reshape(nrow, -1), jnp.exp(b_last).reshape(nb, -1)


def _hg_scan(nctx):
    hd, n = HG_HEAD_DIM, HG_CHUNK
    rb = HG_BLOCK * n

    def geometry(z):
        bsz, tlen, d5 = z.shape
        return bsz, tlen, d5 // 5, (d5 // 5) // hd, tlen // n, nctx // n

    def rev_chunk(j, nch, ncc):
        return jnp.where(j < ncc, ncc - 1 - j, nch - 1 + ncc - j)

    def rows(c):
        return pl.ds(pl.multiple_of(c * n, n), n)

    seq_params = pltpu.CompilerParams(dimension_semantics=("parallel", "parallel"), vmem_limit_bytes=HG_VMEM_LIMIT)
    nt, tn = (((1,), (1,)), ((), ())), (((0,), (0,)), ((), ()))

    def local_specs(z):
        bsz, tlen, d, nh, nch, ncc = geometry(z)
        col = lambda k: pl.BlockSpec((None, rb, hd), lambda b, h, t: (b, t, k * nh + h))
        blk = pl.BlockSpec((None, rb, hd), lambda b, h, t: (b, t, h))
        dec = pl.BlockSpec((None, HG_BLOCK, hd), lambda b, h, t: (b, t, h))
        lbs = pl.BlockSpec((1, hd), lambda b, h, t: (0, h))
        return [col(0), col(1), col(3), col(4), lbs], blk, dec

    def local_fwd(z, lb):
        bsz, tlen, d, nh, nch, ncc = geometry(z)
        ins, blk, dec = local_specs(z)

        def body(q_ref, v_ref, ff_ref, fb_ref, lb_ref, o_ref, qf_ref, kf_ref, df_ref, qb_ref, kb_ref, db_ref):
            q, v, lbv = q_ref[...].astype(F32), v_ref[...].astype(F32), lb_ref[...]
            of, qe, ks, dc = _hg_local(q, v, ff_ref[...].astype(F32), lbv, False)
            qf_ref[...], kf_ref[...], df_ref[...] = qe.astype(BF16), ks.astype(BF16), dc
            ob, qe, ks, dc = _hg_local(q, v, fb_ref[...].astype(F32), lbv, True)
            qb_ref[...], kb_ref[...], db_ref[...] = qe.astype(BF16), ks.astype(BF16), dc
            o_ref[...] = of + ob

        act = jax.ShapeDtypeStruct((bsz, tlen, d), BF16)
        dcs = jax.ShapeDtypeStruct((bsz, nch, d), F32)
        return pl.pallas_call(
            body, name="hg_local", grid=(bsz, nh, tlen // rb), in_specs=ins,
            out_specs=[blk, blk, blk, dec, blk, blk, dec],
            out_shape=[jax.ShapeDtypeStruct((bsz, tlen, d), F32), act, act, dcs, act, act, dcs],
            compiler_params=_cparams(("parallel", "parallel", "parallel")),
        )(z, z, z, z, lb)

    def local_bwd(z, lb, do, dv_in, dqf, dkf, ddf, dqb, dkb, ddb):
        bsz, tlen, d, nh, nch, ncc = geometry(z)
        ins, blk, dec = local_specs(z)

        def body(q_ref, v_ref, ff_ref, fb_ref, lb_ref, do_ref, dvi_ref, dqf_ref, dkf_ref, ddf_ref, dqb_ref, dkb_ref,
                 ddb_ref, dq_ref, dv_ref, dff_ref, dfb_ref, dlb_ref):
            q, v, lbv = q_ref[...].astype(F32), v_ref[...].astype(F32), lb_ref[...]
            dov = do_ref[...]
            dq, dv, dlb = jnp.zeros_like(q), dvi_ref[...], jnp.zeros_like(lbv)
            for rev, f_ref, df_ref, cts in ((False, ff_ref, dff_ref, (dqf_ref, dkf_ref, ddf_ref)),
                                            (True, fb_ref, dfb_ref, (dqb_ref, dkb_ref, ddb_ref))):
                _, vjp = jax.vjp(functools.partial(_hg_local, rev=rev), q, v, f_ref[...].astype(F32), lbv)
                g = vjp((dov, cts[0][...].astype(F32), cts[1][...].astype(F32), cts[2][...]))
                dq, dv, dlb = dq + g[0], dv + g[1], dlb + g[3]
                df_ref[...] = g[2].astype(df_ref.dtype)
            dq_ref[...] = dq.astype(dq_ref.dtype)
            dv_ref[...] = dv.astype(dv_ref.dtype)

            @pl.when(pl.program_id(2) == 0)
            def _():
                dlb_ref[...] = dlb

            @pl.when(pl.program_id(2) != 0)
            def _():
                dlb_ref[...] += dlb

        act = jax.ShapeDtypeStruct((bsz, tlen, d), BF16)
        return pl.pallas_call(
            body, name="hg_local_bwd", grid=(bsz, nh, tlen // rb),
            in_specs=ins + [blk, blk, blk, blk, dec, blk, blk, dec],
            out_specs=[blk, blk, blk, blk, pl.BlockSpec((None, 1, hd), lambda b, h, t: (b, 0, h))],
            out_shape=[act, act, act, act, jax.ShapeDtypeStruct((bsz, 1, d), F32)],
            compiler_params=_cparams(("parallel", "parallel", "arbitrary")),
        )(z, z, z, z, lb, do, dv_in, dqf, dkf, ddf, dqb, dkb, ddb)

    def head_spec(z, k=None):
        bsz, tlen, d, nh, nch, ncc = geometry(z)
        if k is None:
            return pl.BlockSpec((None, tlen, hd), lambda b, h: (b, 0, h))
        return pl.BlockSpec((None, tlen, hd), lambda b, h: (b, 0, k * nh + h))

    def dec_spec(z):
        bsz, tlen, d, nh, nch, ncc = geometry(z)
        return pl.BlockSpec((None, nch, hd), lambda b, h: (b, 0, h))

    def state_fwd(z, o_in, qf, kf, df, qb, kb, db):
        bsz, tlen, d, nh, nch, ncc = geometry(z)
        hs, ds = head_spec(z), dec_spec(z)

        def body(v_ref, oi_ref, qf_ref, kf_ref, df_ref, qb_ref, kb_ref, db_ref, o_ref):
            o_ref[...] = oi_ref[...]
            chains = ((False, qf_ref, kf_ref, df_ref), (True, qb_ref, kb_ref, db_ref))

            def step(j, carry):
                out = []
                for (rev, q_ref, k_ref, d_ref), st in zip(chains, carry):
                    c = rev_chunk(j, nch, ncc) if rev else j
                    sl = rows(c)
                    o_ref[sl, :] += lax.dot_general(q_ref[sl, :], st.astype(BF16), nt, preferred_element_type=F32)
                    out.append(st * d_ref[pl.ds(c, 1), :] + lax.dot_general(v_ref[sl, :], k_ref[sl, :], tn,
                                                                              preferred_element_type=F32))
                return tuple(out)

            zero = jnp.zeros((hd, hd), F32)
            lax.fori_loop(0, nch, step, (zero, zero), unroll=HG_UNROLL)

        return pl.pallas_call(
            body, name="hg_state", grid=(bsz, nh), in_specs=[head_spec(z, 1), hs, hs, hs, ds, hs, hs, ds],
            out_specs=hs, out_shape=jax.ShapeDtypeStruct((bsz, tlen, d), F32), compiler_params=seq_params,
        )(z, o_in, qf, kf, df, qb, kb, db)

    def state_bwd(z, do, qf, kf, df, qb, kb, db):
        bsz, tlen, d, nh, nch, ncc = geometry(z)
        hs, ds = head_spec(z), dec_spec(z)

        def body(v_ref, do_ref, qf_ref, kf_ref, df_ref, qb_ref, kb_ref, db_ref,
                 dv_ref, dqf_ref, dkf_ref, ddf_ref, dqb_ref, dkb_ref, ddb_ref, stf_ref, stb_ref):
            chains = ((False, qf_ref, kf_ref, df_ref, dqf_ref, dkf_ref, ddf_ref, stf_ref),
                      (True, qb_ref, kb_ref, db_ref, dqb_ref, dkb_ref, ddb_ref, stb_ref))

            def fstep(j, carry):
                out = []
                for (rev, q_ref, k_ref, d_ref, _, _, _, st_ref), st in zip(chains, carry):
                    c = rev_chunk(j, nch, ncc) if rev else j
                    sl = rows(c)
                    st_ref[j] = st
                    out.append(st * d_ref[pl.ds(c, 1), :] + lax.dot_general(v_ref[sl, :], k_ref[sl, :], tn,
                                                                              preferred_element_type=F32))
                return tuple(out)

            zero = jnp.zeros((hd, hd), F32)
            lax.fori_loop(0, nch, fstep, (zero, zero), unroll=HG_UNROLL)
            dv_ref[...] = jnp.zeros_like(dv_ref)

            def bstep(i, carry):
                j = nch - 1 - i
                out = []
                for (rev, q_ref, k_ref, d_ref, dq_ref, dk_ref, dd_ref, st_ref), dst in zip(chains, carry):
                    c = rev_chunk(j, nch, ncc) if rev else j
                    sl = rows(c)
                    st = st_ref[j]
                    dob = do_ref[sl, :].astype(BF16)
                    dstb = dst.astype(BF16)
                    dec = d_ref[pl.ds(c, 1), :]
                    dq_ref[sl, :] = jnp.dot(dob, st.astype(BF16), preferred_element_type=F32).astype(dq_ref.dtype)
                    dk_ref[sl, :] = jnp.dot(v_ref[sl, :], dstb, preferred_element_type=F32).astype(dk_ref.dtype)
                    dv_ref[sl, :] += lax.dot_general(k_ref[sl, :], dstb, nt, preferred_element_type=F32)
                    dd_ref[pl.ds(c, 1), :] = jnp.sum(dst * st, axis=0, keepdims=True)
                    out.append(dst * dec + lax.dot_general(dob, q_ref[sl, :], tn, preferred_element_type=F32))
                return tuple(out)

            lax.fori_loop(0, nch, bstep, (zero, zero), unroll=HG_UNROLL)

        act = jax.ShapeDtypeStruct((bsz, tlen, d), BF16)
        dcs = jax.ShapeDtypeStruct((bsz, nch, d), F32)
        return pl.pallas_call(
            body, name="hg_state_bwd", grid=(bsz, nh), in_specs=[head_spec(z, 1), hs, hs, hs, ds, hs, hs, ds],
            out_specs=[hs, hs, hs, ds, hs, hs, ds],
            out_shape=[jax.ShapeDtypeStruct((bsz, tlen, d), F32), act, act, dcs, act, act, dcs],
            scratch_shapes=[pltpu.VMEM((nch, hd, hd), F32), pltpu.VMEM((nch, hd, hd), F32)],
            compiler_params=seq_params,
        )(z, do, qf, kf, df, qb, kb, db)

    @jax.custom_vjp
    def scan(z, lb):
        return state_fwd(z, *local_fwd(z, lb))

    def scan_fwd(z, lb):
        loc = local_fwd(z, lb)
        return state_fwd(z, *loc), (z, lb, loc[1:])

    def scan_bwd(res, do):
        z, lb, loc = res
        dv_in, *dstate = state_bwd(z, do, *loc)
        dq, dv, dff, dfb, dlb = local_bwd(z, lb, do, dv_in, *dstate)
        dz = jnp.concatenate([dq, dv, jnp.zeros_like(dff), dff, dfb], axis=-1)
        return dz, jnp.sum(dlb, axis=0)

    scan.defvjp(scan_fwd, scan_bwd)
    return scan


def _hg_readout(o, gate, g_norm):
    bsz, tlen, d = gate.shape
    nh = d // HG_HEAD_DIM
    rb = TB * nh

    def fn(x, g, gt):
        return x * lax.rsqrt(jnp.mean(x * x, axis=-1, keepdims=True) + EPS) * gt * (g * jax.nn.sigmoid(g))

    spec = pl.BlockSpec((None, rb, HG_HEAD_DIM), lambda b, t: (b, t, 0))
    gspec = pl.BlockSpec((rb, HG_HEAD_DIM), lambda b, t: (0, 0))
    grid = (bsz, tlen // TB)

    def fwd_call(x, g, gt):
        def body(x_ref, g_ref, gt_ref, y_ref):
            y_ref[...] = fn(x_ref[...], g_ref[...].astype(F32), gt_ref[...]).astype(y_ref.dtype)

        return pl.pallas_call(
            body, name="hg_readout_fwd", grid=grid, in_specs=[spec, spec, gspec], out_specs=spec,
            out_shape=jax.ShapeDtypeStruct(g.shape, BF16), compiler_params=_cparams(("parallel", "parallel")),
        )(x, g, gt)

    def bwd_call(x, g, gt, dy):
        def body(x_ref, g_ref, gt_ref, dy_ref, dx_ref, dg_ref, dgt_ref):
            _, vjp = jax.vjp(fn, x_ref[...], g_ref[...].astype(F32), gt_ref[...])
            dx, dg, dgt = vjp(dy_ref[...].astype(F32))
            dx_ref[...] = dx
            dg_ref[...] = dg.astype(dg_ref.dtype)

            @pl.when((pl.program_id(0) == 0) & (pl.program_id(1) == 0))
            def _():
                dgt_ref[...] = jnp.zeros_like(dgt_ref)

            dgt_ref[...] += dgt

        return pl.pallas_call(
            body, name="hg_readout_bwd", grid=grid, in_specs=[spec, spec, gspec, spec],
            out_specs=[spec, spec, gspec],
            out_shape=[jax.ShapeDtypeStruct(x.shape, F32), jax.ShapeDtypeStruct(g.shape, BF16),
                       jax.ShapeDtypeStruct(gt.shape, F32)],
            compiler_params=_cparams(("arbitrary", "arbitrary")),
        )(x, g, gt, dy)

    @jax.custom_vjp
    def op(x, g, gt):
        return fwd_call(x, g, gt)

    def op_fwd(x, g, gt):
        return fwd_call(x, g, gt), (x, g, gt)

    def op_bwd(res, dy):
        return tuple(bwd_call(*res, dy))

    op.defvjp(op_fwd, op_bwd)
    heads = lambda t: t.reshape(bsz, tlen * nh, HG_HEAD_DIM)
    gt = jnp.tile(g_norm.reshape(nh, HG_HEAD_DIM), (TB, 1))
    return op(heads(o), heads(gate), gt).reshape(bsz, tlen, d)


def _hgrn_core(z, lower_bound, g_norm, nctx):
    d = g_norm.shape[-1]
    o = _hg_scan(nctx)(z, lower_bound.reshape(1, d))
    return _hg_readout(o, z[..., 2 * d:3 * d], g_norm)


S5_LC = 16


def _s5_mats(lam_re, lam_im, log_dt, b_re, b_im, c_re, c_im):
    hi = lax.Precision.HIGHEST
    _, ng, ns = lam_re.shape
    lc, gs = S5_LC, S5_GROUP
    lam_re = jnp.minimum(lam_re, -1e-4)
    dt = jnp.exp(log_dt)[:, None, :, None]
    k = jnp.arange(lc + 1, dtype=F32)[None, :, None, None]
    mag, ang = jnp.exp(lam_re[:, None] * dt * k), lam_im[:, None] * dt * k
    p_re, p_im = mag * jnp.cos(ang), mag * jnp.sin(ang)
    a_re, a_im = p_re[:, 1], p_im[:, 1]
    den = lam_re * lam_re + lam_im * lam_im
    f_re = ((a_re - 1) * lam_re + a_im * lam_im) / den
    f_im = (a_im * lam_re - (a_re - 1) * lam_im) / den
    bt_re, bt_im = b_re.transpose(0, 2, 1), b_im.transpose(0, 2, 1)
    bb_re = f_re[:, :, None] * bt_re - f_im[:, :, None] * bt_im
    bb_im = f_re[:, :, None] * bt_im + f_im[:, :, None] * bt_re
    w_re = c_re[:, None] * p_re[:, :, :, None] - c_im[:, None] * p_im[:, :, :, None]
    w_im = c_re[:, None] * p_im[:, :, :, None] + c_im[:, None] * p_re[:, :, :, None]
    kk = jnp.sum(w_re[:, :lc, :, :, None, :] * bb_re[:, None, :, None, :, :]
                 - w_im[:, :lc, :, :, None, :] * bb_im[:, None, :, None, :, :], axis=-1)
    t = jnp.arange(lc)
    lag = jnp.stack([t[:, None] - t[None, :], t[None, :] - t[:, None]])
    lag_hot = (lag[..., None] == jnp.arange(lc)).astype(F32)
    mt = jnp.einsum('rtsk,rkgcd->rgsdtc', lag_hot, kk, precision=hi).reshape(2, ng, lc * gs, lc * gs)
    left = jnp.stack([lc - 1 - t, t])
    left_hot = (left[..., None] == jnp.arange(lc + 1)).astype(F32)
    pw_re = jnp.einsum('rsk,rkgn->rsgn', left_hot, p_re, precision=hi)
    pw_im = jnp.einsum('rsk,rkgn->rsgn', left_hot, p_im, precision=hi)
    pr = pw_re[:, :, :, None] * bb_re[:, None] - pw_im[:, :, :, None] * bb_im[:, None]
    pi = pw_re[:, :, :, None] * bb_im[:, None] + pw_im[:, :, :, None] * bb_re[:, None]
    pt = jnp.concatenate([pr, pi], axis=-1).transpose(0, 2, 1, 3, 4).reshape(2, ng, lc * gs, 2 * ns)
    since = jnp.stack([t + 1, lc - t])
    since_hot = (since[..., None] == jnp.arange(lc + 1)).astype(F32)
    q = jnp.concatenate([jnp.einsum('rtk,rkgcn->rtgcn', since_hot, w_re, precision=hi),
                         -jnp.einsum('rtk,rkgcn->rtgcn', since_hot, w_im, precision=hi)], axis=-1)
    qt = q.transpose(0, 2, 4, 1, 3).reshape(2, ng, 2 * ns, lc * gs)
    a16 = jnp.concatenate([p_re[:, lc], p_im[:, lc]], axis=-1)
    return mt, pt, qt, a16


def _s5_bmm(terms, out_dtype, name, sum_dirs=False, ndirs=2):
    ng = terms[0][0].shape[-3]
    ops, dlist = [], []
    for a, b, dn in terms:
        ops += [a, b]
        dlist.append(dn)
    (ca,), (cb,) = dlist[0]
    om, on = terms[0][0].shape[-2:][1 - ca], terms[0][1].shape[-2:][1 - cb]

    def spec(o):
        if o.ndim == 4:
            return pl.BlockSpec((None, None) + o.shape[2:], lambda g, d: (d, g, 0, 0))
        return pl.BlockSpec((None,) + o.shape[1:], lambda g, d: (g, 0, 0))

    def body(*refs):
        acc = None
        for j, dn in enumerate(dlist):
            a, b = refs[2 * j][...].astype(BF16), refs[2 * j + 1][...].astype(BF16)
            r = lax.dot_general(a, b, (dn, ((), ())), preferred_element_type=F32)
            acc = r if acc is None else acc + r
        o_ref = refs[-1]
        if sum_dirs:
            @pl.when(pl.program_id(1) == 0)
            def _():
                o_ref[...] = acc.astype(o_ref.dtype)

            @pl.when(pl.program_id(1) != 0)
            def _():
                o_ref[...] = (o_ref[...].astype(F32) + acc).astype(o_ref.dtype)
        else:
            o_ref[...] = acc.astype(o_ref.dtype)

    if sum_dirs:
        out_spec = pl.BlockSpec((None, om, on), lambda g, d: (g, 0, 0))
        out_shape = jax.ShapeDtypeStruct((ng, om, on), out_dtype)
    else:
        out_spec = pl.BlockSpec((None, None, om, on), lambda g, d: (d, g, 0, 0))
        out_shape = jax.ShapeDtypeStruct((2, ng, om, on), out_dtype)
    return pl.pallas_call(
        body, name=name, grid=(ng, ndirs), in_specs=[spec(o) for o in ops], out_specs=out_spec, out_shape=out_shape,
        compiler_params=_cparams(("parallel", "arbitrary" if sum_dirs else "parallel")),
    )(*ops)


def _s5_row_block(rows, size=32):
    return size if rows % size == 0 else rows


def _s5_chunk_order(j, d, nc, ncc):
    return jnp.where(d == 0, j, jnp.where(j < ncc, ncc - 1 - j, nc - 1 + ncc - j))


def _s5_scan_fwd(z, a1, a2, ncc, name):
    nd, nc, rows, lanes = z.shape
    rb = _s5_row_block(rows, 64)

    def body(z_ref, a1_ref, a2_ref, x_ref):
        a1v, a2v = a1_ref[...], a2_ref[...]
        d = pl.program_id(0)

        def step(j, x):
            c = _s5_chunk_order(j, d, nc, ncc)
            x_ref[c] = x
            return a1v * x + a2v * pltpu.roll(x, lanes // 2, axis=1) + z_ref[c]

        lax.fori_loop(0, nc, step, jnp.zeros((rb, lanes), F32), unroll=2)

    blk = pl.BlockSpec((None, nc, rb, lanes), lambda d, r: (d, 0, r, 0))
    par = pl.BlockSpec((None, rb, lanes), lambda d, r: (d, r, 0))
    return pl.pallas_call(
        body, name=name, grid=(nd, rows // rb), in_specs=[blk, par, par], out_specs=blk,
        out_shape=jax.ShapeDtypeStruct(z.shape, F32), compiler_params=_cparams(("parallel", "parallel")),
    )(z, a1, a2)


def _s5_scan_bwd(dxp, xp, a1, a2b, ncc, name):
    nd, nc, rows, lanes = dxp.shape
    rb = _s5_row_block(rows)

    def body(dxp_ref, xp_ref, a1_ref, a2_ref, dz_ref, p1_ref, p2_ref):
        a1v, a2v = a1_ref[...], a2_ref[...]
        zero = jnp.zeros((rb, lanes), F32)
        d = pl.program_id(0)

        def step(i, carry):
            g_next, nxt, p1, p2 = carry
            c = _s5_chunk_order(nc - 1 - i, d, nc, ncc)
            g = nxt + a1v * g_next + a2v * pltpu.roll(g_next, lanes // 2, axis=1)
            dz_ref[c] = g
            x = xp_ref[c]
            return g, dxp_ref[c], p1 + x * g, p2 + pltpu.roll(x, lanes // 2, axis=1) * g

        _, _, p1, p2 = lax.fori_loop(0, nc, step, (zero, zero, zero, zero), unroll=2)
        p1_ref[...] = p1
        p2_ref[...] = p2

    blk = pl.BlockSpec((None, nc, rb, lanes), lambda d, r: (d, 0, r, 0))
    par = pl.BlockSpec((None, rb, lanes), lambda d, r: (d, r, 0))
    return pl.pallas_call(
        body, name=name, grid=(nd, rows // rb), in_specs=[blk, blk, par, par], out_specs=[blk, par, par],
        out_shape=[jax.ShapeDtypeStruct(dxp.shape, F32), jax.ShapeDtypeStruct((nd, rows, lanes), F32),
                   jax.ShapeDtypeStruct((nd, rows, lanes), F32)],
        compiler_params=_cparams(("parallel", "parallel")),
    )(dxp, xp, a1, a2b)


def _s5_rows(t, bsz):
    nd, ng, m, k = t.shape
    return t.reshape(nd, ng, bsz, m // bsz, k).transpose(0, 3, 2, 1, 4).reshape(nd, m // bsz, bsz * ng, k)


def _s5_groups(t, bsz):
    nd, nc, rows, k = t.shape
    return t.reshape(nd, nc, bsz, rows // bsz, k).transpose(0, 3, 2, 1, 4).reshape(nd, rows // bsz, bsz * nc, k)


def _s5_coeffs(a16, bsz):
    half = a16.shape[-1] // 2
    re, im = a16[..., :half], a16[..., half:]
    tile = lambda v: jnp.tile(v, (1, bsz, 1))
    return tile(jnp.concatenate([re, re], -1)), tile(jnp.concatenate([-im, im], -1)), tile(jnp.concatenate([im, -im], -1))


def _s5_apply(bsz, ncc):
    nn, nt, tn = ((1,), (0,)), ((1,), (1,)), ((0,), (0,))

    def run(u, mt, pt, qt, a16):
        a1, a2, _ = _s5_coeffs(a16, bsz)
        z = _s5_bmm([(u, pt, nn)], F32, "s5_z")
        xp = _s5_scan_fwd(_s5_rows(z, bsz), a1, a2, ncc, "s5_scan")
        xg = _s5_groups(xp, bsz).astype(BF16)
        y = _s5_bmm([(u, mt, nn), (xg, qt, nn)], BF16, "s5_y", sum_dirs=True)
        return y, (xp, xg)

    @jax.custom_vjp
    def apply(u, mt, pt, qt, a16):
        return run(u, mt.astype(BF16), pt.astype(BF16), qt.astype(BF16), a16)[0]

    def apply_fwd(u, mt, pt, qt, a16):
        mtb, ptb, qtb = mt.astype(BF16), pt.astype(BF16), qt.astype(BF16)
        y, (xp, xg) = run(u, mtb, ptb, qtb, a16)
        return y, (u, mtb, ptb, qtb, a16, xp, xg)

    def apply_bwd(res, dy):
        u, mtb, ptb, qtb, a16, xp, xg = res
        a1, _, a2b = _s5_coeffs(a16, bsz)
        dyb = dy.astype(BF16)
        dmt = _s5_bmm([(u, dyb, tn)], F32, "s5_dmt", sum_dirs=True, ndirs=1)
        dqt = _s5_bmm([(xg, dyb, tn)], F32, "s5_dqt")
        dxp = _s5_bmm([(dyb, qtb, nt)], F32, "s5_dxp")
        dz, p1, p2 = _s5_scan_bwd(_s5_rows(dxp, bsz), xp, a1, a2b, ncc, "s5_scan_bwd")
        dzg = _s5_groups(dz, bsz).astype(BF16)
        dpt = _s5_bmm([(u, dzg, tn)], F32, "s5_dpt")
        du = _s5_bmm([(dyb, mtb, nt), (dzg, ptb, nt)], BF16, "s5_du", sum_dirs=True)
        half = a16.shape[-1] // 2
        p1 = jnp.sum(p1.reshape(2, bsz, -1, 2 * half), axis=1)
        p2 = jnp.sum(p2.reshape(2, bsz, -1, 2 * half), axis=1)
        da16 = jnp.concatenate([p1[..., :half] + p1[..., half:], p2[..., half:] - p2[..., :half]], axis=-1)
        return du, jnp.stack([dmt, dmt]), dpt, dqt, da16

    apply.defvjp(apply_fwd, apply_bwd)
    return apply


def _s5_placement(lanes):
    lc, gs = S5_LC, S5_GROUP
    lane = jnp.arange(lanes)
    col = (lane // gs) * (lc * gs) + lane % gs
    t = jnp.arange(lc)
    return (col[None, :, None] + t[:, None, None] * gs == jnp.arange(lanes * lc)[None, None, :]).astype(BF16)


def _s5_relayout(bsz, tlen, d):
    lc, gs, lanes = S5_LC, S5_GROUP, 128
    nc, ng, gb = tlen // lc, d // S5_GROUP, 128 // S5_GROUP
    width = lanes * lc
    tok = pl.BlockSpec((None, tlen, lanes), lambda b, j: (b, 0, j))
    grp = pl.BlockSpec((gb, nc, lc * gs), lambda b, j: (j, b, 0))
    plc = pl.BlockSpec((lc, lanes, width), lambda b, j: (0, 0, 0))
    grid = (bsz, d // lanes)
    sem = _cparams(("parallel", "parallel"))

    def chunk_call(a):
        def body(a_ref, p_ref, o_ref, f_ref):
            f_ref[...] = a_ref[...].astype(F32)
            acc = None
            for t in range(lc):
                rows = f_ref[pl.ds(t, nc, stride=lc), :].astype(BF16)
                part = jnp.dot(rows, p_ref[t], preferred_element_type=F32)
                acc = part if acc is None else acc + part
            for g in range(gb):
                o_ref[g] = acc[:, g * lc * gs:(g + 1) * lc * gs].astype(o_ref.dtype)

        return pl.pallas_call(
            body, name="s5_chunk", grid=grid, in_specs=[tok, plc], out_specs=grp,
            out_shape=jax.ShapeDtypeStruct((ng, bsz * nc, lc * gs), BF16),
            scratch_shapes=[pltpu.VMEM((tlen, lanes), F32)], compiler_params=sem,
        )(a, _s5_placement(lanes))

    def unchunk_call(y):
        def body(y_ref, p_ref, o_ref, f_ref):
            cat = jnp.concatenate([y_ref[g] for g in range(gb)], axis=1)
            for t in range(lc):
                f_ref[pl.ds(t, nc, stride=lc), :] = lax.dot_general(cat, p_ref[t], (((1,), (1,)), ((), ())),
                                                                    preferred_element_type=F32)
            o_ref[...] = f_ref[...].astype(o_ref.dtype)

        return pl.pallas_call(
            body, name="s5_unchunk", grid=grid, in_specs=[grp, plc], out_specs=tok,
            out_shape=jax.ShapeDtypeStruct((bsz, tlen, d), BF16),
            scratch_shapes=[pltpu.VMEM((tlen, lanes), F32)], compiler_params=sem,
        )(y, _s5_placement(lanes))

    @jax.custom_vjp
    def chunk(a):
        return chunk_call(a)

    chunk.defvjp(lambda a: (chunk_call(a), None), lambda _, g: (unchunk_call(g),))

    @jax.custom_vjp
    def unchunk(y):
        return unchunk_call(y)

    unchunk.defvjp(lambda y: (unchunk_call(y), None), lambda _, g: (chunk_call(g),))
    return chunk, unchunk


def _s5_core(a, p, nctx):
    bsz, tlen, d = a.shape
    mt, pt, qt, a16 = _s5_mats(p["s5_lam_re"], p["s5_lam_im"], p["s5_log_dt"], p["s5_b_re"], p["s5_b_im"],
                               p["s5_c_re"], p["s5_c_im"])
    chunk, unchunk = _s5_relayout(bsz, tlen, d)
    y = unchunk(_s5_apply(bsz, nctx // S5_LC)(chunk(a), mt, pt, qt, a16))
    return jax.nn.gelu(p["s5_d"] * a.astype(F32) + y.astype(F32))


NA_LANES = 256
NA_MASKED = -1e30


def _na_tables(rpb):
    hi = lax.Precision.HIGHEST
    nh = rpb.shape[0]
    q = jnp.arange(GRID_W)
    kc = jnp.arange(GRID_W)
    q_start = jnp.clip(q - NA_COLS // 2, 0, GRID_W - NA_COLS)
    inwin = (kc[None, :] >= q_start[:, None]) & (kc[None, :] < q_start[:, None] + NA_COLS)
    dc = kc[None, :] - q[:, None] + NA_COLS - 1
    onehot = ((dc[:, :, None] == jnp.arange(2 * NA_COLS - 1)) & inwin[:, :, None]).astype(F32)
    a = jnp.arange(NA_ROWS)[None, :] - jnp.arange(NA_ROWS)[:, None] + NA_ROWS - 1
    tab = jnp.einsum('hskc,qlc->hsqkl', rpb[:, a, :], onehot, precision=hi)
    tab = jnp.where(inwin[None, None, :, None, :], tab, NA_MASKED)
    return tab.reshape(nh, NA_ROWS, GRID_W, NA_ROWS * GRID_W)


def _na_math(q2, kw, vw, kc, vc, bias, dh):
    scale = dh ** -0.5
    nq, nhb = q2.shape[0], NA_LANES // dh
    lane_head = lax.broadcasted_iota(jnp.int32, (1, NA_LANES), 1) // dh
    nt = (((1,), (1,)), ((), ()))
    kwb, vwb, kcb, vcb = (t.astype(BF16) for t in (kw, vw, kc, vc))
    qs = jnp.concatenate([jnp.where(lane_head == j, q2, 0.0) for j in range(nhb)], axis=0).astype(BF16)
    s_loc = lax.dot_general(qs, kwb, nt, preferred_element_type=F32) * scale + bias.reshape(nhb * nq, -1)
    s_ctx = lax.dot_general(qs, kcb, nt, preferred_element_type=F32) * scale
    m = jnp.maximum(jnp.max(s_loc, axis=-1, keepdims=True), jnp.max(s_ctx, axis=-1, keepdims=True))
    m = lax.stop_gradient(m)
    p_loc, p_ctx = jnp.exp(s_loc - m), jnp.exp(s_ctx - m)
    inv = 1.0 / (jnp.sum(p_loc, axis=-1, keepdims=True) + jnp.sum(p_ctx, axis=-1, keepdims=True))
    o_all = (jnp.dot((p_loc * inv).astype(BF16), vwb, preferred_element_type=F32)
             + jnp.dot((p_ctx * inv).astype(BF16), vcb, preferred_element_type=F32))
    out = jnp.zeros(q2.shape, F32)
    for j in range(nhb):
        out = out + jnp.where(lane_head == j, o_all[j * nq:(j + 1) * nq], 0.0)
    return out


def _na_attention(nctx, nh):
    def geometry(z):
        bsz, tlen, d3 = z.shape
        d = d3 // 3
        rows = (tlen - nctx) // GRID_W
        return bsz, tlen, d, rows, d // nh, d // NA_LANES

    def key_row0(r, rows):
        return jnp.clip(r - NA_ROWS // 2, 0, rows - NA_ROWS)

    def specs(z):
        bsz, tlen, d, rows, dh, nlb = geometry(z)
        hpb = NA_LANES // dh
        qs = pl.BlockSpec((None, GRID_W, NA_LANES), lambda b, h, r: (b, nctx // GRID_W + r, h))
        ks = pl.BlockSpec((None, tlen, NA_LANES), lambda b, h, r: (b, 0, nlb + h))
        vs = pl.BlockSpec((None, tlen, NA_LANES), lambda b, h, r: (b, 0, 2 * nlb + h))
        bs = pl.BlockSpec((hpb, None, GRID_W, NA_ROWS * GRID_W), lambda b, h, r: (h, r - key_row0(r, rows), 0, 0))
        os_ = pl.BlockSpec((None, GRID_W, NA_LANES), lambda b, h, r: (b, r, h))
        return qs, ks, vs, bs, os_

    def window(r, rows):
        return pl.ds(pl.multiple_of(nctx + key_row0(r, rows) * GRID_W, GRID_W), NA_ROWS * GRID_W)

    def fwd_call(z, bias):
        bsz, tlen, d, rows, dh, nlb = geometry(z)
        qs, ks, vs, bs, os_ = specs(z)

        def body(q_ref, k_ref, v_ref, b_ref, o_ref):
            win = window(pl.program_id(2), rows)
            o_ref[...] = _na_math(q_ref[...].astype(F32), k_ref[win, :], v_ref[win, :], k_ref[0:nctx, :],
                                  v_ref[0:nctx, :], b_ref[...], dh).astype(o_ref.dtype)

        return pl.pallas_call(
            body, name="na_fwd", grid=(bsz, nlb, rows), in_specs=[qs, ks, vs, bs], out_specs=os_,
            out_shape=jax.ShapeDtypeStruct((bsz, tlen - nctx, d), BF16),
            compiler_params=_cparams(("parallel", "parallel", "arbitrary")),
        )(z, z, z, bias)

    def bwd_call(z, bias, do):
        bsz, tlen, d, rows, dh, nlb = geometry(z)
        hpb = NA_LANES // dh
        qs, ks, vs, bs, os_ = specs(z)

        def body(q_ref, k_ref, v_ref, b_ref, do_ref, dq_ref, dk_ref, dv_ref, db_ref):
            r = pl.program_id(2)
            win = window(r, rows)

            @pl.when(r == 0)
            def _():
                dk_ref[...] = jnp.zeros_like(dk_ref)
                dv_ref[...] = jnp.zeros_like(dv_ref)

            prim = (q_ref[...].astype(F32), k_ref[win, :].astype(F32), v_ref[win, :].astype(F32),
                    k_ref[0:nctx, :].astype(F32), v_ref[0:nctx, :].astype(F32), b_ref[...])
            _, vjp = jax.vjp(functools.partial(_na_math, dh=dh), *prim)
            dq, dkw, dvw, dkc, dvc, db = vjp(do_ref[...].astype(F32))
            dq_ref[...] = dq.astype(dq_ref.dtype)
            dk_ref[win, :] += dkw
            dv_ref[win, :] += dvw
            dk_ref[0:nctx, :] += dkc
            dv_ref[0:nctx, :] += dvc
            prev = jnp.maximum(r - 1, 0)
            first = (r == 0) | ((r - key_row0(r, rows)) != (prev - key_row0(prev, rows)))

            @pl.when(first)
            def _():
                db_ref[...] = db

            @pl.when(jnp.logical_not(first))
            def _():
                db_ref[...] += db

        acc = pl.BlockSpec((None, tlen, NA_LANES), lambda b, h, r: (b, 0, h))
        dbs = pl.BlockSpec((None, hpb, None, GRID_W, NA_ROWS * GRID_W),
                           lambda b, h, r: (b, h, r - key_row0(r, rows), 0, 0))
        return pl.pallas_call(
            body, name="na_bwd", grid=(bsz, nlb, rows), in_specs=[qs, ks, vs, bs, os_],
            out_specs=[os_, acc, acc, dbs],
            out_shape=[jax.ShapeDtypeStruct((bsz, tlen - nctx, d), BF16), jax.ShapeDtypeStruct((bsz, tlen, d), F32),
                       jax.ShapeDtypeStruct((bsz, tlen, d), F32), jax.ShapeDtypeStruct((bsz,) + bias.shape, F32)],
            compiler_params=_cparams(("parallel", "parallel", "arbitrary")),
        )(z, z, z, bias, do)

    @jax.custom_vjp
    def attend(z, bias):
        return fwd_call(z, bias)

    def attend_fwd(z, bias):
        return fwd_call(z, bias), (z, bias)

    def attend_bwd(res, do):
        z, bias = res
        dq, dk, dv, db = bwd_call(z, bias, do)
        dq = jnp.pad(dq, ((0, 0), (nctx, 0), (0, 0)))
        dz = jnp.concatenate([dq, dk.astype(BF16), dv.astype(BF16)], axis=-1)
        return dz, _sum_leading(db.reshape(db.shape[0], -1, NA_ROWS * GRID_W), "na_sum_dbias").reshape(bias.shape)

    attend.defvjp(attend_fwd, attend_bwd)
    return attend


def _na_core(z, rpb, nctx):
    o = _na_attention(nctx, rpb.shape[0])(z, _na_tables(rpb))
    return jnp.pad(o, ((0, 0), (nctx, 0), (0, 0)))


def _forward(x, ctx, mod, p, wts):
    nctx = ctx.shape[1]
    h = jnp.concatenate([ctx, x], axis=1)
    lb_all = jnp.cumsum(jax.nn.softmax(p["hg_lower_bound"], axis=0), axis=0)
    lb_all = lb_all - lb_all[0]
    gains = p["norm_gains"]
    pre = _rowwise(_fn_pre, [BF16], "pre", 2)
    (a,) = pre([h], gains[0, 0:1], mod[0][:, :, 0:2])
    for i in range(DEPTH):
        tag = f"l{i}"
        if i == 0:
            z = _linear(False, tag + "_sc_in")(a, wts["sc_w_in"][0], p["sc_w_in"][0])
            yc = _conv_core(z, p["sc_conv"][0], nctx)
            y = _linear(False, tag + "_sc_out")(yc, wts["sc_w_out"][0], p["sc_w_out"][0])
        elif i == 1:
            z = _linear(False, tag + "_hg_in")(a, wts["hg_w_in"][0], p["hg_w_in"][0])
            yc = _hgrn_core(z, lb_all[i], p["hg_norm"][0], nctx)
            y = _linear(False, tag + "_hg_out")(yc, wts["hg_w_out"][0], p["hg_w_out"][0])
        elif i == 2:
            sp = {k: v[0] for k, v in p.items() if k.startswith("s5_") and k != "s5_w_glu"}
            gz = _s5_core(a, sp, nctx)
            vg = _linear(False, tag + "_s5_glu")(gz.astype(BF16), wts["s5_w_glu"][0], p["s5_w_glu"][0]).astype(F32)
            d = gz.shape[-1]
            y = (vg[..., :d] * jax.nn.sigmoid(vg[..., d:])).astype(BF16)
        else:
            z = _linear(False, tag + "_na_qkv")(a, wts["na_w_qkv"][0], p["na_w_qkv"][0])
            yc = _na_core(z, p["na_rpb"][0], nctx)
            y = _linear(False, tag + "_na_out")(yc, wts["na_w_out"][0], p["na_w_out"][0])
        h, a2 = _rowwise(_fn_post_pre, [F32, BF16], tag + "_mix_post", 2)(
            [h, y], gains[i, 1:3], mod[i][:, :, 2:5])
        u = _linear(False, tag + "_mlp_in")(a2, wts["mlp_w_in"][i], p["mlp_w_in"][i])
        f = _linear(True, tag + "_mlp_out")(u, wts["mlp_w_out"][i], p["mlp_w_out"][i])
        if i + 1 < DEPTH:
            cols = jnp.stack([gains[i, 3], gains[i + 1, 0]])
            segs = jnp.concatenate([mod[i][:, :, 5:6], mod[i + 1][:, :, 0:2]], axis=2)
            h, a = _rowwise(_fn_post_pre, [F32, BF16], tag + "_mlp_post", 2)([h, f], cols, segs)
        else:
            (h,) = _rowwise(_fn_post, [F32], tag + "_mlp_post", 2)([h, f], gains[i, 3:4], mod[i][:, :, 5:6])
    return h[:, nctx:]


def _local_step(x, ctx, tgt, mod, p, wts):
    y, vjp = jax.vjp(lambda x_, mod_, p_: _forward(x_, ctx, mod_, p_, wts), x, mod, p)
    lblk, dy = _loss_head(y, tgt)
    gx, dmod, gp = vjp(dy)
    return lblk, gx, dmod, gp


PACK_COLS = 512


def _pack_shard(ws):
    return jnp.concatenate([ws[n].reshape(-1, PACK_COLS) for n in BIG], axis=0)


def _unpack_full(buf, shard_shapes):
    out, r = {}, 0
    for n in BIG:
        s = shard_shapes[n]
        nr = math.prod(s) // PACK_COLS
        parts = buf[:, r:r + nr].reshape((N_SHARD,) + s)
        axis = 1 if n in COL_SHARDED else 0
        out[n] = [jnp.concatenate([parts[k, l] for k in range(N_SHARD)], axis=axis) for l in range(s[0])]
        r += nr
    return out


def _pack_grads(gp, shard_shapes):
    per = []
    for k in range(N_SHARD):
        rows = []
        for n in BIG:
            s = shard_shapes[n]
            axis = 1 if n in COL_SHARDED else 0
            width = s[1 + axis]
            for g in gp[n]:
                rows.append(lax.slice_in_dim(g, k * width, (k + 1) * width, axis=axis).reshape(-1, PACK_COLS))
        per.append(jnp.concatenate(rows, axis=0))
    return jnp.stack(per)


def _unpack_shard(buf, shard_shapes):
    out, r = {}, 0
    for n in BIG:
        s = shard_shapes[n]
        nr = math.prod(s) // PACK_COLS
        out[n] = buf[r:r + nr].reshape(s)
        r += nr
    return out


def _shard_cols(a, k, width):
    return lax.dynamic_slice_in_dim(a, k * width, width, axis=a.ndim - 1)


def kernel(x, c, ctx, c_ctx, ada_w, ada_b, norm_gains, mlp_w_in, mlp_w_out, sc_w_in, sc_conv, sc_w_out, hg_w_in, hg_lower_bound, hg_norm, hg_w_out, s5_lam_re, s5_lam_im, s5_log_dt, s5_b_re, s5_b_im, s5_c_re, s5_c_im, s5_d, s5_w_glu, na_w_qkv, na_rpb, na_w_out, loss_target, m_c_ctx, m_ada_w, m_ada_b, m_norm_gains, m_mlp_w_in, m_mlp_w_out, m_sc_w_in, m_sc_conv, m_sc_w_out, m_hg_w_in, m_hg_lower_bound, m_hg_norm, m_hg_w_out, m_s5_lam_re, m_s5_lam_im, m_s5_log_dt, m_s5_b_re, m_s5_b_im, m_s5_c_re, m_s5_c_im, m_s5_d, m_s5_w_glu, m_na_w_qkv, m_na_rpb, m_na_w_out, v_c_ctx, v_ada_w, v_ada_b, v_norm_gains, v_mlp_w_in, v_mlp_w_out, v_sc_w_in, v_sc_conv, v_sc_w_out, v_hg_w_in, v_hg_lower_bound, v_hg_norm, v_hg_w_out, v_s5_lam_re, v_s5_lam_im, v_s5_log_dt, v_s5_b_re, v_s5_b_im, v_s5_c_re, v_s5_c_im, v_s5_d, v_s5_w_glu, v_na_w_qkv, v_na_rpb, v_na_w_out):
    w = dict(c_ctx=c_ctx, ada_w=ada_w, ada_b=ada_b, norm_gains=norm_gains, mlp_w_in=mlp_w_in, mlp_w_out=mlp_w_out,
             sc_w_in=sc_w_in, sc_conv=sc_conv, sc_w_out=sc_w_out, hg_w_in=hg_w_in, hg_lower_bound=hg_lower_bound,
             hg_norm=hg_norm, hg_w_out=hg_w_out, s5_lam_re=s5_lam_re, s5_lam_im=s5_lam_im, s5_log_dt=s5_log_dt,
             s5_b_re=s5_b_re, s5_b_im=s5_b_im, s5_c_re=s5_c_re, s5_c_im=s5_c_im, s5_d=s5_d, s5_w_glu=s5_w_glu,
             na_w_qkv=na_w_qkv, na_rpb=na_rpb, na_w_out=na_w_out)
    mom_m = dict(zip(WEIGHTS, [m_c_ctx, m_ada_w, m_ada_b, m_norm_gains, m_mlp_w_in, m_mlp_w_out, m_sc_w_in, m_sc_conv,
                               m_sc_w_out, m_hg_w_in, m_hg_lower_bound, m_hg_norm, m_hg_w_out, m_s5_lam_re, m_s5_lam_im,
                               m_s5_log_dt, m_s5_b_re, m_s5_b_im, m_s5_c_re, m_s5_c_im, m_s5_d, m_s5_w_glu, m_na_w_qkv,
                               m_na_rpb, m_na_w_out]))
    mom_v = dict(zip(WEIGHTS, [v_c_ctx, v_ada_w, v_ada_b, v_norm_gains, v_mlp_w_in, v_mlp_w_out, v_sc_w_in, v_sc_conv,
                               v_sc_w_out, v_hg_w_in, v_hg_lower_bound, v_hg_norm, v_hg_w_out, v_s5_lam_re, v_s5_lam_im,
                               v_s5_log_dt, v_s5_b_re, v_s5_b_im, v_s5_c_re, v_s5_c_im, v_s5_d, v_s5_w_glu, v_na_w_qkv,
                               v_na_rpb, v_na_w_out]))
    bsz, _, d = x.shape
    ax, ay, ac = lax.axis_index("x"), lax.axis_index("y"), lax.axis_index("c")
    chip = 2 * ax + ay
    dev = 2 * chip + ac
    n_dev = 2 * N_SHARD
    dsh = d // N_SHARD

    shard_shapes = {n: w[n].shape for n in BIG}
    own = _pack_shard({n: w[n].astype(BF16) for n in BIG})
    packed = lax.dynamic_update_slice(_gather_shards(own, "gather_weights"), own[None], (chip, 0, 0))
    wts = _unpack_full(packed, shard_shapes)
    full = {n: [jnp.zeros(a.shape, F32) for a in v] for n, v in wts.items()}

    small_shapes = [c.shape] + [w[n].shape for n in SMALL_SHARDED]
    buf_a = _all_gather8(_pad_rows(_pack_rows([c] + [w[n] for n in SMALL_SHARDED])), "gather_small")
    per_dev = [_unpack_rows(buf_a[k], small_shapes) for k in range(n_dev)]
    c_all = jnp.concatenate([per_dev[k][0] for k in range(n_dev)], axis=0)
    for j, n in enumerate(SMALL_SHARDED):
        full[n] = jnp.concatenate([per_dev[2 * s][1 + j] for s in range(N_SHARD)], axis=-1)
    for n in SMALL_REPL:
        full[n] = w[n]

    n_all = c_all.shape[0]
    s_rows = 32
    cond = jnp.concatenate([c_all, c_ctx[None]], axis=0)
    s_all = jnp.pad(jax.nn.silu(cond), ((0, s_rows - n_all - 1), (0, 0))).astype(BF16)
    mod_part = _ada_fwd(s_all, ada_w)
    nsh = mod_part.shape[-1]
    buf_b = _all_gather8(_pad_rows(mod_part.reshape(-1, LANES)), "gather_mod")
    nrow_b = mod_part.size // LANES
    mod_raw = jnp.concatenate([buf_b[2 * s, :nrow_b].reshape(mod_part.shape) for s in range(N_SHARD)], axis=-1)
    mod_raw = mod_raw + ada_b[:, None, :]
    mod_lat = lax.dynamic_slice_in_dim(mod_raw, dev * bsz, bsz, axis=1).reshape(DEPTH, bsz, 1, N_MOD, d)
    mod_ctx = jnp.broadcast_to(mod_raw[:, n_all].reshape(DEPTH, 1, 1, N_MOD, d), (DEPTH, bsz, 1, N_MOD, d))
    mod = jnp.concatenate([mod_ctx, mod_lat], axis=2)

    lblk, grad_x, dmod, gp = _local_step(x, ctx, loss_target, mod, full, wts)

    dmod_rows = jnp.concatenate([dmod[:, :, 1].reshape(DEPTH, bsz, N_MOD * d),
                                 jnp.sum(dmod[:, :, 0], axis=1).reshape(DEPTH, 1, N_MOD * d)], axis=1)
    c_list = [dmod_rows] + [gp[n] for n in SMALL_SHARDED + SMALL_REPL] + [jnp.sum(lblk).reshape(1)]
    c_shapes = [a.shape for a in c_list]
    buf_c = _all_gather8(_pad_rows(_pack_rows(c_list)), "gather_grads")
    sum_c = _sum_leading(buf_c, "sum_grads")
    summed = _unpack_rows(sum_c, c_shapes)
    grads = {}
    for j, n in enumerate(SMALL_SHARDED):
        grads[n] = _shard_cols(summed[1 + j], chip, w[n].shape[-1])
    for j, n in enumerate(SMALL_REPL):
        grads[n] = summed[1 + len(SMALL_SHARDED) + j]
    loss = summed[-1][0]

    dmod_dev = [_unpack_rows(buf_c[k], c_shapes[:1])[0] for k in range(n_dev)]
    dm_lat = jnp.concatenate([t[:, :bsz] for t in dmod_dev], axis=1)
    dm_ctx = summed[0][:, bsz:bsz + 1]
    dm_all = jnp.concatenate([dm_lat, dm_ctx], axis=1)
    grads["ada_b"] = _sum_leading(jnp.moveaxis(dm_all, 1, 0).reshape(n_all + 1, -1, LANES), "sum_ada_b").reshape(ada_b.shape)
    dm_sh = jnp.pad(_shard_cols(dm_all, chip, nsh), ((0, 0), (0, s_rows - n_all - 1), (0, 0)))
    grads["ada_w"] = _ada_dw(s_all, dm_sh)
    ds_part = _ada_ds(dm_sh, ada_w)
    buf_d = _all_gather8(_pad_rows(ds_part[n_all:n_all + 1]), "gather_dcond")
    ds_ctx = _sum_leading(jnp.stack([buf_d[2 * s] for s in range(N_SHARD)]), "sum_dcond")[0]
    grads["c_ctx"] = jax.vjp(jax.nn.silu, c_ctx)[1](ds_ctx)[0]

    g_pack = _pack_grads(gp, shard_shapes)
    c_idx = jnp.reshape(ac, (1,)).astype(jnp.int32)
    s_idx = jnp.reshape(chip, (1,)).astype(jnp.int32)
    part, part_wire = _add_own_half(g_pack, _swap_other_half(g_pack, "rs_swap_half"), c_idx, "rs_add_sibling")
    mine = _add_arrivals(part, _send_to_chips(part_wire, "rs_send_chips"), s_idx, "rs_add_chips")
    other = _swap_reduced_half(mine, "rs_swap_reduced")
    joined = jnp.concatenate([jnp.where(ac == 0, mine, other), jnp.where(ac == 0, other, mine)], axis=0)
    grads.update(_unpack_shard(joined, shard_shapes))

    delta, new_m, new_v = {}, {}, {}
    for n in BIG + ["ada_w"]:
        delta[n], new_m[n], new_v[n] = _adamw(w[n], grads[n], mom_m[n], mom_v[n], "adamw_" + n)
    small = [n for n in WEIGHTS if n not in BIG and n != "ada_w"]
    shapes = [w[n].shape for n in small]
    packs = [_pad_rows(_pack_rows([src[n] for n in small])) for src in (w, grads, mom_m, mom_v)]
    outs = _adamw(*packs, "adamw_small")
    for tgt, buf in zip((delta, new_m, new_v), outs):
        for n, a in zip(small, _unpack_rows(buf, shapes)):
            tgt[n] = a
    return (loss, grad_x, *[grads[n] for n in WEIGHTS], *[delta[n] for n in WEIGHTS],
            *[new_m[n] for n in WEIGHTS], *[new_v[n] for n in WEIGHTS])
```
